```python
import math
import jax, jax.numpy as jnp
from jax import lax
import numpy as np

D_MODEL = 1024
BATCH = 16
SEQ = 256
DEPTH = 4
DEC_BATCH = 8
DEC_SEQ = 4096
PAST_LEN = 256

GRID_W = 64
BLOCK = 128
N_MIXERS = 3
N_A = len(range(0, DEPTH, N_MIXERS))
N_B = len(range(1, DEPTH, N_MIXERS))
N_C = len(range(2, DEPTH, N_MIXERS))
N_MOD = 9
D_FF = 2816
A_WIDTH = D_MODEL
A_GROUPS = 8
A_GROUP_DIM = A_WIDTH // A_GROUPS
A_CHUNK = 128
B_HEADS = 16
B_KV_HEADS = 4
B_HEAD_DIM = 64
B_WINDOW = 128
C_HEADS = 16
C_NOPE = 64
C_ROPE = 32
C_VDIM = 64
C_Q_LORA = 512
C_KV_LORA = 256
ROPE_BASE = 10000.0
EPS = 1e-6
NEG_INF = -1e30

kernel_name = "hybrid_dit_interleaved_gmlp_swa_mla_step"


def rms_norm(x, gain=None):
    xf = x.astype(jnp.float32)
    y = xf * lax.rsqrt(jnp.mean(xf * xf, axis=-1, keepdims=True) + EPS)
    if gain is not None:
        y = y * gain.astype(jnp.float32)
    return y.astype(x.dtype)


def modulation(cond, w, b):
    m = jax.nn.silu(cond) @ w + b
    return m.reshape(cond.shape[0], N_MOD, D_MODEL)


def ada_rms(x, m, k):
    return rms_norm(x) * (1.0 + m[:, 3 * k + 1, None]) + m[:, 3 * k, None]


def swiglu(h, w_in, w_out):
    g, u = jnp.split(h @ w_in, 2, axis=-1)
    return (jax.nn.silu(g) * u) @ w_out


def half_ffn(x, m, k, w_in, w_out):
    return x + 0.5 * m[:, 3 * k + 2, None] * swiglu(ada_rms(x, m, k), w_in, w_out)


def axial_rope_tables(n, rot_dim):
    quarter = rot_dim // 4
    inv = ROPE_BASE ** (-jnp.arange(quarter, dtype=jnp.float32) / quarter)
    t = jnp.arange(n)
    row = (t // GRID_W).astype(jnp.float32)
    col = (t % GRID_W).astype(jnp.float32)
    ang = jnp.stack([row[:, None] * inv, col[:, None] * inv], axis=1)
    return jnp.cos(ang)[:, None], jnp.sin(ang)[:, None]


def apply_axial_rope(x, cos, sin):
    shp = x.shape
    xs = x.reshape(shp[:-1] + (2, 2, shp[-1] // 4)).astype(jnp.float32)
    x1, x2 = xs[..., 0, :], xs[..., 1, :]
    out = jnp.stack([x1 * cos - x2 * sin, x2 * cos + x1 * sin], axis=-2)
    return out.reshape(shp).astype(x.dtype)


def attend(q, k, v, sink, mask):
    scale = q.shape[-1] ** -0.5
    s = jnp.einsum('bqhgd,bkhd->bhgqk', q, k).astype(jnp.float32) * scale
    if mask is not None:
        s = jnp.where(mask, s, NEG_INF)
    if sink is None:
        p = jax.nn.softmax(s, axis=-1)
    else:
        sk = sink.astype(jnp.float32)[None, :, :, None, None]
        mx = jnp.maximum(jnp.max(s, axis=-1, keepdims=True), sk)
        e = jnp.exp(s - mx)
        p = e / (jnp.sum(e, axis=-1, keepdims=True) + jnp.exp(sk - mx))
    return jnp.einsum('bhgqk,bkhd->bqhgd', p.astype(v.dtype), v)


def sweep_query_blocks(fn, q):
    bsz, n = q.shape[0], q.shape[1]
    n_blk = n // BLOCK
    qb = jnp.moveaxis(q.reshape((bsz, n_blk, BLOCK) + q.shape[2:]), 1, 0)
    out = lax.map(lambda a: fn(a[0], a[1]), (qb, jnp.arange(n_blk)))
    out = jnp.moveaxis(out, 0, 1)
    return out.reshape((bsz, n) + out.shape[3:])


def chunk_gmlp(h, w_in, v_gain, w_s, b_s, w_out):
    bsz, n, _ = h.shape
    uv = jax.nn.gelu(h @ w_in, approximate=False)
    u, v = jnp.split(uv, 2, axis=-1)
    v = rms_norm(v, v_gain)
    v = v.reshape(bsz, n // A_CHUNK, A_CHUNK, A_GROUPS, A_GROUP_DIM)
    sv = jnp.einsum('gpq,bcqgd->bcpgd', w_s, v) + b_s.T[None, None, :, :, None]
    return (u * sv.reshape(bsz, n, A_WIDTH)) @ w_out


def gqa_project(h, w_qkv, q_gain, k_gain):
    bsz, n, _ = h.shape
    q, k, v = jnp.split(h @ w_qkv, [B_HEADS * B_HEAD_DIM, (B_HEADS + B_KV_HEADS) * B_HEAD_DIM], axis=-1)
    q = rms_norm(q.reshape(bsz, n, B_HEADS, B_HEAD_DIM), q_gain)
    k = rms_norm(k.reshape(bsz, n, B_KV_HEADS, B_HEAD_DIM), k_gain)
    v = v.reshape(bsz, n, B_KV_HEADS, B_HEAD_DIM)
    return q, k, v


def window_attn_ctx(h, w_qkv, q_gain, k_gain, sink, w_o):
    bsz, n, _ = h.shape
    q, k, v = gqa_project(h, w_qkv, q_gain, k_gain)
    q = q.reshape(bsz, n, B_KV_HEADS, B_HEADS // B_KV_HEADS, B_HEAD_DIM)
    sk = sink.reshape(B_KV_HEADS, B_HEADS // B_KV_HEADS)
    o = sweep_query_blocks(lambda qb, j: attend(qb, k, v, sk, None), q)
    return o.reshape(bsz, n, B_HEADS * B_HEAD_DIM) @ w_o, k, v


def window_attn_lat(h, ck, cv, w_qkv, q_gain, k_gain, sink, w_o):
    bsz, n, _ = h.shape
    lc = ck.shape[1]
    q, k, v = gqa_project(h, w_qkv, q_gain, k_gain)
    cos, sin = axial_rope_tables(n, B_HEAD_DIM)
    q = apply_axial_rope(q, cos, sin).reshape(bsz, n, B_KV_HEADS, B_HEADS // B_KV_HEADS, B_HEAD_DIM)
    k = apply_axial_rope(k, cos, sin)
    pad = ((0, 0), (BLOCK, BLOCK), (0, 0), (0, 0))
    kp, vp = jnp.pad(k, pad), jnp.pad(v, pad)
    qi = jnp.arange(BLOCK)[:, None]
    pk = jnp.arange(3 * BLOCK)[None, :]
    band = jnp.abs(pk - BLOCK - qi) <= B_WINDOW
    ctx_ok = jnp.ones((BLOCK, lc), dtype=bool)
    sk = sink.reshape(B_KV_HEADS, B_HEADS // B_KV_HEADS)

    def blk(qb, j):
        kb = lax.dynamic_slice_in_dim(kp, j * BLOCK, 3 * BLOCK, axis=1)
        vb = lax.dynamic_slice_in_dim(vp, j * BLOCK, 3 * BLOCK, axis=1)
        kpos = j * BLOCK + pk - BLOCK
        mask = jnp.concatenate([band & (kpos >= 0) & (kpos < n), ctx_ok], axis=1)
        return attend(qb, jnp.concatenate([kb, ck], axis=1), jnp.concatenate([vb, cv], axis=1), sk, mask)

    o = sweep_query_blocks(blk, q)
    return o.reshape(bsz, n, B_HEADS * B_HEAD_DIM) @ w_o


def mla_down(h, w_down, cq_gain, ckv_gain):
    c_q, c_kv, k_rope = jnp.split(h @ w_down, [C_Q_LORA, C_Q_LORA + C_KV_LORA], axis=-1)
    return rms_norm(c_q, cq_gain), rms_norm(c_kv, ckv_gain), k_rope


def mla_queries(c_q, w_uq, q_gain):
    bsz, n, _ = c_q.shape
    return rms_norm((c_q @ w_uq).reshape(bsz, n, C_HEADS, C_NOPE + C_ROPE), q_gain)


def mla_keys(c_kv, k_rope, w_ukv, k_gain):
    bsz, n, _ = c_kv.shape
    kv = (c_kv @ w_ukv).reshape(bsz, n, C_HEADS, C_NOPE + C_VDIM)
    k_nope, v = jnp.split(kv, [C_NOPE], axis=-1)
    k = jnp.concatenate([k_nope, jnp.broadcast_to(k_rope[:, :, None, :], (bsz, n, C_HEADS, C_ROPE))], axis=-1)
    return rms_norm(k, k_gain), v


def rope_tail(x, cos, sin):
    return jnp.concatenate([x[..., :C_NOPE], apply_axial_rope(x[..., C_NOPE:], cos, sin)], axis=-1)


def mla_ctx(h, w_down, cq_gain, ckv_gain, w_uq, w_ukv, q_gain, k_gain, w_o):
    bsz, n, _ = h.shape
    c_q, c_kv, k_rope = mla_down(h, w_down, cq_gain, ckv_gain)
    q = mla_queries(c_q, w_uq, q_gain)[:, :, :, None, :]
    k, v = mla_keys(c_kv, k_rope, w_ukv, k_gain)
    o = sweep_query_blocks(lambda qb, j: attend(qb, k, v, None, None), q)
    return o.reshape(bsz, n, C_HEADS * C_VDIM) @ w_o, c_kv, k_rope


def mla_lat(h, cache_ckv, cache_krope, w_down, cq_gain, ckv_gain, w_uq, w_ukv, q_gain, k_gain, w_o):
    bsz, n, _ = h.shape
    c_q, c_kv, k_rope = mla_down(h, w_down, cq_gain, ckv_gain)
    cos, sin = axial_rope_tables(n, C_ROPE)
    q = rope_tail(mla_queries(c_q, w_uq, q_gain), cos, sin)[:, :, :, None, :]
    k, v = mla_keys(c_kv, k_rope, w_ukv, k_gain)
    k = rope_tail(k, cos, sin)
    ck, cv = mla_keys(cache_ckv, cache_krope, w_ukv, k_gain)
    k_all = jnp.concatenate([k, ck], axis=1)
    v_all = jnp.concatenate([v, cv], axis=1)
    o = sweep_query_blocks(lambda qb, j: attend(qb, k_all, v_all, None, None), q)
    return o.reshape(bsz, n, C_HEADS * C_VDIM) @ w_o


def setup_inputs(seed: int = 0) -> dict:
    key = jax.random.key(seed)
    ks = iter(jax.random.split(key, 48))

    def nrm(shape, scale=1.0):
        return jax.random.normal(next(ks), shape, jnp.float32) * scale

    def gain(shape):
        return 1.0 + nrm(shape, 0.02)

    D = D_MODEL
    hd_c = C_NOPE + C_ROPE
    return {
        "x_prompt": nrm((BATCH, SEQ, D)),
        "x_sample": nrm((DEC_BATCH, DEC_SEQ, D)),
        "c": nrm((DEC_BATCH, D)),
        "cache_win_k": nrm((DEC_BATCH, N_B, PAST_LEN, B_KV_HEADS, B_HEAD_DIM)),
        "cache_win_v": nrm((DEC_BATCH, N_B, PAST_LEN, B_KV_HEADS, B_HEAD_DIM)),
        "cache_mla_ckv": nrm((DEC_BATCH, N_C, PAST_LEN, C_KV_LORA)),
        "cache_mla_krope": nrm((DEC_BATCH, N_C, PAST_LEN, C_ROPE)),
        "c_ctx": nrm((D,)),
        "ada_w": nrm((DEPTH, D, N_MOD * D), 0.02),
        "ada_b": nrm((DEPTH, N_MOD * D), 0.02),
        "ffn_w_in": nrm((DEPTH, 2, D, 2 * D_FF), D ** -0.5),
        "ffn_w_out": nrm((DEPTH, 2, D_FF, D), D_FF ** -0.5),
        "gmlp_w_in": nrm((N_A, D, 2 * A_WIDTH), D ** -0.5),
        "gmlp_v_gain": gain((N_A, A_WIDTH)),
        "gmlp_w_s": nrm((N_A, A_GROUPS, A_CHUNK, A_CHUNK), A_CHUNK ** -0.5),
        "gmlp_b_s": gain((N_A, A_GROUPS, A_CHUNK)),
        "gmlp_w_out": nrm((N_A, A_WIDTH, D), A_WIDTH ** -0.5),
        "win_w_qkv": nrm((N_B, D, (B_HEADS + 2 * B_KV_HEADS) * B_HEAD_DIM), D ** -0.5),
        "win_q_gain": gain((N_B, B_HEAD_DIM)),
        "win_k_gain": gain((N_B, B_HEAD_DIM)),
        "win_sink": nrm((N_B, B_HEADS), 0.5),
        "win_w_o": nrm((N_B, B_HEADS * B_HEAD_DIM, D), (B_HEADS * B_HEAD_DIM) ** -0.5),
        "mla_w_down": nrm((N_C, D, C_Q_LORA + C_KV_LORA + C_ROPE), D ** -0.5),
        "mla_cq_gain": gain((N_C, C_Q_LORA)),
        "mla_ckv_gain": gain((N_C, C_KV_LORA)),
        "mla_w_uq": nrm((N_C, C_Q_LORA, C_HEADS * hd_c), C_Q_LORA ** -0.5),
        "mla_w_ukv": nrm((N_C, C_KV_LORA, C_HEADS * (C_NOPE + C_VDIM)), C_KV_LORA ** -0.5),
        "mla_q_gain": gain((N_C, hd_c)),
        "mla_k_gain": gain((N_C, hd_c)),
        "mla_w_o": nrm((N_C, C_HEADS * C_VDIM, D), (C_HEADS * C_VDIM) ** -0.5),
    }


def reference(x_prompt, x_sample, c, cache_win_k, cache_win_v, cache_mla_ckv, cache_mla_krope, c_ctx,
              ada_w, ada_b, ffn_w_in, ffn_w_out,
              gmlp_w_in, gmlp_v_gain, gmlp_w_s, gmlp_b_s, gmlp_w_out,
              win_w_qkv, win_q_gain, win_k_gain, win_sink, win_w_o,
              mla_w_down, mla_cq_gain, mla_ckv_gain, mla_w_uq, mla_w_ukv, mla_q_gain, mla_k_gain, mla_w_o):
    xp, xs = x_prompt, x_sample
    win_k_list, win_v_list, ckv_list, krope_list = [], [], [], []
    ia = ib = ic = 0
    for l in range(DEPTH):
        mp = modulation(c_ctx[None, :], ada_w[l], ada_b[l])
        ms = modulation(c, ada_w[l], ada_b[l])
        xp = half_ffn(xp, mp, 0, ffn_w_in[l, 0], ffn_w_out[l, 0])
        xs = half_ffn(xs, ms, 0, ffn_w_in[l, 0], ffn_w_out[l, 0])
        hp = ada_rms(xp, mp, 1)
        hs = ada_rms(xs, ms, 1)
        kind = l % N_MIXERS
        if kind == 0:
            wa = (gmlp_w_in[ia], gmlp_v_gain[ia], gmlp_w_s[ia], gmlp_b_s[ia], gmlp_w_out[ia])
            yp = chunk_gmlp(hp, *wa)
            ys = chunk_gmlp(hs, *wa)
            ia += 1
        elif kind == 1:
            wb = (win_w_qkv[ib], win_q_gain[ib], win_k_gain[ib], win_sink[ib], win_w_o[ib])
            yp, k_ctx, v_ctx = window_attn_ctx(hp, *wb)
            ys = window_attn_lat(hs, cache_win_k[:, ib], cache_win_v[:, ib], *wb)
            win_k_list.append(k_ctx)
            win_v_list.append(v_ctx)
            ib += 1
        else:
            wc = (mla_w_down[ic], mla_cq_gain[ic], mla_ckv_gain[ic], mla_w_uq[ic], mla_w_ukv[ic],
                  mla_q_gain[ic], mla_k_gain[ic], mla_w_o[ic])
            yp, ckv_ctx, krope_ctx = mla_ctx(hp, *wc)
            ys = mla_lat(hs, cache_mla_ckv[:, ic], cache_mla_krope[:, ic], *wc)
            ckv_list.append(ckv_ctx)
            krope_list.append(krope_ctx)
            ic += 1
        xp = xp + mp[:, 5, None] * yp
        xs = xs + ms[:, 5, None] * ys
        xp = half_ffn(xp, mp, 2, ffn_w_in[l, 1], ffn_w_out[l, 1])
        xs = half_ffn(xs, ms, 2, ffn_w_in[l, 1], ffn_w_out[l, 1])
    new_win_k = jnp.stack(win_k_list, axis=1)
    new_win_v = jnp.stack(win_v_list, axis=1)
    new_mla_ckv = jnp.stack(ckv_list, axis=1)
    new_mla_krope = jnp.stack(krope_list, axis=1)
    return (xp, xs, new_win_k, new_win_v, new_mla_ckv, new_mla_krope)
```

```python
import functools
import math

import jax
import jax.numpy as jnp
from jax import lax
from jax.experimental import pallas as pl
from jax.experimental.pallas import tpu as pltpu

D_MODEL = 1024
BATCH = 16
SEQ = 256
DEPTH = 4
DEC_BATCH = 8
DEC_SEQ = 4096
PAST_LEN = 256
GRID_W = 64
N_MIXERS = 3
N_MOD = 9
D_FF = 2816
A_WIDTH = D_MODEL
A_GROUPS = 8
A_CHUNK = 128
B_HEADS = 16
B_KV_HEADS = 4
B_HEAD_DIM = 64
B_WINDOW = 128
C_HEADS = 16
C_NOPE = 64
C_ROPE = 32
C_VDIM = 64
C_Q_LORA = 512
C_KV_LORA = 256
ROPE_BASE = 10000.0
EPS = 1e-6
NEG_INF = -1e30

LANES = 128
N_SAMPLE = DEC_BATCH * DEC_SEQ
N_PROMPT = BATCH * SEQ
N_TOK = N_SAMPLE + N_PROMPT
N_COND = 16
TM = 512
N_TILES = N_TOK // TM
N_SAMPLE_TILES = N_SAMPLE // TM
TILES_PER_SEQ = DEC_SEQ // TM
MOD_TN = 1536
BQ = 128
CQ = 512
C_HEAD_PAD = 128
C_DOWN_PAD = C_Q_LORA + C_KV_LORA + LANES
VMEM_LIMIT_BYTES = 56 * 1024 * 1024

BF = jnp.bfloat16
F32 = jnp.float32


def _params(*sem):
    return pltpu.CompilerParams(dimension_semantics=sem, vmem_limit_bytes=VMEM_LIMIT_BYTES)


def _dot(a, b):
    return jnp.dot(a, b, preferred_element_type=F32)


def _dot_t(a, b):
    return lax.dot_general(a, b, (((1,), (1,)), ((), ())), preferred_element_type=F32)


def _rms(x):
    return x * lax.rsqrt(jnp.mean(x * x, axis=-1, keepdims=True) + EPS)


def _ada(x, mod_ref, k):
    shift = mod_ref[3 * k:3 * k + 1, :]
    scale = mod_ref[3 * k + 1:3 * k + 2, :]
    return _rms(x) * (1.0 + scale) + shift


def _const_spec(shape):
    nd = len(shape)
    return pl.BlockSpec(shape, lambda *_: (0,) * nd, pipeline_mode=pl.Buffered(1))


def _tile_row(i):
    return jnp.minimum(i * TM // DEC_SEQ, DEC_BATCH)


def _tok_spec(width):
    return pl.BlockSpec((TM, width), lambda i: (i, 0))


_MOD_SPEC = pl.BlockSpec((None, N_MOD, D_MODEL), lambda i: (_tile_row(i), 0, 0))


def _rope_tile(i):
    return jnp.where(i < N_SAMPLE_TILES, i % TILES_PER_SEQ, TILES_PER_SEQ)


_ROPE_SPEC = pl.BlockSpec((TM, LANES), lambda i: (_rope_tile(i), 0))


def _mod_kernel(c_ref, w_ref, b_ref, o_ref):
    a = jax.nn.silu(c_ref[...]).astype(BF)
    o_ref[...] = _dot(a, w_ref[...].astype(BF)) + b_ref[...]


def _modulation(cond, ada_w, ada_b):
    n_out = N_MOD * D_MODEL
    out = pl.pallas_call(
        _mod_kernel,
        grid=(DEPTH, n_out // MOD_TN),
        in_specs=[
            pl.BlockSpec((N_COND, D_MODEL), lambda l, j: (0, 0)),
            pl.BlockSpec((None, D_MODEL, MOD_TN), lambda l, j: (l, 0, j)),
            pl.BlockSpec((None, 1, MOD_TN), lambda l, j: (l, 0, j)),
        ],
        out_specs=pl.BlockSpec((None, N_COND, MOD_TN), lambda l, j: (l, 0, j)),
        out_shape=jax.ShapeDtypeStruct((DEPTH, N_COND, n_out), F32),
        compiler_params=_params("arbitrary", "arbitrary"),
        name="modulation",
    )(cond, ada_w, ada_b.reshape(DEPTH, 1, n_out))
    return out.reshape(DEPTH, N_COND, N_MOD, D_MODEL)


def _ffn_kernel(x_ref, mod_ref, win_ref, wout_ref, o_ref, *, k):
    x = x_ref[...]
    hb = _ada(x, mod_ref, k).astype(BF)
    gu = _dot(hb, win_ref[...])
    act = (jax.nn.silu(gu[:, :D_FF]) * gu[:, D_FF:]).astype(BF)
    y = _dot(act, wout_ref[...])
    gate = mod_ref[3 * k + 2:3 * k + 3, :]
    o_ref[...] = x + (0.5 * gate) * y


def _ffn(x, mod, w_in, w_out, k):
    return pl.pallas_call(
        functools.partial(_ffn_kernel, k=k),
        grid=(N_TILES,),
        in_specs=[_tok_spec(D_MODEL), _MOD_SPEC, _const_spec((D_MODEL, 2 * D_FF)), _const_spec((D_FF, D_MODEL))],
        out_specs=_tok_spec(D_MODEL),
        out_shape=jax.ShapeDtypeStruct((N_TOK, D_MODEL), F32),
        compiler_params=_params("arbitrary"),
        name="ffn",
    )(x, mod, w_in, w_out)


def _gmlp_kernel(x_ref, mod_ref, win_ref, vg_ref, ws_ref, bs_ref, wout_ref, o_ref):
    x = x_ref[...]
    hb = _ada(x, mod_ref, 1).astype(BF)
    pre = _dot(hb, win_ref[...])
    uv = 0.5 * pre * (1.0 + lax.erf(pre * math.sqrt(0.5)))
    u = uv[:, :A_WIDTH]
    v = (_rms(uv[:, A_WIDTH:]) * vg_ref[...]).astype(BF)
    bias = bs_ref[...]
    rows = []
    for c in range(TM // A_CHUNK):
        cols = [_dot(ws_ref[g], v[c * A_CHUNK:(c + 1) * A_CHUNK, g * LANES:(g + 1) * LANES])
                for g in range(A_GROUPS)]
        rows.append(jnp.concatenate(cols, axis=1) + bias)
    sv = jnp.concatenate(rows, axis=0)
    y = _dot((u * sv).astype(BF), wout_ref[...])
    o_ref[...] = x + mod_ref[5:6, :] * y


def _gmlp(x, mod, w_in, v_gain, w_s, b_s, w_out):
    bias = jnp.repeat(b_s.T, A_WIDTH // A_GROUPS, axis=1)
    return pl.pallas_call(
        _gmlp_kernel,
        grid=(N_TILES,),
        in_specs=[_tok_spec(D_MODEL), _MOD_SPEC,
                  _const_spec((D_MODEL, 2 * A_WIDTH)), _const_spec((1, A_WIDTH)),
                  _const_spec((A_GROUPS, A_CHUNK, A_CHUNK)), _const_spec((A_CHUNK, A_WIDTH)),
                  _const_spec((A_WIDTH, D_MODEL))],
        out_specs=_tok_spec(D_MODEL),
        out_shape=jax.ShapeDtypeStruct((N_TOK, D_MODEL), F32),
        compiler_params=_params("arbitrary"),
        name="gmlp",
    )(x, mod, w_in.astype(BF), v_gain.reshape(1, A_WIDTH), w_s.astype(BF), bias, w_out.astype(BF))


def _swap_pairs(y, step):
    lane = lax.broadcasted_iota(jnp.int32, y.shape, 1)
    return jnp.where((lane & step) != 0, pltpu.roll(y, step, 1), pltpu.roll(y, LANES - step, 1))


def _rope_tables(rot_dim, lane_of_dim, tile_dims):
    quarter = rot_dim // 4
    inv = ROPE_BASE ** (-jnp.arange(quarter, dtype=F32) / quarter)
    t = jnp.arange(DEC_SEQ)
    row = (t // GRID_W).astype(F32)
    col = (t % GRID_W).astype(F32)
    ang = jnp.stack([row[:, None] * inv, col[:, None] * inv], axis=1)
    cos, sin = jnp.cos(ang), jnp.sin(ang)
    d = jnp.asarray(lane_of_dim)
    dd = jnp.maximum(d, 0)
    axis, member, freq = dd // (2 * quarter), (dd % (2 * quarter)) // quarter, dd % quarter
    rot = (d >= 0)[None, :]
    c_tab = jnp.where(rot, cos[:, axis, freq], 1.0)
    s_tab = jnp.where(rot, jnp.where(member == 0, -1.0, 1.0)[None, :] * sin[:, axis, freq], 0.0)
    ident_c = jnp.ones((TM, LANES), F32)
    ident_s = jnp.zeros((TM, LANES), F32)
    return jnp.concatenate([c_tab, ident_c], axis=0), jnp.concatenate([s_tab, ident_s], axis=0)


def _bproj_kernel(x_ref, mod_ref, w_ref, qg_ref, kg_ref, cos_ref, sin_ref, q_ref, k_ref, v_ref):
    hb = _ada(x_ref[...], mod_ref, 1).astype(BF)
    qkv = _dot(hb, w_ref[...])
    lo = lax.broadcasted_iota(jnp.int32, (TM, LANES), 1) < B_HEAD_DIM
    cos, sin = cos_ref[...], sin_ref[...]

    def norm_rope(t, gain):
        sq = t * t
        s_lo = jnp.sum(jnp.where(lo, sq, 0.0), axis=-1, keepdims=True)
        s_hi = jnp.sum(jnp.where(lo, 0.0, sq), axis=-1, keepdims=True)
        r = jnp.where(lo, lax.rsqrt(s_lo / B_HEAD_DIM + EPS), lax.rsqrt(s_hi / B_HEAD_DIM + EPS))
        y = t * r * gain
        return y * cos + _swap_pairs(y, B_HEAD_DIM // 4) * sin

    q_scale = B_HEAD_DIM ** -0.5
    nq = B_HEADS * B_HEAD_DIM
    nk = B_KV_HEADS * B_HEAD_DIM
    for j in range(nq // LANES):
        sl = slice(j * LANES, (j + 1) * LANES)
        q_ref[:, sl] = (norm_rope(qkv[:, sl], qg_ref[...]) * q_scale).astype(BF)
    for j in range(nk // LANES):
        sl = slice(j * LANES, (j + 1) * LANES)
        k_ref[:, sl] = norm_rope(qkv[:, nq + j * LANES:nq + (j + 1) * LANES], kg_ref[...])
    v_ref[...] = qkv[:, nq + nk:]


def _bproj(x, mod, w_qkv, q_gain, k_gain, cos, sin):
    nq = B_HEADS * B_HEAD_DIM
    nk = B_KV_HEADS * B_HEAD_DIM
    return pl.pallas_call(
        _bproj_kernel,
        grid=(N_TILES,),
        in_specs=[_tok_spec(D_MODEL), _MOD_SPEC, _const_spec((D_MODEL, nq + 2 * nk)),
                  _const_spec((1, LANES)), _const_spec((1, LANES)), _ROPE_SPEC, _ROPE_SPEC],
        out_specs=[_tok_spec(nq), _tok_spec(nk), _tok_spec(nk)],
        out_shape=[jax.ShapeDtypeStruct((N_TOK, nq), BF),
                   jax.ShapeDtypeStruct((N_TOK, nk), F32),
                   jax.ShapeDtypeStruct((N_TOK, nk), F32)],
        compiler_params=_params("arbitrary"),
        name="gqa_proj",
    )(x, mod, w_qkv.astype(BF), jnp.tile(q_gain, 2).reshape(1, LANES), jnp.tile(k_gain, 2).reshape(1, LANES),
      cos, sin)


def _gqa_attend(q, kcat, vcat, valid, sink_ref):
    nk = kcat.shape[0]
    lo = lax.broadcasted_iota(jnp.int32, (nk, LANES), 1) < B_HEAD_DIM
    lo_q = lax.broadcasted_iota(jnp.int32, (q.shape[0], LANES), 1) < B_HEAD_DIM
    group = B_HEADS // B_KV_HEADS
    outs = []
    for g in range(B_KV_HEADS):
        sl = slice((g // 2) * LANES, (g // 2 + 1) * LANES)
        own = lo if g % 2 == 0 else jnp.logical_not(lo)
        k_own = jnp.where(own, kcat[:, sl], 0.0)
        k_swp = pltpu.roll(k_own, B_HEAD_DIM, 1)
        v_own = vcat[:, sl]
        v_swp = pltpu.roll(v_own, B_HEAD_DIM, 1)
        k_half = (k_own, k_swp) if g % 2 == 0 else (k_swp, k_own)
        v_half = (v_own, v_swp) if g % 2 == 0 else (v_swp, v_own)
        k_half = tuple(t.astype(BF) for t in k_half)
        v_half = tuple(t.astype(BF) for t in v_half)
        for pair in range(group // 2):
            p_idx = g * (group // 2) + pair
            qp = q[:, p_idx * LANES:(p_idx + 1) * LANES]
            o_half = []
            for e in range(2):
                h = 2 * p_idx + e
                s = _dot_t(qp, k_half[e])
                if valid is not None:
                    s = jnp.where(valid, s, NEG_INF)
                sk = sink_ref[h]
                m = jnp.maximum(jnp.max(s, axis=-1, keepdims=True), sk)
                p = jnp.exp(s - m)
                den = jnp.sum(p, axis=-1, keepdims=True) + jnp.exp(sk - m)
                o_half.append(_dot(p.astype(BF), v_half[e]) / den)
            outs.append(jnp.where(lo_q, o_half[0], o_half[1]))
    return jnp.concatenate(outs, axis=1)


def _battn_lat_kernel(x_ref, mod_ref, q_ref, kp_ref, kc_ref, kn_ref, vp_ref, vc_ref, vn_ref,
                      ck_ref, cv_ref, sink_ref, wo_ref, o_ref):
    j = pl.program_id(1)
    kcat = jnp.concatenate([kp_ref[...], kc_ref[...], kn_ref[...], ck_ref[...]], axis=0)
    vcat = jnp.concatenate([vp_ref[...], vc_ref[...], vn_ref[...], cv_ref[...]], axis=0)
    nk = 3 * BQ + PAST_LEN
    qi = lax.broadcasted_iota(jnp.int32, (BQ, nk), 0)
    pk = lax.broadcasted_iota(jnp.int32, (BQ, nk), 1)
    kpos = j * BQ + pk - BQ
    band = (jnp.abs(pk - BQ - qi) <= B_WINDOW) & (kpos >= 0) & (kpos < DEC_SEQ)
    valid = band | (pk >= 3 * BQ)
    o = _gqa_attend(q_ref[...], kcat, vcat, valid, sink_ref)
    y = _dot(o.astype(BF), wo_ref[...])
    o_ref[...] = x_ref[...] + mod_ref[5:6, :] * y


def _battn_ctx_kernel(x_ref, mod_ref, q_ref, k_ref, v_ref, sink_ref, wo_ref, o_ref):
    o = _gqa_attend(q_ref[...], k_ref[...], v_ref[...], None, sink_ref)
    y = _dot(o.astype(BF), wo_ref[...])
    o_ref[...] = x_ref[...] + mod_ref[5:6, :] * y


def _battn(x, mod, q, k, v, cache_k, cache_v, sink, w_o):
    nq = B_HEADS * B_HEAD_DIM
    nk = B_KV_HEADS * B_HEAD_DIM
    nb = DEC_SEQ // BQ
    smem = pl.BlockSpec(memory_space=pltpu.SMEM)

    def kv_spec(off):
        return pl.BlockSpec((BQ, nk), lambda b, j: (b * nb + jnp.clip(j + off, 0, nb - 1), 0))

    x = pl.pallas_call(
        _battn_lat_kernel,
        grid=(DEC_BATCH, nb),
        in_specs=[pl.BlockSpec((BQ, D_MODEL), lambda b, j: (b * nb + j, 0)),
                  pl.BlockSpec((None, N_MOD, D_MODEL), lambda b, j: (b, 0, 0)),
                  pl.BlockSpec((BQ, nq), lambda b, j: (b * nb + j, 0)),
                  kv_spec(-1), kv_spec(0), kv_spec(1), kv_spec(-1), kv_spec(0), kv_spec(1),
                  pl.BlockSpec((None, PAST_LEN, nk), lambda b, j: (b, 0, 0)),
                  pl.BlockSpec((None, PAST_LEN, nk), lambda b, j: (b, 0, 0)),
                  smem, _const_spec((nq, D_MODEL))],
        out_specs=pl.BlockSpec((BQ, D_MODEL), lambda b, j: (b * nb + j, 0)),
        out_shape=jax.ShapeDtypeStruct((N_TOK, D_MODEL), F32),
        input_output_aliases={0: 0},
        compiler_params=_params("arbitrary", "arbitrary"),
        name="gqa_attn_latent",
    )(x, mod, q, k, k, k, v, v, v, cache_k, cache_v, sink, w_o)
    off = N_SAMPLE // SEQ
    return pl.pallas_call(
        _battn_ctx_kernel,
        grid=(BATCH,),
        in_specs=[pl.BlockSpec((SEQ, D_MODEL), lambda b: (off + b, 0)),
                  pl.BlockSpec((None, N_MOD, D_MODEL), lambda b: (DEC_BATCH, 0, 0)),
                  pl.BlockSpec((SEQ, nq), lambda b: (off + b, 0)),
                  pl.BlockSpec((SEQ, nk), lambda b: (off + b, 0)),
                  pl.BlockSpec((SEQ, nk), lambda b: (off + b, 0)),
                  smem, _const_spec((nq, D_MODEL))],
        out_specs=pl.BlockSpec((SEQ, D_MODEL), lambda b: (off + b, 0)),
        out_shape=jax.ShapeDtypeStruct((N_TOK, D_MODEL), F32),
        input_output_aliases={0: 0},
        compiler_params=_params("arbitrary"),
        name="gqa_attn_context",
    )(x, mod, q, k, v, sink, w_o)


def _mla_head_norm_rope(t, gain, cos, sin):
    r = lax.rsqrt(jnp.sum(t * t, axis=-1, keepdims=True) / (C_NOPE + C_ROPE) + EPS)
    y = t * r * gain
    if cos is None:
        return y
    return y * cos + _swap_pairs(y, C_ROPE // 4) * sin


def _mla_keys_values(c_kv_b, k_rope_tile, wuk_ref, wuv_ref, kg_ref, cos, sin, k_ref, v_ref):
    kn = _dot(c_kv_b, wuk_ref[...])
    v_ref[...] = _dot(c_kv_b, wuv_ref[...]).astype(BF)
    for h in range(C_HEADS):
        sl = slice(h * C_HEAD_PAD, (h + 1) * C_HEAD_PAD)
        k_ref[:, sl] = _mla_head_norm_rope(kn[:, sl] + k_rope_tile, kg_ref[...], cos, sin).astype(BF)


def _cproj_kernel(x_ref, mod_ref, wd_ref, cqg_ref, ckvg_ref, wuq_ref, wuk_ref, wuv_ref, qg_ref, kg_ref,
                  cos_ref, sin_ref, q_ref, k_ref, v_ref, ckv_ref, kr_ref):
    hb = _ada(x_ref[...], mod_ref, 1).astype(BF)
    d = _dot(hb, wd_ref[...])
    c_q = _rms(d[:, :C_Q_LORA]) * cqg_ref[...]
    c_kv = _rms(d[:, C_Q_LORA:C_Q_LORA + C_KV_LORA]) * ckvg_ref[...]
    k_rope = d[:, C_Q_LORA + C_KV_LORA:]
    ckv_ref[...] = c_kv
    kr_ref[...] = k_rope
    cos, sin = cos_ref[...], sin_ref[...]
    q = _dot(c_q.astype(BF), wuq_ref[...])
    q_scale = (C_NOPE + C_ROPE) ** -0.5
    for h in range(C_HEADS):
        sl = slice(h * C_HEAD_PAD, (h + 1) * C_HEAD_PAD)
        q_ref[:, sl] = (_mla_head_norm_rope(q[:, sl], qg_ref[...], cos, sin) * q_scale).astype(BF)
    _mla_keys_values(c_kv.astype(BF), k_rope, wuk_ref, wuv_ref, kg_ref, cos, sin, k_ref, v_ref)


def _cctx_kernel(ckv_ref, kr_ref, wuk_ref, wuv_ref, kg_ref, k_ref, v_ref):
    _mla_keys_values(ckv_ref[...].astype(BF), kr_ref[...], wuk_ref, wuv_ref, kg_ref, None, None, k_ref, v_ref)


def _mla_weights(w_down, w_uq, w_ukv, q_gain, k_gain):
    hd = C_NOPE + C_ROPE
    pad_lanes = C_HEAD_PAD - hd
    wd = jnp.concatenate([
        w_down[:, :C_Q_LORA + C_KV_LORA],
        jnp.zeros((D_MODEL, C_NOPE), F32), w_down[:, C_Q_LORA + C_KV_LORA:], jnp.zeros((D_MODEL, pad_lanes), F32),
    ], axis=1)
    wuq = jnp.pad(w_uq.reshape(C_Q_LORA, C_HEADS, hd), ((0, 0), (0, 0), (0, pad_lanes)))
    wukv = w_ukv.reshape(C_KV_LORA, C_HEADS, C_NOPE + C_VDIM)
    wuk = jnp.pad(wukv[:, :, :C_NOPE], ((0, 0), (0, 0), (0, C_HEAD_PAD - C_NOPE)))
    wuv = wukv[:, :, C_NOPE:]
    return (wd.astype(BF), wuq.reshape(C_Q_LORA, C_HEADS * C_HEAD_PAD).astype(BF),
            wuk.reshape(C_KV_LORA, C_HEADS * C_HEAD_PAD).astype(BF),
            wuv.reshape(C_KV_LORA, C_HEADS * C_VDIM).astype(BF),
            jnp.pad(q_gain, (0, pad_lanes)).reshape(1, C_HEAD_PAD),
            jnp.pad(k_gain, (0, pad_lanes)).reshape(1, C_HEAD_PAD))


def _cproj(x, mod, wd, cq_gain, ckv_gain, wuq, wuk, wuv, qg, kg, cos, sin):
    wq = C_HEADS * C_HEAD_PAD
    wv = C_HEADS * C_VDIM
    return pl.pallas_call(
        _cproj_kernel,
        grid=(N_TILES,),
        in_specs=[_tok_spec(D_MODEL), _MOD_SPEC, _const_spec((D_MODEL, C_DOWN_PAD)),
                  _const_spec((1, C_Q_LORA)), _const_spec((1, C_KV_LORA)),
                  _const_spec((C_Q_LORA, wq)), _const_spec((C_KV_LORA, wq)), _const_spec((C_KV_LORA, wv)),
                  _const_spec((1, C_HEAD_PAD)), _const_spec((1, C_HEAD_PAD)), _ROPE_SPEC, _ROPE_SPEC],
        out_specs=[_tok_spec(wq), _tok_spec(wq), _tok_spec(wv), _tok_spec(C_KV_LORA), _tok_spec(LANES)],
        out_shape=[jax.ShapeDtypeStruct((N_TOK, wq), BF), jax.ShapeDtypeStruct((N_TOK, wq), BF),
                   jax.ShapeDtypeStruct((N_TOK, wv), BF), jax.ShapeDtypeStruct((N_TOK, C_KV_LORA), F32),
                   jax.ShapeDtypeStruct((N_TOK, LANES), F32)],
        compiler_params=_params("arbitrary"),
        name="mla_proj",
    )(x, mod, wd, cq_gain.reshape(1, C_Q_LORA), ckv_gain.reshape(1, C_KV_LORA), wuq, wuk, wuv, qg, kg, cos, sin)


def _cctx(cache_ckv, cache_krope, wuk, wuv, kg):
    n = DEC_BATCH * PAST_LEN
    wq = C_HEADS * C_HEAD_PAD
    wv = C_HEADS * C_VDIM
    kr = jnp.pad(cache_krope.reshape(n, C_ROPE), ((0, 0), (C_NOPE, C_HEAD_PAD - C_NOPE - C_ROPE)))
    return pl.pallas_call(
        _cctx_kernel,
        grid=(n // TM,),
        in_specs=[_tok_spec(C_KV_LORA), _tok_spec(LANES), _const_spec((C_KV_LORA, wq)),
                  _const_spec((C_KV_LORA, wv)), _const_spec((1, C_HEAD_PAD))],
        out_specs=[_tok_spec(wq), _tok_spec(wv)],
        out_shape=[jax.ShapeDtypeStruct((n, wq), BF), jax.ShapeDtypeStruct((n, wv), BF)],
        compiler_params=_params("arbitrary"),
        name="mla_context_keys",
    )(cache_ckv.reshape(n, C_KV_LORA), kr, wuk, wuv, kg)


def _mla_attend(q_ref, kv_refs, o_ref):
    tq = q_ref.shape[0]
    lo = lax.broadcasted_iota(jnp.int32, (tq, LANES), 1) < C_VDIM
    o_half = []
    for e in range(2):
        sl = slice(e * C_HEAD_PAD, (e + 1) * C_HEAD_PAD)
        qh = q_ref[:, sl]
        scores = [_dot_t(qh, k_ref[:, sl]) for k_ref, _ in kv_refs]
        m = functools.reduce(jnp.maximum, [jnp.max(s, axis=-1, keepdims=True) for s in scores])
        probs = [jnp.exp(s - m) for s in scores]
        den = functools.reduce(jnp.add, [jnp.sum(p, axis=-1, keepdims=True) for p in probs])
        o = functools.reduce(jnp.add, [_dot(p.astype(BF), v_ref[...]) for p, (_, v_ref) in zip(probs, kv_refs)])
        o_half.append(o / den)
    o_ref[...] = jnp.where(lo, o_half[0], o_half[1]).astype(BF)


def _cattn_lat_kernel(q_ref, k_ref, v_ref, ck_ref, cv_ref, o_ref):
    _mla_attend(q_ref, ((k_ref, v_ref), (ck_ref, cv_ref)), o_ref)


def _cattn_ctx_kernel(q_ref, k_ref, v_ref, oin_ref, o_ref):
    del oin_ref
    _mla_attend(q_ref, ((k_ref, v_ref),), o_ref)


def _cattn(q, k, v, ck, cv):
    pair_w = 2 * C_HEAD_PAD
    n_pairs = C_HEADS // 2
    nqt = DEC_SEQ // CQ
    o = pl.pallas_call(
        _cattn_lat_kernel,
        grid=(DEC_BATCH, n_pairs, nqt),
        in_specs=[pl.BlockSpec((CQ, pair_w), lambda b, p, t: (b * nqt + t, p)),
                  pl.BlockSpec((DEC_SEQ, pair_w), lambda b, p, t: (b, p)),
                  pl.BlockSpec((DEC_SEQ, LANES), lambda b, p, t: (b, p)),
                  pl.BlockSpec((PAST_LEN, pair_w), lambda b, p, t: (b, p)),
                  pl.BlockSpec((PAST_LEN, LANES), lambda b, p, t: (b, p))],
        out_specs=pl.BlockSpec((CQ, LANES), lambda b, p, t: (b * nqt + t, p)),
        out_shape=jax.ShapeDtypeStruct((N_TOK, C_HEADS * C_VDIM), BF),
        compiler_params=_params("arbitrary", "arbitrary", "arbitrary"),
        name="mla_attn_latent",
    )(q, k, v, ck, cv)
    off = N_SAMPLE // SEQ
    return pl.pallas_call(
        _cattn_ctx_kernel,
        grid=(BATCH, n_pairs),
        in_specs=[pl.BlockSpec((SEQ, pair_w), lambda b, p: (off + b, p)),
                  pl.BlockSpec((SEQ, pair_w), lambda b, p: (off + b, p)),
                  pl.BlockSpec((SEQ, LANES), lambda b, p: (off + b, p)),
                  pl.BlockSpec(memory_space=pl.ANY)],
        out_specs=pl.BlockSpec((SEQ, LANES), lambda b, p: (off + b, p)),
        out_shape=jax.ShapeDtypeStruct((N_TOK, C_HEADS * C_VDIM), BF),
        input_output_aliases={3: 0},
        compiler_params=_params("arbitrary", "arbitrary"),
        name="mla_attn_context",
    )(q, k, v, o)


def _oproj_kernel(x_ref, mod_ref, o_ref, wo_ref, y_ref):
    y_ref[...] = x_ref[...] + mod_ref[5:6, :] * _dot(o_ref[...], wo_ref[...])


def _oproj(x, mod, o, w_o):
    width = o.shape[1]
    return pl.pallas_call(
        _oproj_kernel,
        grid=(N_TILES,),
        in_specs=[_tok_spec(D_MODEL), _MOD_SPEC, _tok_spec(width), _const_spec((width, D_MODEL))],
        out_specs=_tok_spec(D_MODEL),
        out_shape=jax.ShapeDtypeStruct((N_TOK, D_MODEL), F32),
        compiler_params=_params("arbitrary"),
        name="attn_out_proj",
    )(x, mod, o, w_o)


def kernel(x_prompt, x_sample, c, cache_win_k, cache_win_v, cache_mla_ckv, cache_mla_krope, c_ctx,
           ada_w, ada_b, ffn_w_in, ffn_w_out,
           gmlp_w_in, gmlp_v_gain, gmlp_w_s, gmlp_b_s, gmlp_w_out,
           win_w_qkv, win_q_gain, win_k_gain, win_sink, win_w_o,
           mla_w_down, mla_cq_gain, mla_ckv_gain, mla_w_uq, mla_w_ukv, mla_q_gain, mla_k_gain, mla_w_o):
    x = jnp.concatenate([x_sample.reshape(N_SAMPLE, D_MODEL), x_prompt.reshape(N_PROMPT, D_MODEL)], axis=0)
    cond = jnp.concatenate([c, c_ctx[None, :], jnp.zeros((N_COND - DEC_BATCH - 1, D_MODEL), F32)], axis=0)
    mods = _modulation(cond, ada_w, ada_b)
    w_in_b = ffn_w_in.astype(BF)
    w_out_b = ffn_w_out.astype(BF)

    lane = jnp.arange(LANES)
    b_cos, b_sin = _rope_tables(B_HEAD_DIM, lane % B_HEAD_DIM, None)
    c_lane = jnp.where((lane >= C_NOPE) & (lane < C_NOPE + C_ROPE), lane - C_NOPE, -1)
    c_cos, c_sin = _rope_tables(C_ROPE, c_lane, None)

    nk = B_KV_HEADS * B_HEAD_DIM
    win_k, win_v, mla_ckv, mla_krope = [], [], [], []
    ia = ib = ic = 0
    for l in range(DEPTH):
        mod = mods[l]
        x = _ffn(x, mod, w_in_b[l, 0], w_out_b[l, 0], 0)
        kind = l % N_MIXERS
        if kind == 0:
            x = _gmlp(x, mod, gmlp_w_in[ia], gmlp_v_gain[ia], gmlp_w_s[ia], gmlp_b_s[ia], gmlp_w_out[ia])
            ia += 1
        elif kind == 1:
            q, k, v = _bproj(x, mod, win_w_qkv[ib], win_q_gain[ib], win_k_gain[ib], b_cos, b_sin)
            x = _battn(x, mod, q, k, v,
                       cache_win_k[:, ib].reshape(DEC_BATCH, PAST_LEN, nk),
                       cache_win_v[:, ib].reshape(DEC_BATCH, PAST_LEN, nk),
                       win_sink[ib], win_w_o[ib].astype(BF))
            win_k.append(k[N_SAMPLE:].reshape(BATCH, SEQ, B_KV_HEADS, B_HEAD_DIM))
            win_v.append(v[N_SAMPLE:].reshape(BATCH, SEQ, B_KV_HEADS, B_HEAD_DIM))
            ib += 1
        else:
            wd, wuq, wuk, wuv, qg, kg = _mla_weights(mla_w_down[ic], mla_w_uq[ic], mla_w_ukv[ic],
                                                     mla_q_gain[ic], mla_k_gain[ic])
            q, k, v, ckv, kr = _cproj(x, mod, wd, mla_cq_gain[ic], mla_ckv_gain[ic], wuq, wuk, wuv, qg, kg,
                                      c_cos, c_sin)
            ck, cv = _cctx(cache_mla_ckv[:, ic], cache_mla_krope[:, ic], wuk, wuv, kg)
            o = _cattn(q, k, v, ck, cv)
            x = _oproj(x, mod, o, mla_w_o[ic].astype(BF))
            mla_ckv.append(ckv[N_SAMPLE:].reshape(BATCH, SEQ, C_KV_LORA))
            mla_krope.append(kr[N_SAMPLE:, C_NOPE:C_NOPE + C_ROPE].reshape(BATCH, SEQ, C_ROPE))
            ic += 1
        x = _ffn(x, mod, w_in_b[l, 1], w_out_b[l, 1], 2)
    return (x[N_SAMPLE:].reshape(BATCH, SEQ, D_MODEL), x[:N_SAMPLE].reshape(DEC_BATCH, DEC_SEQ, D_MODEL),
            jnp.stack(win_k, axis=1), jnp.stack(win_v, axis=1),
            jnp.stack(mla_ckv, axis=1), jnp.stack(mla_krope, axis=1))
```

```python
import functools
import math

import jax
import jax.numpy as jnp
from jax import lax
from jax.experimental import pallas as pl
from jax.experimental.pallas import tpu as pltpu

D_MODEL = 1024
BATCH = 16
SEQ = 256
DEPTH = 4
DEC_BATCH = 8
DEC_SEQ = 4096
PAST_LEN = 256
GRID_W = 64
N_MIXERS = 3
N_MOD = 9
D_FF = 2816
A_WIDTH = D_MODEL
A_GROUPS = 8
A_CHUNK = 128
B_HEADS = 16
B_KV_HEADS = 4
B_HEAD_DIM = 64
B_WINDOW = 128
C_HEADS = 16
C_NOPE = 64
C_ROPE = 32
C_VDIM = 64
C_Q_LORA = 512
C_KV_LORA = 256
ROPE_BASE = 10000.0
EPS = 1e-6
NEG_INF = -1e30

LANES = 128
N_SAMPLE = DEC_BATCH * DEC_SEQ
N_PROMPT = BATCH * SEQ
N_TOK = N_SAMPLE + N_PROMPT
N_COND = 16
TM = 512
N_TILES = N_TOK // TM
N_SAMPLE_TILES = N_SAMPLE // TM
TILES_PER_SEQ = DEC_SEQ // TM
MOD_TN = 1536
BQ = 256
CQ = 512
CK = 512
C_HEAD_PAD = 128
C_DOWN_PAD = C_Q_LORA + C_KV_LORA + 2 * LANES
VMEM_LIMIT_BYTES = 56 * 1024 * 1024

LOG2E = math.log2(math.e)

BF = jnp.bfloat16
F32 = jnp.float32


def _params(*sem):
    return pltpu.CompilerParams(dimension_semantics=sem, vmem_limit_bytes=VMEM_LIMIT_BYTES)


def _dot(a, b):
    return jnp.dot(a, b, preferred_element_type=F32)


def _dot_t(a, b):
    return lax.dot_general(a, b, (((1,), (1,)), ((), ())), preferred_element_type=F32)


def _rms(x):
    return x * lax.rsqrt(jnp.mean(x * x, axis=-1, keepdims=True) + EPS)


def _ada(x, mod_ref, k):
    shift = mod_ref[3 * k:3 * k + 1, :]
    scale = mod_ref[3 * k + 1:3 * k + 2, :]
    return _rms(x) * (1.0 + scale) + shift


def _const_spec(shape):
    nd = len(shape)
    return pl.BlockSpec(shape, lambda *_: (0,) * nd, pipeline_mode=pl.Buffered(1))


def _tile_row(i):
    return jnp.minimum(i * TM // DEC_SEQ, DEC_BATCH)


def _tok_spec(width):
    return pl.BlockSpec((TM, width), lambda i: (i, 0))


_MOD_SPEC = pl.BlockSpec((None, N_MOD, D_MODEL), lambda i: (_tile_row(i), 0, 0))


def _rope_tile(i):
    return jnp.where(i < N_SAMPLE_TILES, i % TILES_PER_SEQ, TILES_PER_SEQ)


_ROPE_SPEC = pl.BlockSpec((TM, LANES), lambda i: (_rope_tile(i), 0))


def _mod_kernel(c_ref, w_ref, b_ref, o_ref):
    a = jax.nn.silu(c_ref[...]).astype(BF)
    o_ref[...] = _dot(a, w_ref[...].astype(BF)) + b_ref[...]


def _modulation(cond, ada_w, ada_b):
    n_out = N_MOD * D_MODEL
    out = pl.pallas_call(
        _mod_kernel,
        grid=(DEPTH, n_out // MOD_TN),
        in_specs=[
            pl.BlockSpec((N_COND, D_MODEL), lambda l, j: (0, 0)),
            pl.BlockSpec((None, D_MODEL, MOD_TN), lambda l, j: (l, 0, j)),
            pl.BlockSpec((None, 1, MOD_TN), lambda l, j: (l, 0, j)),
        ],
        out_specs=pl.BlockSpec((None, N_COND, MOD_TN), lambda l, j: (l, 0, j)),
        out_shape=jax.ShapeDtypeStruct((DEPTH, N_COND, n_out), F32),
        compiler_params=_params("arbitrary", "arbitrary"),
        name="modulation",
    )(cond, ada_w, ada_b.reshape(DEPTH, 1, n_out))
    return out.reshape(DEPTH, N_COND, N_MOD, D_MODEL)


def _ffn_kernel(x_ref, mod_ref, win_ref, wout_ref, o_ref, *, k):
    x = x_ref[...]
    hb = _ada(x, mod_ref, k).astype(BF)
    gu = _dot(hb, win_ref[...])
    act = (jax.nn.silu(gu[:, :D_FF]) * gu[:, D_FF:]).astype(BF)
    y = _dot(act, wout_ref[...])
    gate = mod_ref[3 * k + 2:3 * k + 3, :]
    o_ref[...] = x + (0.5 * gate) * y


def _ffn(x, mod, w_in, w_out, k):
    return pl.pallas_call(
        functools.partial(_ffn_kernel, k=k),
        grid=(N_TILES,),
        in_specs=[_tok_spec(D_MODEL), _MOD_SPEC, _const_spec((D_MODEL, 2 * D_FF)), _const_spec((D_FF, D_MODEL))],
        out_specs=_tok_spec(D_MODEL),
        out_shape=jax.ShapeDtypeStruct((N_TOK, D_MODEL), F32),
        compiler_params=_params("arbitrary"),
        name="ffn",
    )(x, mod, w_in, w_out)


def _gmlp_kernel(x_ref, mod_ref, win_ref, vg_ref, ws_ref, bs_ref, wout_ref, o_ref):
    x = x_ref[...]
    hb = _ada(x, mod_ref, 1).astype(BF)
    pre = _dot(hb, win_ref[...])
    uv = 0.5 * pre * (1.0 + lax.erf(pre * math.sqrt(0.5)))
    u = uv[:, :A_WIDTH]
    v = (_rms(uv[:, A_WIDTH:]) * vg_ref[...]).astype(BF)
    bias = bs_ref[...]
    rows = []
    for c in range(TM // A_CHUNK):
        cols = [_dot(ws_ref[g], v[c * A_CHUNK:(c + 1) * A_CHUNK, g * LANES:(g + 1) * LANES])
                for g in range(A_GROUPS)]
        rows.append(jnp.concatenate(cols, axis=1) + bias)
    sv = jnp.concatenate(rows, axis=0)
    y = _dot((u * sv).astype(BF), wout_ref[...])
    o_ref[...] = x + mod_ref[5:6, :] * y


def _gmlp(x, mod, w_in, v_gain, w_s, b_s, w_out):
    bias = jnp.repeat(b_s.T, A_WIDTH // A_GROUPS, axis=1)
    return pl.pallas_call(
        _gmlp_kernel,
        grid=(N_TILES,),
        in_specs=[_tok_spec(D_MODEL), _MOD_SPEC,
                  _const_spec((D_MODEL, 2 * A_WIDTH)), _const_spec((1, A_WIDTH)),
                  _const_spec((A_GROUPS, A_CHUNK, A_CHUNK)), _const_spec((A_CHUNK, A_WIDTH)),
                  _const_spec((A_WIDTH, D_MODEL))],
        out_specs=_tok_spec(D_MODEL),
        out_shape=jax.ShapeDtypeStruct((N_TOK, D_MODEL), F32),
        compiler_params=_params("arbitrary"),
        name="gmlp",
    )(x, mod, w_in.astype(BF), v_gain.reshape(1, A_WIDTH), w_s.astype(BF), bias, w_out.astype(BF))


def _swap_pairs(y, step):
    lane = lax.broadcasted_iota(jnp.int32, y.shape, 1)
    return jnp.where((lane & step) != 0, pltpu.roll(y, step, 1), pltpu.roll(y, LANES - step, 1))


def _rope_tables(rot_dim, lane_of_dim):
    quarter = rot_dim // 4
    inv = ROPE_BASE ** (-jnp.arange(quarter, dtype=F32) / quarter)
    t = jnp.arange(DEC_SEQ)
    row = (t // GRID_W).astype(F32)
    col = (t % GRID_W).astype(F32)
    ang = jnp.stack([row[:, None] * inv, col[:, None] * inv], axis=1)
    cos, sin = jnp.cos(ang), jnp.sin(ang)
    d = jnp.asarray(lane_of_dim)
    dd = jnp.maximum(d, 0)
    axis, member, freq = dd // (2 * quarter), (dd % (2 * quarter)) // quarter, dd % quarter
    rot = (d >= 0)[None, :]
    c_tab = jnp.where(rot, cos[:, axis, freq], 1.0)
    s_tab = jnp.where(rot, jnp.where(member == 0, -1.0, 1.0)[None, :] * sin[:, axis, freq], 0.0)
    ident_c = jnp.ones((TM, LANES), F32)
    ident_s = jnp.zeros((TM, LANES), F32)
    return jnp.concatenate([c_tab, ident_c], axis=0), jnp.concatenate([s_tab, ident_s], axis=0)


def _bproj_kernel(x_ref, mod_ref, w_ref, qg_ref, kg_ref, cos_ref, sin_ref, q_ref, k_ref, v_ref):
    hb = _ada(x_ref[...], mod_ref, 1).astype(BF)
    qkv = _dot(hb, w_ref[...])
    lo = lax.broadcasted_iota(jnp.int32, (TM, LANES), 1) < B_HEAD_DIM
    cos, sin = cos_ref[...], sin_ref[...]

    def norm_rope(t, gain):
        sq = t * t
        s_lo = jnp.sum(jnp.where(lo, sq, 0.0), axis=-1, keepdims=True)
        s_hi = jnp.sum(jnp.where(lo, 0.0, sq), axis=-1, keepdims=True)
        r = jnp.where(lo, lax.rsqrt(s_lo / B_HEAD_DIM + EPS), lax.rsqrt(s_hi / B_HEAD_DIM + EPS))
        y = t * r * gain
        return y * cos + _swap_pairs(y, B_HEAD_DIM // 4) * sin

    q_scale = B_HEAD_DIM ** -0.5 * LOG2E
    nq = B_HEADS * B_HEAD_DIM
    nk = B_KV_HEADS * B_HEAD_DIM
    for j in range(nq // LANES):
        sl = slice(j * LANES, (j + 1) * LANES)
        q_ref[:, sl] = (norm_rope(qkv[:, sl], qg_ref[...]) * q_scale).astype(BF)
    for j in range(nk // LANES):
        sl = slice(j * LANES, (j + 1) * LANES)
        k_ref[:, sl] = norm_rope(qkv[:, nq + j * LANES:nq + (j + 1) * LANES], kg_ref[...])
    v_ref[...] = qkv[:, nq + nk:]


def _bproj(x, mod, w_qkv, q_gain, k_gain, cos, sin):
    nq = B_HEADS * B_HEAD_DIM
    nk = B_KV_HEADS * B_HEAD_DIM
    return pl.pallas_call(
        _bproj_kernel,
        grid=(N_TILES,),
        in_specs=[_tok_spec(D_MODEL), _MOD_SPEC, _const_spec((D_MODEL, nq + 2 * nk)),
                  _const_spec((1, LANES)), _const_spec((1, LANES)), _ROPE_SPEC, _ROPE_SPEC],
        out_specs=[_tok_spec(nq), _tok_spec(nk), _tok_spec(nk)],
        out_shape=[jax.ShapeDtypeStruct((N_TOK, nq), BF),
                   jax.ShapeDtypeStruct((N_TOK, nk), F32),
                   jax.ShapeDtypeStruct((N_TOK, nk), F32)],
        compiler_params=_params("arbitrary"),
        name="gqa_proj",
    )(x, mod, w_qkv.astype(BF), jnp.tile(q_gain, 2).reshape(1, LANES), jnp.tile(k_gain, 2).reshape(1, LANES),
      cos, sin)


def _gqa_attend(q, kcat, vcat, bias, sink_ref):
    nk = kcat.shape[0]
    lo = lax.broadcasted_iota(jnp.int32, (nk, LANES), 1) < B_HEAD_DIM
    lo_q = lax.broadcasted_iota(jnp.int32, (q.shape[0], LANES), 1) < B_HEAD_DIM
    group = B_HEADS // B_KV_HEADS
    outs = []
    for g in range(B_KV_HEADS):
        sl = slice((g // 2) * LANES, (g // 2 + 1) * LANES)
        own = lo if g % 2 == 0 else jnp.logical_not(lo)
        k_own = jnp.where(own, kcat[:, sl], 0.0)
        k_swp = pltpu.roll(k_own, B_HEAD_DIM, 1)
        v_own = jnp.where(own, vcat[:, sl], 1.0)
        v_swp = pltpu.roll(v_own, B_HEAD_DIM, 1)
        k_half = (k_own, k_swp) if g % 2 == 0 else (k_swp, k_own)
        v_half = (v_own, v_swp) if g % 2 == 0 else (v_swp, v_own)
        k_half = tuple(t.astype(BF) for t in k_half)
        v_half = tuple(t.astype(BF) for t in v_half)
        for pair in range(group // 2):
            p_idx = g * (group // 2) + pair
            qp = q[:, p_idx * LANES:(p_idx + 1) * LANES]
            o_half = []
            for e in range(2):
                s = _dot_t(qp, k_half[e])
                if bias is not None:
                    nb = bias.shape[1]
                    s = jnp.concatenate([s[:, :nb] + bias, s[:, nb:]], axis=1)
                sk = sink_ref[2 * p_idx + e] * LOG2E
                m = jnp.maximum(jnp.max(s, axis=-1, keepdims=True), sk)
                ov = _dot(jnp.exp2(s - m).astype(BF), v_half[e])
                o_half.append(ov / (pltpu.roll(ov, B_HEAD_DIM, 1) + jnp.exp2(sk - m)))
            outs.append(jnp.where(lo_q, o_half[0], o_half[1]))
    return jnp.concatenate(outs, axis=1)


def _battn_lat_kernel(x_ref, mod_ref, q_ref, kp_ref, kc_ref, kn_ref, vp_ref, vc_ref, vn_ref,
                      ck_ref, cv_ref, sink_ref, wo_ref, o_ref):
    j = pl.program_id(1)
    kcat = jnp.concatenate([kp_ref[...], kc_ref[...], kn_ref[...], ck_ref[...]], axis=0)
    vcat = jnp.concatenate([vp_ref[...], vc_ref[...], vn_ref[...], cv_ref[...]], axis=0)
    n_lat = BQ + 2 * B_WINDOW
    qi = lax.broadcasted_iota(jnp.int32, (BQ, n_lat), 0)
    pk = lax.broadcasted_iota(jnp.int32, (BQ, n_lat), 1)
    kpos = j * BQ + pk - B_WINDOW
    valid = (jnp.abs(pk - B_WINDOW - qi) <= B_WINDOW) & (kpos >= 0) & (kpos < DEC_SEQ)
    bias = jnp.where(valid, 0.0, NEG_INF)
    o = _gqa_attend(q_ref[...], kcat, vcat, bias, sink_ref)
    y = _dot(o.astype(BF), wo_ref[...])
    o_ref[...] = x_ref[...] + mod_ref[5:6, :] * y


def _battn_ctx_kernel(x_ref, mod_ref, q_ref, k_ref, v_ref, sink_ref, wo_ref, o_ref):
    o = _gqa_attend(q_ref[...], k_ref[...], v_ref[...], None, sink_ref)
    y = _dot(o.astype(BF), wo_ref[...])
    o_ref[...] = x_ref[...] + mod_ref[5:6, :] * y


def _battn(x, mod, q, k, v, cache_k, cache_v, sink, w_o):
    nq = B_HEADS * B_HEAD_DIM
    nk = B_KV_HEADS * B_HEAD_DIM
    nb = DEC_SEQ // BQ
    nw = DEC_SEQ // B_WINDOW
    per = BQ // B_WINDOW
    smem = pl.BlockSpec(memory_space=pltpu.SMEM)
    cur_spec = pl.BlockSpec((BQ, nk), lambda b, j: (b * nb + j, 0))
    prev_spec = pl.BlockSpec((B_WINDOW, nk), lambda b, j: (b * nw + jnp.maximum(per * j - 1, 0), 0))
    next_spec = pl.BlockSpec((B_WINDOW, nk), lambda b, j: (b * nw + jnp.minimum(per * j + per, nw - 1), 0))

    x = pl.pallas_call(
        _battn_lat_kernel,
        grid=(DEC_BATCH, nb),
        in_specs=[pl.BlockSpec((BQ, D_MODEL), lambda b, j: (b * nb + j, 0)),
                  pl.BlockSpec((None, N_MOD, D_MODEL), lambda b, j: (b, 0, 0)),
                  pl.BlockSpec((BQ, nq), lambda b, j: (b * nb + j, 0)),
                  prev_spec, cur_spec, next_spec, prev_spec, cur_spec, next_spec,
                  pl.BlockSpec((None, PAST_LEN, nk), lambda b, j: (b, 0, 0)),
                  pl.BlockSpec((None, PAST_LEN, nk), lambda b, j: (b, 0, 0)),
                  smem, _const_spec((nq, D_MODEL))],
        out_specs=pl.BlockSpec((BQ, D_MODEL), lambda b, j: (b * nb + j, 0)),
        out_shape=jax.ShapeDtypeStruct((N_TOK, D_MODEL), F32),
        input_output_aliases={0: 0},
        compiler_params=_params("arbitrary", "arbitrary"),
        name="gqa_attn_latent",
    )(x, mod, q, k, k, k, v, v, v, cache_k, cache_v, sink, w_o)
    off = N_SAMPLE // SEQ
    return pl.pallas_call(
        _battn_ctx_kernel,
        grid=(BATCH,),
        in_specs=[pl.BlockSpec((SEQ, D_MODEL), lambda b: (off + b, 0)),
                  pl.BlockSpec((None, N_MOD, D_MODEL), lambda b: (DEC_BATCH, 0, 0)),
                  pl.BlockSpec((SEQ, nq), lambda b: (off + b, 0)),
                  pl.BlockSpec((SEQ, nk), lambda b: (off + b, 0)),
                  pl.BlockSpec((SEQ, nk), lambda b: (off + b, 0)),
                  smem, _const_spec((nq, D_MODEL))],
        out_specs=pl.BlockSpec((SEQ, D_MODEL), lambda b: (off + b, 0)),
        out_shape=jax.ShapeDtypeStruct((N_TOK, D_MODEL), F32),
        input_output_aliases={0: 0},
        compiler_params=_params("arbitrary"),
        name="gqa_attn_context",
    )(x, mod, q, k, v, sink, w_o)


def _mla_head_norm_rope(t, t_swap, gain, gain_swap, cos, sin):
    r = lax.rsqrt(jnp.sum(t * t, axis=-1, keepdims=True) / (C_NOPE + C_ROPE) + EPS)
    y = t * r * gain
    if cos is None:
        return y
    return y * cos + (t_swap * r * gain_swap) * sin


def _mla_keys_values(c_kv_b, k_rope, k_rope_swap, wuk_ref, wuv_ref, kg_ref, kgs_ref, cos, sin, k_ref, v_ref):
    kn = _dot(c_kv_b, wuk_ref[...])
    v = _dot(c_kv_b, wuv_ref[...])
    upper = (lax.broadcasted_iota(jnp.int32, v.shape, 1) & C_VDIM) != 0
    v_ref[...] = jnp.where(upper, 1.0, v).astype(BF)
    gain_swap = None if kgs_ref is None else kgs_ref[...]
    for h in range(C_HEADS):
        sl = slice(h * C_HEAD_PAD, (h + 1) * C_HEAD_PAD)
        k_ref[:, sl] = _mla_head_norm_rope(kn[:, sl] + k_rope, k_rope_swap, kg_ref[...], gain_swap,
                                           cos, sin).astype(BF)


def _cproj_kernel(x_ref, mod_ref, wd_ref, cqg_ref, ckvg_ref, wuq_ref, wuk_ref, wuv_ref,
                  qg_ref, qgs_ref, kg_ref, kgs_ref, cos_ref, sin_ref, q_ref, k_ref, v_ref, ckv_ref, kr_ref):
    hb = _ada(x_ref[...], mod_ref, 1).astype(BF)
    d = _dot(hb, wd_ref[...])
    c_q = _rms(d[:, :C_Q_LORA]) * cqg_ref[...]
    c_kv = _rms(d[:, C_Q_LORA:C_Q_LORA + C_KV_LORA]) * ckvg_ref[...]
    k_rope = d[:, C_Q_LORA + C_KV_LORA:C_Q_LORA + C_KV_LORA + LANES]
    k_rope_swap = d[:, C_Q_LORA + C_KV_LORA + LANES:]
    ckv_ref[...] = c_kv
    kr_ref[...] = k_rope
    cos, sin = cos_ref[...], sin_ref[...]
    q2 = _dot(c_q.astype(BF), wuq_ref[...])
    wq = C_HEADS * C_HEAD_PAD
    q_scale = (C_NOPE + C_ROPE) ** -0.5 * LOG2E
    for h in range(C_HEADS):
        sl = slice(h * C_HEAD_PAD, (h + 1) * C_HEAD_PAD)
        sl_swap = slice(wq + h * C_HEAD_PAD, wq + (h + 1) * C_HEAD_PAD)
        qh = _mla_head_norm_rope(q2[:, sl], q2[:, sl_swap], qg_ref[...], qgs_ref[...], cos, sin)
        q_ref[:, sl] = (qh * q_scale).astype(BF)
    _mla_keys_values(c_kv.astype(BF), k_rope, k_rope_swap, wuk_ref, wuv_ref, kg_ref, kgs_ref, cos, sin,
                     k_ref, v_ref)


def _cctx_kernel(ckv_ref, kr_ref, wuk_ref, wuv_ref, kg_ref, k_ref, v_ref):
    _mla_keys_values(ckv_ref[...].astype(BF), kr_ref[...], None, wuk_ref, wuv_ref, kg_ref, None, None, None,
                     k_ref, v_ref)


def _mla_weights(w_down, w_uq, w_ukv, q_gain, k_gain):
    hd = C_NOPE + C_ROPE
    pad_lanes = C_HEAD_PAD - hd
    lane = jnp.arange(C_HEAD_PAD)
    is_rope = (lane >= C_NOPE) & (lane < hd)
    partner = jnp.where(is_rope, lane ^ (C_ROPE // 4), lane)

    def swapped(t):
        return jnp.where(is_rope, jnp.take(t, partner, axis=-1), 0.0)

    kr_cols = jnp.pad(w_down[:, C_Q_LORA + C_KV_LORA:], ((0, 0), (C_NOPE, pad_lanes)))
    wd = jnp.concatenate([w_down[:, :C_Q_LORA + C_KV_LORA], kr_cols, swapped(kr_cols)], axis=1)
    wuq = jnp.pad(w_uq.reshape(C_Q_LORA, C_HEADS, hd), ((0, 0), (0, 0), (0, pad_lanes)))
    wq = C_HEADS * C_HEAD_PAD
    wuq = jnp.concatenate([wuq.reshape(C_Q_LORA, wq), swapped(wuq).reshape(C_Q_LORA, wq)], axis=1)
    wukv = w_ukv.reshape(C_KV_LORA, C_HEADS, C_NOPE + C_VDIM)
    wuk = jnp.pad(wukv[:, :, :C_NOPE], ((0, 0), (0, 0), (0, C_HEAD_PAD - C_NOPE)))
    wuv = jnp.pad(wukv[:, :, C_NOPE:], ((0, 0), (0, 0), (0, C_HEAD_PAD - C_VDIM)))
    qg = jnp.pad(q_gain, (0, pad_lanes))
    kg = jnp.pad(k_gain, (0, pad_lanes))
    row = lambda t: t.reshape(1, C_HEAD_PAD)
    return (wd.astype(BF), wuq.astype(BF), wuk.reshape(C_KV_LORA, wq).astype(BF),
            wuv.reshape(C_KV_LORA, wq).astype(BF), row(qg), row(swapped(qg)), row(kg), row(swapped(kg)))


def _cproj(x, mod, wd, cq_gain, ckv_gain, wuq, wuk, wuv, qg, qgs, kg, kgs, cos, sin):
    wq = C_HEADS * C_HEAD_PAD
    gain_spec = _const_spec((1, C_HEAD_PAD))
    return pl.pallas_call(
        _cproj_kernel,
        grid=(N_TILES,),
        in_specs=[_tok_spec(D_MODEL), _MOD_SPEC, _const_spec((D_MODEL, C_DOWN_PAD)),
                  _const_spec((1, C_Q_LORA)), _const_spec((1, C_KV_LORA)),
                  _const_spec((C_Q_LORA, 2 * wq)), _const_spec((C_KV_LORA, wq)), _const_spec((C_KV_LORA, wq)),
                  gain_spec, gain_spec, gain_spec, gain_spec, _ROPE_SPEC, _ROPE_SPEC],
        out_specs=[_tok_spec(wq), _tok_spec(wq), _tok_spec(wq), _tok_spec(C_KV_LORA), _tok_spec(LANES)],
        out_shape=[jax.ShapeDtypeStruct((N_TOK, wq), BF), jax.ShapeDtypeStruct((N_TOK, wq), BF),
                   jax.ShapeDtypeStruct((N_TOK, wq), BF), jax.ShapeDtypeStruct((N_TOK, C_KV_LORA), F32),
                   jax.ShapeDtypeStruct((N_TOK, LANES), F32)],
        compiler_params=_params("arbitrary"),
        name="mla_proj",
    )(x, mod, wd, cq_gain.reshape(1, C_Q_LORA), ckv_gain.reshape(1, C_KV_LORA), wuq, wuk, wuv,
      qg, qgs, kg, kgs, cos, sin)


def _cctx(cache_ckv, cache_krope, wuk, wuv, kg):
    n = DEC_BATCH * PAST_LEN
    wq = C_HEADS * C_HEAD_PAD
    kr = jnp.pad(cache_krope.reshape(n, C_ROPE), ((0, 0), (C_NOPE, C_HEAD_PAD - C_NOPE - C_ROPE)))
    return pl.pallas_call(
        _cctx_kernel,
        grid=(n // TM,),
        in_specs=[_tok_spec(C_KV_LORA), _tok_spec(LANES), _const_spec((C_KV_LORA, wq)),
                  _const_spec((C_KV_LORA, wq)), _const_spec((1, C_HEAD_PAD))],
        out_specs=[_tok_spec(wq), _tok_spec(wq)],
        out_shape=[jax.ShapeDtypeStruct((n, wq), BF), jax.ShapeDtypeStruct((n, wq), BF)],
        compiler_params=_params("arbitrary"),
        name="mla_context_keys",
    )(cache_ckv.reshape(n, C_KV_LORA), kr, wuk, wuv, kg)


def _mla_attend(q_ref, kv_refs, o_ref):
    tq = q_ref.shape[0]
    lo = lax.broadcasted_iota(jnp.int32, (tq, LANES), 1) < C_VDIM
    acc_sum = []
    for e in range(2):
        sl = slice(e * C_HEAD_PAD, (e + 1) * C_HEAD_PAD)
        qh = q_ref[:, sl]
        m = acc = None
        for k_ref, v_ref in kv_refs:
            nk = k_ref.shape[0]
            ck = min(CK, nk)
            for c in range(nk // ck):
                rows = slice(c * ck, (c + 1) * ck)
                s = _dot_t(qh, k_ref[rows, sl])
                cm = jnp.max(s, axis=-1, keepdims=True)
                m_new = cm if m is None else jnp.maximum(m, cm)
                pv = _dot(jnp.exp2(s - m_new).astype(BF), v_ref[rows, sl])
                acc = pv if acc is None else jnp.exp2(m - m_new) * acc + pv
                m = m_new
        acc_sum.append(acc)
    r0 = pltpu.roll(acc_sum[0], C_VDIM, 1)
    r1 = pltpu.roll(acc_sum[1], C_VDIM, 1)
    o_ref[...] = jnp.where(lo, acc_sum[0] / r0, r1 / acc_sum[1]).astype(BF)


def _cattn_lat_kernel(q_ref, k_ref, v_ref, ck_ref, cv_ref, o_ref):
    _mla_attend(q_ref, ((k_ref, v_ref), (ck_ref, cv_ref)), o_ref)


def _cattn_ctx_kernel(q_ref, k_ref, v_ref, oin_ref, o_ref):
    del oin_ref
    _mla_attend(q_ref, ((k_ref, v_ref),), o_ref)


def _cattn(q, k, v, ck, cv):
    pair_w = 2 * C_HEAD_PAD
    n_pairs = C_HEADS // 2
    nqt = DEC_SEQ // CQ
    o = pl.pallas_call(
        _cattn_lat_kernel,
        grid=(DEC_BATCH, n_pairs, nqt),
        in_specs=[pl.BlockSpec((CQ, pair_w), lambda b, p, t: (b * nqt + t, p)),
                  pl.BlockSpec((DEC_SEQ, pair_w), lambda b, p, t: (b, p)),
                  pl.BlockSpec((DEC_SEQ, pair_w), lambda b, p, t: (b, p)),
                  pl.BlockSpec((PAST_LEN, pair_w), lambda b, p, t: (b, p)),
                  pl.BlockSpec((PAST_LEN, pair_w), lambda b, p, t: (b, p))],
        out_specs=pl.BlockSpec((CQ, LANES), lambda b, p, t: (b * nqt + t, p)),
        out_shape=jax.ShapeDtypeStruct((N_TOK, C_HEADS * C_VDIM), BF),
        compiler_params=_params("arbitrary", "arbitrary", "arbitrary"),
        name="mla_attn_latent",
    )(q, k, v, ck, cv)
    off = N_SAMPLE // SEQ
    return pl.pallas_call(
        _cattn_ctx_kernel,
        grid=(BATCH, n_pairs),
        in_specs=[pl.BlockSpec((SEQ, pair_w), lambda b, p: (off + b, p)),
                  pl.BlockSpec((SEQ, pair_w), lambda b, p: (off + b, p)),
                  pl.BlockSpec((SEQ, pair_w), lambda b, p: (off + b, p)),
                  pl.BlockSpec(memory_space=pl.ANY)],
        out_specs=pl.BlockSpec((SEQ, LANES), lambda b, p: (off + b, p)),
        out_shape=jax.ShapeDtypeStruct((N_TOK, C_HEADS * C_VDIM), BF),
        input_output_aliases={3: 0},
        compiler_params=_params("arbitrary", "arbitrary"),
        name="mla_attn_context",
    )(q, k, v, o)


def _oproj_kernel(x_ref, mod_ref, o_ref, wo_ref, y_ref):
    y_ref[...] = x_ref[...] + mod_ref[5:6, :] * _dot(o_ref[...], wo_ref[...])


def _oproj(x, mod, o, w_o):
    width = o.shape[1]
    return pl.pallas_call(
        _oproj_kernel,
        grid=(N_TILES,),
        in_specs=[_tok_spec(D_MODEL), _MOD_SPEC, _tok_spec(width), _const_spec((width, D_MODEL))],
        out_specs=_tok_spec(D_MODEL),
        out_shape=jax.ShapeDtypeStruct((N_TOK, D_MODEL), F32),
        compiler_params=_params("arbitrary"),
        name="attn_out_proj",
    )(x, mod, o, w_o)


def kernel(x_prompt, x_sample, c, cache_win_k, cache_win_v, cache_mla_ckv, cache_mla_krope, c_ctx,
           ada_w, ada_b, ffn_w_in, ffn_w_out,
           gmlp_w_in, gmlp_v_gain, gmlp_w_s, gmlp_b_s, gmlp_w_out,
           win_w_qkv, win_q_gain, win_k_gain, win_sink, win_w_o,
           mla_w_down, mla_cq_gain, mla_ckv_gain, mla_w_uq, mla_w_ukv, mla_q_gain, mla_k_gain, mla_w_o):
    x = jnp.concatenate([x_sample.reshape(N_SAMPLE, D_MODEL), x_prompt.reshape(N_PROMPT, D_MODEL)], axis=0)
    cond = jnp.concatenate([c, c_ctx[None, :], jnp.zeros((N_COND - DEC_BATCH - 1, D_MODEL), F32)], axis=0)
    mods = _modulation(cond, ada_w, ada_b)
    w_in_b = ffn_w_in.astype(BF)
    w_out_b = ffn_w_out.astype(BF)

    lane = jnp.arange(LANES)
    b_cos, b_sin = _rope_tables(B_HEAD_DIM, lane % B_HEAD_DIM)
    c_lane = jnp.where((lane >= C_NOPE) & (lane < C_NOPE + C_ROPE), lane - C_NOPE, -1)
    c_cos, c_sin = _rope_tables(C_ROPE, c_lane)

    nk = B_KV_HEADS * B_HEAD_DIM
    win_k, win_v, mla_ckv, mla_krope = [], [], [], []
    ia = ib = ic = 0
    for l in range(DEPTH):
        mod = mods[l]
        x = _ffn(x, mod, w_in_b[l, 0], w_out_b[l, 0], 0)
        kind = l % N_MIXERS
        if kind == 0:
            x = _gmlp(x, mod, gmlp_w_in[ia], gmlp_v_gain[ia], gmlp_w_s[ia], gmlp_b_s[ia], gmlp_w_out[ia])
            ia += 1
        elif kind == 1:
            q, k, v = _bproj(x, mod, win_w_qkv[ib], win_q_gain[ib], win_k_gain[ib], b_cos, b_sin)
            x = _battn(x, mod, q, k, v,
                       cache_win_k[:, ib].reshape(DEC_BATCH, PAST_LEN, nk),
                       cache_win_v[:, ib].reshape(DEC_BATCH, PAST_LEN, nk),
                       win_sink[ib], win_w_o[ib].astype(BF))
            win_k.append(k[N_SAMPLE:].reshape(BATCH, SEQ, B_KV_HEADS, B_HEAD_DIM))
            win_v.append(v[N_SAMPLE:].reshape(BATCH, SEQ, B_KV_HEADS, B_HEAD_DIM))
            ib += 1
        else:
            wd, wuq, wuk, wuv, qg, qgs, kg, kgs = _mla_weights(mla_w_down[ic], mla_w_uq[ic], mla_w_ukv[ic],
                                                               mla_q_gain[ic], mla_k_gain[ic])
            q, k, v, ckv, kr = _cproj(x, mod, wd, mla_cq_gain[ic], mla_ckv_gain[ic], wuq, wuk, wuv,
                                      qg, qgs, kg, kgs, c_cos, c_sin)
            ck, cv = _cctx(cache_mla_ckv[:, ic], cache_mla_krope[:, ic], wuk, wuv, kg)
            o = _cattn(q, k, v, ck, cv)
            x = _oproj(x, mod, o, mla_w_o[ic].astype(BF))
            mla_ckv.append(ckv[N_SAMPLE:].reshape(BATCH, SEQ, C_KV_LORA))
            mla_krope.append(kr[N_SAMPLE:, C_NOPE:C_NOPE + C_ROPE].reshape(BATCH, SEQ, C_ROPE))
            ic += 1
        x = _ffn(x, mod, w_in_b[l, 1], w_out_b[l, 1], 2)
    return (x[N_SAMPLE:].reshape(BATCH, SEQ, D_MODEL), x[:N_SAMPLE].reshape(DEC_BATCH, DEC_SEQ, D_MODEL),
            jnp.stack(win_k, axis=1), jnp.stack(win_v, axis=1),
            jnp.stack(mla_ckv, axis=1), jnp.stack(mla_krope, axis=1))
```

```python
import functools
import math

import jax
import jax.numpy as jnp
from jax import lax
from jax.experimental import pallas as pl
from jax.experimental.pallas import tpu as pltpu

D_MODEL = 1024
BATCH = 16
SEQ = 256
DEPTH = 4
DEC_BATCH = 8
DEC_SEQ = 4096
PAST_LEN = 256
GRID_W = 64
N_MIXERS = 3
N_MOD = 9
D_FF = 2816
A_WIDTH = D_MODEL
A_GROUPS = 8
A_CHUNK = 128
B_HEADS = 16
B_KV_HEADS = 4
B_HEAD_DIM = 64
B_WINDOW = 128
C_HEADS = 16
C_NOPE = 64
C_ROPE = 32
C_VDIM = 64
C_Q_LORA = 512
C_KV_LORA = 256
ROPE_BASE = 10000.0
EPS = 1e-6
NEG_INF = -1e30

LANES = 128
N_SAMPLE = DEC_BATCH * DEC_SEQ
N_PROMPT = BATCH * SEQ
N_TOK = N_SAMPLE + N_PROMPT
N_COND = 16
TM = 512
N_TILES = N_TOK // TM
N_SAMPLE_TILES = N_SAMPLE // TM
TILES_PER_SEQ = DEC_SEQ // TM
MOD_TN = 1536
BQ = 256
CQ = 1024
CK = 2048
C_HEAD_PAD = 128
C_DOWN_PAD = C_Q_LORA + C_KV_LORA + 2 * LANES
VMEM_LIMIT_BYTES = 56 * 1024 * 1024

LOG2E = math.log2(math.e)

BF = jnp.bfloat16
F32 = jnp.float32


def _params(*sem):
    return pltpu.CompilerParams(dimension_semantics=sem, vmem_limit_bytes=VMEM_LIMIT_BYTES)


def _dot(a, b):
    return jnp.dot(a, b, preferred_element_type=F32)


def _dot_t(a, b):
    return lax.dot_general(a, b, (((1,), (1,)), ((), ())), preferred_element_type=F32)


def _rms(x):
    return x * lax.rsqrt(jnp.mean(x * x, axis=-1, keepdims=True) + EPS)


def _ada(x, mod_ref, k):
    shift = mod_ref[3 * k:3 * k + 1, :]
    scale = mod_ref[3 * k + 1:3 * k + 2, :]
    return _rms(x) * (1.0 + scale) + shift


def _const_spec(shape):
    nd = len(shape)
    return pl.BlockSpec(shape, lambda *_: (0,) * nd, pipeline_mode=pl.Buffered(1))


def _tile_row(i):
    return jnp.minimum(i * TM // DEC_SEQ, DEC_BATCH)


def _tok_spec(width):
    return pl.BlockSpec((TM, width), lambda i: (i, 0))


_MOD_SPEC = pl.BlockSpec((None, N_MOD, D_MODEL), lambda i: (_tile_row(i), 0, 0))


def _rope_tile(i):
    return jnp.where(i < N_SAMPLE_TILES, i % TILES_PER_SEQ, TILES_PER_SEQ)


_ROPE_SPEC = pl.BlockSpec((TM, LANES), lambda i: (_rope_tile(i), 0))


def _mod_kernel(c_ref, w_ref, b_ref, o_ref):
    a = jax.nn.silu(c_ref[...]).astype(BF)
    o_ref[...] = _dot(a, w_ref[...].astype(BF)) + b_ref[...]


def _modulation(cond, ada_w, ada_b):
    n_out = N_MOD * D_MODEL
    out = pl.pallas_call(
        _mod_kernel,
        grid=(DEPTH, n_out // MOD_TN),
        in_specs=[
            pl.BlockSpec((N_COND, D_MODEL), lambda l, j: (0, 0)),
            pl.BlockSpec((None, D_MODEL, MOD_TN), lambda l, j: (l, 0, j)),
            pl.BlockSpec((None, 1, MOD_TN), lambda l, j: (l, 0, j)),
        ],
        out_specs=pl.BlockSpec((None, N_COND, MOD_TN), lambda l, j: (l, 0, j)),
        out_shape=jax.ShapeDtypeStruct((DEPTH, N_COND, n_out), F32),
        compiler_params=_params("arbitrary", "arbitrary"),
        name="modulation",
    )(cond, ada_w, ada_b.reshape(DEPTH, 1, n_out))
    return out.reshape(DEPTH, N_COND, N_MOD, D_MODEL)


def _ffn_kernel(*refs, k, split_in, split_out):
    n_x = 2 if split_in else 1
    x_refs, (mod_ref, win_ref, wout_ref), o_refs = refs[:n_x], refs[n_x:n_x + 3], refs[n_x + 3:]
    is_sample = pl.program_id(0) < N_SAMPLE_TILES
    x = jnp.where(is_sample, x_refs[0][...], x_refs[1][...]) if split_in else x_refs[0][...]
    hb = _ada(x, mod_ref, k).astype(BF)
    gu = _dot(hb, win_ref[...])
    act = (jax.nn.silu(gu[:, :D_FF]) * gu[:, D_FF:]).astype(BF)
    y = _dot(act, wout_ref[...])
    gate = mod_ref[3 * k + 2:3 * k + 3, :]
    out = x + (0.5 * gate) * y
    if split_out:
        o_refs[1][...] = out

        @pl.when(is_sample)
        def _():
            o_refs[0][...] = out
    else:
        o_refs[0][...] = out


_SAMPLE_SPEC = pl.BlockSpec((TM, D_MODEL), lambda i: (jnp.minimum(i, N_SAMPLE_TILES - 1), 0))
_PROMPT_SPEC = pl.BlockSpec((TM, D_MODEL), lambda i: (jnp.maximum(i - N_SAMPLE_TILES, 0), 0))


def _ffn(xs, mod, w_in, w_out, layer, half, split_in=False, split_out=False):
    def w_spec(rows, cols):
        return pl.BlockSpec((None, None, rows, cols), lambda i: (layer, half, 0, 0), pipeline_mode=pl.Buffered(1))

    x_specs = [_SAMPLE_SPEC, _PROMPT_SPEC] if split_in else [_tok_spec(D_MODEL)]
    if split_out:
        out_specs = [_SAMPLE_SPEC, _PROMPT_SPEC]
        out_shape = [jax.ShapeDtypeStruct((N_SAMPLE, D_MODEL), F32), jax.ShapeDtypeStruct((N_PROMPT, D_MODEL), F32)]
    else:
        out_specs = _tok_spec(D_MODEL)
        out_shape = jax.ShapeDtypeStruct((N_TOK, D_MODEL), F32)
    return pl.pallas_call(
        functools.partial(_ffn_kernel, k=2 * half, split_in=split_in, split_out=split_out),
        grid=(N_TILES,),
        in_specs=x_specs + [_MOD_SPEC, w_spec(D_MODEL, 2 * D_FF), w_spec(D_FF, D_MODEL)],
        out_specs=out_specs,
        out_shape=out_shape,
        compiler_params=_params("arbitrary"),
        name="ffn",
    )(*(xs if split_in else (xs,)), mod, w_in, w_out)


def _gmlp_kernel(x_ref, mod_ref, win_ref, vg_ref, ws_ref, bs_ref, wout_ref, o_ref):
    x = x_ref[...]
    hb = _ada(x, mod_ref, 1).astype(BF)
    pre = _dot(hb, win_ref[...])
    uv = 0.5 * pre * (1.0 + lax.erf(pre * math.sqrt(0.5)))
    u = uv[:, :A_WIDTH]
    v = (_rms(uv[:, A_WIDTH:]) * vg_ref[...]).astype(BF)
    bias = bs_ref[...]
    rows = []
    for c in range(TM // A_CHUNK):
        cols = [_dot(ws_ref[g], v[c * A_CHUNK:(c + 1) * A_CHUNK, g * LANES:(g + 1) * LANES])
                for g in range(A_GROUPS)]
        rows.append(jnp.concatenate(cols, axis=1) + bias)
    sv = jnp.concatenate(rows, axis=0)
    y = _dot((u * sv).astype(BF), wout_ref[...])
    o_ref[...] = x + mod_ref[5:6, :] * y


def _gmlp(x, mod, w_in, v_gain, w_s, b_s, w_out):
    bias = jnp.repeat(b_s.T, A_WIDTH // A_GROUPS, axis=1)
    return pl.pallas_call(
        _gmlp_kernel,
        grid=(N_TILES,),
        in_specs=[_tok_spec(D_MODEL), _MOD_SPEC,
                  _const_spec((D_MODEL, 2 * A_WIDTH)), _const_spec((1, A_WIDTH)),
                  _const_spec((A_GROUPS, A_CHUNK, A_CHUNK)), _const_spec((A_CHUNK, A_WIDTH)),
                  _const_spec((A_WIDTH, D_MODEL))],
        out_specs=_tok_spec(D_MODEL),
        out_shape=jax.ShapeDtypeStruct((N_TOK, D_MODEL), F32),
        compiler_params=_params("arbitrary"),
        name="gmlp",
    )(x, mod, w_in.astype(BF), v_gain.reshape(1, A_WIDTH), w_s.astype(BF), bias, w_out.astype(BF))


def _swap_pairs(y, step):
    lane = lax.broadcasted_iota(jnp.int32, y.shape, 1)
    return jnp.where((lane & step) != 0, pltpu.roll(y, step, 1), pltpu.roll(y, LANES - step, 1))


def _rope_tables(rot_dim, lane_of_dim):
    quarter = rot_dim // 4
    inv = ROPE_BASE ** (-jnp.arange(quarter, dtype=F32) / quarter)
    t = jnp.arange(DEC_SEQ)
    row = (t // GRID_W).astype(F32)
    col = (t % GRID_W).astype(F32)
    ang = jnp.stack([row[:, None] * inv, col[:, None] * inv], axis=1)
    cos, sin = jnp.cos(ang), jnp.sin(ang)
    d = jnp.asarray(lane_of_dim)
    dd = jnp.maximum(d, 0)
    axis, member, freq = dd // (2 * quarter), (dd % (2 * quarter)) // quarter, dd % quarter
    rot = (d >= 0)[None, :]
    c_tab = jnp.where(rot, cos[:, axis, freq], 1.0)
    s_tab = jnp.where(rot, jnp.where(member == 0, -1.0, 1.0)[None, :] * sin[:, axis, freq], 0.0)
    ident_c = jnp.ones((TM, LANES), F32)
    ident_s = jnp.zeros((TM, LANES), F32)
    return jnp.concatenate([c_tab, ident_c], axis=0), jnp.concatenate([s_tab, ident_s], axis=0)


def _bproj_kernel(x_ref, mod_ref, w_ref, qg_ref, kg_ref, cos_ref, sin_ref, q_ref, k_ref, v_ref):
    hb = _ada(x_ref[...], mod_ref, 1).astype(BF)
    qkv = _dot(hb, w_ref[...])
    lo = lax.broadcasted_iota(jnp.int32, (TM, LANES), 1) < B_HEAD_DIM
    cos, sin = cos_ref[...], sin_ref[...]

    def norm_rope(t, gain):
        sq = t * t
        s_lo = jnp.sum(jnp.where(lo, sq, 0.0), axis=-1, keepdims=True)
        s_hi = jnp.sum(jnp.where(lo, 0.0, sq), axis=-1, keepdims=True)
        r = jnp.where(lo, lax.rsqrt(s_lo / B_HEAD_DIM + EPS), lax.rsqrt(s_hi / B_HEAD_DIM + EPS))
        y = t * r * gain
        return y * cos + _swap_pairs(y, B_HEAD_DIM // 4) * sin

    q_scale = B_HEAD_DIM ** -0.5 * LOG2E
    nq = B_HEADS * B_HEAD_DIM
    nk = B_KV_HEADS * B_HEAD_DIM
    for j in range(nq // LANES):
        sl = slice(j * LANES, (j + 1) * LANES)
        q_ref[:, sl] = (norm_rope(qkv[:, sl], qg_ref[...]) * q_scale).astype(BF)
    for j in range(nk // LANES):
        sl = slice(j * LANES, (j + 1) * LANES)
        k_ref[:, sl] = norm_rope(qkv[:, nq + j * LANES:nq + (j + 1) * LANES], kg_ref[...])
    v_ref[...] = qkv[:, nq + nk:]


def _bproj(x, mod, w_qkv, q_gain, k_gain, cos, sin):
    nq = B_HEADS * B_HEAD_DIM
    nk = B_KV_HEADS * B_HEAD_DIM
    return pl.pallas_call(
        _bproj_kernel,
        grid=(N_TILES,),
        in_specs=[_tok_spec(D_MODEL), _MOD_SPEC, _const_spec((D_MODEL, nq + 2 * nk)),
                  _const_spec((1, LANES)), _const_spec((1, LANES)), _ROPE_SPEC, _ROPE_SPEC],
        out_specs=[_tok_spec(nq), _tok_spec(nk), _tok_spec(nk)],
        out_shape=[jax.ShapeDtypeStruct((N_TOK, nq), BF),
                   jax.ShapeDtypeStruct((N_TOK, nk), F32),
                   jax.ShapeDtypeStruct((N_TOK, nk), F32)],
        compiler_params=_params("arbitrary"),
        name="gqa_proj",
    )(x, mod, w_qkv.astype(BF), jnp.tile(q_gain, 2).reshape(1, LANES), jnp.tile(k_gain, 2).reshape(1, LANES),
      cos, sin)


def _gqa_attend(q, kcat, vcat, bias, sink_ref):
    tq = q.shape[0]
    nk = kcat.shape[0]
    lo = lax.broadcasted_iota(jnp.int32, (nk, LANES), 1) < B_HEAD_DIM
    lo_q = lax.broadcasted_iota(jnp.int32, (2 * tq, LANES), 1) < B_HEAD_DIM
    first = lax.broadcasted_iota(jnp.int32, (2 * tq, 1), 0) < tq
    if bias is not None:
        bias = jnp.concatenate([bias, bias], axis=0)
    outs = []
    for g in range(B_KV_HEADS):
        sl = slice((g // 2) * LANES, (g // 2 + 1) * LANES)
        own = lo if g % 2 == 0 else jnp.logical_not(lo)
        k_own = jnp.where(own, kcat[:, sl], 0.0)
        k_swp = pltpu.roll(k_own, B_HEAD_DIM, 1)
        v_own = jnp.where(own, vcat[:, sl], 1.0)
        v_swp = pltpu.roll(v_own, B_HEAD_DIM, 1)
        k_half = (k_own, k_swp) if g % 2 == 0 else (k_swp, k_own)
        v_half = (v_own, v_swp) if g % 2 == 0 else (v_swp, v_own)
        qg = jnp.concatenate([q[:, (2 * g) * LANES:(2 * g + 1) * LANES],
                              q[:, (2 * g + 1) * LANES:(2 * g + 2) * LANES]], axis=0)
        s_all = _dot_t(qg, jnp.concatenate(k_half, axis=0).astype(BF))
        o_half = []
        for e in range(2):
            s = s_all[:, e * nk:(e + 1) * nk]
            if bias is not None:
                nb = bias.shape[1]
                s = jnp.concatenate([s[:, :nb] + bias, s[:, nb:]], axis=1)
            sk = jnp.where(first, sink_ref[4 * g + e], sink_ref[4 * g + 2 + e]) * LOG2E
            m = jnp.maximum(jnp.max(s, axis=-1, keepdims=True), sk)
            ov = _dot(jnp.exp2(s - m).astype(BF), v_half[e].astype(BF))
            o_half.append(ov / (pltpu.roll(ov, B_HEAD_DIM, 1) + jnp.exp2(sk - m)))
        o_g = jnp.where(lo_q, o_half[0], o_half[1])
        outs += [o_g[:tq], o_g[tq:]]
    return jnp.concatenate(outs, axis=1)


def _battn_lat_kernel(x_ref, mod_ref, q_ref, kp_ref, kc_ref, kn_ref, vp_ref, vc_ref, vn_ref,
                      ck_ref, cv_ref, sink_ref, wo_ref, o_ref):
    j = pl.program_id(1)
    kcat = jnp.concatenate([kp_ref[...], kc_ref[...], kn_ref[...], ck_ref[...]], axis=0)
    vcat = jnp.concatenate([vp_ref[...], vc_ref[...], vn_ref[...], cv_ref[...]], axis=0)
    n_lat = BQ + 2 * B_WINDOW
    qi = lax.broadcasted_iota(jnp.int32, (BQ, n_lat), 0)
    pk = lax.broadcasted_iota(jnp.int32, (BQ, n_lat), 1)
    kpos = j * BQ + pk - B_WINDOW
    valid = (jnp.abs(pk - B_WINDOW - qi) <= B_WINDOW) & (kpos >= 0) & (kpos < DEC_SEQ)
    bias = jnp.where(valid, 0.0, NEG_INF)
    o = _gqa_attend(q_ref[...], kcat, vcat, bias, sink_ref)
    y = _dot(o.astype(BF), wo_ref[...])
    o_ref[...] = x_ref[...] + mod_ref[5:6, :] * y


def _battn_ctx_kernel(x_ref, mod_ref, q_ref, k_ref, v_ref, sink_ref, wo_ref, o_ref):
    o = _gqa_attend(q_ref[...], k_ref[...], v_ref[...], None, sink_ref)
    y = _dot(o.astype(BF), wo_ref[...])
    o_ref[...] = x_ref[...] + mod_ref[5:6, :] * y


def _battn(x, mod, q, k, v, cache_k, cache_v, sink, w_o):
    nq = B_HEADS * B_HEAD_DIM
    nk = B_KV_HEADS * B_HEAD_DIM
    nb = DEC_SEQ // BQ
    nw = DEC_SEQ // B_WINDOW
    per = BQ // B_WINDOW
    smem = pl.BlockSpec(memory_space=pltpu.SMEM)
    cur_spec = pl.BlockSpec((BQ, nk), lambda b, j: (b * nb + j, 0))
    prev_spec = pl.BlockSpec((B_WINDOW, nk), lambda b, j: (b * nw + jnp.maximum(per * j - 1, 0), 0))
    next_spec = pl.BlockSpec((B_WINDOW, nk), lambda b, j: (b * nw + jnp.minimum(per * j + per, nw - 1), 0))

    x = pl.pallas_call(
        _battn_lat_kernel,
        grid=(DEC_BATCH, nb),
        in_specs=[pl.BlockSpec((BQ, D_MODEL), lambda b, j: (b * nb + j, 0)),
                  pl.BlockSpec((None, N_MOD, D_MODEL), lambda b, j: (b, 0, 0)),
                  pl.BlockSpec((BQ, nq), lambda b, j: (b * nb + j, 0)),
                  prev_spec, cur_spec, next_spec, prev_spec, cur_spec, next_spec,
                  pl.BlockSpec((None, PAST_LEN, nk), lambda b, j: (b, 0, 0)),
                  pl.BlockSpec((None, PAST_LEN, nk), lambda b, j: (b, 0, 0)),
                  smem, _const_spec((nq, D_MODEL))],
        out_specs=pl.BlockSpec((BQ, D_MODEL), lambda b, j: (b * nb + j, 0)),
        out_shape=jax.ShapeDtypeStruct((N_TOK, D_MODEL), F32),
        input_output_aliases={0: 0},
        compiler_params=_params("arbitrary", "arbitrary"),
        name="gqa_attn_latent",
    )(x, mod, q, k, k, k, v, v, v, cache_k, cache_v, sink, w_o)
    off = N_SAMPLE // SEQ
    return pl.pallas_call(
        _battn_ctx_kernel,
        grid=(BATCH,),
        in_specs=[pl.BlockSpec((SEQ, D_MODEL), lambda b: (off + b, 0)),
                  pl.BlockSpec((None, N_MOD, D_MODEL), lambda b: (DEC_BATCH, 0, 0)),
                  pl.BlockSpec((SEQ, nq), lambda b: (off + b, 0)),
                  pl.BlockSpec((SEQ, nk), lambda b: (off + b, 0)),
                  pl.BlockSpec((SEQ, nk), lambda b: (off + b, 0)),
                  smem, _const_spec((nq, D_MODEL))],
        out_specs=pl.BlockSpec((SEQ, D_MODEL), lambda b: (off + b, 0)),
        out_shape=jax.ShapeDtypeStruct((N_TOK, D_MODEL), F32),
        input_output_aliases={0: 0},
        compiler_params=_params("arbitrary"),
        name="gqa_attn_context",
    )(x, mod, q, k, v, sink, w_o)


def _mla_head_norm_rope(t, t_swap, tab, tab_swap):
    r = lax.rsqrt(jnp.sum(t * t, axis=-1, keepdims=True) / (C_NOPE + C_ROPE) + EPS)
    if t_swap is None:
        return t * r * tab
    return r * (t * tab + t_swap * tab_swap)


def _mla_keys_values(c_kv_b, k_rope, k_rope_swap, wuk_ref, wuv_ref, tab, tab_swap, k_ref, v_ref):
    kn = _dot(c_kv_b, wuk_ref[...])
    v = _dot(c_kv_b, wuv_ref[...])
    upper = (lax.broadcasted_iota(jnp.int32, v.shape, 1) & C_VDIM) != 0
    v_ref[...] = jnp.where(upper, 1.0, v).astype(BF)
    for h in range(C_HEADS):
        sl = slice(h * C_HEAD_PAD, (h + 1) * C_HEAD_PAD)
        k_ref[:, sl] = _mla_head_norm_rope(kn[:, sl] + k_rope, k_rope_swap, tab, tab_swap).astype(BF)


def _cproj_kernel(x_ref, mod_ref, wd_ref, cqg_ref, ckvg_ref, wuq_ref, wuk_ref, wuv_ref,
                  qc_ref, qs_ref, kc_ref, ks_ref, q_ref, k_ref, v_ref, ckv_ref, kr_ref):
    hb = _ada(x_ref[...], mod_ref, 1).astype(BF)
    d = _dot(hb, wd_ref[...])
    c_q = _rms(d[:, :C_Q_LORA]) * cqg_ref[...]
    c_kv = _rms(d[:, C_Q_LORA:C_Q_LORA + C_KV_LORA]) * ckvg_ref[...]
    k_rope = d[:, C_Q_LORA + C_KV_LORA:C_Q_LORA + C_KV_LORA + LANES]
    k_rope_swap = d[:, C_Q_LORA + C_KV_LORA + LANES:]
    ckv_ref[...] = c_kv
    kr_ref[...] = k_rope
    q2 = _dot(c_q.astype(BF), wuq_ref[...])
    wq = C_HEADS * C_HEAD_PAD
    q_tab, q_tab_swap = qc_ref[...], qs_ref[...]
    for h in range(C_HEADS):
        sl = slice(h * C_HEAD_PAD, (h + 1) * C_HEAD_PAD)
        sl_swap = slice(wq + h * C_HEAD_PAD, wq + (h + 1) * C_HEAD_PAD)
        q_ref[:, sl] = _mla_head_norm_rope(q2[:, sl], q2[:, sl_swap], q_tab, q_tab_swap).astype(BF)
    _mla_keys_values(c_kv.astype(BF), k_rope, k_rope_swap, wuk_ref, wuv_ref, kc_ref[...], ks_ref[...], k_ref, v_ref)


def _cctx_kernel(ckv_ref, kr_ref, wuk_ref, wuv_ref, kg_ref, k_ref, v_ref):
    _mla_keys_values(ckv_ref[...].astype(BF), kr_ref[...], None, wuk_ref, wuv_ref, kg_ref[...], None, k_ref, v_ref)


def _mla_weights(w_down, w_uq, w_ukv, q_gain, k_gain):
    hd = C_NOPE + C_ROPE
    pad_lanes = C_HEAD_PAD - hd
    lane = jnp.arange(C_HEAD_PAD)
    is_rope = (lane >= C_NOPE) & (lane < hd)
    partner = jnp.where(is_rope, lane ^ (C_ROPE // 4), lane)

    def swapped(t):
        return jnp.where(is_rope, jnp.take(t, partner, axis=-1), 0.0)

    kr_cols = jnp.pad(w_down[:, C_Q_LORA + C_KV_LORA:], ((0, 0), (C_NOPE, pad_lanes)))
    wd = jnp.concatenate([w_down[:, :C_Q_LORA + C_KV_LORA], kr_cols, swapped(kr_cols)], axis=1)
    wuq = jnp.pad(w_uq.reshape(C_Q_LORA, C_HEADS, hd), ((0, 0), (0, 0), (0, pad_lanes)))
    wq = C_HEADS * C_HEAD_PAD
    wuq = jnp.concatenate([wuq.reshape(C_Q_LORA, wq), swapped(wuq).reshape(C_Q_LORA, wq)], axis=1)
    wukv = w_ukv.reshape(C_KV_LORA, C_HEADS, C_NOPE + C_VDIM)
    wuk = jnp.pad(wukv[:, :, :C_NOPE], ((0, 0), (0, 0), (0, C_HEAD_PAD - C_NOPE)))
    wuv = jnp.pad(wukv[:, :, C_NOPE:], ((0, 0), (0, 0), (0, C_HEAD_PAD - C_VDIM)))
    qg = jnp.pad(q_gain, (0, pad_lanes))
    kg = jnp.pad(k_gain, (0, pad_lanes))
    row = lambda t: t.reshape(1, C_HEAD_PAD)
    return (wd.astype(BF), wuq.astype(BF), wuk.reshape(C_KV_LORA, wq).astype(BF),
            wuv.reshape(C_KV_LORA, wq).astype(BF), row(qg), row(swapped(qg)), row(kg), row(swapped(kg)))


def _cproj(x, mod, wd, cq_gain, ckv_gain, wuq, wuk, wuv, q_tab, q_tab_swap, k_tab, k_tab_swap):
    wq = C_HEADS * C_HEAD_PAD
    return pl.pallas_call(
        _cproj_kernel,
        grid=(N_TILES,),
        in_specs=[_tok_spec(D_MODEL), _MOD_SPEC, _const_spec((D_MODEL, C_DOWN_PAD)),
                  _const_spec((1, C_Q_LORA)), _const_spec((1, C_KV_LORA)),
                  _const_spec((C_Q_LORA, 2 * wq)), _const_spec((C_KV_LORA, wq)), _const_spec((C_KV_LORA, wq)),
                  _ROPE_SPEC, _ROPE_SPEC, _ROPE_SPEC, _ROPE_SPEC],
        out_specs=[_tok_spec(wq), _tok_spec(wq), _tok_spec(wq), _tok_spec(C_KV_LORA), _tok_spec(LANES)],
        out_shape=[jax.ShapeDtypeStruct((N_TOK, wq), BF), jax.ShapeDtypeStruct((N_TOK, wq), BF),
                   jax.ShapeDtypeStruct((N_TOK, wq), BF), jax.ShapeDtypeStruct((N_TOK, C_KV_LORA), F32),
                   jax.ShapeDtypeStruct((N_TOK, LANES), F32)],
        compiler_params=_params("arbitrary"),
        name="mla_proj",
    )(x, mod, wd, cq_gain.reshape(1, C_Q_LORA), ckv_gain.reshape(1, C_KV_LORA), wuq, wuk, wuv,
      q_tab, q_tab_swap, k_tab, k_tab_swap)


def _cctx(cache_ckv, cache_krope, wuk, wuv, kg):
    n = DEC_BATCH * PAST_LEN
    wq = C_HEADS * C_HEAD_PAD
    kr = jnp.pad(cache_krope.reshape(n, C_ROPE), ((0, 0), (C_NOPE, C_HEAD_PAD - C_NOPE - C_ROPE)))
    return pl.pallas_call(
        _cctx_kernel,
        grid=(n // TM,),
        in_specs=[_tok_spec(C_KV_LORA), _tok_spec(LANES), _const_spec((C_KV_LORA, wq)),
                  _const_spec((C_KV_LORA, wq)), _const_spec((1, C_HEAD_PAD))],
        out_specs=[_tok_spec(wq), _tok_spec(wq)],
        out_shape=[jax.ShapeDtypeStruct((n, wq), BF), jax.ShapeDtypeStruct((n, wq), BF)],
        compiler_params=_params("arbitrary"),
        name="mla_context_keys",
    )(cache_ckv.reshape(n, C_KV_LORA), kr, wuk, wuv, kg)


def _mla_attend(q_ref, kv_refs, o_ref):
    tq = q_ref.shape[0]
    lo = lax.broadcasted_iota(jnp.int32, (tq, LANES), 1) < C_VDIM
    acc_sum = []
    for e in range(2):
        sl = slice(e * C_HEAD_PAD, (e + 1) * C_HEAD_PAD)
        qh = q_ref[:, sl]
        m = acc = None
        for k_ref, v_ref in kv_refs:
            nk = k_ref.shape[0]
            ck = min(CK, nk)
            for c in range(nk // ck):
                rows = slice(c * ck, (c + 1) * ck)
                s = _dot_t(qh, k_ref[rows, sl])
                cm = jnp.max(s, axis=-1, keepdims=True)
                m_new = cm if m is None else jnp.maximum(m, cm)
                pv = _dot(jnp.exp2(s - m_new).astype(BF), v_ref[rows, sl])
                acc = pv if acc is None else jnp.exp2(m - m_new) * acc + pv
                m = m_new
        acc_sum.append(acc)
    r0 = pltpu.roll(acc_sum[0], C_VDIM, 1)
    r1 = pltpu.roll(acc_sum[1], C_VDIM, 1)
    o_ref[...] = jnp.where(lo, acc_sum[0] / r0, r1 / acc_sum[1]).astype(BF)


def _cattn_lat_kernel(q_ref, k_ref, v_ref, ck_ref, cv_ref, o_ref):
    _mla_attend(q_ref, ((k_ref, v_ref), (ck_ref, cv_ref)), o_ref)


def _cattn_ctx_kernel(q_ref, k_ref, v_ref, oin_ref, o_ref):
    del oin_ref
    _mla_attend(q_ref, ((k_ref, v_ref),), o_ref)


def _cattn(q, k, v, ck, cv):
    pair_w = 2 * C_HEAD_PAD
    n_pairs = C_HEADS // 2
    nqt = DEC_SEQ // CQ
    o = pl.pallas_call(
        _cattn_lat_kernel,
        grid=(DEC_BATCH, n_pairs, nqt),
        in_specs=[pl.BlockSpec((CQ, pair_w), lambda b, p, t: (b * nqt + t, p)),
                  pl.BlockSpec((DEC_SEQ, pair_w), lambda b, p, t: (b, p)),
                  pl.BlockSpec((DEC_SEQ, pair_w), lambda b, p, t: (b, p)),
                  pl.BlockSpec((PAST_LEN, pair_w), lambda b, p, t: (b, p)),
                  pl.BlockSpec((PAST_LEN, pair_w), lambda b, p, t: (b, p))],
        out_specs=pl.BlockSpec((CQ, LANES), lambda b, p, t: (b * nqt + t, p)),
        out_shape=jax.ShapeDtypeStruct((N_TOK, C_HEADS * C_VDIM), BF),
        compiler_params=_params("arbitrary", "arbitrary", "arbitrary"),
        name="mla_attn_latent",
    )(q, k, v, ck, cv)
    off = N_SAMPLE // SEQ
    return pl.pallas_call(
        _cattn_ctx_kernel,
        grid=(BATCH, n_pairs),
        in_specs=[pl.BlockSpec((SEQ, pair_w), lambda b, p: (off + b, p)),
                  pl.BlockSpec((SEQ, pair_w), lambda b, p: (off + b, p)),
                  pl.BlockSpec((SEQ, pair_w), lambda b, p: (off + b, p)),
                  pl.BlockSpec(memory_space=pl.ANY)],
        out_specs=pl.BlockSpec((SEQ, LANES), lambda b, p: (off + b, p)),
        out_shape=jax.ShapeDtypeStruct((N_TOK, C_HEADS * C_VDIM), BF),
        input_output_aliases={3: 0},
        compiler_params=_params("arbitrary", "arbitrary"),
        name="mla_attn_context",
    )(q, k, v, o)


def _oproj_kernel(x_ref, mod_ref, o_ref, wo_ref, y_ref):
    y_ref[...] = x_ref[...] + mod_ref[5:6, :] * _dot(o_ref[...], wo_ref[...])


def _oproj(x, mod, o, w_o):
    width = o.shape[1]
    return pl.pallas_call(
        _oproj_kernel,
        grid=(N_TILES,),
        in_specs=[_tok_spec(D_MODEL), _MOD_SPEC, _tok_spec(width), _const_spec((width, D_MODEL))],
        out_specs=_tok_spec(D_MODEL),
        out_shape=jax.ShapeDtypeStruct((N_TOK, D_MODEL), F32),
        compiler_params=_params("arbitrary"),
        name="attn_out_proj",
    )(x, mod, o, w_o)


def kernel(x_prompt, x_sample, c, cache_win_k, cache_win_v, cache_mla_ckv, cache_mla_krope, c_ctx,
           ada_w, ada_b, ffn_w_in, ffn_w_out,
           gmlp_w_in, gmlp_v_gain, gmlp_w_s, gmlp_b_s, gmlp_w_out,
           win_w_qkv, win_q_gain, win_k_gain, win_sink, win_w_o,
           mla_w_down, mla_cq_gain, mla_ckv_gain, mla_w_uq, mla_w_ukv, mla_q_gain, mla_k_gain, mla_w_o):
    x = (x_sample.reshape(N_SAMPLE, D_MODEL), x_prompt.reshape(N_PROMPT, D_MODEL))
    cond = jnp.concatenate([c, c_ctx[None, :], jnp.zeros((N_COND - DEC_BATCH - 1, D_MODEL), F32)], axis=0)
    mods = _modulation(cond, ada_w, ada_b)
    w_in_b = ffn_w_in.astype(BF)
    w_out_b = ffn_w_out.astype(BF)

    lane = jnp.arange(LANES)
    b_cos, b_sin = _rope_tables(B_HEAD_DIM, lane % B_HEAD_DIM)
    c_lane = jnp.where((lane >= C_NOPE) & (lane < C_NOPE + C_ROPE), lane - C_NOPE, -1)
    c_cos, c_sin = _rope_tables(C_ROPE, c_lane)

    nk = B_KV_HEADS * B_HEAD_DIM
    win_k, win_v, mla_ckv, mla_krope = [], [], [], []
    ia = ib = ic = 0
    for l in range(DEPTH):
        mod = mods[l]
        x = _ffn(x, mod, w_in_b, w_out_b, l, 0, split_in=(l == 0))
        kind = l % N_MIXERS
        if kind == 0:
            x = _gmlp(x, mod, gmlp_w_in[ia], gmlp_v_gain[ia], gmlp_w_s[ia], gmlp_b_s[ia], gmlp_w_out[ia])
            ia += 1
        elif kind == 1:
            q, k, v = _bproj(x, mod, win_w_qkv[ib], win_q_gain[ib], win_k_gain[ib], b_cos, b_sin)
            x = _battn(x, mod, q, k, v,
                       cache_win_k[:, ib].reshape(DEC_BATCH, PAST_LEN, nk),
                       cache_win_v[:, ib].reshape(DEC_BATCH, PAST_LEN, nk),
                       win_sink[ib], win_w_o[ib].astype(BF))
            win_k.append(k[N_SAMPLE:].reshape(BATCH, SEQ, B_KV_HEADS, B_HEAD_DIM))
            win_v.append(v[N_SAMPLE:].reshape(BATCH, SEQ, B_KV_HEADS, B_HEAD_DIM))
            ib += 1
        else:
            wd, wuq, wuk, wuv, qg, qgs, kg, kgs = _mla_weights(mla_w_down[ic], mla_w_uq[ic], mla_w_ukv[ic],
                                                               mla_q_gain[ic], mla_k_gain[ic])
            q_scale = (C_NOPE + C_ROPE) ** -0.5 * LOG2E
            q, k, v, ckv, kr = _cproj(x, mod, wd, mla_cq_gain[ic], mla_ckv_gain[ic], wuq, wuk, wuv,
                                      c_cos * (qg * q_scale), c_sin * (qgs * q_scale), c_cos * kg, c_sin * kgs)
            ck, cv = _cctx(cache_mla_ckv[:, ic], cache_mla_krope[:, ic], wuk, wuv, kg)
            o = _cattn(q, k, v, ck, cv)
            x = _oproj(x, mod, o, mla_w_o[ic].astype(BF))
            mla_ckv.append(ckv[N_SAMPLE:].reshape(BATCH, SEQ, C_KV_LORA))
            mla_krope.append(kr[N_SAMPLE:, C_NOPE:C_NOPE + C_ROPE].reshape(BATCH, SEQ, C_ROPE))
            ic += 1
        x = _ffn(x, mod, w_in_b, w_out_b, l, 1, split_out=(l == DEPTH - 1))
    y_sample, y_prompt = x
    return (y_prompt.reshape(BATCH, SEQ, D_MODEL), y_sample.reshape(DEC_BATCH, DEC_SEQ, D_MODEL),
            jnp.stack(win_k, axis=1), jnp.stack(win_v, axis=1),
            jnp.stack(mla_ckv, axis=1), jnp.stack(mla_krope, axis=1))
```

```python
import functools
import math

import jax
import jax.numpy as jnp
from jax import lax
from jax.experimental import pallas as pl
from jax.experimental.pallas import tpu as pltpu

D_MODEL = 1024
BATCH = 16
SEQ = 256
DEPTH = 4
DEC_BATCH = 8
DEC_SEQ = 4096
PAST_LEN = 256
GRID_W = 64
N_MIXERS = 3
N_MOD = 9
D_FF = 2816
A_WIDTH = D_MODEL
A_GROUPS = 8
A_CHUNK = 128
B_HEADS = 16
B_KV_HEADS = 4
B_HEAD_DIM = 64
B_WINDOW = 128
C_HEADS = 16
C_NOPE = 64
C_ROPE = 32
C_VDIM = 64
C_Q_LORA = 512
C_KV_LORA = 256
ROPE_BASE = 10000.0
EPS = 1e-6
NEG_INF = -1e30

LANES = 128
N_SAMPLE = DEC_BATCH * DEC_SEQ
N_PROMPT = BATCH * SEQ
N_TOK = N_SAMPLE + N_PROMPT
N_COND = 16
TM = 512
FM = 1024
FF_CHUNKS = ((0, 1536), (1536, 1280))
N_TILES = N_TOK // TM
N_SAMPLE_TILES = N_SAMPLE // TM
TILES_PER_SEQ = DEC_SEQ // TM
MOD_TN = 1536
BQ = 256
CQ = 1024
CK = 2048
C_HEAD_PAD = 128
C_DOWN_PAD = C_Q_LORA + C_KV_LORA + 2 * LANES
VMEM_LIMIT_BYTES = 56 * 1024 * 1024

LOG2E = math.log2(math.e)

BF = jnp.bfloat16
F32 = jnp.float32


def _params(*sem):
    return pltpu.CompilerParams(dimension_semantics=sem, vmem_limit_bytes=VMEM_LIMIT_BYTES)


def _dot(a, b):
    return jnp.dot(a, b, preferred_element_type=F32)


def _dot_t(a, b):
    return lax.dot_general(a, b, (((1,), (1,)), ((), ())), preferred_element_type=F32)


def _rms(x):
    return x * lax.rsqrt(jnp.mean(x * x, axis=-1, keepdims=True) + EPS)


def _ada(x, mod_ref, k):
    shift = mod_ref[3 * k:3 * k + 1, :]
    scale = mod_ref[3 * k + 1:3 * k + 2, :]
    return _rms(x) * (1.0 + scale) + shift


def _const_spec(shape):
    nd = len(shape)
    return pl.BlockSpec(shape, lambda *_: (0,) * nd, pipeline_mode=pl.Buffered(1))


def _tok_spec(width, tm=TM):
    return pl.BlockSpec((tm, width), lambda i: (i, 0))


def _mod_spec(tm):
    return pl.BlockSpec((None, N_MOD, D_MODEL), lambda i: (jnp.minimum(i * tm // DEC_SEQ, DEC_BATCH), 0, 0))


_MOD_SPEC = _mod_spec(TM)


def _rope_tile(i):
    return jnp.where(i < N_SAMPLE_TILES, i % TILES_PER_SEQ, TILES_PER_SEQ)


_ROPE_SPEC = pl.BlockSpec((TM, LANES), lambda i: (_rope_tile(i), 0))


def _mod_kernel(c_ref, w_ref, b_ref, o_ref):
    a = jax.nn.silu(c_ref[...]).astype(BF)
    o_ref[...] = _dot(a, w_ref[...].astype(BF)) + b_ref[...]


def _modulation(cond, ada_w, ada_b):
    n_out = N_MOD * D_MODEL
    out = pl.pallas_call(
        _mod_kernel,
        grid=(DEPTH, n_out // MOD_TN),
        in_specs=[
            pl.BlockSpec((N_COND, D_MODEL), lambda l, j: (0, 0)),
            pl.BlockSpec((None, D_MODEL, MOD_TN), lambda l, j: (l, 0, j)),
            pl.BlockSpec((None, 1, MOD_TN), lambda l, j: (l, 0, j)),
        ],
        out_specs=pl.BlockSpec((None, N_COND, MOD_TN), lambda l, j: (l, 0, j)),
        out_shape=jax.ShapeDtypeStruct((DEPTH, N_COND, n_out), F32),
        compiler_params=_params("arbitrary", "arbitrary"),
        name="modulation",
    )(cond, ada_w, ada_b.reshape(DEPTH, 1, n_out))
    return out.reshape(DEPTH, N_COND, N_MOD, D_MODEL)


def _ffn_kernel(*refs, k, split_in, split_out):
    n_x = 2 if split_in else 1
    x_refs, (mod_ref, win_ref, wout_ref), o_refs = refs[:n_x], refs[n_x:n_x + 3], refs[n_x + 3:]
    is_sample = pl.program_id(0) < N_SAMPLE // FM
    x = jnp.where(is_sample, x_refs[0][...], x_refs[1][...]) if split_in else x_refs[0][...]
    hb = _ada(x, mod_ref, k).astype(BF)
    y = None
    for c0, cw in FF_CHUNKS:
        g = _dot(hb, win_ref[:, c0:c0 + cw])
        u = _dot(hb, win_ref[:, D_FF + c0:D_FF + c0 + cw])
        yc = _dot((jax.nn.silu(g) * u).astype(BF), wout_ref[c0:c0 + cw, :])
        y = yc if y is None else y + yc
    gate = mod_ref[3 * k + 2:3 * k + 3, :]
    out = x + (0.5 * gate) * y
    if split_out:
        o_refs[1][...] = out

        @pl.when(is_sample)
        def _():
            o_refs[0][...] = out
    else:
        o_refs[0][...] = out


_SAMPLE_SPEC = pl.BlockSpec((FM, D_MODEL), lambda i: (jnp.minimum(i, N_SAMPLE // FM - 1), 0))
_PROMPT_SPEC = pl.BlockSpec((FM, D_MODEL), lambda i: (jnp.maximum(i - N_SAMPLE // FM, 0), 0))


def _ffn(xs, mod, w_in, w_out, layer, half, split_in=False, split_out=False):
    def w_spec(rows, cols):
        return pl.BlockSpec((None, None, rows, cols), lambda i: (layer, half, 0, 0), pipeline_mode=pl.Buffered(1))

    x_specs = [_SAMPLE_SPEC, _PROMPT_SPEC] if split_in else [_tok_spec(D_MODEL, FM)]
    if split_out:
        out_specs = [_SAMPLE_SPEC, _PROMPT_SPEC]
        out_shape = [jax.ShapeDtypeStruct((N_SAMPLE, D_MODEL), F32), jax.ShapeDtypeStruct((N_PROMPT, D_MODEL), F32)]
    else:
        out_specs = _tok_spec(D_MODEL, FM)
        out_shape = jax.ShapeDtypeStruct((N_TOK, D_MODEL), F32)
    return pl.pallas_call(
        functools.partial(_ffn_kernel, k=2 * half, split_in=split_in, split_out=split_out),
        grid=(N_TOK // FM,),
        in_specs=x_specs + [_mod_spec(FM), w_spec(D_MODEL, 2 * D_FF), w_spec(D_FF, D_MODEL)],
        out_specs=out_specs,
        out_shape=out_shape,
        compiler_params=_params("arbitrary"),
        name="ffn",
    )(*(xs if split_in else (xs,)), mod, w_in, w_out)


def _gmlp_kernel(x_ref, mod_ref, win_ref, vg_ref, ws_ref, bs_ref, wout_ref, o_ref):
    x = x_ref[...]
    hb = _ada(x, mod_ref, 1).astype(BF)
    pre = _dot(hb, win_ref[...])
    uv = 0.5 * pre * (1.0 + lax.erf(pre * math.sqrt(0.5)))
    u = uv[:, :A_WIDTH]
    v = (_rms(uv[:, A_WIDTH:]) * vg_ref[...]).astype(BF)
    bias = bs_ref[...]
    rows = []
    for c in range(FM // A_CHUNK):
        cols = [_dot(ws_ref[g], v[c * A_CHUNK:(c + 1) * A_CHUNK, g * LANES:(g + 1) * LANES])
                for g in range(A_GROUPS)]
        rows.append(jnp.concatenate(cols, axis=1) + bias)
    sv = jnp.concatenate(rows, axis=0)
    y = _dot((u * sv).astype(BF), wout_ref[...])
    o_ref[...] = x + mod_ref[5:6, :] * y


def _gmlp(x, mod, w_in, v_gain, w_s, b_s, w_out):
    bias = jnp.repeat(b_s.T, A_WIDTH // A_GROUPS, axis=1)
    return pl.pallas_call(
        _gmlp_kernel,
        grid=(N_TOK // FM,),
        in_specs=[_tok_spec(D_MODEL, FM), _mod_spec(FM),
                  _const_spec((D_MODEL, 2 * A_WIDTH)), _const_spec((1, A_WIDTH)),
                  _const_spec((A_GROUPS, A_CHUNK, A_CHUNK)), _const_spec((A_CHUNK, A_WIDTH)),
                  _const_spec((A_WIDTH, D_MODEL))],
        out_specs=_tok_spec(D_MODEL, FM),
        out_shape=jax.ShapeDtypeStruct((N_TOK, D_MODEL), F32),
        compiler_params=_params("arbitrary"),
        name="gmlp",
    )(x, mod, w_in.astype(BF), v_gain.reshape(1, A_WIDTH), w_s.astype(BF), bias, w_out.astype(BF))


def _swap_pairs(y, step):
    lane = lax.broadcasted_iota(jnp.int32, y.shape, 1)
    return jnp.where((lane & step) != 0, pltpu.roll(y, step, 1), pltpu.roll(y, LANES - step, 1))


def _rope_tables(rot_dim, lane_of_dim):
    quarter = rot_dim // 4
    inv = ROPE_BASE ** (-jnp.arange(quarter, dtype=F32) / quarter)
    t = jnp.arange(DEC_SEQ)
    row = (t // GRID_W).astype(F32)
    col = (t % GRID_W).astype(F32)
    ang = jnp.stack([row[:, None] * inv, col[:, None] * inv], axis=1)
    cos, sin = jnp.cos(ang), jnp.sin(ang)
    d = jnp.asarray(lane_of_dim)
    dd = jnp.maximum(d, 0)
    axis, member, freq = dd // (2 * quarter), (dd % (2 * quarter)) // quarter, dd % quarter
    rot = (d >= 0)[None, :]
    c_tab = jnp.where(rot, cos[:, axis, freq], 1.0)
    s_tab = jnp.where(rot, jnp.where(member == 0, -1.0, 1.0)[None, :] * sin[:, axis, freq], 0.0)
    ident_c = jnp.ones((TM, LANES), F32)
    ident_s = jnp.zeros((TM, LANES), F32)
    return jnp.concatenate([c_tab, ident_c], axis=0), jnp.concatenate([s_tab, ident_s], axis=0)


def _bproj_kernel(x_ref, mod_ref, w_ref, ones_ref, qc_ref, qs_ref, kc_ref, ks_ref, q_ref, k_ref, v_ref):
    hb = _ada(x_ref[...], mod_ref, 1).astype(BF)
    qkv = _dot(hb, w_ref[...])
    head_ones = ones_ref[...]

    def norm_rope(t4, tab, tab_swap):
        sq = t4 * t4
        hi = sq.astype(BF)
        lo = (sq - hi.astype(F32)).astype(BF)
        r = lax.rsqrt((_dot(hi, head_ones) + _dot(lo, head_ones)) / B_HEAD_DIM + EPS)
        halves = []
        for j in range(2):
            sl = slice(j * LANES, (j + 1) * LANES)
            t = t4[:, sl]
            halves.append(r[:, sl] * (t * tab + _swap_pairs(t, B_HEAD_DIM // 4) * tab_swap))
        return halves

    nq = B_HEADS * B_HEAD_DIM
    nk = B_KV_HEADS * B_HEAD_DIM
    q_tab, q_tab_swap = qc_ref[...], qs_ref[...]
    for j in range(nq // (2 * LANES)):
        halves = norm_rope(qkv[:, 2 * j * LANES:(2 * j + 2) * LANES], q_tab, q_tab_swap)
        q_ref[:, 2 * j * LANES:(2 * j + 1) * LANES] = halves[0].astype(BF)
        q_ref[:, (2 * j + 1) * LANES:(2 * j + 2) * LANES] = halves[1].astype(BF)
    halves = norm_rope(qkv[:, nq:nq + nk], kc_ref[...], ks_ref[...])
    k_ref[:, :LANES] = halves[0]
    k_ref[:, LANES:] = halves[1]
    v_ref[...] = qkv[:, nq + nk:]


def _bproj(x, mod, w_qkv, q_gain, k_gain, cos, sin):
    nq = B_HEADS * B_HEAD_DIM
    nk = B_KV_HEADS * B_HEAD_DIM
    q_scale = B_HEAD_DIM ** -0.5 * LOG2E
    lane = jnp.arange(LANES)
    qg = jnp.tile(q_gain, LANES // B_HEAD_DIM)
    kg = jnp.tile(k_gain, LANES // B_HEAD_DIM)
    partner = lane ^ (B_HEAD_DIM // 4)
    head = jnp.arange(nk) // B_HEAD_DIM
    head_ones = (head[:, None] == head[None, :]).astype(BF)
    return pl.pallas_call(
        _bproj_kernel,
        grid=(N_TILES,),
        in_specs=[_tok_spec(D_MODEL), _MOD_SPEC, _const_spec((D_MODEL, nq + 2 * nk)), _const_spec((nk, nk)),
                  _ROPE_SPEC, _ROPE_SPEC, _ROPE_SPEC, _ROPE_SPEC],
        out_specs=[_tok_spec(nq), _tok_spec(nk), _tok_spec(nk)],
        out_shape=[jax.ShapeDtypeStruct((N_TOK, nq), BF),
                   jax.ShapeDtypeStruct((N_TOK, nk), F32),
                   jax.ShapeDtypeStruct((N_TOK, nk), F32)],
        compiler_params=_params("arbitrary"),
        name="gqa_proj",
    )(x, mod, w_qkv.astype(BF), head_ones,
      cos * (qg * q_scale), sin * (qg[partner] * q_scale), cos * kg, sin * kg[partner])


def _gqa_attend(q, kcat, vcat, bias, sink_ref):
    tq = q.shape[0]
    nk = kcat.shape[0]
    lo = lax.broadcasted_iota(jnp.int32, (nk, LANES), 1) < B_HEAD_DIM
    lo_q = lax.broadcasted_iota(jnp.int32, (2 * tq, LANES), 1) < B_HEAD_DIM
    first = lax.broadcasted_iota(jnp.int32, (2 * tq, 1), 0) < tq
    if bias is not None:
        bias = jnp.concatenate([bias, bias], axis=0)
    outs = []
    for g in range(B_KV_HEADS):
        sl = slice((g // 2) * LANES, (g // 2 + 1) * LANES)
        own = lo if g % 2 == 0 else jnp.logical_not(lo)
        k_own = jnp.where(own, kcat[:, sl], 0.0)
        k_swp = pltpu.roll(k_own, B_HEAD_DIM, 1)
        v_own = jnp.where(own, vcat[:, sl], 1.0)
        v_swp = pltpu.roll(v_own, B_HEAD_DIM, 1)
        k_half = (k_own, k_swp) if g % 2 == 0 else (k_swp, k_own)
        v_half = (v_own, v_swp) if g % 2 == 0 else (v_swp, v_own)
        qg = jnp.concatenate([q[:, (2 * g) * LANES:(2 * g + 1) * LANES],
                              q[:, (2 * g + 1) * LANES:(2 * g + 2) * LANES]], axis=0)
        s_all = _dot_t(qg, jnp.concatenate(k_half, axis=0).astype(BF))
        o_half = []
        for e in range(2):
            s = s_all[:, e * nk:(e + 1) * nk]
            if bias is not None:
                nb = bias.shape[1]
                s = jnp.concatenate([s[:, :nb] + bias, s[:, nb:]], axis=1)
            sk = jnp.where(first, sink_ref[4 * g + e], sink_ref[4 * g + 2 + e]) * LOG2E
            m = jnp.maximum(jnp.max(s, axis=-1, keepdims=True), sk)
            ov = _dot(jnp.exp2(s - m).astype(BF), v_half[e].astype(BF))
            o_half.append(ov / (pltpu.roll(ov, B_HEAD_DIM, 1) + jnp.exp2(sk - m)))
        o_g = jnp.where(lo_q, o_half[0], o_half[1])
        outs += [o_g[:tq], o_g[tq:]]
    return jnp.concatenate(outs, axis=1)


def _battn_lat_kernel(x_ref, mod_ref, q_ref, kp_ref, kc_ref, kn_ref, vp_ref, vc_ref, vn_ref,
                      ck_ref, cv_ref, sink_ref, wo_ref, o_ref):
    j = pl.program_id(1)
    kcat = jnp.concatenate([kp_ref[...], kc_ref[...], kn_ref[...], ck_ref[...]], axis=0)
    vcat = jnp.concatenate([vp_ref[...], vc_ref[...], vn_ref[...], cv_ref[...]], axis=0)
    n_lat = BQ + 2 * B_WINDOW
    qi = lax.broadcasted_iota(jnp.int32, (BQ, n_lat), 0)
    pk = lax.broadcasted_iota(jnp.int32, (BQ, n_lat), 1)
    kpos = j * BQ + pk - B_WINDOW
    valid = (jnp.abs(pk - B_WINDOW - qi) <= B_WINDOW) & (kpos >= 0) & (kpos < DEC_SEQ)
    bias = jnp.where(valid, 0.0, NEG_INF)
    o = _gqa_attend(q_ref[...], kcat, vcat, bias, sink_ref)
    y = _dot(o.astype(BF), wo_ref[...])
    o_ref[...] = x_ref[...] + mod_ref[5:6, :] * y


def _battn_ctx_kernel(x_ref, mod_ref, q_ref, k_ref, v_ref, sink_ref, wo_ref, o_ref):
    o = _gqa_attend(q_ref[...], k_ref[...], v_ref[...], None, sink_ref)
    y = _dot(o.astype(BF), wo_ref[...])
    o_ref[...] = x_ref[...] + mod_ref[5:6, :] * y


def _battn(x, mod, q, k, v, cache_k, cache_v, sink, w_o):
    nq = B_HEADS * B_HEAD_DIM
    nk = B_KV_HEADS * B_HEAD_DIM
    nb = DEC_SEQ // BQ
    nw = DEC_SEQ // B_WINDOW
    per = BQ // B_WINDOW
    smem = pl.BlockSpec(memory_space=pltpu.SMEM)
    cur_spec = pl.BlockSpec((BQ, nk), lambda b, j: (b * nb + j, 0))
    prev_spec = pl.BlockSpec((B_WINDOW, nk), lambda b, j: (b * nw + jnp.maximum(per * j - 1, 0), 0))
    next_spec = pl.BlockSpec((B_WINDOW, nk), lambda b, j: (b * nw + jnp.minimum(per * j + per, nw - 1), 0))

    x = pl.pallas_call(
        _battn_lat_kernel,
        grid=(DEC_BATCH, nb),
        in_specs=[pl.BlockSpec((BQ, D_MODEL), lambda b, j: (b * nb + j, 0)),
                  pl.BlockSpec((None, N_MOD, D_MODEL), lambda b, j: (b, 0, 0)),
                  pl.BlockSpec((BQ, nq), lambda b, j: (b * nb + j, 0)),
                  prev_spec, cur_spec, next_spec, prev_spec, cur_spec, next_spec,
                  pl.BlockSpec((None, PAST_LEN, nk), lambda b, j: (b, 0, 0)),
                  pl.BlockSpec((None, PAST_LEN, nk), lambda b, j: (b, 0, 0)),
                  smem, _const_spec((nq, D_MODEL))],
        out_specs=pl.BlockSpec((BQ, D_MODEL), lambda b, j: (b * nb + j, 0)),
        out_shape=jax.ShapeDtypeStruct((N_TOK, D_MODEL), F32),
        input_output_aliases={0: 0},
        compiler_params=_params("arbitrary", "arbitrary"),
        name="gqa_attn_latent",
    )(x, mod, q, k, k, k, v, v, v, cache_k, cache_v, sink, w_o)
    off = N_SAMPLE // SEQ
    return pl.pallas_call(
        _battn_ctx_kernel,
        grid=(BATCH,),
        in_specs=[pl.BlockSpec((SEQ, D_MODEL), lambda b: (off + b, 0)),
                  pl.BlockSpec((None, N_MOD, D_MODEL), lambda b: (DEC_BATCH, 0, 0)),
                  pl.BlockSpec((SEQ, nq), lambda b: (off + b, 0)),
                  pl.BlockSpec((SEQ, nk), lambda b: (off + b, 0)),
                  pl.BlockSpec((SEQ, nk), lambda b: (off + b, 0)),
                  smem, _const_spec((nq, D_MODEL))],
        out_specs=pl.BlockSpec((SEQ, D_MODEL), lambda b: (off + b, 0)),
        out_shape=jax.ShapeDtypeStruct((N_TOK, D_MODEL), F32),
        input_output_aliases={0: 0},
        compiler_params=_params("arbitrary"),
        name="gqa_attn_context",
    )(x, mod, q, k, v, sink, w_o)


def _mla_head_norm_rope(t, t_swap, tab, tab_swap):
    r = lax.rsqrt(jnp.sum(t * t, axis=-1, keepdims=True) / (C_NOPE + C_ROPE) + EPS)
    if t_swap is None:
        return t * r * tab
    return r * (t * tab + t_swap * tab_swap)


def _mla_keys_values(c_kv_b, k_rope, k_rope_swap, wuk_ref, wuv_ref, tab, tab_swap, k_ref, v_ref):
    kn = _dot(c_kv_b, wuk_ref[...])
    v = _dot(c_kv_b, wuv_ref[...])
    upper = (lax.broadcasted_iota(jnp.int32, v.shape, 1) & C_VDIM) != 0
    v_ref[...] = jnp.where(upper, 1.0, v).astype(BF)
    for h in range(C_HEADS):
        sl = slice(h * C_HEAD_PAD, (h + 1) * C_HEAD_PAD)
        k_ref[:, sl] = _mla_head_norm_rope(kn[:, sl] + k_rope, k_rope_swap, tab, tab_swap).astype(BF)


def _cproj_kernel(x_ref, mod_ref, wd_ref, cqg_ref, ckvg_ref, wuq_ref, wuk_ref, wuv_ref,
                  qc_ref, qs_ref, kc_ref, ks_ref, q_ref, k_ref, v_ref, ckv_ref, kr_ref):
    hb = _ada(x_ref[...], mod_ref, 1).astype(BF)
    d = _dot(hb, wd_ref[...])
    c_q = _rms(d[:, :C_Q_LORA]) * cqg_ref[...]
    c_kv = _rms(d[:, C_Q_LORA:C_Q_LORA + C_KV_LORA]) * ckvg_ref[...]
    k_rope = d[:, C_Q_LORA + C_KV_LORA:C_Q_LORA + C_KV_LORA + LANES]
    k_rope_swap = d[:, C_Q_LORA + C_KV_LORA + LANES:]
    ckv_ref[...] = c_kv
    kr_ref[...] = k_rope
    q2 = _dot(c_q.astype(BF), wuq_ref[...])
    wq = C_HEADS * C_HEAD_PAD
    q_tab, q_tab_swap = qc_ref[...], qs_ref[...]
    for h in range(C_HEADS):
        sl = slice(h * C_HEAD_PAD, (h + 1) * C_HEAD_PAD)
        sl_swap = slice(wq + h * C_HEAD_PAD, wq + (h + 1) * C_HEAD_PAD)
        q_ref[:, sl] = _mla_head_norm_rope(q2[:, sl], q2[:, sl_swap], q_tab, q_tab_swap).astype(BF)
    _mla_keys_values(c_kv.astype(BF), k_rope, k_rope_swap, wuk_ref, wuv_ref, kc_ref[...], ks_ref[...], k_ref, v_ref)


def _cctx_kernel(ckv_ref, kr_ref, wuk_ref, wuv_ref, kg_ref, k_ref, v_ref):
    _mla_keys_values(ckv_ref[...].astype(BF), kr_ref[...], None, wuk_ref, wuv_ref, kg_ref[...], None, k_ref, v_ref)


def _mla_weights(w_down, w_uq, w_ukv, q_gain, k_gain):
    hd = C_NOPE + C_ROPE
    pad_lanes = C_HEAD_PAD - hd
    lane = jnp.arange(C_HEAD_PAD)
    is_rope = (lane >= C_NOPE) & (lane < hd)
    partner = jnp.where(is_rope, lane ^ (C_ROPE // 4), lane)

    def swapped(t):
        return jnp.where(is_rope, jnp.take(t, partner, axis=-1), 0.0)

    kr_cols = jnp.pad(w_down[:, C_Q_LORA + C_KV_LORA:], ((0, 0), (C_NOPE, pad_lanes)))
    wd = jnp.concatenate([w_down[:, :C_Q_LORA + C_KV_LORA], kr_cols, swapped(kr_cols)], axis=1)
    wuq = jnp.pad(w_uq.reshape(C_Q_LORA, C_HEADS, hd), ((0, 0), (0, 0), (0, pad_lanes)))
    wq = C_HEADS * C_HEAD_PAD
    wuq = jnp.concatenate([wuq.reshape(C_Q_LORA, wq), swapped(wuq).reshape(C_Q_LORA, wq)], axis=1)
    wukv = w_ukv.reshape(C_KV_LORA, C_HEADS, C_NOPE + C_VDIM)
    wuk = jnp.pad(wukv[:, :, :C_NOPE], ((0, 0), (0, 0), (0, C_HEAD_PAD - C_NOPE)))
    wuv = jnp.pad(wukv[:, :, C_NOPE:], ((0, 0), (0, 0), (0, C_HEAD_PAD - C_VDIM)))
    qg = jnp.pad(q_gain, (0, pad_lanes))
    kg = jnp.pad(k_gain, (0, pad_lanes))
    row = lambda t: t.reshape(1, C_HEAD_PAD)
    return (wd.astype(BF), wuq.astype(BF), wuk.reshape(C_KV_LORA, wq).astype(BF),
            wuv.reshape(C_KV_LORA, wq).astype(BF), row(qg), row(swapped(qg)), row(kg), row(swapped(kg)))


def _cproj(x, mod, wd, cq_gain, ckv_gain, wuq, wuk, wuv, q_tab, q_tab_swap, k_tab, k_tab_swap):
    wq = C_HEADS * C_HEAD_PAD
    return pl.pallas_call(
        _cproj_kernel,
        grid=(N_TILES,),
        in_specs=[_tok_spec(D_MODEL), _MOD_SPEC, _const_spec((D_MODEL, C_DOWN_PAD)),
                  _const_spec((1, C_Q_LORA)), _const_spec((1, C_KV_LORA)),
                  _const_spec((C_Q_LORA, 2 * wq)), _const_spec((C_KV_LORA, wq)), _const_spec((C_KV_LORA, wq)),
                  _ROPE_SPEC, _ROPE_SPEC, _ROPE_SPEC, _ROPE_SPEC],
        out_specs=[_tok_spec(wq), _tok_spec(wq), _tok_spec(wq), _tok_spec(C_KV_LORA), _tok_spec(LANES)],
        out_shape=[jax.ShapeDtypeStruct((N_TOK, wq), BF), jax.ShapeDtypeStruct((N_TOK, wq), BF),
                   jax.ShapeDtypeStruct((N_TOK, wq), BF), jax.ShapeDtypeStruct((N_TOK, C_KV_LORA), F32),
                   jax.ShapeDtypeStruct((N_TOK, LANES), F32)],
        compiler_params=_params("arbitrary"),
        name="mla_proj",
    )(x, mod, wd, cq_gain.reshape(1, C_Q_LORA), ckv_gain.reshape(1, C_KV_LORA), wuq, wuk, wuv,
      q_tab, q_tab_swap, k_tab, k_tab_swap)


def _cctx(cache_ckv, cache_krope, wuk, wuv, kg):
    n = DEC_BATCH * PAST_LEN
    wq = C_HEADS * C_HEAD_PAD
    kr = jnp.pad(cache_krope.reshape(n, C_ROPE), ((0, 0), (C_NOPE, C_HEAD_PAD - C_NOPE - C_ROPE)))
    return pl.pallas_call(
        _cctx_kernel,
        grid=(n // TM,),
        in_specs=[_tok_spec(C_KV_LORA), _tok_spec(LANES), _const_spec((C_KV_LORA, wq)),
                  _const_spec((C_KV_LORA, wq)), _const_spec((1, C_HEAD_PAD))],
        out_specs=[_tok_spec(wq), _tok_spec(wq)],
        out_shape=[jax.ShapeDtypeStruct((n, wq), BF), jax.ShapeDtypeStruct((n, wq), BF)],
        compiler_params=_params("arbitrary"),
        name="mla_context_keys",
    )(cache_ckv.reshape(n, C_KV_LORA), kr, wuk, wuv, kg)


def _mla_attend(q_ref, kv_refs, o_ref, pair=0):
    tq = q_ref.shape[0]
    lo = lax.broadcasted_iota(jnp.int32, (tq, LANES), 1) < C_VDIM
    acc_sum = []
    for e in range(2):
        sl = slice((2 * pair + e) * C_HEAD_PAD, (2 * pair + e + 1) * C_HEAD_PAD)
        qh = q_ref[:, sl]
        m = acc = None
        for k_ref, v_ref in kv_refs:
            nk = k_ref.shape[0]
            ck = min(CK, nk)
            for c in range(nk // ck):
                rows = slice(c * ck, (c + 1) * ck)
                s = _dot_t(qh, k_ref[rows, sl])
                cm = jnp.max(s, axis=-1, keepdims=True)
                m_new = cm if m is None else jnp.maximum(m, cm)
                pv = _dot(jnp.exp2(s - m_new).astype(BF), v_ref[rows, sl])
                acc = pv if acc is None else jnp.exp2(m - m_new) * acc + pv
                m = m_new
        acc_sum.append(acc)
    r0 = pltpu.roll(acc_sum[0], C_VDIM, 1)
    r1 = pltpu.roll(acc_sum[1], C_VDIM, 1)
    o_ref[:, pair * LANES:(pair + 1) * LANES] = jnp.where(lo, acc_sum[0] / r0, r1 / acc_sum[1]).astype(BF)


def _cattn_lat_kernel(q_ref, k_ref, v_ref, ck_ref, cv_ref, o_ref):
    _mla_attend(q_ref, ((k_ref, v_ref), (ck_ref, cv_ref)), o_ref)


def _cattn_ctx_kernel(q_ref, k_ref, v_ref, oin_ref, o_ref):
    del oin_ref
    for pair in range(C_HEADS // 2):
        _mla_attend(q_ref, ((k_ref, v_ref),), o_ref, pair)


def _cattn(q, k, v, ck, cv):
    pair_w = 2 * C_HEAD_PAD
    n_pairs = C_HEADS // 2
    nqt = DEC_SEQ // CQ
    o = pl.pallas_call(
        _cattn_lat_kernel,
        grid=(DEC_BATCH, n_pairs, nqt),
        in_specs=[pl.BlockSpec((CQ, pair_w), lambda b, p, t: (b * nqt + t, p)),
                  pl.BlockSpec((DEC_SEQ, pair_w), lambda b, p, t: (b, p)),
                  pl.BlockSpec((DEC_SEQ, pair_w), lambda b, p, t: (b, p)),
                  pl.BlockSpec((PAST_LEN, pair_w), lambda b, p, t: (b, p)),
                  pl.BlockSpec((PAST_LEN, pair_w), lambda b, p, t: (b, p))],
        out_specs=pl.BlockSpec((CQ, LANES), lambda b, p, t: (b * nqt + t, p)),
        out_shape=jax.ShapeDtypeStruct((N_TOK, C_HEADS * C_VDIM), BF),
        compiler_params=_params("arbitrary", "arbitrary", "arbitrary"),
        name="mla_attn_latent",
    )(q, k, v, ck, cv)
    off = N_SAMPLE // SEQ
    return pl.pallas_call(
        _cattn_ctx_kernel,
        grid=(BATCH,),
        in_specs=[pl.BlockSpec((SEQ, n_pairs * pair_w), lambda b: (off + b, 0)),
                  pl.BlockSpec((SEQ, n_pairs * pair_w), lambda b: (off + b, 0)),
                  pl.BlockSpec((SEQ, n_pairs * pair_w), lambda b: (off + b, 0)),
                  pl.BlockSpec(memory_space=pl.ANY)],
        out_specs=pl.BlockSpec((SEQ, n_pairs * LANES), lambda b: (off + b, 0)),
        out_shape=jax.ShapeDtypeStruct((N_TOK, C_HEADS * C_VDIM), BF),
        input_output_aliases={3: 0},
        compiler_params=_params("arbitrary"),
        name="mla_attn_context",
    )(q, k, v, o)


def _oproj_kernel(x_ref, mod_ref, o_ref, wo_ref, y_ref):
    y_ref[...] = x_ref[...] + mod_ref[5:6, :] * _dot(o_ref[...], wo_ref[...])


def _oproj(x, mod, o, w_o):
    width = o.shape[1]
    return pl.pallas_call(
        _oproj_kernel,
        grid=(N_TILES,),
        in_specs=[_tok_spec(D_MODEL), _MOD_SPEC, _tok_spec(width), _const_spec((width, D_MODEL))],
        out_specs=_tok_spec(D_MODEL),
        out_shape=jax.ShapeDtypeStruct((N_TOK, D_MODEL), F32),
        compiler_params=_params("arbitrary"),
        name="attn_out_proj",
    )(x, mod, o, w_o)


def kernel(x_prompt, x_sample, c, cache_win_k, cache_win_v, cache_mla_ckv, cache_mla_krope, c_ctx,
           ada_w, ada_b, ffn_w_in, ffn_w_out,
           gmlp_w_in, gmlp_v_gain, gmlp_w_s, gmlp_b_s, gmlp_w_out,
           win_w_qkv, win_q_gain, win_k_gain, win_sink, win_w_o,
           mla_w_down, mla_cq_gain, mla_ckv_gain, mla_w_uq, mla_w_ukv, mla_q_gain, mla_k_gain, mla_w_o):
    x = (x_sample.reshape(N_SAMPLE, D_MODEL), x_prompt.reshape(N_PROMPT, D_MODEL))
    cond = jnp.concatenate([c, c_ctx[None, :], jnp.zeros((N_COND - DEC_BATCH - 1, D_MODEL), F32)], axis=0)
    mods = _modulation(cond, ada_w, ada_b)
    w_in_b = ffn_w_in.astype(BF)
    w_out_b = ffn_w_out.astype(BF)

    lane = jnp.arange(LANES)
    b_cos, b_sin = _rope_tables(B_HEAD_DIM, lane % B_HEAD_DIM)
    c_lane = jnp.where((lane >= C_NOPE) & (lane < C_NOPE + C_ROPE), lane - C_NOPE, -1)
    c_cos, c_sin = _rope_tables(C_ROPE, c_lane)

    nk = B_KV_HEADS * B_HEAD_DIM
    win_k, win_v, mla_ckv, mla_krope = [], [], [], []
    ia = ib = ic = 0
    for l in range(DEPTH):
        mod = mods[l]
        x = _ffn(x, mod, w_in_b, w_out_b, l, 0, split_in=(l == 0))
        kind = l % N_MIXERS
        if kind == 0:
            x = _gmlp(x, mod, gmlp_w_in[ia], gmlp_v_gain[ia], gmlp_w_s[ia], gmlp_b_s[ia], gmlp_w_out[ia])
            ia += 1
        elif kind == 1:
            q, k, v = _bproj(x, mod, win_w_qkv[ib], win_q_gain[ib], win_k_gain[ib], b_cos, b_sin)
            x = _battn(x, mod, q, k, v,
                       cache_win_k[:, ib].reshape(DEC_BATCH, PAST_LEN, nk),
                       cache_win_v[:, ib].reshape(DEC_BATCH, PAST_LEN, nk),
                       win_sink[ib], win_w_o[ib].astype(BF))
            win_k.append(k[N_SAMPLE:].reshape(BATCH, SEQ, B_KV_HEADS, B_HEAD_DIM))
            win_v.append(v[N_SAMPLE:].reshape(BATCH, SEQ, B_KV_HEADS, B_HEAD_DIM))
            ib += 1
        else:
            wd, wuq, wuk, wuv, qg, qgs, kg, kgs = _mla_weights(mla_w_down[ic], mla_w_uq[ic], mla_w_ukv[ic],
                                                               mla_q_gain[ic], mla_k_gain[ic])
            q_scale = (C_NOPE + C_ROPE) ** -0.5 * LOG2E
            q, k, v, ckv, kr = _cproj(x, mod, wd, mla_cq_gain[ic], mla_ckv_gain[ic], wuq, wuk, wuv,
                                      c_cos * (qg * q_scale), c_sin * (qgs * q_scale), c_cos * kg, c_sin * kgs)
            ck, cv = _cctx(cache_mla_ckv[:, ic], cache_mla_krope[:, ic], wuk, wuv, kg)
            o = _cattn(q, k, v, ck, cv)
            x = _oproj(x, mod, o, mla_w_o[ic].astype(BF))
            mla_ckv.append(ckv[N_SAMPLE:].reshape(BATCH, SEQ, C_KV_LORA))
            mla_krope.append(kr[N_SAMPLE:, C_NOPE:C_NOPE + C_ROPE].reshape(BATCH, SEQ, C_ROPE))
            ic += 1
        x = _ffn(x, mod, w_in_b, w_out_b, l, 1, split_out=(l == DEPTH - 1))
    y_sample, y_prompt = x
    return (y_prompt.reshape(BATCH, SEQ, D_MODEL), y_sample.reshape(DEC_BATCH, DEC_SEQ, D_MODEL),
            jnp.stack(win_k, axis=1), jnp.stack(win_v, axis=1),
            jnp.stack(mla_ckv, axis=1), jnp.stack(mla_krope, axis=1))
```

```python
import functools
import math

import jax
import jax.numpy as jnp
from jax import lax
from jax.experimental import pallas as pl
from jax.experimental.pallas import tpu as pltpu

D_MODEL = 1024
BATCH = 16
SEQ = 256
DEPTH = 4
DEC_BATCH = 8
DEC_SEQ = 4096
PAST_LEN = 256
GRID_W = 64
N_MIXERS = 3
N_MOD = 9
D_FF = 2816
A_WIDTH = D_MODEL
A_GROUPS = 8
A_CHUNK = 128
B_HEADS = 16
B_KV_HEADS = 4
B_HEAD_DIM = 64
B_WINDOW = 128
C_HEADS = 16
C_NOPE = 64
C_ROPE = 32
C_VDIM = 64
C_Q_LORA = 512
C_KV_LORA = 256
ROPE_BASE = 10000.0
EPS = 1e-6
NEG_INF = -1e30

LANES = 128
N_SAMPLE = DEC_BATCH * DEC_SEQ
N_PROMPT = BATCH * SEQ
N_TOK = N_SAMPLE + N_PROMPT
N_COND = 16
TM = 512
FM = 1024
FF_CHUNKS = ((0, 1536), (1536, 1280))
N_TILES = N_TOK // TM
N_SAMPLE_TILES = N_SAMPLE // TM
TILES_PER_SEQ = DEC_SEQ // TM
MOD_TN = 1536
SHIFT_TN = 1408
BQ = 256
CQ = 1024
CK = 2048
C_HEAD_PAD = 128
C_DOWN_PAD = C_Q_LORA + C_KV_LORA + 2 * LANES
VMEM_LIMIT_BYTES = 56 * 1024 * 1024

LOG2E = math.log2(math.e)

BF = jnp.bfloat16
F32 = jnp.float32


def _params(*sem):
    return pltpu.CompilerParams(dimension_semantics=sem, vmem_limit_bytes=VMEM_LIMIT_BYTES)


def _dot(a, b):
    return jnp.dot(a, b, preferred_element_type=F32)


def _dot_t(a, b):
    return lax.dot_general(a, b, (((1,), (1,)), ((), ())), preferred_element_type=F32)


def _rms(x):
    return x * lax.rsqrt(jnp.mean(x * x, axis=-1, keepdims=True) + EPS)


def _ada(x, mod_ref, k):
    shift = mod_ref[3 * k:3 * k + 1, :]
    scale = mod_ref[3 * k + 1:3 * k + 2, :]
    return _rms(x) * (1.0 + scale) + shift


def _const_spec(shape):
    nd = len(shape)
    return pl.BlockSpec(shape, lambda *_: (0,) * nd, pipeline_mode=pl.Buffered(1))


def _tok_spec(width, tm=TM):
    return pl.BlockSpec((tm, width), lambda i: (i, 0))


def _mod_spec(tm):
    return pl.BlockSpec((None, N_MOD, D_MODEL), lambda i: (jnp.minimum(i * tm // DEC_SEQ, DEC_BATCH), 0, 0))


_MOD_SPEC = _mod_spec(TM)


def _rope_tile(i):
    return jnp.where(i < N_SAMPLE_TILES, i % TILES_PER_SEQ, TILES_PER_SEQ)


_ROPE_SPEC = pl.BlockSpec((TM, LANES), lambda i: (_rope_tile(i), 0))


def _mod_kernel(c_ref, w_ref, b_ref, o_ref):
    a = jax.nn.silu(c_ref[...]).astype(BF)
    o_ref[...] = _dot(a, w_ref[...].astype(BF)) + b_ref[...]


def _modulation(cond, ada_w, ada_b):
    n_out = N_MOD * D_MODEL
    out = pl.pallas_call(
        _mod_kernel,
        grid=(DEPTH, n_out // MOD_TN),
        in_specs=[
            pl.BlockSpec((N_COND, D_MODEL), lambda l, j: (0, 0)),
            pl.BlockSpec((None, D_MODEL, MOD_TN), lambda l, j: (l, 0, j)),
            pl.BlockSpec((None, 1, MOD_TN), lambda l, j: (l, 0, j)),
        ],
        out_specs=pl.BlockSpec((None, N_COND, MOD_TN), lambda l, j: (l, 0, j)),
        out_shape=jax.ShapeDtypeStruct((DEPTH, N_COND, n_out), F32),
        compiler_params=_params("arbitrary", "arbitrary"),
        name="modulation",
    )(cond, ada_w, ada_b.reshape(DEPTH, 1, n_out))
    return out.reshape(DEPTH, N_COND, N_MOD, D_MODEL)


def _ffn_shift_kernel(s_ref, w_ref, o_ref):
    o_ref[...] = _dot(s_ref[...].astype(BF), w_ref[...])


def _ffn_shift_terms(mods, w_in):
    shifts = jnp.stack([mods[:, :, 0], mods[:, :, 6]], axis=1)
    out = pl.pallas_call(
        _ffn_shift_kernel,
        grid=(DEPTH, 2, 2 * D_FF // SHIFT_TN),
        in_specs=[pl.BlockSpec((None, None, N_COND, D_MODEL), lambda l, h, j: (l, h, 0, 0)),
                  pl.BlockSpec((None, None, D_MODEL, SHIFT_TN), lambda l, h, j: (l, h, 0, j))],
        out_specs=pl.BlockSpec((None, None, N_COND, SHIFT_TN), lambda l, h, j: (l, h, 0, j)),
        out_shape=jax.ShapeDtypeStruct((DEPTH, 2, N_COND, 2 * D_FF), F32),
        compiler_params=_params("arbitrary", "arbitrary", "arbitrary"),
        name="ffn_shift_terms",
    )(shifts, w_in)
    return out.reshape(DEPTH, 2, N_COND, 1, 2 * D_FF)


def _ffn_kernel(*refs, k, split_in, split_out):
    n_x = 2 if split_in else 1
    x_refs, (mod_ref, sh_ref, win_ref, wout_ref), o_refs = refs[:n_x], refs[n_x:n_x + 4], refs[n_x + 4:]
    is_sample = pl.program_id(0) < N_SAMPLE // FM
    x = jnp.where(is_sample, x_refs[0][...], x_refs[1][...]) if split_in else x_refs[0][...]
    xa = (x * (1.0 + mod_ref[3 * k + 1:3 * k + 2, :])).astype(BF)
    rinv = lax.rsqrt(jnp.mean(x * x, axis=-1, keepdims=True) + EPS)
    y = None
    for c0, cw in FF_CHUNKS:
        g = _dot(xa, win_ref[:, c0:c0 + cw]) * rinv + sh_ref[:, c0:c0 + cw]
        u = _dot(xa, win_ref[:, D_FF + c0:D_FF + c0 + cw]) * rinv + sh_ref[:, D_FF + c0:D_FF + c0 + cw]
        yc = _dot((jax.nn.silu(g) * u).astype(BF), wout_ref[c0:c0 + cw, :])
        y = yc if y is None else y + yc
    gate = mod_ref[3 * k + 2:3 * k + 3, :]
    out = x + (0.5 * gate) * y
    if split_out:
        o_refs[1][...] = out

        @pl.when(is_sample)
        def _():
            o_refs[0][...] = out
    else:
        o_refs[0][...] = out


_SAMPLE_SPEC = pl.BlockSpec((FM, D_MODEL), lambda i: (jnp.minimum(i, N_SAMPLE // FM - 1), 0))
_PROMPT_SPEC = pl.BlockSpec((FM, D_MODEL), lambda i: (jnp.maximum(i - N_SAMPLE // FM, 0), 0))


def _ffn(xs, mod, shift_terms, w_in, w_out, layer, half, split_in=False, split_out=False):
    def w_spec(rows, cols):
        return pl.BlockSpec((None, None, rows, cols), lambda i: (layer, half, 0, 0), pipeline_mode=pl.Buffered(1))

    sh_spec = pl.BlockSpec((None, None, None, 1, 2 * D_FF),
                           lambda i: (layer, half, jnp.minimum(i * FM // DEC_SEQ, DEC_BATCH), 0, 0))

    x_specs = [_SAMPLE_SPEC, _PROMPT_SPEC] if split_in else [_tok_spec(D_MODEL, FM)]
    if split_out:
        out_specs = [_SAMPLE_SPEC, _PROMPT_SPEC]
        out_shape = [jax.ShapeDtypeStruct((N_SAMPLE, D_MODEL), F32), jax.ShapeDtypeStruct((N_PROMPT, D_MODEL), F32)]
    else:
        out_specs = _tok_spec(D_MODEL, FM)
        out_shape = jax.ShapeDtypeStruct((N_TOK, D_MODEL), F32)
    return pl.pallas_call(
        functools.partial(_ffn_kernel, k=2 * half, split_in=split_in, split_out=split_out),
        grid=(N_TOK // FM,),
        in_specs=x_specs + [_mod_spec(FM), sh_spec, w_spec(D_MODEL, 2 * D_FF), w_spec(D_FF, D_MODEL)],
        out_specs=out_specs,
        out_shape=out_shape,
        compiler_params=_params("arbitrary"),
        name="ffn",
    )(*(xs if split_in else (xs,)), mod, shift_terms, w_in, w_out)


def _gmlp_kernel(x_ref, mod_ref, win_ref, vg_ref, ws_ref, bs_ref, wout_ref, o_ref):
    x = x_ref[...]
    hb = _ada(x, mod_ref, 1).astype(BF)
    pre = _dot(hb, win_ref[...])
    uv = 0.5 * pre * (1.0 + lax.erf(pre * math.sqrt(0.5)))
    u = uv[:, :A_WIDTH]
    v = (_rms(uv[:, A_WIDTH:]) * vg_ref[...]).astype(BF)
    bias = bs_ref[...]
    rows = []
    for c in range(FM // A_CHUNK):
        cols = [_dot(ws_ref[g], v[c * A_CHUNK:(c + 1) * A_CHUNK, g * LANES:(g + 1) * LANES])
                for g in range(A_GROUPS)]
        rows.append(jnp.concatenate(cols, axis=1) + bias)
    sv = jnp.concatenate(rows, axis=0)
    y = _dot((u * sv).astype(BF), wout_ref[...])
    o_ref[...] = x + mod_ref[5:6, :] * y


def _gmlp(x, mod, w_in, v_gain, w_s, b_s, w_out):
    bias = jnp.repeat(b_s.T, A_WIDTH // A_GROUPS, axis=1)
    return pl.pallas_call(
        _gmlp_kernel,
        grid=(N_TOK // FM,),
        in_specs=[_tok_spec(D_MODEL, FM), _mod_spec(FM),
                  _const_spec((D_MODEL, 2 * A_WIDTH)), _const_spec((1, A_WIDTH)),
                  _const_spec((A_GROUPS, A_CHUNK, A_CHUNK)), _const_spec((A_CHUNK, A_WIDTH)),
                  _const_spec((A_WIDTH, D_MODEL))],
        out_specs=_tok_spec(D_MODEL, FM),
        out_shape=jax.ShapeDtypeStruct((N_TOK, D_MODEL), F32),
        compiler_params=_params("arbitrary"),
        name="gmlp",
    )(x, mod, w_in.astype(BF), v_gain.reshape(1, A_WIDTH), w_s.astype(BF), bias, w_out.astype(BF))


def _swap_pairs(y, step):
    lane = lax.broadcasted_iota(jnp.int32, y.shape, 1)
    return jnp.where((lane & step) != 0, pltpu.roll(y, step, 1), pltpu.roll(y, LANES - step, 1))


def _rope_tables(rot_dim, lane_of_dim):
    quarter = rot_dim // 4
    inv = ROPE_BASE ** (-jnp.arange(quarter, dtype=F32) / quarter)
    t = jnp.arange(DEC_SEQ)
    row = (t // GRID_W).astype(F32)
    col = (t % GRID_W).astype(F32)
    ang = jnp.stack([row[:, None] * inv, col[:, None] * inv], axis=1)
    cos, sin = jnp.cos(ang), jnp.sin(ang)
    d = jnp.asarray(lane_of_dim)
    dd = jnp.maximum(d, 0)
    axis, member, freq = dd // (2 * quarter), (dd % (2 * quarter)) // quarter, dd % quarter
    rot = (d >= 0)[None, :]
    c_tab = jnp.where(rot, cos[:, axis, freq], 1.0)
    s_tab = jnp.where(rot, jnp.where(member == 0, -1.0, 1.0)[None, :] * sin[:, axis, freq], 0.0)
    ident_c = jnp.ones((TM, LANES), F32)
    ident_s = jnp.zeros((TM, LANES), F32)
    return jnp.concatenate([c_tab, ident_c], axis=0), jnp.concatenate([s_tab, ident_s], axis=0)


def _bproj_kernel(x_ref, mod_ref, w_ref, ones_ref, qc_ref, qs_ref, kc_ref, ks_ref, q_ref, k_ref, v_ref):
    hb = _ada(x_ref[...], mod_ref, 1).astype(BF)
    qkv = _dot(hb, w_ref[...])
    head_ones = ones_ref[...]

    def norm_rope(t4, tab, tab_swap):
        sq = t4 * t4
        hi = sq.astype(BF)
        lo = (sq - hi.astype(F32)).astype(BF)
        r = lax.rsqrt((_dot(hi, head_ones) + _dot(lo, head_ones)) / B_HEAD_DIM + EPS)
        halves = []
        for j in range(2):
            sl = slice(j * LANES, (j + 1) * LANES)
            t = t4[:, sl]
            halves.append(r[:, sl] * (t * tab + _swap_pairs(t, B_HEAD_DIM // 4) * tab_swap))
        return halves

    nq = B_HEADS * B_HEAD_DIM
    nk = B_KV_HEADS * B_HEAD_DIM
    q_tab, q_tab_swap = qc_ref[...], qs_ref[...]
    for j in range(nq // (2 * LANES)):
        halves = norm_rope(qkv[:, 2 * j * LANES:(2 * j + 2) * LANES], q_tab, q_tab_swap)
        q_ref[:, 2 * j * LANES:(2 * j + 1) * LANES] = halves[0].astype(BF)
        q_ref[:, (2 * j + 1) * LANES:(2 * j + 2) * LANES] = halves[1].astype(BF)
    halves = norm_rope(qkv[:, nq:nq + nk], kc_ref[...], ks_ref[...])
    k_ref[:, :LANES] = halves[0]
    k_ref[:, LANES:] = halves[1]
    v_ref[...] = qkv[:, nq + nk:]


def _bproj(x, mod, w_qkv, q_gain, k_gain, cos, sin):
    nq = B_HEADS * B_HEAD_DIM
    nk = B_KV_HEADS * B_HEAD_DIM
    q_scale = B_HEAD_DIM ** -0.5 * LOG2E
    lane = jnp.arange(LANES)
    qg = jnp.tile(q_gain, LANES // B_HEAD_DIM)
    kg = jnp.tile(k_gain, LANES // B_HEAD_DIM)
    partner = lane ^ (B_HEAD_DIM // 4)
    head = jnp.arange(nk) // B_HEAD_DIM
    head_ones = (head[:, None] == head[None, :]).astype(BF)
    return pl.pallas_call(
        _bproj_kernel,
        grid=(N_TILES,),
        in_specs=[_tok_spec(D_MODEL), _MOD_SPEC, _const_spec((D_MODEL, nq + 2 * nk)), _const_spec((nk, nk)),
                  _ROPE_SPEC, _ROPE_SPEC, _ROPE_SPEC, _ROPE_SPEC],
        out_specs=[_tok_spec(nq), _tok_spec(nk), _tok_spec(nk)],
        out_shape=[jax.ShapeDtypeStruct((N_TOK, nq), BF),
                   jax.ShapeDtypeStruct((N_TOK, nk), F32),
                   jax.ShapeDtypeStruct((N_TOK, nk), F32)],
        compiler_params=_params("arbitrary"),
        name="gqa_proj",
    )(x, mod, w_qkv.astype(BF), head_ones,
      cos * (qg * q_scale), sin * (qg[partner] * q_scale), cos * kg, sin * kg[partner])


def _gqa_attend(q, kcat, vcat, bias, sink_ref):
    tq = q.shape[0]
    nk = kcat.shape[0]
    lo = lax.broadcasted_iota(jnp.int32, (nk, LANES), 1) < B_HEAD_DIM
    lo_q = lax.broadcasted_iota(jnp.int32, (2 * tq, LANES), 1) < B_HEAD_DIM
    first = lax.broadcasted_iota(jnp.int32, (2 * tq, 1), 0) < tq
    if bias is not None:
        bias = jnp.concatenate([bias, bias], axis=0)
    outs = []
    for g in range(B_KV_HEADS):
        sl = slice((g // 2) * LANES, (g // 2 + 1) * LANES)
        own = lo if g % 2 == 0 else jnp.logical_not(lo)
        k_own = jnp.where(own, kcat[:, sl], 0.0)
        k_swp = pltpu.roll(k_own, B_HEAD_DIM, 1)
        v_own = jnp.where(own, vcat[:, sl], 1.0)
        v_swp = pltpu.roll(v_own, B_HEAD_DIM, 1)
        k_half = (k_own, k_swp) if g % 2 == 0 else (k_swp, k_own)
        v_half = (v_own, v_swp) if g % 2 == 0 else (v_swp, v_own)
        qg = jnp.concatenate([q[:, (2 * g) * LANES:(2 * g + 1) * LANES],
                              q[:, (2 * g + 1) * LANES:(2 * g + 2) * LANES]], axis=0)
        s_all = _dot_t(qg, jnp.concatenate(k_half, axis=0).astype(BF))
        o_half = []
        for e in range(2):
            s = s_all[:, e * nk:(e + 1) * nk]
            if bias is not None:
                nb = bias.shape[1]
                s = jnp.concatenate([s[:, :nb] + bias, s[:, nb:]], axis=1)
            sk = jnp.where(first, sink_ref[4 * g + e], sink_ref[4 * g + 2 + e]) * LOG2E
            m = jnp.maximum(jnp.max(s, axis=-1, keepdims=True), sk)
            ov = _dot(jnp.exp2(s - m).astype(BF), v_half[e].astype(BF))
            o_half.append(ov / (pltpu.roll(ov, B_HEAD_DIM, 1) + jnp.exp2(sk - m)))
        o_g = jnp.where(lo_q, o_half[0], o_half[1])
        outs += [o_g[:tq], o_g[tq:]]
    return jnp.concatenate(outs, axis=1)


def _battn_lat_kernel(x_ref, mod_ref, q_ref, kp_ref, kc_ref, kn_ref, vp_ref, vc_ref, vn_ref,
                      ck_ref, cv_ref, sink_ref, wo_ref, o_ref):
    j = pl.program_id(1)
    kcat = jnp.concatenate([kp_ref[...], kc_ref[...], kn_ref[...], ck_ref[...]], axis=0)
    vcat = jnp.concatenate([vp_ref[...], vc_ref[...], vn_ref[...], cv_ref[...]], axis=0)
    n_lat = BQ + 2 * B_WINDOW
    qi = lax.broadcasted_iota(jnp.int32, (BQ, n_lat), 0)
    pk = lax.broadcasted_iota(jnp.int32, (BQ, n_lat), 1)
    kpos = j * BQ + pk - B_WINDOW
    valid = (jnp.abs(pk - B_WINDOW - qi) <= B_WINDOW) & (kpos >= 0) & (kpos < DEC_SEQ)
    bias = jnp.where(valid, 0.0, NEG_INF)
    o = _gqa_attend(q_ref[...], kcat, vcat, bias, sink_ref)
    y = _dot(o.astype(BF), wo_ref[...])
    o_ref[...] = x_ref[...] + mod_ref[5:6, :] * y


def _battn_ctx_kernel(x_ref, mod_ref, q_ref, k_ref, v_ref, sink_ref, wo_ref, o_ref):
    o = _gqa_attend(q_ref[...], k_ref[...], v_ref[...], None, sink_ref)
    y = _dot(o.astype(BF), wo_ref[...])
    o_ref[...] = x_ref[...] + mod_ref[5:6, :] * y


def _battn(x, mod, q, k, v, cache_k, cache_v, sink, w_o):
    nq = B_HEADS * B_HEAD_DIM
    nk = B_KV_HEADS * B_HEAD_DIM
    nb = DEC_SEQ // BQ
    nw = DEC_SEQ // B_WINDOW
    per = BQ // B_WINDOW
    smem = pl.BlockSpec(memory_space=pltpu.SMEM)
    cur_spec = pl.BlockSpec((BQ, nk), lambda b, j: (b * nb + j, 0))
    prev_spec = pl.BlockSpec((B_WINDOW, nk), lambda b, j: (b * nw + jnp.maximum(per * j - 1, 0), 0))
    next_spec = pl.BlockSpec((B_WINDOW, nk), lambda b, j: (b * nw + jnp.minimum(per * j + per, nw - 1), 0))

    x = pl.pallas_call(
        _battn_lat_kernel,
        grid=(DEC_BATCH, nb),
        in_specs=[pl.BlockSpec((BQ, D_MODEL), lambda b, j: (b * nb + j, 0)),
                  pl.BlockSpec((None, N_MOD, D_MODEL), lambda b, j: (b, 0, 0)),
                  pl.BlockSpec((BQ, nq), lambda b, j: (b * nb + j, 0)),
                  prev_spec, cur_spec, next_spec, prev_spec, cur_spec, next_spec,
                  pl.BlockSpec((None, PAST_LEN, nk), lambda b, j: (b, 0, 0)),
                  pl.BlockSpec((None, PAST_LEN, nk), lambda b, j: (b, 0, 0)),
                  smem, _const_spec((nq, D_MODEL))],
        out_specs=pl.BlockSpec((BQ, D_MODEL), lambda b, j: (b * nb + j, 0)),
        out_shape=jax.ShapeDtypeStruct((N_TOK, D_MODEL), F32),
        input_output_aliases={0: 0},
        compiler_params=_params("arbitrary", "arbitrary"),
        name="gqa_attn_latent",
    )(x, mod, q, k, k, k, v, v, v, cache_k, cache_v, sink, w_o)
    off = N_SAMPLE // SEQ
    return pl.pallas_call(
        _battn_ctx_kernel,
        grid=(BATCH,),
        in_specs=[pl.BlockSpec((SEQ, D_MODEL), lambda b: (off + b, 0)),
                  pl.BlockSpec((None, N_MOD, D_MODEL), lambda b: (DEC_BATCH, 0, 0)),
                  pl.BlockSpec((SEQ, nq), lambda b: (off + b, 0)),
                  pl.BlockSpec((SEQ, nk), lambda b: (off + b, 0)),
                  pl.BlockSpec((SEQ, nk), lambda b: (off + b, 0)),
                  smem, _const_spec((nq, D_MODEL))],
        out_specs=pl.BlockSpec((SEQ, D_MODEL), lambda b: (off + b, 0)),
        out_shape=jax.ShapeDtypeStruct((N_TOK, D_MODEL), F32),
        input_output_aliases={0: 0},
        compiler_params=_params("arbitrary"),
        name="gqa_attn_context",
    )(x, mod, q, k, v, sink, w_o)


C_SWAP_W = C_HEADS * C_ROPE


def _mla_head_norm_rope(t, t_swap, tab, tab_swap):
    r = lax.rsqrt(jnp.sum(t * t, axis=-1, keepdims=True) / (C_NOPE + C_ROPE) + EPS)
    if t_swap is None:
        return t * r * tab
    return r * (t * tab + t_swap * tab_swap)


def _mla_keys_values(c_kv_b, k_rope, k_rope_swap, wukv_ref, tab, tab_swap, k_ref, v_ref):
    kv = _dot(c_kv_b, wukv_ref[...])
    lower = (lax.broadcasted_iota(jnp.int32, kv.shape, 1) & C_NOPE) == 0
    v_ref[...] = jnp.where(lower, 1.0, kv).astype(BF)
    k_nope = jnp.where(lower, kv, 0.0)
    for h in range(C_HEADS):
        sl = slice(h * C_HEAD_PAD, (h + 1) * C_HEAD_PAD)
        k_ref[:, sl] = _mla_head_norm_rope(k_nope[:, sl] + k_rope, k_rope_swap, tab, tab_swap).astype(BF)


def _cproj_kernel(x_ref, mod_ref, wd_ref, cqg_ref, ckvg_ref, wuq_ref, wukv_ref,
                  qc_ref, qs_ref, kc_ref, ks_ref, q_ref, k_ref, v_ref, ckv_ref, kr_ref):
    hb = _ada(x_ref[...], mod_ref, 1).astype(BF)
    d = _dot(hb, wd_ref[...])
    c_q = _rms(d[:, :C_Q_LORA]) * cqg_ref[...]
    c_kv = _rms(d[:, C_Q_LORA:C_Q_LORA + C_KV_LORA]) * ckvg_ref[...]
    k_rope = d[:, C_Q_LORA + C_KV_LORA:C_Q_LORA + C_KV_LORA + LANES]
    k_rope_swap = d[:, C_Q_LORA + C_KV_LORA + LANES:]
    ckv_ref[...] = c_kv
    kr_ref[...] = k_rope
    q2 = _dot(c_q.astype(BF), wuq_ref[...])
    wq = C_HEADS * C_HEAD_PAD
    q_tab, q_tab_swap = qc_ref[...], qs_ref[...]
    per_tile = LANES // C_ROPE
    for h in range(C_HEADS):
        sl = slice(h * C_HEAD_PAD, (h + 1) * C_HEAD_PAD)
        t_swap = q2[:, wq + (h // per_tile) * LANES:wq + (h // per_tile + 1) * LANES]
        shift = (C_NOPE - C_ROPE * (h % per_tile)) % LANES
        if shift:
            t_swap = pltpu.roll(t_swap, shift, 1)
        q_ref[:, sl] = _mla_head_norm_rope(q2[:, sl], t_swap, q_tab, q_tab_swap).astype(BF)
    _mla_keys_values(c_kv.astype(BF), k_rope, k_rope_swap, wukv_ref, kc_ref[...], ks_ref[...], k_ref, v_ref)


def _cctx_kernel(ckv_ref, kr_ref, wukv_ref, kg_ref, k_ref, v_ref):
    _mla_keys_values(ckv_ref[...].astype(BF), kr_ref[...], None, wukv_ref, kg_ref[...], None, k_ref, v_ref)


def _mla_weights(w_down, w_uq, q_gain, k_gain):
    hd = C_NOPE + C_ROPE
    pad_lanes = C_HEAD_PAD - hd
    lane = jnp.arange(C_HEAD_PAD)
    is_rope = (lane >= C_NOPE) & (lane < hd)
    partner = jnp.where(is_rope, lane ^ (C_ROPE // 4), lane)

    def swapped(t):
        return jnp.where(is_rope, jnp.take(t, partner, axis=-1), 0.0)

    kr_cols = jnp.pad(w_down[:, C_Q_LORA + C_KV_LORA:], ((0, 0), (C_NOPE, pad_lanes)))
    wd = jnp.concatenate([w_down[:, :C_Q_LORA + C_KV_LORA], kr_cols, swapped(kr_cols)], axis=1)
    wuq = jnp.pad(w_uq.reshape(C_Q_LORA, C_HEADS, hd), ((0, 0), (0, 0), (0, pad_lanes)))
    wuq_swap = swapped(wuq)[:, :, C_NOPE:hd].reshape(C_Q_LORA, C_SWAP_W)
    wuq = jnp.concatenate([wuq.reshape(C_Q_LORA, C_HEADS * C_HEAD_PAD), wuq_swap], axis=1)
    qg = jnp.pad(q_gain, (0, pad_lanes))
    kg = jnp.pad(k_gain, (0, pad_lanes))
    row = lambda t: t.reshape(1, C_HEAD_PAD)
    return wd.astype(BF), wuq.astype(BF), row(qg), row(swapped(qg)), row(kg), row(swapped(kg))


def _cproj(x, mod, wd, cq_gain, ckv_gain, wuq, wukv, q_tab, q_tab_swap, k_tab, k_tab_swap):
    wq = C_HEADS * C_HEAD_PAD
    return pl.pallas_call(
        _cproj_kernel,
        grid=(N_TILES,),
        in_specs=[_tok_spec(D_MODEL), _MOD_SPEC, _const_spec((D_MODEL, C_DOWN_PAD)),
                  _const_spec((1, C_Q_LORA)), _const_spec((1, C_KV_LORA)),
                  _const_spec((C_Q_LORA, wq + C_SWAP_W)), _const_spec((C_KV_LORA, wq)),
                  _ROPE_SPEC, _ROPE_SPEC, _ROPE_SPEC, _ROPE_SPEC],
        out_specs=[_tok_spec(wq), _tok_spec(wq), _tok_spec(wq), _tok_spec(C_KV_LORA), _tok_spec(LANES)],
        out_shape=[jax.ShapeDtypeStruct((N_TOK, wq), BF), jax.ShapeDtypeStruct((N_TOK, wq), BF),
                   jax.ShapeDtypeStruct((N_TOK, wq), BF), jax.ShapeDtypeStruct((N_TOK, C_KV_LORA), F32),
                   jax.ShapeDtypeStruct((N_TOK, LANES), F32)],
        compiler_params=_params("arbitrary"),
        name="mla_proj",
    )(x, mod, wd, cq_gain.reshape(1, C_Q_LORA), ckv_gain.reshape(1, C_KV_LORA), wuq, wukv,
      q_tab, q_tab_swap, k_tab, k_tab_swap)


def _cctx(cache_ckv, cache_krope, wukv, kg):
    n = DEC_BATCH * PAST_LEN
    wq = C_HEADS * C_HEAD_PAD
    kr = jnp.pad(cache_krope.reshape(n, C_ROPE), ((0, 0), (C_NOPE, C_HEAD_PAD - C_NOPE - C_ROPE)))
    return pl.pallas_call(
        _cctx_kernel,
        grid=(n // TM,),
        in_specs=[_tok_spec(C_KV_LORA), _tok_spec(LANES), _const_spec((C_KV_LORA, wq)),
                  _const_spec((1, C_HEAD_PAD))],
        out_specs=[_tok_spec(wq), _tok_spec(wq)],
        out_shape=[jax.ShapeDtypeStruct((n, wq), BF), jax.ShapeDtypeStruct((n, wq), BF)],
        compiler_params=_params("arbitrary"),
        name="mla_context_keys",
    )(cache_ckv.reshape(n, C_KV_LORA), kr, wukv, kg)


def _mla_attend(q_ref, kv_refs, o_ref, pair=0):
    tq = q_ref.shape[0]
    lo = lax.broadcasted_iota(jnp.int32, (tq, LANES), 1) < C_VDIM
    chunks = []
    for k_ref, v_ref in kv_refs:
        nk = k_ref.shape[0]
        if nk < CK and chunks:
            chunks[-1].append((k_ref, v_ref, slice(0, nk)))
        else:
            ck = min(CK, nk)
            chunks += [[(k_ref, v_ref, slice(c * ck, (c + 1) * ck))] for c in range(nk // ck)]
    sum_acc = []
    for e in range(2):
        sl = slice((2 * pair + e) * C_HEAD_PAD, (2 * pair + e + 1) * C_HEAD_PAD)
        qh = q_ref[:, sl]
        m = acc = None
        for parts in chunks:
            kk = [k_ref[rows, sl] for k_ref, _, rows in parts]
            vv = [v_ref[rows, sl] for _, v_ref, rows in parts]
            s = _dot_t(qh, kk[0] if len(kk) == 1 else jnp.concatenate(kk, axis=0))
            cm = jnp.max(s, axis=-1, keepdims=True)
            m_new = cm if m is None else jnp.maximum(m, cm)
            pv = _dot(jnp.exp2(s - m_new).astype(BF), vv[0] if len(vv) == 1 else jnp.concatenate(vv, axis=0))
            acc = pv if acc is None else jnp.exp2(m - m_new) * acc + pv
            m = m_new
        sum_acc.append(acc)
    r0 = pltpu.roll(sum_acc[0], C_VDIM, 1)
    r1 = pltpu.roll(sum_acc[1], C_VDIM, 1)
    o_ref[:, pair * LANES:(pair + 1) * LANES] = jnp.where(lo, r0 / sum_acc[0], sum_acc[1] / r1).astype(BF)


def _cattn_lat_kernel(q_ref, k_ref, v_ref, ck_ref, cv_ref, o_ref):
    _mla_attend(q_ref, ((k_ref, v_ref), (ck_ref, cv_ref)), o_ref)


def _cattn_ctx_kernel(q_ref, k_ref, v_ref, oin_ref, o_ref):
    del oin_ref
    for pair in range(C_HEADS // 2):
        _mla_attend(q_ref, ((k_ref, v_ref),), o_ref, pair)


def _cattn(q, k, v, ck, cv):
    pair_w = 2 * C_HEAD_PAD
    n_pairs = C_HEADS // 2
    nqt = DEC_SEQ // CQ
    o = pl.pallas_call(
        _cattn_lat_kernel,
        grid=(DEC_BATCH, n_pairs, nqt),
        in_specs=[pl.BlockSpec((CQ, pair_w), lambda b, p, t: (b * nqt + t, p)),
                  pl.BlockSpec((DEC_SEQ, pair_w), lambda b, p, t: (b, p)),
                  pl.BlockSpec((DEC_SEQ, pair_w), lambda b, p, t: (b, p)),
                  pl.BlockSpec((PAST_LEN, pair_w), lambda b, p, t: (b, p)),
                  pl.BlockSpec((PAST_LEN, pair_w), lambda b, p, t: (b, p))],
        out_specs=pl.BlockSpec((CQ, LANES), lambda b, p, t: (b * nqt + t, p)),
        out_shape=jax.ShapeDtypeStruct((N_TOK, C_HEADS * C_VDIM), BF),
        compiler_params=_params("arbitrary", "arbitrary", "arbitrary"),
        name="mla_attn_latent",
    )(q, k, v, ck, cv)
    off = N_SAMPLE // SEQ
    return pl.pallas_call(
        _cattn_ctx_kernel,
        grid=(BATCH,),
        in_specs=[pl.BlockSpec((SEQ, n_pairs * pair_w), lambda b: (off + b, 0)),
                  pl.BlockSpec((SEQ, n_pairs * pair_w), lambda b: (off + b, 0)),
                  pl.BlockSpec((SEQ, n_pairs * pair_w), lambda b: (off + b, 0)),
                  pl.BlockSpec(memory_space=pl.ANY)],
        out_specs=pl.BlockSpec((SEQ, n_pairs * LANES), lambda b: (off + b, 0)),
        out_shape=jax.ShapeDtypeStruct((N_TOK, C_HEADS * C_VDIM), BF),
        input_output_aliases={3: 0},
        compiler_params=_params("arbitrary"),
        name="mla_attn_context",
    )(q, k, v, o)


def _oproj_kernel(x_ref, mod_ref, o_ref, wo_ref, y_ref):
    y_ref[...] = x_ref[...] + mod_ref[5:6, :] * _dot(o_ref[...], wo_ref[...])


def _oproj(x, mod, o, w_o):
    width = o.shape[1]
    return pl.pallas_call(
        _oproj_kernel,
        grid=(N_TILES,),
        in_specs=[_tok_spec(D_MODEL), _MOD_SPEC, _tok_spec(width), _const_spec((width, D_MODEL))],
        out_specs=_tok_spec(D_MODEL),
        out_shape=jax.ShapeDtypeStruct((N_TOK, D_MODEL), F32),
        compiler_params=_params("arbitrary"),
        name="attn_out_proj",
    )(x, mod, o, w_o)


def kernel(x_prompt, x_sample, c, cache_win_k, cache_win_v, cache_mla_ckv, cache_mla_krope, c_ctx,
           ada_w, ada_b, ffn_w_in, ffn_w_out,
           gmlp_w_in, gmlp_v_gain, gmlp_w_s, gmlp_b_s, gmlp_w_out,
           win_w_qkv, win_q_gain, win_k_gain, win_sink, win_w_o,
           mla_w_down, mla_cq_gain, mla_ckv_gain, mla_w_uq, mla_w_ukv, mla_q_gain, mla_k_gain, mla_w_o):
    x = (x_sample.reshape(N_SAMPLE, D_MODEL), x_prompt.reshape(N_PROMPT, D_MODEL))
    cond = jnp.concatenate([c, c_ctx[None, :], jnp.zeros((N_COND - DEC_BATCH - 1, D_MODEL), F32)], axis=0)
    mods = _modulation(cond, ada_w, ada_b)
    w_in_b = ffn_w_in.astype(BF)
    w_out_b = ffn_w_out.astype(BF)
    shift_terms = _ffn_shift_terms(mods, w_in_b)

    lane = jnp.arange(LANES)
    b_cos, b_sin = _rope_tables(B_HEAD_DIM, lane % B_HEAD_DIM)
    c_lane = jnp.where((lane >= C_NOPE) & (lane < C_NOPE + C_ROPE), lane - C_NOPE, -1)
    c_cos, c_sin = _rope_tables(C_ROPE, c_lane)

    nk = B_KV_HEADS * B_HEAD_DIM
    win_k, win_v, mla_ckv, mla_krope = [], [], [], []
    ia = ib = ic = 0
    for l in range(DEPTH):
        mod = mods[l]
        x = _ffn(x, mod, shift_terms, w_in_b, w_out_b, l, 0, split_in=(l == 0))
        kind = l % N_MIXERS
        if kind == 0:
            x = _gmlp(x, mod, gmlp_w_in[ia], gmlp_v_gain[ia], gmlp_w_s[ia], gmlp_b_s[ia], gmlp_w_out[ia])
            ia += 1
        elif kind == 1:
            q, k, v = _bproj(x, mod, win_w_qkv[ib], win_q_gain[ib], win_k_gain[ib], b_cos, b_sin)
            x = _battn(x, mod, q, k, v,
                       cache_win_k[:, ib].reshape(DEC_BATCH, PAST_LEN, nk),
                       cache_win_v[:, ib].reshape(DEC_BATCH, PAST_LEN, nk),
                       win_sink[ib], win_w_o[ib].astype(BF))
            win_k.append(k[N_SAMPLE:].reshape(BATCH, SEQ, B_KV_HEADS, B_HEAD_DIM))
            win_v.append(v[N_SAMPLE:].reshape(BATCH, SEQ, B_KV_HEADS, B_HEAD_DIM))
            ib += 1
        else:
            wd, wuq, qg, qgs, kg, kgs = _mla_weights(mla_w_down[ic], mla_w_uq[ic], mla_q_gain[ic], mla_k_gain[ic])
            wukv = mla_w_ukv[ic].astype(BF)
            q_scale = (C_NOPE + C_ROPE) ** -0.5 * LOG2E
            q, k, v, ckv, kr = _cproj(x, mod, wd, mla_cq_gain[ic], mla_ckv_gain[ic], wuq, wukv,
                                      c_cos * (qg * q_scale), c_sin * (qgs * q_scale), c_cos * kg, c_sin * kgs)
            ck, cv = _cctx(cache_mla_ckv[:, ic], cache_mla_krope[:, ic], wukv, kg)
            o = _cattn(q, k, v, ck, cv)
            x = _oproj(x, mod, o, mla_w_o[ic].astype(BF))
            mla_ckv.append(ckv[N_SAMPLE:].reshape(BATCH, SEQ, C_KV_LORA))
            mla_krope.append(kr[N_SAMPLE:, C_NOPE:C_NOPE + C_ROPE].reshape(BATCH, SEQ, C_ROPE))
            ic += 1
        x = _ffn(x, mod, shift_terms, w_in_b, w_out_b, l, 1, split_out=(l == DEPTH - 1))
    y_sample, y_prompt = x
    return (y_prompt.reshape(BATCH, SEQ, D_MODEL), y_sample.reshape(DEC_BATCH, DEC_SEQ, D_MODEL),
            jnp.stack(win_k, axis=1), jnp.stack(win_v, axis=1),
            jnp.stack(mla_ckv, axis=1), jnp.stack(mla_krope, axis=1))
```

```python
import functools
import math

import jax
import jax.numpy as jnp
from jax import lax
from jax.experimental import pallas as pl
from jax.experimental.pallas import tpu as pltpu

D_MODEL = 1024
BATCH = 16
SEQ = 256
DEPTH = 4
DEC_BATCH = 8
DEC_SEQ = 4096
PAST_LEN = 256
GRID_W = 64
N_MIXERS = 3
N_MOD = 9
D_FF = 2816
A_WIDTH = D_MODEL
A_GROUPS = 8
A_CHUNK = 128
B_HEADS = 16
B_KV_HEADS = 4
B_HEAD_DIM = 64
B_WINDOW = 128
C_HEADS = 16
C_NOPE = 64
C_ROPE = 32
C_VDIM = 64
C_Q_LORA = 512
C_KV_LORA = 256
ROPE_BASE = 10000.0
EPS = 1e-6
NEG_INF = -1e30

LANES = 128
N_SAMPLE = DEC_BATCH * DEC_SEQ
N_PROMPT = BATCH * SEQ
N_TOK = N_SAMPLE + N_PROMPT
N_COND = 16
TM = 512
FM = 1024
FF_CHUNKS = ((0, 1536), (1536, 1280))
N_TILES = N_TOK // TM
N_SAMPLE_TILES = N_SAMPLE // TM
TILES_PER_SEQ = DEC_SEQ // TM
MOD_TN = 1536
SHIFT_TN = 1408
BQ = 256
CQ = 1024
CK = 2048
C_HEAD_PAD = 128
C_DOWN_PAD = C_Q_LORA + C_KV_LORA + 2 * LANES
VMEM_LIMIT_BYTES = 56 * 1024 * 1024

LOG2E = math.log2(math.e)

BF = jnp.bfloat16
F32 = jnp.float32


def _params(*sem):
    return pltpu.CompilerParams(dimension_semantics=sem, vmem_limit_bytes=VMEM_LIMIT_BYTES)


def _dot(a, b):
    return jnp.dot(a, b, preferred_element_type=F32)


def _dot_t(a, b):
    return lax.dot_general(a, b, (((1,), (1,)), ((), ())), preferred_element_type=F32)


def _rms(x):
    return x * lax.rsqrt(jnp.mean(x * x, axis=-1, keepdims=True) + EPS)


def _ada(x, mod_ref, k):
    shift = mod_ref[3 * k:3 * k + 1, :]
    scale = mod_ref[3 * k + 1:3 * k + 2, :]
    return _rms(x) * (1.0 + scale) + shift


def _const_spec(shape):
    nd = len(shape)
    return pl.BlockSpec(shape, lambda *_: (0,) * nd, pipeline_mode=pl.Buffered(1))


def _tok_spec(width, tm=TM):
    return pl.BlockSpec((tm, width), lambda i: (i, 0))


def _mod_spec(tm):
    return pl.BlockSpec((None, N_MOD, D_MODEL), lambda i: (jnp.minimum(i * tm // DEC_SEQ, DEC_BATCH), 0, 0))


_MOD_SPEC = _mod_spec(TM)


def _rope_tile(i):
    return jnp.where(i < N_SAMPLE_TILES, i % TILES_PER_SEQ, TILES_PER_SEQ)


_ROPE_SPEC = pl.BlockSpec((TM, LANES), lambda i: (_rope_tile(i), 0))


def _mod_kernel(c_ref, w_ref, b_ref, o_ref):
    a = jax.nn.silu(c_ref[...]).astype(BF)
    o_ref[...] = _dot(a, w_ref[...].astype(BF)) + b_ref[...]


def _modulation(cond, ada_w, ada_b):
    n_out = N_MOD * D_MODEL
    out = pl.pallas_call(
        _mod_kernel,
        grid=(DEPTH, n_out // MOD_TN),
        in_specs=[
            pl.BlockSpec((N_COND, D_MODEL), lambda l, j: (0, 0)),
            pl.BlockSpec((None, D_MODEL, MOD_TN), lambda l, j: (l, 0, j)),
            pl.BlockSpec((None, 1, MOD_TN), lambda l, j: (l, 0, j)),
        ],
        out_specs=pl.BlockSpec((None, N_COND, MOD_TN), lambda l, j: (l, 0, j)),
        out_shape=jax.ShapeDtypeStruct((DEPTH, N_COND, n_out), F32),
        compiler_params=_params("arbitrary", "arbitrary"),
        name="modulation",
    )(cond, ada_w, ada_b.reshape(DEPTH, 1, n_out))
    return out.reshape(DEPTH, N_COND, N_MOD, D_MODEL)


def _ffn_shift_kernel(s_ref, w_ref, o_ref):
    o_ref[...] = _dot(s_ref[...].astype(BF), w_ref[...])


def _ffn_shift_terms(mods, w_in):
    shifts = jnp.stack([mods[:, :, 0], mods[:, :, 6]], axis=1)
    out = pl.pallas_call(
        _ffn_shift_kernel,
        grid=(DEPTH, 2, 2 * D_FF // SHIFT_TN),
        in_specs=[pl.BlockSpec((None, None, N_COND, D_MODEL), lambda l, h, j: (l, h, 0, 0)),
                  pl.BlockSpec((None, None, D_MODEL, SHIFT_TN), lambda l, h, j: (l, h, 0, j))],
        out_specs=pl.BlockSpec((None, None, N_COND, SHIFT_TN), lambda l, h, j: (l, h, 0, j)),
        out_shape=jax.ShapeDtypeStruct((DEPTH, 2, N_COND, 2 * D_FF), F32),
        compiler_params=_params("arbitrary", "arbitrary", "arbitrary"),
        name="ffn_shift_terms",
    )(shifts, w_in)
    return out.reshape(DEPTH, 2, N_COND, 1, 2 * D_FF)


def _ffn_kernel(*refs, k, split_in, split_out, pre_proj):
    n_x = 2 if split_in else 1
    n_in = n_x + (2 if pre_proj else 0)
    x_refs, (mod_ref, sh_ref, win_ref, wout_ref), o_refs = refs[:n_x], refs[n_in:n_in + 4], refs[n_in + 4:]
    is_sample = pl.program_id(0) < N_SAMPLE // FM
    x = jnp.where(is_sample, x_refs[0][...], x_refs[1][...]) if split_in else x_refs[0][...]
    if pre_proj:
        attn_ref, wo_ref = refs[n_x:n_in]
        x = x + mod_ref[5:6, :] * _dot(attn_ref[...], wo_ref[...])
    xa = (x * (1.0 + mod_ref[3 * k + 1:3 * k + 2, :])).astype(BF)
    rinv = lax.rsqrt(jnp.mean(x * x, axis=-1, keepdims=True) + EPS)
    y = None
    for c0, cw in FF_CHUNKS:
        g = _dot(xa, win_ref[:, c0:c0 + cw]) * rinv + sh_ref[:, c0:c0 + cw]
        u = _dot(xa, win_ref[:, D_FF + c0:D_FF + c0 + cw]) * rinv + sh_ref[:, D_FF + c0:D_FF + c0 + cw]
        yc = _dot((jax.nn.silu(g) * u).astype(BF), wout_ref[c0:c0 + cw, :])
        y = yc if y is None else y + yc
    gate = mod_ref[3 * k + 2:3 * k + 3, :]
    out = x + (0.5 * gate) * y
    if split_out:
        o_refs[1][...] = out

        @pl.when(is_sample)
        def _():
            o_refs[0][...] = out
    else:
        o_refs[0][...] = out


_SAMPLE_SPEC = pl.BlockSpec((FM, D_MODEL), lambda i: (jnp.minimum(i, N_SAMPLE // FM - 1), 0))
_PROMPT_SPEC = pl.BlockSpec((FM, D_MODEL), lambda i: (jnp.maximum(i - N_SAMPLE // FM, 0), 0))


def _ffn(xs, mod, shift_terms, w_in, w_out, layer, half, split_in=False, split_out=False, pre_proj=None):
    def w_spec(rows, cols):
        return pl.BlockSpec((None, None, rows, cols), lambda i: (layer, half, 0, 0), pipeline_mode=pl.Buffered(1))

    sh_spec = pl.BlockSpec((None, None, None, 1, 2 * D_FF),
                           lambda i: (layer, half, jnp.minimum(i * FM // DEC_SEQ, DEC_BATCH), 0, 0))

    x_specs = [_SAMPLE_SPEC, _PROMPT_SPEC] if split_in else [_tok_spec(D_MODEL, FM)]
    operands = list(xs) if split_in else [xs]
    if pre_proj is not None:
        x_specs += [_tok_spec(D_MODEL, FM), _const_spec((D_MODEL, D_MODEL))]
        operands += list(pre_proj)
    if split_out:
        out_specs = [_SAMPLE_SPEC, _PROMPT_SPEC]
        out_shape = [jax.ShapeDtypeStruct((N_SAMPLE, D_MODEL), F32), jax.ShapeDtypeStruct((N_PROMPT, D_MODEL), F32)]
    else:
        out_specs = _tok_spec(D_MODEL, FM)
        out_shape = jax.ShapeDtypeStruct((N_TOK, D_MODEL), F32)
    return pl.pallas_call(
        functools.partial(_ffn_kernel, k=2 * half, split_in=split_in, split_out=split_out,
                          pre_proj=pre_proj is not None),
        grid=(N_TOK // FM,),
        in_specs=x_specs + [_mod_spec(FM), sh_spec, w_spec(D_MODEL, 2 * D_FF), w_spec(D_FF, D_MODEL)],
        out_specs=out_specs,
        out_shape=out_shape,
        compiler_params=_params("arbitrary"),
        name="ffn",
    )(*operands, mod, shift_terms, w_in, w_out)


def _gmlp_kernel(x_ref, mod_ref, win_ref, vg_ref, ws_ref, bs_ref, wout_ref, o_ref):
    x = x_ref[...]
    hb = _ada(x, mod_ref, 1).astype(BF)
    pre = _dot(hb, win_ref[...])
    uv = 0.5 * pre * (1.0 + lax.erf(pre * math.sqrt(0.5)))
    u = uv[:, :A_WIDTH]
    v = (_rms(uv[:, A_WIDTH:]) * vg_ref[...]).astype(BF)
    bias = bs_ref[...]
    rows = []
    for c in range(FM // A_CHUNK):
        cols = [_dot(ws_ref[g], v[c * A_CHUNK:(c + 1) * A_CHUNK, g * LANES:(g + 1) * LANES])
                for g in range(A_GROUPS)]
        rows.append(jnp.concatenate(cols, axis=1) + bias)
    sv = jnp.concatenate(rows, axis=0)
    y = _dot((u * sv).astype(BF), wout_ref[...])
    o_ref[...] = x + mod_ref[5:6, :] * y


def _gmlp(x, mod, w_in, v_gain, w_s, b_s, w_out):
    bias = jnp.repeat(b_s.T, A_WIDTH // A_GROUPS, axis=1)
    return pl.pallas_call(
        _gmlp_kernel,
        grid=(N_TOK // FM,),
        in_specs=[_tok_spec(D_MODEL, FM), _mod_spec(FM),
                  _const_spec((D_MODEL, 2 * A_WIDTH)), _const_spec((1, A_WIDTH)),
                  _const_spec((A_GROUPS, A_CHUNK, A_CHUNK)), _const_spec((A_CHUNK, A_WIDTH)),
                  _const_spec((A_WIDTH, D_MODEL))],
        out_specs=_tok_spec(D_MODEL, FM),
        out_shape=jax.ShapeDtypeStruct((N_TOK, D_MODEL), F32),
        compiler_params=_params("arbitrary"),
        name="gmlp",
    )(x, mod, w_in.astype(BF), v_gain.reshape(1, A_WIDTH), w_s.astype(BF), bias, w_out.astype(BF))


def _swap_pairs(y, step):
    lane = lax.broadcasted_iota(jnp.int32, y.shape, 1)
    return jnp.where((lane & step) != 0, pltpu.roll(y, step, 1), pltpu.roll(y, LANES - step, 1))


def _rope_tables(rot_dim, lane_of_dim):
    quarter = rot_dim // 4
    inv = ROPE_BASE ** (-jnp.arange(quarter, dtype=F32) / quarter)
    t = jnp.arange(DEC_SEQ)
    row = (t // GRID_W).astype(F32)
    col = (t % GRID_W).astype(F32)
    ang = jnp.stack([row[:, None] * inv, col[:, None] * inv], axis=1)
    cos, sin = jnp.cos(ang), jnp.sin(ang)
    d = jnp.asarray(lane_of_dim)
    dd = jnp.maximum(d, 0)
    axis, member, freq = dd // (2 * quarter), (dd % (2 * quarter)) // quarter, dd % quarter
    rot = (d >= 0)[None, :]
    c_tab = jnp.where(rot, cos[:, axis, freq], 1.0)
    s_tab = jnp.where(rot, jnp.where(member == 0, -1.0, 1.0)[None, :] * sin[:, axis, freq], 0.0)
    ident_c = jnp.ones((TM, LANES), F32)
    ident_s = jnp.zeros((TM, LANES), F32)
    return jnp.concatenate([c_tab, ident_c], axis=0), jnp.concatenate([s_tab, ident_s], axis=0)


def _bproj_kernel(x_ref, mod_ref, w_ref, ones_ref, qc_ref, qs_ref, kc_ref, ks_ref, q_ref, k_ref, v_ref):
    hb = _ada(x_ref[...], mod_ref, 1).astype(BF)
    qkv = _dot(hb, w_ref[...])
    head_ones = ones_ref[...]

    def norm_rope(t4, tab, tab_swap):
        sq = t4 * t4
        hi = sq.astype(BF)
        lo = (sq - hi.astype(F32)).astype(BF)
        r = lax.rsqrt((_dot(hi, head_ones) + _dot(lo, head_ones)) / B_HEAD_DIM + EPS)
        halves = []
        for j in range(2):
            sl = slice(j * LANES, (j + 1) * LANES)
            t = t4[:, sl]
            halves.append(r[:, sl] * (t * tab + _swap_pairs(t, B_HEAD_DIM // 4) * tab_swap))
        return halves

    nq = B_HEADS * B_HEAD_DIM
    nk = B_KV_HEADS * B_HEAD_DIM
    q_tab, q_tab_swap = qc_ref[...], qs_ref[...]
    for j in range(nq // (2 * LANES)):
        halves = norm_rope(qkv[:, 2 * j * LANES:(2 * j + 2) * LANES], q_tab, q_tab_swap)
        q_ref[:, 2 * j * LANES:(2 * j + 1) * LANES] = halves[0].astype(BF)
        q_ref[:, (2 * j + 1) * LANES:(2 * j + 2) * LANES] = halves[1].astype(BF)
    halves = norm_rope(qkv[:, nq:nq + nk], kc_ref[...], ks_ref[...])
    k_ref[:, :LANES] = halves[0]
    k_ref[:, LANES:] = halves[1]
    v_ref[...] = qkv[:, nq + nk:]


def _bproj(x, mod, w_qkv, q_gain, k_gain, cos, sin):
    nq = B_HEADS * B_HEAD_DIM
    nk = B_KV_HEADS * B_HEAD_DIM
    q_scale = B_HEAD_DIM ** -0.5 * LOG2E
    lane = jnp.arange(LANES)
    qg = jnp.tile(q_gain, LANES // B_HEAD_DIM)
    kg = jnp.tile(k_gain, LANES // B_HEAD_DIM)
    partner = lane ^ (B_HEAD_DIM // 4)
    head = jnp.arange(nk) // B_HEAD_DIM
    head_ones = (head[:, None] == head[None, :]).astype(BF)
    return pl.pallas_call(
        _bproj_kernel,
        grid=(N_TILES,),
        in_specs=[_tok_spec(D_MODEL), _MOD_SPEC, _const_spec((D_MODEL, nq + 2 * nk)), _const_spec((nk, nk)),
                  _ROPE_SPEC, _ROPE_SPEC, _ROPE_SPEC, _ROPE_SPEC],
        out_specs=[_tok_spec(nq), _tok_spec(nk), _tok_spec(nk)],
        out_shape=[jax.ShapeDtypeStruct((N_TOK, nq), BF),
                   jax.ShapeDtypeStruct((N_TOK, nk), F32),
                   jax.ShapeDtypeStruct((N_TOK, nk), F32)],
        compiler_params=_params("arbitrary"),
        name="gqa_proj",
    )(x, mod, w_qkv.astype(BF), head_ones,
      cos * (qg * q_scale), sin * (qg[partner] * q_scale), cos * kg, sin * kg[partner])


def _gqa_attend(q, kcat, vcat, bias, sink_ref):
    tq = q.shape[0]
    nk = kcat.shape[0]
    lo = lax.broadcasted_iota(jnp.int32, (nk, LANES), 1) < B_HEAD_DIM
    lo_q = lax.broadcasted_iota(jnp.int32, (2 * tq, LANES), 1) < B_HEAD_DIM
    first = lax.broadcasted_iota(jnp.int32, (2 * tq, 1), 0) < tq
    if bias is not None:
        bias = jnp.concatenate([bias, bias], axis=0)
    outs = []
    for g in range(B_KV_HEADS):
        sl = slice((g // 2) * LANES, (g // 2 + 1) * LANES)
        own = lo if g % 2 == 0 else jnp.logical_not(lo)
        k_own = jnp.where(own, kcat[:, sl], 0.0)
        k_swp = pltpu.roll(k_own, B_HEAD_DIM, 1)
        v_own = jnp.where(own, vcat[:, sl], 1.0)
        v_swp = pltpu.roll(v_own, B_HEAD_DIM, 1)
        k_half = (k_own, k_swp) if g % 2 == 0 else (k_swp, k_own)
        v_half = (v_own, v_swp) if g % 2 == 0 else (v_swp, v_own)
        qg = jnp.concatenate([q[:, (2 * g) * LANES:(2 * g + 1) * LANES],
                              q[:, (2 * g + 1) * LANES:(2 * g + 2) * LANES]], axis=0)
        s_all = _dot_t(qg, jnp.concatenate(k_half, axis=0).astype(BF))
        o_half = []
        for e in range(2):
            s = s_all[:, e * nk:(e + 1) * nk]
            if bias is not None:
                nb = bias.shape[1]
                s = jnp.concatenate([s[:, :nb] + bias, s[:, nb:]], axis=1)
            sk = jnp.where(first, sink_ref[4 * g + e], sink_ref[4 * g + 2 + e]) * LOG2E
            m = jnp.maximum(jnp.max(s, axis=-1, keepdims=True), sk)
            ov = _dot(jnp.exp2(s - m).astype(BF), v_half[e].astype(BF))
            o_half.append(ov / (pltpu.roll(ov, B_HEAD_DIM, 1) + jnp.exp2(sk - m)))
        o_g = jnp.where(lo_q, o_half[0], o_half[1])
        outs += [o_g[:tq], o_g[tq:]]
    return jnp.concatenate(outs, axis=1)


def _battn_lat_kernel(x_ref, mod_ref, q_ref, kp_ref, kc_ref, kn_ref, vp_ref, vc_ref, vn_ref,
                      ck_ref, cv_ref, sink_ref, wo_ref, o_ref):
    j = pl.program_id(1)
    kcat = jnp.concatenate([kp_ref[...], kc_ref[...], kn_ref[...], ck_ref[...]], axis=0)
    vcat = jnp.concatenate([vp_ref[...], vc_ref[...], vn_ref[...], cv_ref[...]], axis=0)
    n_lat = BQ + 2 * B_WINDOW
    qi = lax.broadcasted_iota(jnp.int32, (BQ, n_lat), 0)
    pk = lax.broadcasted_iota(jnp.int32, (BQ, n_lat), 1)
    kpos = j * BQ + pk - B_WINDOW
    valid = (jnp.abs(pk - B_WINDOW - qi) <= B_WINDOW) & (kpos >= 0) & (kpos < DEC_SEQ)
    bias = jnp.where(valid, 0.0, NEG_INF)
    o = _gqa_attend(q_ref[...], kcat, vcat, bias, sink_ref)
    y = _dot(o.astype(BF), wo_ref[...])
    o_ref[...] = x_ref[...] + mod_ref[5:6, :] * y


def _battn_ctx_kernel(x_ref, mod_ref, q_ref, k_ref, v_ref, sink_ref, wo_ref, o_ref):
    o = _gqa_attend(q_ref[...], k_ref[...], v_ref[...], None, sink_ref)
    y = _dot(o.astype(BF), wo_ref[...])
    o_ref[...] = x_ref[...] + mod_ref[5:6, :] * y


def _battn(x, mod, q, k, v, cache_k, cache_v, sink, w_o):
    nq = B_HEADS * B_HEAD_DIM
    nk = B_KV_HEADS * B_HEAD_DIM
    nb = DEC_SEQ // BQ
    nw = DEC_SEQ // B_WINDOW
    per = BQ // B_WINDOW
    smem = pl.BlockSpec(memory_space=pltpu.SMEM)
    cur_spec = pl.BlockSpec((BQ, nk), lambda b, j: (b * nb + j, 0))
    prev_spec = pl.BlockSpec((B_WINDOW, nk), lambda b, j: (b * nw + jnp.maximum(per * j - 1, 0), 0))
    next_spec = pl.BlockSpec((B_WINDOW, nk), lambda b, j: (b * nw + jnp.minimum(per * j + per, nw - 1), 0))

    x = pl.pallas_call(
        _battn_lat_kernel,
        grid=(DEC_BATCH, nb),
        in_specs=[pl.BlockSpec((BQ, D_MODEL), lambda b, j: (b * nb + j, 0)),
                  pl.BlockSpec((None, N_MOD, D_MODEL), lambda b, j: (b, 0, 0)),
                  pl.BlockSpec((BQ, nq), lambda b, j: (b * nb + j, 0)),
                  prev_spec, cur_spec, next_spec, prev_spec, cur_spec, next_spec,
                  pl.BlockSpec((None, PAST_LEN, nk), lambda b, j: (b, 0, 0)),
                  pl.BlockSpec((None, PAST_LEN, nk), lambda b, j: (b, 0, 0)),
                  smem, _const_spec((nq, D_MODEL))],
        out_specs=pl.BlockSpec((BQ, D_MODEL), lambda b, j: (b * nb + j, 0)),
        out_shape=jax.ShapeDtypeStruct((N_TOK, D_MODEL), F32),
        input_output_aliases={0: 0},
        compiler_params=_params("arbitrary", "arbitrary"),
        name="gqa_attn_latent",
    )(x, mod, q, k, k, k, v, v, v, cache_k, cache_v, sink, w_o)
    off = N_SAMPLE // SEQ
    return pl.pallas_call(
        _battn_ctx_kernel,
        grid=(BATCH,),
        in_specs=[pl.BlockSpec((SEQ, D_MODEL), lambda b: (off + b, 0)),
                  pl.BlockSpec((None, N_MOD, D_MODEL), lambda b: (DEC_BATCH, 0, 0)),
                  pl.BlockSpec((SEQ, nq), lambda b: (off + b, 0)),
                  pl.BlockSpec((SEQ, nk), lambda b: (off + b, 0)),
                  pl.BlockSpec((SEQ, nk), lambda b: (off + b, 0)),
                  smem, _const_spec((nq, D_MODEL))],
        out_specs=pl.BlockSpec((SEQ, D_MODEL), lambda b: (off + b, 0)),
        out_shape=jax.ShapeDtypeStruct((N_TOK, D_MODEL), F32),
        input_output_aliases={0: 0},
        compiler_params=_params("arbitrary"),
        name="gqa_attn_context",
    )(x, mod, q, k, v, sink, w_o)


C_SWAP_W = C_HEADS * C_ROPE


def _mla_head_norm_rope(t, t_swap, tab, tab_swap):
    r = lax.rsqrt(jnp.sum(t * t, axis=-1, keepdims=True) / (C_NOPE + C_ROPE) + EPS)
    if t_swap is None:
        return t * r * tab
    return r * (t * tab + t_swap * tab_swap)


def _mla_keys_values(c_kv_b, k_rope, k_rope_swap, wukv_ref, tab, tab_swap, k_ref, v_ref):
    kv = _dot(c_kv_b, wukv_ref[...])
    lower = (lax.broadcasted_iota(jnp.int32, kv.shape, 1) & C_NOPE) == 0
    v_ref[...] = jnp.where(lower, 1.0, kv).astype(BF)
    k_nope = jnp.where(lower, kv, 0.0)
    for h in range(C_HEADS):
        sl = slice(h * C_HEAD_PAD, (h + 1) * C_HEAD_PAD)
        k_ref[:, sl] = _mla_head_norm_rope(k_nope[:, sl] + k_rope, k_rope_swap, tab, tab_swap).astype(BF)


def _cproj_kernel(x_ref, mod_ref, wd_ref, cqg_ref, ckvg_ref, wuq_ref, wukv_ref,
                  qc_ref, qs_ref, kc_ref, ks_ref, q_ref, k_ref, v_ref, ckv_ref, kr_ref):
    hb = _ada(x_ref[...], mod_ref, 1).astype(BF)
    d = _dot(hb, wd_ref[...])
    c_q = _rms(d[:, :C_Q_LORA]) * cqg_ref[...]
    c_kv = _rms(d[:, C_Q_LORA:C_Q_LORA + C_KV_LORA]) * ckvg_ref[...]
    k_rope = d[:, C_Q_LORA + C_KV_LORA:C_Q_LORA + C_KV_LORA + LANES]
    k_rope_swap = d[:, C_Q_LORA + C_KV_LORA + LANES:]
    ckv_ref[...] = c_kv
    kr_ref[...] = k_rope
    q2 = _dot(c_q.astype(BF), wuq_ref[...])
    wq = C_HEADS * C_HEAD_PAD
    q_tab, q_tab_swap = qc_ref[...], qs_ref[...]
    per_tile = LANES // C_ROPE
    for h in range(C_HEADS):
        sl = slice(h * C_HEAD_PAD, (h + 1) * C_HEAD_PAD)
        t_swap = q2[:, wq + (h // per_tile) * LANES:wq + (h // per_tile + 1) * LANES]
        shift = (C_NOPE - C_ROPE * (h % per_tile)) % LANES
        if shift:
            t_swap = pltpu.roll(t_swap, shift, 1)
        q_ref[:, sl] = _mla_head_norm_rope(q2[:, sl], t_swap, q_tab, q_tab_swap).astype(BF)
    _mla_keys_values(c_kv.astype(BF), k_rope, k_rope_swap, wukv_ref, kc_ref[...], ks_ref[...], k_ref, v_ref)


def _cctx_kernel(ckv_ref, kr_ref, wukv_ref, kg_ref, k_ref, v_ref):
    _mla_keys_values(ckv_ref[...].astype(BF), kr_ref[...], None, wukv_ref, kg_ref[...], None, k_ref, v_ref)


def _mla_weights(w_down, w_uq, q_gain, k_gain):
    hd = C_NOPE + C_ROPE
    pad_lanes = C_HEAD_PAD - hd
    lane = jnp.arange(C_HEAD_PAD)
    is_rope = (lane >= C_NOPE) & (lane < hd)
    partner = jnp.where(is_rope, lane ^ (C_ROPE // 4), lane)

    def swapped(t):
        return jnp.where(is_rope, jnp.take(t, partner, axis=-1), 0.0)

    kr_cols = jnp.pad(w_down[:, C_Q_LORA + C_KV_LORA:], ((0, 0), (C_NOPE, pad_lanes)))
    wd = jnp.concatenate([w_down[:, :C_Q_LORA + C_KV_LORA], kr_cols, swapped(kr_cols)], axis=1)
    wuq = jnp.pad(w_uq.reshape(C_Q_LORA, C_HEADS, hd), ((0, 0), (0, 0), (0, pad_lanes)))
    wuq_swap = swapped(wuq)[:, :, C_NOPE:hd].reshape(C_Q_LORA, C_SWAP_W)
    wuq = jnp.concatenate([wuq.reshape(C_Q_LORA, C_HEADS * C_HEAD_PAD), wuq_swap], axis=1)
    qg = jnp.pad(q_gain, (0, pad_lanes))
    kg = jnp.pad(k_gain, (0, pad_lanes))
    row = lambda t: t.reshape(1, C_HEAD_PAD)
    return wd.astype(BF), wuq.astype(BF), row(qg), row(swapped(qg)), row(kg), row(swapped(kg))


def _cproj(x, mod, wd, cq_gain, ckv_gain, wuq, wukv, q_tab, q_tab_swap, k_tab, k_tab_swap):
    wq = C_HEADS * C_HEAD_PAD
    return pl.pallas_call(
        _cproj_kernel,
        grid=(N_TILES,),
        in_specs=[_tok_spec(D_MODEL), _MOD_SPEC, _const_spec((D_MODEL, C_DOWN_PAD)),
                  _const_spec((1, C_Q_LORA)), _const_spec((1, C_KV_LORA)),
                  _const_spec((C_Q_LORA, wq + C_SWAP_W)), _const_spec((C_KV_LORA, wq)),
                  _ROPE_SPEC, _ROPE_SPEC, _ROPE_SPEC, _ROPE_SPEC],
        out_specs=[_tok_spec(wq), _tok_spec(wq), _tok_spec(wq), _tok_spec(C_KV_LORA), _tok_spec(LANES)],
        out_shape=[jax.ShapeDtypeStruct((N_TOK, wq), BF), jax.ShapeDtypeStruct((N_TOK, wq), BF),
                   jax.ShapeDtypeStruct((N_TOK, wq), BF), jax.ShapeDtypeStruct((N_TOK, C_KV_LORA), F32),
                   jax.ShapeDtypeStruct((N_TOK, LANES), F32)],
        compiler_params=_params("arbitrary"),
        name="mla_proj",
    )(x, mod, wd, cq_gain.reshape(1, C_Q_LORA), ckv_gain.reshape(1, C_KV_LORA), wuq, wukv,
      q_tab, q_tab_swap, k_tab, k_tab_swap)


def _cctx(cache_ckv, cache_krope, wukv, kg):
    n = DEC_BATCH * PAST_LEN
    wq = C_HEADS * C_HEAD_PAD
    kr = jnp.pad(cache_krope.reshape(n, C_ROPE), ((0, 0), (C_NOPE, C_HEAD_PAD - C_NOPE - C_ROPE)))
    return pl.pallas_call(
        _cctx_kernel,
        grid=(n // TM,),
        in_specs=[_tok_spec(C_KV_LORA), _tok_spec(LANES), _const_spec((C_KV_LORA, wq)),
                  _const_spec((1, C_HEAD_PAD))],
        out_specs=[_tok_spec(wq), _tok_spec(wq)],
        out_shape=[jax.ShapeDtypeStruct((n, wq), BF), jax.ShapeDtypeStruct((n, wq), BF)],
        compiler_params=_params("arbitrary"),
        name="mla_context_keys",
    )(cache_ckv.reshape(n, C_KV_LORA), kr, wukv, kg)


def _mla_attend(q_ref, kv_refs, o_ref, pair=0):
    tq = q_ref.shape[0]
    lo = lax.broadcasted_iota(jnp.int32, (tq, LANES), 1) < C_VDIM
    chunks = []
    for k_ref, v_ref in kv_refs:
        nk = k_ref.shape[0]
        if nk < CK and chunks:
            chunks[-1].append((k_ref, v_ref, slice(0, nk)))
        else:
            ck = min(CK, nk)
            chunks += [[(k_ref, v_ref, slice(c * ck, (c + 1) * ck))] for c in range(nk // ck)]
    sum_acc = []
    for e in range(2):
        sl = slice((2 * pair + e) * C_HEAD_PAD, (2 * pair + e + 1) * C_HEAD_PAD)
        qh = q_ref[:, sl]
        m = acc = None
        for parts in chunks:
            kk = [k_ref[rows, sl] for k_ref, _, rows in parts]
            vv = [v_ref[rows, sl] for _, v_ref, rows in parts]
            s = _dot_t(qh, kk[0] if len(kk) == 1 else jnp.concatenate(kk, axis=0))
            cm = jnp.max(s, axis=-1, keepdims=True)
            m_new = cm if m is None else jnp.maximum(m, cm)
            pv = _dot(jnp.exp2(s - m_new).astype(BF), vv[0] if len(vv) == 1 else jnp.concatenate(vv, axis=0))
            acc = pv if acc is None else jnp.exp2(m - m_new) * acc + pv
            m = m_new
        sum_acc.append(acc)
    r0 = pltpu.roll(sum_acc[0], C_VDIM, 1)
    r1 = pltpu.roll(sum_acc[1], C_VDIM, 1)
    o_ref[:, pair * LANES:(pair + 1) * LANES] = jnp.where(lo, r0 / sum_acc[0], sum_acc[1] / r1).astype(BF)


def _cattn_lat_kernel(q_ref, k_ref, v_ref, ck_ref, cv_ref, o_ref):
    _mla_attend(q_ref, ((k_ref, v_ref), (ck_ref, cv_ref)), o_ref)


def _cattn_ctx_kernel(q_ref, k_ref, v_ref, oin_ref, o_ref):
    del oin_ref
    for pair in range(C_HEADS // 2):
        _mla_attend(q_ref, ((k_ref, v_ref),), o_ref, pair)


def _cattn(q, k, v, ck, cv):
    pair_w = 2 * C_HEAD_PAD
    n_pairs = C_HEADS // 2
    nqt = DEC_SEQ // CQ
    o = pl.pallas_call(
        _cattn_lat_kernel,
        grid=(DEC_BATCH, n_pairs, nqt),
        in_specs=[pl.BlockSpec((CQ, pair_w), lambda b, p, t: (b * nqt + t, p)),
                  pl.BlockSpec((DEC_SEQ, pair_w), lambda b, p, t: (b, p)),
                  pl.BlockSpec((DEC_SEQ, pair_w), lambda b, p, t: (b, p)),
                  pl.BlockSpec((PAST_LEN, pair_w), lambda b, p, t: (b, p)),
                  pl.BlockSpec((PAST_LEN, pair_w), lambda b, p, t: (b, p))],
        out_specs=pl.BlockSpec((CQ, LANES), lambda b, p, t: (b * nqt + t, p)),
        out_shape=jax.ShapeDtypeStruct((N_TOK, C_HEADS * C_VDIM), BF),
        compiler_params=_params("arbitrary", "arbitrary", "arbitrary"),
        name="mla_attn_latent",
    )(q, k, v, ck, cv)
    off = N_SAMPLE // SEQ
    return pl.pallas_call(
        _cattn_ctx_kernel,
        grid=(BATCH,),
        in_specs=[pl.BlockSpec((SEQ, n_pairs * pair_w), lambda b: (off + b, 0)),
                  pl.BlockSpec((SEQ, n_pairs * pair_w), lambda b: (off + b, 0)),
                  pl.BlockSpec((SEQ, n_pairs * pair_w), lambda b: (off + b, 0)),
                  pl.BlockSpec(memory_space=pl.ANY)],
        out_specs=pl.BlockSpec((SEQ, n_pairs * LANES), lambda b: (off + b, 0)),
        out_shape=jax.ShapeDtypeStruct((N_TOK, C_HEADS * C_VDIM), BF),
        input_output_aliases={3: 0},
        compiler_params=_params("arbitrary"),
        name="mla_attn_context",
    )(q, k, v, o)


def kernel(x_prompt, x_sample, c, cache_win_k, cache_win_v, cache_mla_ckv, cache_mla_krope, c_ctx,
           ada_w, ada_b, ffn_w_in, ffn_w_out,
           gmlp_w_in, gmlp_v_gain, gmlp_w_s, gmlp_b_s, gmlp_w_out,
           win_w_qkv, win_q_gain, win_k_gain, win_sink, win_w_o,
           mla_w_down, mla_cq_gain, mla_ckv_gain, mla_w_uq, mla_w_ukv, mla_q_gain, mla_k_gain, mla_w_o):
    x = (x_sample.reshape(N_SAMPLE, D_MODEL), x_prompt.reshape(N_PROMPT, D_MODEL))
    cond = jnp.concatenate([c, c_ctx[None, :], jnp.zeros((N_COND - DEC_BATCH - 1, D_MODEL), F32)], axis=0)
    mods = _modulation(cond, ada_w, ada_b)
    w_in_b = ffn_w_in.astype(BF)
    w_out_b = ffn_w_out.astype(BF)
    shift_terms = _ffn_shift_terms(mods, w_in_b)

    lane = jnp.arange(LANES)
    b_cos, b_sin = _rope_tables(B_HEAD_DIM, lane % B_HEAD_DIM)
    c_lane = jnp.where((lane >= C_NOPE) & (lane < C_NOPE + C_ROPE), lane - C_NOPE, -1)
    c_cos, c_sin = _rope_tables(C_ROPE, c_lane)

    nk = B_KV_HEADS * B_HEAD_DIM
    win_k, win_v, mla_ckv, mla_krope = [], [], [], []
    ia = ib = ic = 0
    for l in range(DEPTH):
        mod = mods[l]
        pre_proj = None
        x = _ffn(x, mod, shift_terms, w_in_b, w_out_b, l, 0, split_in=(l == 0))
        kind = l % N_MIXERS
        if kind == 0:
            x = _gmlp(x, mod, gmlp_w_in[ia], gmlp_v_gain[ia], gmlp_w_s[ia], gmlp_b_s[ia], gmlp_w_out[ia])
            ia += 1
        elif kind == 1:
            q, k, v = _bproj(x, mod, win_w_qkv[ib], win_q_gain[ib], win_k_gain[ib], b_cos, b_sin)
            x = _battn(x, mod, q, k, v,
                       cache_win_k[:, ib].reshape(DEC_BATCH, PAST_LEN, nk),
                       cache_win_v[:, ib].reshape(DEC_BATCH, PAST_LEN, nk),
                       win_sink[ib], win_w_o[ib].astype(BF))
            win_k.append(k[N_SAMPLE:].reshape(BATCH, SEQ, B_KV_HEADS, B_HEAD_DIM))
            win_v.append(v[N_SAMPLE:].reshape(BATCH, SEQ, B_KV_HEADS, B_HEAD_DIM))
            ib += 1
        else:
            wd, wuq, qg, qgs, kg, kgs = _mla_weights(mla_w_down[ic], mla_w_uq[ic], mla_q_gain[ic], mla_k_gain[ic])
            wukv = mla_w_ukv[ic].astype(BF)
            q_scale = (C_NOPE + C_ROPE) ** -0.5 * LOG2E
            q, k, v, ckv, kr = _cproj(x, mod, wd, mla_cq_gain[ic], mla_ckv_gain[ic], wuq, wukv,
                                      c_cos * (qg * q_scale), c_sin * (qgs * q_scale), c_cos * kg, c_sin * kgs)
            ck, cv = _cctx(cache_mla_ckv[:, ic], cache_mla_krope[:, ic], wukv, kg)
            pre_proj = (_cattn(q, k, v, ck, cv), mla_w_o[ic].astype(BF))
            mla_ckv.append(ckv[N_SAMPLE:].reshape(BATCH, SEQ, C_KV_LORA))
            mla_krope.append(kr[N_SAMPLE:, C_NOPE:C_NOPE + C_ROPE].reshape(BATCH, SEQ, C_ROPE))
            ic += 1
        x = _ffn(x, mod, shift_terms, w_in_b, w_out_b, l, 1, split_out=(l == DEPTH - 1), pre_proj=pre_proj)
    y_sample, y_prompt = x
    return (y_prompt.reshape(BATCH, SEQ, D_MODEL), y_sample.reshape(DEC_BATCH, DEC_SEQ, D_MODEL),
            jnp.stack(win_k, axis=1), jnp.stack(win_v, axis=1),
            jnp.stack(mla_ckv, axis=1), jnp.stack(mla_krope, axis=1))
```

```python
import functools
import math

import jax
import jax.numpy as jnp
import numpy as np
from jax import lax
from jax.experimental import pallas as pl
from jax.experimental.pallas import tpu as pltpu

D_MODEL = 1024
BATCH = 16
SEQ = 256
DEPTH = 4
DEC_BATCH = 8
DEC_SEQ = 4096
PAST_LEN = 256
GRID_W = 64
N_MIXERS = 3
N_MOD = 9
D_FF = 2816
A_WIDTH = D_MODEL
A_GROUPS = 8
A_CHUNK = 128
B_HEADS = 16
B_KV_HEADS = 4
B_HEAD_DIM = 64
B_WINDOW = 128
C_HEADS = 16
C_NOPE = 64
C_ROPE = 32
C_VDIM = 64
C_Q_LORA = 512
C_KV_LORA = 256
ROPE_BASE = 10000.0
EPS = 1e-6
NEG_INF = -1e30

LANES = 128
N_SAMPLE = DEC_BATCH * DEC_SEQ
N_PROMPT = BATCH * SEQ
N_TOK = N_SAMPLE + N_PROMPT
N_COND = 16
TM = 512
FM = 1024
FF_CHUNKS = ((0, 1536), (1536, 1280))
N_TILES = N_TOK // TM
N_SAMPLE_TILES = N_SAMPLE // TM
TILES_PER_SEQ = DEC_SEQ // TM
MOD_TN = 1536
SHIFT_TN = 1408
BQ = 256
CQ = 1024
CK = 2048
C_HEAD_PAD = 128
C_DOWN_PAD = C_Q_LORA + C_KV_LORA + 2 * LANES
VMEM_LIMIT_BYTES = 56 * 1024 * 1024

LOG2E = math.log2(math.e)

BF = jnp.bfloat16
F32 = jnp.float32


def _params(*sem):
    return pltpu.CompilerParams(dimension_semantics=sem, vmem_limit_bytes=VMEM_LIMIT_BYTES)


def _dot(a, b):
    return jnp.dot(a, b, preferred_element_type=F32)


def _dot_t(a, b):
    return lax.dot_general(a, b, (((1,), (1,)), ((), ())), preferred_element_type=F32)


def _rms(x):
    return x * lax.rsqrt(jnp.mean(x * x, axis=-1, keepdims=True) + EPS)


def _ada(x, mod_ref, k):
    shift = mod_ref[3 * k:3 * k + 1, :]
    scale = mod_ref[3 * k + 1:3 * k + 2, :]
    return _rms(x) * (1.0 + scale) + shift


def _const_spec(shape):
    nd = len(shape)
    return pl.BlockSpec(shape, lambda *_: (0,) * nd, pipeline_mode=pl.Buffered(1))


def _tok_spec(width, tm=TM):
    return pl.BlockSpec((tm, width), lambda i: (i, 0))


def _mod_spec(tm):
    return pl.BlockSpec((None, N_MOD, D_MODEL), lambda i: (jnp.minimum(i * tm // DEC_SEQ, DEC_BATCH), 0, 0))


_MOD_SPEC = _mod_spec(TM)


def _rope_tile(i):
    return jnp.where(i < N_SAMPLE_TILES, i % TILES_PER_SEQ, TILES_PER_SEQ)


_ROPE_SPEC = pl.BlockSpec((TM, LANES), lambda i: (_rope_tile(i), 0))


def _mod_kernel(c_ref, w_ref, b_ref, o_ref):
    a = jax.nn.silu(c_ref[...]).astype(BF)
    o_ref[...] = _dot(a, w_ref[...].astype(BF)) + b_ref[...]


def _modulation(cond, ada_w, ada_b):
    n_out = N_MOD * D_MODEL
    out = pl.pallas_call(
        _mod_kernel,
        grid=(DEPTH, n_out // MOD_TN),
        in_specs=[
            pl.BlockSpec((N_COND, D_MODEL), lambda l, j: (0, 0)),
            pl.BlockSpec((None, D_MODEL, MOD_TN), lambda l, j: (l, 0, j)),
            pl.BlockSpec((None, 1, MOD_TN), lambda l, j: (l, 0, j)),
        ],
        out_specs=pl.BlockSpec((None, N_COND, MOD_TN), lambda l, j: (l, 0, j)),
        out_shape=jax.ShapeDtypeStruct((DEPTH, N_COND, n_out), F32),
        compiler_params=_params("arbitrary", "arbitrary"),
        name="modulation",
    )(cond, ada_w, ada_b.reshape(DEPTH, 1, n_out))
    return out.reshape(DEPTH, N_COND, N_MOD, D_MODEL)


def _ffn_prep_kernel(s_ref, w_ref, wb_ref, o_ref):
    wb = w_ref[...].astype(BF)
    wb_ref[...] = wb
    o_ref[...] = _dot(s_ref[...].astype(BF), wb)


def _ffn_prep(mods, w_in):
    shifts = jnp.stack([mods[:, :, 0], mods[:, :, 6]], axis=1)
    w_spec = pl.BlockSpec((None, None, D_MODEL, SHIFT_TN), lambda l, h, j: (l, h, 0, j))
    w_b, out = pl.pallas_call(
        _ffn_prep_kernel,
        grid=(DEPTH, 2, 2 * D_FF // SHIFT_TN),
        in_specs=[pl.BlockSpec((None, None, N_COND, D_MODEL), lambda l, h, j: (l, h, 0, 0)), w_spec],
        out_specs=[w_spec, pl.BlockSpec((None, None, N_COND, SHIFT_TN), lambda l, h, j: (l, h, 0, j))],
        out_shape=[jax.ShapeDtypeStruct(w_in.shape, BF),
                   jax.ShapeDtypeStruct((DEPTH, 2, N_COND, 2 * D_FF), F32)],
        compiler_params=_params("arbitrary", "arbitrary", "arbitrary"),
        name="ffn_prep",
    )(shifts, w_in)
    return w_b, out.reshape(DEPTH, 2, N_COND, 1, 2 * D_FF)


def _ffn_kernel(*refs, k, split_in, split_out, pre_proj):
    n_x = 2 if split_in else 1
    n_in = n_x + (2 if pre_proj else 0)
    x_refs, (mod_ref, sh_ref, win_ref, wout_ref), o_refs = refs[:n_x], refs[n_in:n_in + 4], refs[n_in + 4:]
    is_sample = pl.program_id(0) < N_SAMPLE // FM
    x = jnp.where(is_sample, x_refs[0][...], x_refs[1][...]) if split_in else x_refs[0][...]
    if pre_proj:
        attn_ref, wo_ref = refs[n_x:n_in]
        x = x + mod_ref[5:6, :] * _dot(attn_ref[...], wo_ref[...])
    xa = (x * (1.0 + mod_ref[3 * k + 1:3 * k + 2, :])).astype(BF)
    rinv = lax.rsqrt(jnp.mean(x * x, axis=-1, keepdims=True) + EPS)
    y = None
    for c0, cw in FF_CHUNKS:
        g = _dot(xa, win_ref[:, c0:c0 + cw]) * rinv + sh_ref[:, c0:c0 + cw]
        u = _dot(xa, win_ref[:, D_FF + c0:D_FF + c0 + cw]) * rinv + sh_ref[:, D_FF + c0:D_FF + c0 + cw]
        yc = _dot((jax.nn.silu(g) * u).astype(BF), wout_ref[c0:c0 + cw, :])
        y = yc if y is None else y + yc
    gate = mod_ref[3 * k + 2:3 * k + 3, :]
    out = x + (0.5 * gate) * y
    if split_out:
        o_refs[1][...] = out

        @pl.when(is_sample)
        def _():
            o_refs[0][...] = out
    else:
        o_refs[0][...] = out


_SAMPLE_SPEC = pl.BlockSpec((FM, D_MODEL), lambda i: (jnp.minimum(i, N_SAMPLE // FM - 1), 0))
_PROMPT_SPEC = pl.BlockSpec((FM, D_MODEL), lambda i: (jnp.maximum(i - N_SAMPLE // FM, 0), 0))


def _ffn(xs, mod, shift_terms, w_in, w_out, layer, half, split_in=False, split_out=False, pre_proj=None):
    def w_spec(rows, cols):
        return pl.BlockSpec((None, None, rows, cols), lambda i: (layer, half, 0, 0), pipeline_mode=pl.Buffered(1))

    sh_spec = pl.BlockSpec((None, None, None, 1, 2 * D_FF),
                           lambda i: (layer, half, jnp.minimum(i * FM // DEC_SEQ, DEC_BATCH), 0, 0))

    x_specs = [_SAMPLE_SPEC, _PROMPT_SPEC] if split_in else [_tok_spec(D_MODEL, FM)]
    operands = list(xs) if split_in else [xs]
    if pre_proj is not None:
        x_specs += [_tok_spec(D_MODEL, FM), _const_spec((D_MODEL, D_MODEL))]
        operands += list(pre_proj)
    if split_out:
        out_specs = [_SAMPLE_SPEC, _PROMPT_SPEC]
        out_shape = [jax.ShapeDtypeStruct((N_SAMPLE, D_MODEL), F32), jax.ShapeDtypeStruct((N_PROMPT, D_MODEL), F32)]
    else:
        out_specs = _tok_spec(D_MODEL, FM)
        out_shape = jax.ShapeDtypeStruct((N_TOK, D_MODEL), F32)
    return pl.pallas_call(
        functools.partial(_ffn_kernel, k=2 * half, split_in=split_in, split_out=split_out,
                          pre_proj=pre_proj is not None),
        grid=(N_TOK // FM,),
        in_specs=x_specs + [_mod_spec(FM), sh_spec, w_spec(D_MODEL, 2 * D_FF), w_spec(D_FF, D_MODEL)],
        out_specs=out_specs,
        out_shape=out_shape,
        compiler_params=_params("arbitrary"),
        name="ffn",
    )(*operands, mod, shift_terms, w_in, w_out)


def _gmlp_kernel(x_ref, mod_ref, win_ref, vg_ref, ws_ref, bs_ref, wout_ref, o_ref):
    x = x_ref[...]
    hb = _ada(x, mod_ref, 1).astype(BF)
    pre = _dot(hb, win_ref[...])
    uv = 0.5 * pre * (1.0 + lax.erf(pre * math.sqrt(0.5)))
    u = uv[:, :A_WIDTH]
    v = (_rms(uv[:, A_WIDTH:]) * vg_ref[...]).astype(BF)
    bias = bs_ref[...]
    rows = []
    for c in range(FM // A_CHUNK):
        cols = [_dot(ws_ref[g], v[c * A_CHUNK:(c + 1) * A_CHUNK, g * LANES:(g + 1) * LANES])
                for g in range(A_GROUPS)]
        rows.append(jnp.concatenate(cols, axis=1) + bias)
    sv = jnp.concatenate(rows, axis=0)
    y = _dot((u * sv).astype(BF), wout_ref[...])
    o_ref[...] = x + mod_ref[5:6, :] * y


def _gmlp(x, mod, w_in, v_gain, w_s, b_s, w_out):
    bias = jnp.repeat(b_s.T, A_WIDTH // A_GROUPS, axis=1)
    return pl.pallas_call(
        _gmlp_kernel,
        grid=(N_TOK // FM,),
        in_specs=[_tok_spec(D_MODEL, FM), _mod_spec(FM),
                  _const_spec((D_MODEL, 2 * A_WIDTH)), _const_spec((1, A_WIDTH)),
                  _const_spec((A_GROUPS, A_CHUNK, A_CHUNK)), _const_spec((A_CHUNK, A_WIDTH)),
                  _const_spec((A_WIDTH, D_MODEL))],
        out_specs=_tok_spec(D_MODEL, FM),
        out_shape=jax.ShapeDtypeStruct((N_TOK, D_MODEL), F32),
        compiler_params=_params("arbitrary"),
        name="gmlp",
    )(x, mod, w_in.astype(BF), v_gain.reshape(1, A_WIDTH), w_s.astype(BF), bias, w_out.astype(BF))


def _swap_pairs(y, step):
    lane = lax.broadcasted_iota(jnp.int32, y.shape, 1)
    return jnp.where((lane & step) != 0, pltpu.roll(y, step, 1), pltpu.roll(y, LANES - step, 1))


def _rope_tables(rot_dim, lane_of_dim):
    quarter = rot_dim // 4
    inv = np.float32(ROPE_BASE) ** (-np.arange(quarter, dtype=np.float32) / np.float32(quarter))
    t = np.arange(DEC_SEQ)
    row = (t // GRID_W).astype(np.float32)
    col = (t % GRID_W).astype(np.float32)
    ang = np.stack([row[:, None] * inv, col[:, None] * inv], axis=1)
    cos, sin = np.cos(ang), np.sin(ang)
    d = np.asarray(lane_of_dim)
    dd = np.maximum(d, 0)
    axis, member, freq = dd // (2 * quarter), (dd % (2 * quarter)) // quarter, dd % quarter
    rot = (d >= 0)[None, :]
    c_tab = np.where(rot, cos[:, axis, freq], 1.0)
    s_tab = np.where(rot, np.where(member == 0, -1.0, 1.0)[None, :] * sin[:, axis, freq], 0.0)
    ident_c = np.ones((TM, LANES), np.float32)
    ident_s = np.zeros((TM, LANES), np.float32)
    return (np.concatenate([c_tab, ident_c], axis=0).astype(np.float32),
            np.concatenate([s_tab, ident_s], axis=0).astype(np.float32))


def _bproj_kernel(x_ref, mod_ref, w_ref, ones_ref, qc_ref, qs_ref, kc_ref, ks_ref, q_ref, k_ref, v_ref):
    hb = _ada(x_ref[...], mod_ref, 1).astype(BF)
    qkv = _dot(hb, w_ref[...])
    head_ones = ones_ref[...]

    def norm_rope(t4, tab, tab_swap):
        sq = t4 * t4
        hi = sq.astype(BF)
        lo = (sq - hi.astype(F32)).astype(BF)
        r = lax.rsqrt((_dot(hi, head_ones) + _dot(lo, head_ones)) / B_HEAD_DIM + EPS)
        halves = []
        for j in range(2):
            sl = slice(j * LANES, (j + 1) * LANES)
            t = t4[:, sl]
            halves.append(r[:, sl] * (t * tab + _swap_pairs(t, B_HEAD_DIM // 4) * tab_swap))
        return halves

    nq = B_HEADS * B_HEAD_DIM
    nk = B_KV_HEADS * B_HEAD_DIM
    q_tab, q_tab_swap = qc_ref[...], qs_ref[...]
    for j in range(nq // (2 * LANES)):
        halves = norm_rope(qkv[:, 2 * j * LANES:(2 * j + 2) * LANES], q_tab, q_tab_swap)
        q_ref[:, 2 * j * LANES:(2 * j + 1) * LANES] = halves[0].astype(BF)
        q_ref[:, (2 * j + 1) * LANES:(2 * j + 2) * LANES] = halves[1].astype(BF)
    halves = norm_rope(qkv[:, nq:nq + nk], kc_ref[...], ks_ref[...])
    k_ref[:, :LANES] = halves[0]
    k_ref[:, LANES:] = halves[1]
    v_ref[...] = qkv[:, nq + nk:]


def _bproj(x, mod, w_qkv, q_gain, k_gain, cos, sin):
    nq = B_HEADS * B_HEAD_DIM
    nk = B_KV_HEADS * B_HEAD_DIM
    q_scale = B_HEAD_DIM ** -0.5 * LOG2E
    lane = np.arange(LANES)
    qg = jnp.tile(q_gain, LANES // B_HEAD_DIM)
    kg = jnp.tile(k_gain, LANES // B_HEAD_DIM)
    partner = lane ^ (B_HEAD_DIM // 4)
    head = np.arange(nk) // B_HEAD_DIM
    head_ones = jnp.asarray(head[:, None] == head[None, :], BF)
    return pl.pallas_call(
        _bproj_kernel,
        grid=(N_TILES,),
        in_specs=[_tok_spec(D_MODEL), _MOD_SPEC, _const_spec((D_MODEL, nq + 2 * nk)), _const_spec((nk, nk)),
                  _ROPE_SPEC, _ROPE_SPEC, _ROPE_SPEC, _ROPE_SPEC],
        out_specs=[_tok_spec(nq), _tok_spec(nk), _tok_spec(nk)],
        out_shape=[jax.ShapeDtypeStruct((N_TOK, nq), BF),
                   jax.ShapeDtypeStruct((N_TOK, nk), F32),
                   jax.ShapeDtypeStruct((N_TOK, nk), F32)],
        compiler_params=_params("arbitrary"),
        name="gqa_proj",
    )(x, mod, w_qkv.astype(BF), head_ones,
      cos * (qg * q_scale), sin * (qg[partner] * q_scale), cos * kg, sin * kg[partner])


def _gqa_attend(q, kcat, vcat, bias, sink_ref):
    tq = q.shape[0]
    nk = kcat.shape[0]
    lo = lax.broadcasted_iota(jnp.int32, (nk, LANES), 1) < B_HEAD_DIM
    lo_q = lax.broadcasted_iota(jnp.int32, (2 * tq, LANES), 1) < B_HEAD_DIM
    first = lax.broadcasted_iota(jnp.int32, (2 * tq, 1), 0) < tq
    if bias is not None:
        bias = jnp.concatenate([bias, bias], axis=0)
    outs = []
    for g in range(B_KV_HEADS):
        sl = slice((g // 2) * LANES, (g // 2 + 1) * LANES)
        own = lo if g % 2 == 0 else jnp.logical_not(lo)
        k_own = jnp.where(own, kcat[:, sl], 0.0)
        k_swp = pltpu.roll(k_own, B_HEAD_DIM, 1)
        v_own = jnp.where(own, vcat[:, sl], 1.0)
        v_swp = pltpu.roll(v_own, B_HEAD_DIM, 1)
        k_half = (k_own, k_swp) if g % 2 == 0 else (k_swp, k_own)
        v_half = (v_own, v_swp) if g % 2 == 0 else (v_swp, v_own)
        qg = jnp.concatenate([q[:, (2 * g) * LANES:(2 * g + 1) * LANES],
                              q[:, (2 * g + 1) * LANES:(2 * g + 2) * LANES]], axis=0)
        s_all = _dot_t(qg, jnp.concatenate(k_half, axis=0).astype(BF))
        o_half = []
        for e in range(2):
            s = s_all[:, e * nk:(e + 1) * nk]
            if bias is not None:
                nb = bias.shape[1]
                s = jnp.concatenate([s[:, :nb] + bias, s[:, nb:]], axis=1)
            sk = jnp.where(first, sink_ref[4 * g + e], sink_ref[4 * g + 2 + e]) * LOG2E
            m = jnp.maximum(jnp.max(s, axis=-1, keepdims=True), sk)
            ov = _dot(jnp.exp2(s - m).astype(BF), v_half[e].astype(BF))
            o_half.append(ov / (pltpu.roll(ov, B_HEAD_DIM, 1) + jnp.exp2(sk - m)))
        o_g = jnp.where(lo_q, o_half[0], o_half[1])
        outs += [o_g[:tq], o_g[tq:]]
    return jnp.concatenate(outs, axis=1)


def _battn_lat_kernel(x_ref, mod_ref, q_ref, kp_ref, kc_ref, kn_ref, vp_ref, vc_ref, vn_ref,
                      ck_ref, cv_ref, sink_ref, wo_ref, o_ref):
    j = pl.program_id(1)
    kcat = jnp.concatenate([kp_ref[...], kc_ref[...], kn_ref[...], ck_ref[...]], axis=0)
    vcat = jnp.concatenate([vp_ref[...], vc_ref[...], vn_ref[...], cv_ref[...]], axis=0)
    n_lat = BQ + 2 * B_WINDOW
    qi = lax.broadcasted_iota(jnp.int32, (BQ, n_lat), 0)
    pk = lax.broadcasted_iota(jnp.int32, (BQ, n_lat), 1)
    kpos = j * BQ + pk - B_WINDOW
    valid = (jnp.abs(pk - B_WINDOW - qi) <= B_WINDOW) & (kpos >= 0) & (kpos < DEC_SEQ)
    bias = jnp.where(valid, 0.0, NEG_INF)
    o = _gqa_attend(q_ref[...], kcat, vcat, bias, sink_ref)
    y = _dot(o.astype(BF), wo_ref[...])
    o_ref[...] = x_ref[...] + mod_ref[5:6, :] * y


def _battn_ctx_kernel(x_ref, mod_ref, q_ref, k_ref, v_ref, sink_ref, wo_ref, o_ref):
    o = _gqa_attend(q_ref[...], k_ref[...], v_ref[...], None, sink_ref)
    y = _dot(o.astype(BF), wo_ref[...])
    o_ref[...] = x_ref[...] + mod_ref[5:6, :] * y


def _battn(x, mod, q, k, v, cache_k, cache_v, sink, w_o):
    nq = B_HEADS * B_HEAD_DIM
    nk = B_KV_HEADS * B_HEAD_DIM
    nb = DEC_SEQ // BQ
    nw = DEC_SEQ // B_WINDOW
    per = BQ // B_WINDOW
    smem = pl.BlockSpec(memory_space=pltpu.SMEM)
    cur_spec = pl.BlockSpec((BQ, nk), lambda b, j: (b * nb + j, 0))
    prev_spec = pl.BlockSpec((B_WINDOW, nk), lambda b, j: (b * nw + jnp.maximum(per * j - 1, 0), 0))
    next_spec = pl.BlockSpec((B_WINDOW, nk), lambda b, j: (b * nw + jnp.minimum(per * j + per, nw - 1), 0))

    x = pl.pallas_call(
        _battn_lat_kernel,
        grid=(DEC_BATCH, nb),
        in_specs=[pl.BlockSpec((BQ, D_MODEL), lambda b, j: (b * nb + j, 0)),
                  pl.BlockSpec((None, N_MOD, D_MODEL), lambda b, j: (b, 0, 0)),
                  pl.BlockSpec((BQ, nq), lambda b, j: (b * nb + j, 0)),
                  prev_spec, cur_spec, next_spec, prev_spec, cur_spec, next_spec,
                  pl.BlockSpec((None, PAST_LEN, nk), lambda b, j: (b, 0, 0)),
                  pl.BlockSpec((None, PAST_LEN, nk), lambda b, j: (b, 0, 0)),
                  smem, _const_spec((nq, D_MODEL))],
        out_specs=pl.BlockSpec((BQ, D_MODEL), lambda b, j: (b * nb + j, 0)),
        out_shape=jax.ShapeDtypeStruct((N_TOK, D_MODEL), F32),
        input_output_aliases={0: 0},
        compiler_params=_params("arbitrary", "arbitrary"),
        name="gqa_attn_latent",
    )(x, mod, q, k, k, k, v, v, v, cache_k, cache_v, sink, w_o)
    off = N_SAMPLE // SEQ
    return pl.pallas_call(
        _battn_ctx_kernel,
        grid=(BATCH,),
        in_specs=[pl.BlockSpec((SEQ, D_MODEL), lambda b: (off + b, 0)),
                  pl.BlockSpec((None, N_MOD, D_MODEL), lambda b: (DEC_BATCH, 0, 0)),
                  pl.BlockSpec((SEQ, nq), lambda b: (off + b, 0)),
                  pl.BlockSpec((SEQ, nk), lambda b: (off + b, 0)),
                  pl.BlockSpec((SEQ, nk), lambda b: (off + b, 0)),
                  smem, _const_spec((nq, D_MODEL))],
        out_specs=pl.BlockSpec((SEQ, D_MODEL), lambda b: (off + b, 0)),
        out_shape=jax.ShapeDtypeStruct((N_TOK, D_MODEL), F32),
        input_output_aliases={0: 0},
        compiler_params=_params("arbitrary"),
        name="gqa_attn_context",
    )(x, mod, q, k, v, sink, w_o)


C_SWAP_W = C_HEADS * C_ROPE


def _mla_head_norm_rope(t, t_swap, tab, tab_swap):
    r = lax.rsqrt(jnp.sum(t * t, axis=-1, keepdims=True) / (C_NOPE + C_ROPE) + EPS)
    if t_swap is None:
        return t * r * tab
    return r * (t * tab + t_swap * tab_swap)


def _mla_keys_values(c_kv_b, k_rope, k_rope_swap, wukv_ref, tab, tab_swap, k_ref, v_ref):
    kv = _dot(c_kv_b, wukv_ref[...])
    lower = (lax.broadcasted_iota(jnp.int32, kv.shape, 1) & C_NOPE) == 0
    v_ref[...] = jnp.where(lower, 1.0, kv).astype(BF)
    k_nope = jnp.where(lower, kv, 0.0)
    for h in range(C_HEADS):
        sl = slice(h * C_HEAD_PAD, (h + 1) * C_HEAD_PAD)
        k_ref[:, sl] = _mla_head_norm_rope(k_nope[:, sl] + k_rope, k_rope_swap, tab, tab_swap).astype(BF)


def _cproj_kernel(x_ref, mod_ref, wd_ref, cqg_ref, ckvg_ref, wuq_ref, wukv_ref,
                  qc_ref, qs_ref, kc_ref, ks_ref, q_ref, k_ref, v_ref, ckv_ref, kr_ref):
    hb = _ada(x_ref[...], mod_ref, 1).astype(BF)
    d = _dot(hb, wd_ref[...])
    c_q = _rms(d[:, :C_Q_LORA]) * cqg_ref[...]
    c_kv = _rms(d[:, C_Q_LORA:C_Q_LORA + C_KV_LORA]) * ckvg_ref[...]
    k_rope = d[:, C_Q_LORA + C_KV_LORA:C_Q_LORA + C_KV_LORA + LANES]
    k_rope_swap = d[:, C_Q_LORA + C_KV_LORA + LANES:]
    ckv_ref[...] = c_kv
    kr_ref[...] = k_rope
    q2 = _dot(c_q.astype(BF), wuq_ref[...])
    wq = C_HEADS * C_HEAD_PAD
    q_tab, q_tab_swap = qc_ref[...], qs_ref[...]
    per_tile = LANES // C_ROPE
    for h in range(C_HEADS):
        sl = slice(h * C_HEAD_PAD, (h + 1) * C_HEAD_PAD)
        t_swap = q2[:, wq + (h // per_tile) * LANES:wq + (h // per_tile + 1) * LANES]
        shift = (C_NOPE - C_ROPE * (h % per_tile)) % LANES
        if shift:
            t_swap = pltpu.roll(t_swap, shift, 1)
        q_ref[:, sl] = _mla_head_norm_rope(q2[:, sl], t_swap, q_tab, q_tab_swap).astype(BF)
    _mla_keys_values(c_kv.astype(BF), k_rope, k_rope_swap, wukv_ref, kc_ref[...], ks_ref[...], k_ref, v_ref)


def _cctx_kernel(ckv_ref, kr_ref, wukv_ref, kg_ref, k_ref, v_ref):
    _mla_keys_values(ckv_ref[...].astype(BF), kr_ref[...], None, wukv_ref, kg_ref[...], None, k_ref, v_ref)


def _mla_weights(w_down, w_uq, q_gain, k_gain):
    hd = C_NOPE + C_ROPE
    pad_lanes = C_HEAD_PAD - hd
    lane = np.arange(C_HEAD_PAD)
    is_rope = (lane >= C_NOPE) & (lane < hd)
    partner = np.where(is_rope, lane ^ (C_ROPE // 4), lane)

    def swapped(t):
        return jnp.where(is_rope, jnp.take(t, partner, axis=-1), 0.0)

    kr_cols = jnp.pad(w_down[:, C_Q_LORA + C_KV_LORA:], ((0, 0), (C_NOPE, pad_lanes)))
    wd = jnp.concatenate([w_down[:, :C_Q_LORA + C_KV_LORA], kr_cols, swapped(kr_cols)], axis=1)
    wuq = jnp.pad(w_uq.reshape(C_Q_LORA, C_HEADS, hd), ((0, 0), (0, 0), (0, pad_lanes)))
    wuq_swap = swapped(wuq)[:, :, C_NOPE:hd].reshape(C_Q_LORA, C_SWAP_W)
    wuq = jnp.concatenate([wuq.reshape(C_Q_LORA, C_HEADS * C_HEAD_PAD), wuq_swap], axis=1)
    qg = jnp.pad(q_gain, (0, pad_lanes))
    kg = jnp.pad(k_gain, (0, pad_lanes))
    row = lambda t: t.reshape(1, C_HEAD_PAD)
    return wd.astype(BF), wuq.astype(BF), row(qg), row(swapped(qg)), row(kg), row(swapped(kg))


def _cproj(x, mod, wd, cq_gain, ckv_gain, wuq, wukv, q_tab, q_tab_swap, k_tab, k_tab_swap):
    wq = C_HEADS * C_HEAD_PAD
    return pl.pallas_call(
        _cproj_kernel,
        grid=(N_TILES,),
        in_specs=[_tok_spec(D_MODEL), _MOD_SPEC, _const_spec((D_MODEL, C_DOWN_PAD)),
                  _const_spec((1, C_Q_LORA)), _const_spec((1, C_KV_LORA)),
                  _const_spec((C_Q_LORA, wq + C_SWAP_W)), _const_spec((C_KV_LORA, wq)),
                  _ROPE_SPEC, _ROPE_SPEC, _ROPE_SPEC, _ROPE_SPEC],
        out_specs=[_tok_spec(wq), _tok_spec(wq), _tok_spec(wq), _tok_spec(C_KV_LORA), _tok_spec(LANES)],
        out_shape=[jax.ShapeDtypeStruct((N_TOK, wq), BF), jax.ShapeDtypeStruct((N_TOK, wq), BF),
                   jax.ShapeDtypeStruct((N_TOK, wq), BF), jax.ShapeDtypeStruct((N_TOK, C_KV_LORA), F32),
                   jax.ShapeDtypeStruct((N_TOK, LANES), F32)],
        compiler_params=_params("arbitrary"),
        name="mla_proj",
    )(x, mod, wd, cq_gain.reshape(1, C_Q_LORA), ckv_gain.reshape(1, C_KV_LORA), wuq, wukv,
      q_tab, q_tab_swap, k_tab, k_tab_swap)


def _cctx(cache_ckv, cache_krope, wukv, kg):
    n = DEC_BATCH * PAST_LEN
    wq = C_HEADS * C_HEAD_PAD
    kr = jnp.pad(cache_krope.reshape(n, C_ROPE), ((0, 0), (C_NOPE, C_HEAD_PAD - C_NOPE - C_ROPE)))
    return pl.pallas_call(
        _cctx_kernel,
        grid=(n // TM,),
        in_specs=[_tok_spec(C_KV_LORA), _tok_spec(LANES), _const_spec((C_KV_LORA, wq)),
                  _const_spec((1, C_HEAD_PAD))],
        out_specs=[_tok_spec(wq), _tok_spec(wq)],
        out_shape=[jax.ShapeDtypeStruct((n, wq), BF), jax.ShapeDtypeStruct((n, wq), BF)],
        compiler_params=_params("arbitrary"),
        name="mla_context_keys",
    )(cache_ckv.reshape(n, C_KV_LORA), kr, wukv, kg)


def _mla_attend(q_ref, kv_refs, o_ref, pair=0):
    tq = q_ref.shape[0]
    lo = lax.broadcasted_iota(jnp.int32, (tq, LANES), 1) < C_VDIM
    chunks = []
    for k_ref, v_ref in kv_refs:
        nk = k_ref.shape[0]
        if nk < CK and chunks:
            chunks[-1].append((k_ref, v_ref, slice(0, nk)))
        else:
            ck = min(CK, nk)
            chunks += [[(k_ref, v_ref, slice(c * ck, (c + 1) * ck))] for c in range(nk // ck)]
    sum_acc = []
    for e in range(2):
        sl = slice((2 * pair + e) * C_HEAD_PAD, (2 * pair + e + 1) * C_HEAD_PAD)
        qh = q_ref[:, sl]
        m = acc = None
        for parts in chunks:
            kk = [k_ref[rows, sl] for k_ref, _, rows in parts]
            vv = [v_ref[rows, sl] for _, v_ref, rows in parts]
            s = _dot_t(qh, kk[0] if len(kk) == 1 else jnp.concatenate(kk, axis=0))
            cm = jnp.max(s, axis=-1, keepdims=True)
            m_new = cm if m is None else jnp.maximum(m, cm)
            pv = _dot(jnp.exp2(s - m_new).astype(BF), vv[0] if len(vv) == 1 else jnp.concatenate(vv, axis=0))
            acc = pv if acc is None else jnp.exp2(m - m_new) * acc + pv
            m = m_new
        sum_acc.append(acc)
    r0 = pltpu.roll(sum_acc[0], C_VDIM, 1)
    r1 = pltpu.roll(sum_acc[1], C_VDIM, 1)
    o_ref[:, pair * LANES:(pair + 1) * LANES] = jnp.where(lo, r0 / sum_acc[0], sum_acc[1] / r1).astype(BF)


def _cattn_lat_kernel(q_ref, k_ref, v_ref, ck_ref, cv_ref, o_ref):
    _mla_attend(q_ref, ((k_ref, v_ref), (ck_ref, cv_ref)), o_ref)


def _cattn_ctx_kernel(q_ref, k_ref, v_ref, oin_ref, o_ref):
    del oin_ref
    for pair in range(C_HEADS // 2):
        _mla_attend(q_ref, ((k_ref, v_ref),), o_ref, pair)


def _cattn(q, k, v, ck, cv):
    pair_w = 2 * C_HEAD_PAD
    n_pairs = C_HEADS // 2
    nqt = DEC_SEQ // CQ
    o = pl.pallas_call(
        _cattn_lat_kernel,
        grid=(DEC_BATCH, n_pairs, nqt),
        in_specs=[pl.BlockSpec((CQ, pair_w), lambda b, p, t: (b * nqt + t, p)),
                  pl.BlockSpec((DEC_SEQ, pair_w), lambda b, p, t: (b, p)),
                  pl.BlockSpec((DEC_SEQ, pair_w), lambda b, p, t: (b, p)),
                  pl.BlockSpec((PAST_LEN, pair_w), lambda b, p, t: (b, p)),
                  pl.BlockSpec((PAST_LEN, pair_w), lambda b, p, t: (b, p))],
        out_specs=pl.BlockSpec((CQ, LANES), lambda b, p, t: (b * nqt + t, p)),
        out_shape=jax.ShapeDtypeStruct((N_TOK, C_HEADS * C_VDIM), BF),
        compiler_params=_params("arbitrary", "arbitrary", "arbitrary"),
        name="mla_attn_latent",
    )(q, k, v, ck, cv)
    off = N_SAMPLE // SEQ
    return pl.pallas_call(
        _cattn_ctx_kernel,
        grid=(BATCH,),
        in_specs=[pl.BlockSpec((SEQ, n_pairs * pair_w), lambda b: (off + b, 0)),
                  pl.BlockSpec((SEQ, n_pairs * pair_w), lambda b: (off + b, 0)),
                  pl.BlockSpec((SEQ, n_pairs * pair_w), lambda b: (off + b, 0)),
                  pl.BlockSpec(memory_space=pl.ANY)],
        out_specs=pl.BlockSpec((SEQ, n_pairs * LANES), lambda b: (off + b, 0)),
        out_shape=jax.ShapeDtypeStruct((N_TOK, C_HEADS * C_VDIM), BF),
        input_output_aliases={3: 0},
        compiler_params=_params("arbitrary"),
        name="mla_attn_context",
    )(q, k, v, o)


def kernel(x_prompt, x_sample, c, cache_win_k, cache_win_v, cache_mla_ckv, cache_mla_krope, c_ctx,
           ada_w, ada_b, ffn_w_in, ffn_w_out,
           gmlp_w_in, gmlp_v_gain, gmlp_w_s, gmlp_b_s, gmlp_w_out,
           win_w_qkv, win_q_gain, win_k_gain, win_sink, win_w_o,
           mla_w_down, mla_cq_gain, mla_ckv_gain, mla_w_uq, mla_w_ukv, mla_q_gain, mla_k_gain, mla_w_o):
    x = (x_sample.reshape(N_SAMPLE, D_MODEL), x_prompt.reshape(N_PROMPT, D_MODEL))
    cond = jnp.concatenate([c, c_ctx[None, :], jnp.zeros((N_COND - DEC_BATCH - 1, D_MODEL), F32)], axis=0)
    mods = _modulation(cond, ada_w, ada_b)
    w_in_b, shift_terms = _ffn_prep(mods, ffn_w_in)
    w_out_b = ffn_w_out.astype(BF)

    lane = np.arange(LANES)
    b_cos, b_sin = _rope_tables(B_HEAD_DIM, lane % B_HEAD_DIM)
    c_lane = np.where((lane >= C_NOPE) & (lane < C_NOPE + C_ROPE), lane - C_NOPE, -1)
    c_cos, c_sin = _rope_tables(C_ROPE, c_lane)

    nk = B_KV_HEADS * B_HEAD_DIM
    win_k, win_v, mla_ckv, mla_krope = [], [], [], []
    ia = ib = ic = 0
    for l in range(DEPTH):
        mod = mods[l]
        pre_proj = None
        x = _ffn(x, mod, shift_terms, w_in_b, w_out_b, l, 0, split_in=(l == 0))
        kind = l % N_MIXERS
        if kind == 0:
            x = _gmlp(x, mod, gmlp_w_in[ia], gmlp_v_gain[ia], gmlp_w_s[ia], gmlp_b_s[ia], gmlp_w_out[ia])
            ia += 1
        elif kind == 1:
            q, k, v = _bproj(x, mod, win_w_qkv[ib], win_q_gain[ib], win_k_gain[ib], b_cos, b_sin)
            x = _battn(x, mod, q, k, v,
                       cache_win_k[:, ib].reshape(DEC_BATCH, PAST_LEN, nk),
                       cache_win_v[:, ib].reshape(DEC_BATCH, PAST_LEN, nk),
                       win_sink[ib], win_w_o[ib].astype(BF))
            win_k.append(k[N_SAMPLE:].reshape(BATCH, SEQ, B_KV_HEADS, B_HEAD_DIM))
            win_v.append(v[N_SAMPLE:].reshape(BATCH, SEQ, B_KV_HEADS, B_HEAD_DIM))
            ib += 1
        else:
            wd, wuq, qg, qgs, kg, kgs = _mla_weights(mla_w_down[ic], mla_w_uq[ic], mla_q_gain[ic], mla_k_gain[ic])
            wukv = mla_w_ukv[ic].astype(BF)
            q_scale = (C_NOPE + C_ROPE) ** -0.5 * LOG2E
            q, k, v, ckv, kr = _cproj(x, mod, wd, mla_cq_gain[ic], mla_ckv_gain[ic], wuq, wukv,
                                      c_cos * (qg * q_scale), c_sin * (qgs * q_scale), c_cos * kg, c_sin * kgs)
            ck, cv = _cctx(cache_mla_ckv[:, ic], cache_mla_krope[:, ic], wukv, kg)
            pre_proj = (_cattn(q, k, v, ck, cv), mla_w_o[ic].astype(BF))
            mla_ckv.append(ckv[N_SAMPLE:].reshape(BATCH, SEQ, C_KV_LORA))
            mla_krope.append(kr[N_SAMPLE:, C_NOPE:C_NOPE + C_ROPE].reshape(BATCH, SEQ, C_ROPE))
            ic += 1
        x = _ffn(x, mod, shift_terms, w_in_b, w_out_b, l, 1, split_out=(l == DEPTH - 1), pre_proj=pre_proj)
    y_sample, y_prompt = x
    return (y_prompt.reshape(BATCH, SEQ, D_MODEL), y_sample.reshape(DEC_BATCH, DEC_SEQ, D_MODEL),
            jnp.stack(win_k, axis=1), jnp.stack(win_v, axis=1),
            jnp.stack(mla_ckv, axis=1), jnp.stack(mla_krope, axis=1))
```

```python
import functools
import math

import jax
import jax.numpy as jnp
import numpy as np
from jax import lax
from jax.experimental import pallas as pl
from jax.experimental.pallas import tpu as pltpu

D_MODEL = 1024
BATCH = 16
SEQ = 256
DEPTH = 4
DEC_BATCH = 8
DEC_SEQ = 4096
PAST_LEN = 256
GRID_W = 64
N_MIXERS = 3
N_MOD = 9
D_FF = 2816
A_WIDTH = D_MODEL
A_GROUPS = 8
A_CHUNK = 128
B_HEADS = 16
B_KV_HEADS = 4
B_HEAD_DIM = 64
B_WINDOW = 128
C_HEADS = 16
C_NOPE = 64
C_ROPE = 32
C_VDIM = 64
C_Q_LORA = 512
C_KV_LORA = 256
ROPE_BASE = 10000.0
EPS = 1e-6
NEG_INF = -1e30

LANES = 128
N_SAMPLE = DEC_BATCH * DEC_SEQ
N_PROMPT = BATCH * SEQ
N_TOK = N_SAMPLE + N_PROMPT
N_COND = 16
TM = 512
FM = 1024
FF_CHUNKS = ((0, 1536), (1536, 1280))
N_TILES = N_TOK // TM
N_SAMPLE_TILES = N_SAMPLE // TM
TILES_PER_SEQ = DEC_SEQ // TM
MOD_TN = 1536
SHIFT_TN = 1408
BQ = 256
CQ = 1024
CK = 256
C_HEAD_PAD = 128
C_DOWN_PAD = C_Q_LORA + C_KV_LORA + 2 * LANES
VMEM_LIMIT_BYTES = 56 * 1024 * 1024

LOG2E = math.log2(math.e)

BF = jnp.bfloat16
F32 = jnp.float32


def _params(*sem):
    return pltpu.CompilerParams(dimension_semantics=sem, vmem_limit_bytes=VMEM_LIMIT_BYTES)


def _dot(a, b):
    return jnp.dot(a, b, preferred_element_type=F32)


def _dot_t(a, b):
    return lax.dot_general(a, b, (((1,), (1,)), ((), ())), preferred_element_type=F32)


def _rms(x):
    return x * lax.rsqrt(jnp.mean(x * x, axis=-1, keepdims=True) + EPS)


def _ada(x, mod_ref, k):
    shift = mod_ref[3 * k:3 * k + 1, :]
    scale = mod_ref[3 * k + 1:3 * k + 2, :]
    return _rms(x) * (1.0 + scale) + shift


def _const_spec(shape):
    nd = len(shape)
    return pl.BlockSpec(shape, lambda *_: (0,) * nd, pipeline_mode=pl.Buffered(1))


def _tok_spec(width, tm=TM):
    return pl.BlockSpec((tm, width), lambda i: (i, 0))


def _mod_spec(tm):
    return pl.BlockSpec((None, N_MOD, D_MODEL), lambda i: (jnp.minimum(i * tm // DEC_SEQ, DEC_BATCH), 0, 0))


_MOD_SPEC = _mod_spec(TM)


def _rope_tile(i):
    return jnp.where(i < N_SAMPLE_TILES, i % TILES_PER_SEQ, TILES_PER_SEQ)


_ROPE_SPEC = pl.BlockSpec((TM, LANES), lambda i: (_rope_tile(i), 0))


def _mod_kernel(c_ref, w_ref, b_ref, o_ref):
    a = jax.nn.silu(c_ref[...]).astype(BF)
    o_ref[...] = _dot(a, w_ref[...].astype(BF)) + b_ref[...]


def _modulation(cond, ada_w, ada_b):
    n_out = N_MOD * D_MODEL
    out = pl.pallas_call(
        _mod_kernel,
        grid=(DEPTH, n_out // MOD_TN),
        in_specs=[
            pl.BlockSpec((N_COND, D_MODEL), lambda l, j: (0, 0)),
            pl.BlockSpec((None, D_MODEL, MOD_TN), lambda l, j: (l, 0, j)),
            pl.BlockSpec((None, 1, MOD_TN), lambda l, j: (l, 0, j)),
        ],
        out_specs=pl.BlockSpec((None, N_COND, MOD_TN), lambda l, j: (l, 0, j)),
        out_shape=jax.ShapeDtypeStruct((DEPTH, N_COND, n_out), F32),
        compiler_params=_params("arbitrary", "arbitrary"),
        name="modulation",
    )(cond, ada_w, ada_b.reshape(DEPTH, 1, n_out))
    return out.reshape(DEPTH, N_COND, N_MOD, D_MODEL)


def _ffn_prep_kernel(s_ref, w_ref, wb_ref, o_ref):
    wb = w_ref[...].astype(BF)
    wb_ref[...] = wb
    o_ref[...] = _dot(s_ref[...].astype(BF), wb)


def _ffn_prep(mods, w_in):
    shifts = jnp.stack([mods[:, :, 0], mods[:, :, 6]], axis=1)
    w_spec = pl.BlockSpec((None, None, D_MODEL, SHIFT_TN), lambda l, h, j: (l, h, 0, j))
    w_b, out = pl.pallas_call(
        _ffn_prep_kernel,
        grid=(DEPTH, 2, 2 * D_FF // SHIFT_TN),
        in_specs=[pl.BlockSpec((None, None, N_COND, D_MODEL), lambda l, h, j: (l, h, 0, 0)), w_spec],
        out_specs=[w_spec, pl.BlockSpec((None, None, N_COND, SHIFT_TN), lambda l, h, j: (l, h, 0, j))],
        out_shape=[jax.ShapeDtypeStruct(w_in.shape, BF),
                   jax.ShapeDtypeStruct((DEPTH, 2, N_COND, 2 * D_FF), F32)],
        compiler_params=_params("arbitrary", "arbitrary", "arbitrary"),
        name="ffn_prep",
    )(shifts, w_in)
    return w_b, out.reshape(DEPTH, 2, N_COND, 1, 2 * D_FF)


def _ffn_kernel(*refs, k, split_in, split_out, pre_proj):
    n_x = 2 if split_in else 1
    n_in = n_x + (2 if pre_proj else 0)
    x_refs, (mod_ref, sh_ref, win_ref, wout_ref), o_refs = refs[:n_x], refs[n_in:n_in + 4], refs[n_in + 4:]
    is_sample = pl.program_id(0) < N_SAMPLE // FM
    x = jnp.where(is_sample, x_refs[0][...], x_refs[1][...]) if split_in else x_refs[0][...]
    if pre_proj:
        attn_ref, wo_ref = refs[n_x:n_in]
        x = x + mod_ref[5:6, :] * _dot(attn_ref[...], wo_ref[...])
    xa = (x * (1.0 + mod_ref[3 * k + 1:3 * k + 2, :])).astype(BF)
    rinv = lax.rsqrt(jnp.mean(x * x, axis=-1, keepdims=True) + EPS)
    y = None
    for c0, cw in FF_CHUNKS:
        g = _dot(xa, win_ref[:, c0:c0 + cw]) * rinv + sh_ref[:, c0:c0 + cw]
        u = _dot(xa, win_ref[:, D_FF + c0:D_FF + c0 + cw]) * rinv + sh_ref[:, D_FF + c0:D_FF + c0 + cw]
        yc = _dot((jax.nn.silu(g) * u).astype(BF), wout_ref[c0:c0 + cw, :])
        y = yc if y is None else y + yc
    gate = mod_ref[3 * k + 2:3 * k + 3, :]
    out = x + (0.5 * gate) * y
    if split_out:
        o_refs[1][...] = out

        @pl.when(is_sample)
        def _():
            o_refs[0][...] = out
    else:
        o_refs[0][...] = out


_SAMPLE_SPEC = pl.BlockSpec((FM, D_MODEL), lambda i: (jnp.minimum(i, N_SAMPLE // FM - 1), 0))
_PROMPT_SPEC = pl.BlockSpec((FM, D_MODEL), lambda i: (jnp.maximum(i - N_SAMPLE // FM, 0), 0))


def _ffn(xs, mod, shift_terms, w_in, w_out, layer, half, split_in=False, split_out=False, pre_proj=None):
    def w_spec(rows, cols):
        return pl.BlockSpec((None, None, rows, cols), lambda i: (layer, half, 0, 0), pipeline_mode=pl.Buffered(1))

    sh_spec = pl.BlockSpec((None, None, None, 1, 2 * D_FF),
                           lambda i: (layer, half, jnp.minimum(i * FM // DEC_SEQ, DEC_BATCH), 0, 0))

    x_specs = [_SAMPLE_SPEC, _PROMPT_SPEC] if split_in else [_tok_spec(D_MODEL, FM)]
    operands = list(xs) if split_in else [xs]
    if pre_proj is not None:
        x_specs += [_tok_spec(D_MODEL, FM), _const_spec((D_MODEL, D_MODEL))]
        operands += list(pre_proj)
    if split_out:
        out_specs = [_SAMPLE_SPEC, _PROMPT_SPEC]
        out_shape = [jax.ShapeDtypeStruct((N_SAMPLE, D_MODEL), F32), jax.ShapeDtypeStruct((N_PROMPT, D_MODEL), F32)]
    else:
        out_specs = _tok_spec(D_MODEL, FM)
        out_shape = jax.ShapeDtypeStruct((N_TOK, D_MODEL), F32)
    return pl.pallas_call(
        functools.partial(_ffn_kernel, k=2 * half, split_in=split_in, split_out=split_out,
                          pre_proj=pre_proj is not None),
        grid=(N_TOK // FM,),
        in_specs=x_specs + [_mod_spec(FM), sh_spec, w_spec(D_MODEL, 2 * D_FF), w_spec(D_FF, D_MODEL)],
        out_specs=out_specs,
        out_shape=out_shape,
        compiler_params=_params("arbitrary"),
        name="ffn",
    )(*operands, mod, shift_terms, w_in, w_out)


def _gmlp_kernel(x_ref, mod_ref, win_ref, vg_ref, ws_ref, bs_ref, wout_ref, o_ref):
    x = x_ref[...]
    hb = _ada(x, mod_ref, 1).astype(BF)
    pre = _dot(hb, win_ref[...])
    uv = 0.5 * pre * (1.0 + lax.erf(pre * math.sqrt(0.5)))
    u = uv[:, :A_WIDTH]
    v = (_rms(uv[:, A_WIDTH:]) * vg_ref[...]).astype(BF)
    bias = bs_ref[...]
    rows = []
    for c in range(FM // A_CHUNK):
        cols = [_dot(ws_ref[g], v[c * A_CHUNK:(c + 1) * A_CHUNK, g * LANES:(g + 1) * LANES])
                for g in range(A_GROUPS)]
        rows.append(jnp.concatenate(cols, axis=1) + bias)
    sv = jnp.concatenate(rows, axis=0)
    y = _dot((u * sv).astype(BF), wout_ref[...])
    o_ref[...] = x + mod_ref[5:6, :] * y


def _gmlp(x, mod, w_in, v_gain, w_s, b_s, w_out):
    bias = jnp.repeat(b_s.T, A_WIDTH // A_GROUPS, axis=1)
    return pl.pallas_call(
        _gmlp_kernel,
        grid=(N_TOK // FM,),
        in_specs=[_tok_spec(D_MODEL, FM), _mod_spec(FM),
                  _const_spec((D_MODEL, 2 * A_WIDTH)), _const_spec((1, A_WIDTH)),
                  _const_spec((A_GROUPS, A_CHUNK, A_CHUNK)), _const_spec((A_CHUNK, A_WIDTH)),
                  _const_spec((A_WIDTH, D_MODEL))],
        out_specs=_tok_spec(D_MODEL, FM),
        out_shape=jax.ShapeDtypeStruct((N_TOK, D_MODEL), F32),
        compiler_params=_params("arbitrary"),
        name="gmlp",
    )(x, mod, w_in.astype(BF), v_gain.reshape(1, A_WIDTH), w_s.astype(BF), bias, w_out.astype(BF))


def _swap_pairs(y, step):
    lane = lax.broadcasted_iota(jnp.int32, y.shape, 1)
    return jnp.where((lane & step) != 0, pltpu.roll(y, step, 1), pltpu.roll(y, LANES - step, 1))


def _rope_tables(rot_dim, lane_of_dim):
    quarter = rot_dim // 4
    inv = np.float32(ROPE_BASE) ** (-np.arange(quarter, dtype=np.float32) / np.float32(quarter))
    t = np.arange(DEC_SEQ)
    row = (t // GRID_W).astype(np.float32)
    col = (t % GRID_W).astype(np.float32)
    ang = np.stack([row[:, None] * inv, col[:, None] * inv], axis=1)
    cos, sin = np.cos(ang), np.sin(ang)
    d = np.asarray(lane_of_dim)
    dd = np.maximum(d, 0)
    axis, member, freq = dd // (2 * quarter), (dd % (2 * quarter)) // quarter, dd % quarter
    rot = (d >= 0)[None, :]
    c_tab = np.where(rot, cos[:, axis, freq], 1.0)
    s_tab = np.where(rot, np.where(member == 0, -1.0, 1.0)[None, :] * sin[:, axis, freq], 0.0)
    ident_c = np.ones((TM, LANES), np.float32)
    ident_s = np.zeros((TM, LANES), np.float32)
    return (np.concatenate([c_tab, ident_c], axis=0).astype(np.float32),
            np.concatenate([s_tab, ident_s], axis=0).astype(np.float32))


def _bproj_kernel(x_ref, mod_ref, w_ref, ones_ref, qc_ref, qs_ref, kc_ref, ks_ref, q_ref, k_ref, v_ref):
    hb = _ada(x_ref[...], mod_ref, 1).astype(BF)
    qkv = _dot(hb, w_ref[...])
    head_ones = ones_ref[...]

    def norm_rope(t4, tab, tab_swap):
        sq = t4 * t4
        hi = sq.astype(BF)
        lo = (sq - hi.astype(F32)).astype(BF)
        r = lax.rsqrt((_dot(hi, head_ones) + _dot(lo, head_ones)) / B_HEAD_DIM + EPS)
        halves = []
        for j in range(2):
            sl = slice(j * LANES, (j + 1) * LANES)
            t = t4[:, sl]
            halves.append(r[:, sl] * (t * tab + _swap_pairs(t, B_HEAD_DIM // 4) * tab_swap))
        return halves

    nq = B_HEADS * B_HEAD_DIM
    nk = B_KV_HEADS * B_HEAD_DIM
    q_tab, q_tab_swap = qc_ref[...], qs_ref[...]
    for j in range(nq // (2 * LANES)):
        halves = norm_rope(qkv[:, 2 * j * LANES:(2 * j + 2) * LANES], q_tab, q_tab_swap)
        q_ref[:, 2 * j * LANES:(2 * j + 1) * LANES] = halves[0].astype(BF)
        q_ref[:, (2 * j + 1) * LANES:(2 * j + 2) * LANES] = halves[1].astype(BF)
    halves = norm_rope(qkv[:, nq:nq + nk], kc_ref[...], ks_ref[...])
    k_ref[:, :LANES] = halves[0]
    k_ref[:, LANES:] = halves[1]
    v_ref[...] = qkv[:, nq + nk:]


def _bproj(x, mod, w_qkv, q_gain, k_gain, cos, sin):
    nq = B_HEADS * B_HEAD_DIM
    nk = B_KV_HEADS * B_HEAD_DIM
    q_scale = B_HEAD_DIM ** -0.5 * LOG2E
    lane = np.arange(LANES)
    qg = jnp.tile(q_gain, LANES // B_HEAD_DIM)
    kg = jnp.tile(k_gain, LANES // B_HEAD_DIM)
    partner = lane ^ (B_HEAD_DIM // 4)
    head = np.arange(nk) // B_HEAD_DIM
    head_ones = jnp.asarray(head[:, None] == head[None, :], BF)
    return pl.pallas_call(
        _bproj_kernel,
        grid=(N_TILES,),
        in_specs=[_tok_spec(D_MODEL), _MOD_SPEC, _const_spec((D_MODEL, nq + 2 * nk)), _const_spec((nk, nk)),
                  _ROPE_SPEC, _ROPE_SPEC, _ROPE_SPEC, _ROPE_SPEC],
        out_specs=[_tok_spec(nq), _tok_spec(nk), _tok_spec(nk)],
        out_shape=[jax.ShapeDtypeStruct((N_TOK, nq), BF),
                   jax.ShapeDtypeStruct((N_TOK, nk), F32),
                   jax.ShapeDtypeStruct((N_TOK, nk), F32)],
        compiler_params=_params("arbitrary"),
        name="gqa_proj",
    )(x, mod, w_qkv.astype(BF), head_ones,
      cos * (qg * q_scale), sin * (qg[partner] * q_scale), cos * kg, sin * kg[partner])


def _gqa_attend(q, kcat, vcat, bias, sink_ref):
    tq = q.shape[0]
    nk = kcat.shape[0]
    lo = lax.broadcasted_iota(jnp.int32, (nk, LANES), 1) < B_HEAD_DIM
    lo_q = lax.broadcasted_iota(jnp.int32, (2 * tq, LANES), 1) < B_HEAD_DIM
    first = lax.broadcasted_iota(jnp.int32, (2 * tq, 1), 0) < tq
    if bias is not None:
        bias = jnp.concatenate([bias, bias], axis=0)
    outs = []
    for g in range(B_KV_HEADS):
        sl = slice((g // 2) * LANES, (g // 2 + 1) * LANES)
        own = lo if g % 2 == 0 else jnp.logical_not(lo)
        k_own = jnp.where(own, kcat[:, sl], 0.0)
        k_swp = pltpu.roll(k_own, B_HEAD_DIM, 1)
        v_own = jnp.where(own, vcat[:, sl], 1.0)
        v_swp = pltpu.roll(v_own, B_HEAD_DIM, 1)
        k_half = (k_own, k_swp) if g % 2 == 0 else (k_swp, k_own)
        v_half = (v_own, v_swp) if g % 2 == 0 else (v_swp, v_own)
        qg = jnp.concatenate([q[:, (2 * g) * LANES:(2 * g + 1) * LANES],
                              q[:, (2 * g + 1) * LANES:(2 * g + 2) * LANES]], axis=0)
        s_all = _dot_t(qg, jnp.concatenate(k_half, axis=0).astype(BF))
        o_half = []
        for e in range(2):
            s = s_all[:, e * nk:(e + 1) * nk]
            if bias is not None:
                nb = bias.shape[1]
                s = jnp.concatenate([s[:, :nb] + bias, s[:, nb:]], axis=1)
            sk = jnp.where(first, sink_ref[4 * g + e], sink_ref[4 * g + 2 + e]) * LOG2E
            m = jnp.maximum(jnp.max(s, axis=-1, keepdims=True), sk)
            ov = _dot(jnp.exp2(s - m).astype(BF), v_half[e].astype(BF))
            o_half.append(ov / (pltpu.roll(ov, B_HEAD_DIM, 1) + jnp.exp2(sk - m)))
        o_g = jnp.where(lo_q, o_half[0], o_half[1])
        outs += [o_g[:tq], o_g[tq:]]
    return jnp.concatenate(outs, axis=1)


def _battn_lat_kernel(x_ref, mod_ref, q_ref, kp_ref, kc_ref, kn_ref, vp_ref, vc_ref, vn_ref,
                      ck_ref, cv_ref, sink_ref, wo_ref, o_ref):
    j = pl.program_id(1)
    kcat = jnp.concatenate([kp_ref[...], kc_ref[...], kn_ref[...], ck_ref[...]], axis=0)
    vcat = jnp.concatenate([vp_ref[...], vc_ref[...], vn_ref[...], cv_ref[...]], axis=0)
    n_lat = BQ + 2 * B_WINDOW
    qi = lax.broadcasted_iota(jnp.int32, (BQ, n_lat), 0)
    pk = lax.broadcasted_iota(jnp.int32, (BQ, n_lat), 1)
    kpos = j * BQ + pk - B_WINDOW
    valid = (jnp.abs(pk - B_WINDOW - qi) <= B_WINDOW) & (kpos >= 0) & (kpos < DEC_SEQ)
    bias = jnp.where(valid, 0.0, NEG_INF)
    o = _gqa_attend(q_ref[...], kcat, vcat, bias, sink_ref)
    y = _dot(o.astype(BF), wo_ref[...])
    o_ref[...] = x_ref[...] + mod_ref[5:6, :] * y


def _battn_ctx_kernel(x_ref, mod_ref, q_ref, k_ref, v_ref, sink_ref, wo_ref, o_ref):
    o = _gqa_attend(q_ref[...], k_ref[...], v_ref[...], None, sink_ref)
    y = _dot(o.astype(BF), wo_ref[...])
    o_ref[...] = x_ref[...] + mod_ref[5:6, :] * y


def _battn(x, mod, q, k, v, cache_k, cache_v, sink, w_o):
    nq = B_HEADS * B_HEAD_DIM
    nk = B_KV_HEADS * B_HEAD_DIM
    nb = DEC_SEQ // BQ
    nw = DEC_SEQ // B_WINDOW
    per = BQ // B_WINDOW
    smem = pl.BlockSpec(memory_space=pltpu.SMEM)
    cur_spec = pl.BlockSpec((BQ, nk), lambda b, j: (b * nb + j, 0))
    prev_spec = pl.BlockSpec((B_WINDOW, nk), lambda b, j: (b * nw + jnp.maximum(per * j - 1, 0), 0))
    next_spec = pl.BlockSpec((B_WINDOW, nk), lambda b, j: (b * nw + jnp.minimum(per * j + per, nw - 1), 0))

    x = pl.pallas_call(
        _battn_lat_kernel,
        grid=(DEC_BATCH, nb),
        in_specs=[pl.BlockSpec((BQ, D_MODEL), lambda b, j: (b * nb + j, 0)),
                  pl.BlockSpec((None, N_MOD, D_MODEL), lambda b, j: (b, 0, 0)),
                  pl.BlockSpec((BQ, nq), lambda b, j: (b * nb + j, 0)),
                  prev_spec, cur_spec, next_spec, prev_spec, cur_spec, next_spec,
                  pl.BlockSpec((None, PAST_LEN, nk), lambda b, j: (b, 0, 0)),
                  pl.BlockSpec((None, PAST_LEN, nk), lambda b, j: (b, 0, 0)),
                  smem, _const_spec((nq, D_MODEL))],
        out_specs=pl.BlockSpec((BQ, D_MODEL), lambda b, j: (b * nb + j, 0)),
        out_shape=jax.ShapeDtypeStruct((N_TOK, D_MODEL), F32),
        input_output_aliases={0: 0},
        compiler_params=_params("arbitrary", "arbitrary"),
        name="gqa_attn_latent",
    )(x, mod, q, k, k, k, v, v, v, cache_k, cache_v, sink, w_o)
    off = N_SAMPLE // SEQ
    return pl.pallas_call(
        _battn_ctx_kernel,
        grid=(BATCH,),
        in_specs=[pl.BlockSpec((SEQ, D_MODEL), lambda b: (off + b, 0)),
                  pl.BlockSpec((None, N_MOD, D_MODEL), lambda b: (DEC_BATCH, 0, 0)),
                  pl.BlockSpec((SEQ, nq), lambda b: (off + b, 0)),
                  pl.BlockSpec((SEQ, nk), lambda b: (off + b, 0)),
                  pl.BlockSpec((SEQ, nk), lambda b: (off + b, 0)),
                  smem, _const_spec((nq, D_MODEL))],
        out_specs=pl.BlockSpec((SEQ, D_MODEL), lambda b: (off + b, 0)),
        out_shape=jax.ShapeDtypeStruct((N_TOK, D_MODEL), F32),
        input_output_aliases={0: 0},
        compiler_params=_params("arbitrary"),
        name="gqa_attn_context",
    )(x, mod, q, k, v, sink, w_o)


C_SWAP_W = C_HEADS * C_ROPE
C_SHIFT_LANE = C_NOPE + C_ROPE
C_SHIFT_MAX = 50.0


def _mla_head_norm_rope(t, t_swap, tab, tab_swap):
    r = lax.rsqrt(jnp.sum(t * t, axis=-1, keepdims=True) / (C_NOPE + C_ROPE) + EPS)
    if t_swap is None:
        return t * r * tab
    return r * (t * tab + t_swap * tab_swap)


def _mla_keys_values(c_kv_b, k_rope, k_rope_swap, wukv_ref, tab, tab_swap, k_shift, k_ref, v_ref):
    kv = _dot(c_kv_b, wukv_ref[...])
    lower = (lax.broadcasted_iota(jnp.int32, kv.shape, 1) & C_NOPE) == 0
    v_ref[...] = jnp.where(lower, 1.0, kv).astype(BF)
    k_nope = jnp.where(lower, kv, 0.0)
    for h in range(C_HEADS):
        sl = slice(h * C_HEAD_PAD, (h + 1) * C_HEAD_PAD)
        kh = _mla_head_norm_rope(k_nope[:, sl] + k_rope, k_rope_swap, tab, tab_swap)
        k_ref[:, sl] = (kh + k_shift).astype(BF)


def _cproj_kernel(x_ref, mod_ref, wd_ref, cqg_ref, ckvg_ref, wuq_ref, wukv_ref, qone_ref, kshift_ref,
                  qc_ref, qs_ref, kc_ref, ks_ref, q_ref, k_ref, v_ref, ckv_ref, kr_ref):
    hb = _ada(x_ref[...], mod_ref, 1).astype(BF)
    d = _dot(hb, wd_ref[...])
    c_q = _rms(d[:, :C_Q_LORA]) * cqg_ref[...]
    c_kv = _rms(d[:, C_Q_LORA:C_Q_LORA + C_KV_LORA]) * ckvg_ref[...]
    k_rope = d[:, C_Q_LORA + C_KV_LORA:C_Q_LORA + C_KV_LORA + LANES]
    k_rope_swap = d[:, C_Q_LORA + C_KV_LORA + LANES:]
    ckv_ref[...] = c_kv
    kr_ref[...] = k_rope
    q2 = _dot(c_q.astype(BF), wuq_ref[...])
    wq = C_HEADS * C_HEAD_PAD
    q_tab, q_tab_swap = qc_ref[...], qs_ref[...]
    per_tile = LANES // C_ROPE
    for h in range(C_HEADS):
        sl = slice(h * C_HEAD_PAD, (h + 1) * C_HEAD_PAD)
        t_swap = q2[:, wq + (h // per_tile) * LANES:wq + (h // per_tile + 1) * LANES]
        shift = (C_NOPE - C_ROPE * (h % per_tile)) % LANES
        if shift:
            t_swap = pltpu.roll(t_swap, shift, 1)
        q_ref[:, sl] = (_mla_head_norm_rope(q2[:, sl], t_swap, q_tab, q_tab_swap) + qone_ref[...]).astype(BF)
    _mla_keys_values(c_kv.astype(BF), k_rope, k_rope_swap, wukv_ref, kc_ref[...], ks_ref[...], kshift_ref[...],
                     k_ref, v_ref)


def _cctx_kernel(ckv_ref, kr_ref, wukv_ref, kg_ref, kshift_ref, k_ref, v_ref):
    _mla_keys_values(ckv_ref[...].astype(BF), kr_ref[...], None, wukv_ref, kg_ref[...], None, kshift_ref[...],
                     k_ref, v_ref)


def _mla_weights(w_down, w_uq, q_gain, k_gain):
    hd = C_NOPE + C_ROPE
    pad_lanes = C_HEAD_PAD - hd
    lane = np.arange(C_HEAD_PAD)
    is_rope = (lane >= C_NOPE) & (lane < hd)
    partner = np.where(is_rope, lane ^ (C_ROPE // 4), lane)

    def swapped(t):
        return jnp.where(is_rope, jnp.take(t, partner, axis=-1), 0.0)

    kr_cols = jnp.pad(w_down[:, C_Q_LORA + C_KV_LORA:], ((0, 0), (C_NOPE, pad_lanes)))
    wd = jnp.concatenate([w_down[:, :C_Q_LORA + C_KV_LORA], kr_cols, swapped(kr_cols)], axis=1)
    wuq = jnp.pad(w_uq.reshape(C_Q_LORA, C_HEADS, hd), ((0, 0), (0, 0), (0, pad_lanes)))
    wuq_swap = swapped(wuq)[:, :, C_NOPE:hd].reshape(C_Q_LORA, C_SWAP_W)
    wuq = jnp.concatenate([wuq.reshape(C_Q_LORA, C_HEADS * C_HEAD_PAD), wuq_swap], axis=1)
    qg = jnp.pad(q_gain, (0, pad_lanes))
    kg = jnp.pad(k_gain, (0, pad_lanes))
    row = lambda t: t.reshape(1, C_HEAD_PAD)
    bound = 1.02 * math.sqrt(hd) * LOG2E * jnp.max(jnp.abs(q_gain)) * jnp.max(jnp.abs(k_gain))
    shift_lane = lane == C_SHIFT_LANE
    q_one = row(jnp.asarray(shift_lane, F32))
    k_shift = row(jnp.where(shift_lane, -bound, 0.0))
    fixed_shift_ok = (bound <= C_SHIFT_MAX).astype(jnp.int32).reshape(1)
    return (wd.astype(BF), wuq.astype(BF), row(qg), row(swapped(qg)), row(kg), row(swapped(kg)),
            q_one, k_shift, fixed_shift_ok)


def _cproj(x, mod, wd, cq_gain, ckv_gain, wuq, wukv, q_one, k_shift, q_tab, q_tab_swap, k_tab, k_tab_swap):
    wq = C_HEADS * C_HEAD_PAD
    return pl.pallas_call(
        _cproj_kernel,
        grid=(N_TILES,),
        in_specs=[_tok_spec(D_MODEL), _MOD_SPEC, _const_spec((D_MODEL, C_DOWN_PAD)),
                  _const_spec((1, C_Q_LORA)), _const_spec((1, C_KV_LORA)),
                  _const_spec((C_Q_LORA, wq + C_SWAP_W)), _const_spec((C_KV_LORA, wq)),
                  _const_spec((1, C_HEAD_PAD)), _const_spec((1, C_HEAD_PAD)),
                  _ROPE_SPEC, _ROPE_SPEC, _ROPE_SPEC, _ROPE_SPEC],
        out_specs=[_tok_spec(wq), _tok_spec(wq), _tok_spec(wq), _tok_spec(C_KV_LORA), _tok_spec(LANES)],
        out_shape=[jax.ShapeDtypeStruct((N_TOK, wq), BF), jax.ShapeDtypeStruct((N_TOK, wq), BF),
                   jax.ShapeDtypeStruct((N_TOK, wq), BF), jax.ShapeDtypeStruct((N_TOK, C_KV_LORA), F32),
                   jax.ShapeDtypeStruct((N_TOK, LANES), F32)],
        compiler_params=_params("arbitrary"),
        name="mla_proj",
    )(x, mod, wd, cq_gain.reshape(1, C_Q_LORA), ckv_gain.reshape(1, C_KV_LORA), wuq, wukv, q_one, k_shift,
      q_tab, q_tab_swap, k_tab, k_tab_swap)


def _cctx(cache_ckv, cache_krope, wukv, kg, k_shift):
    n = DEC_BATCH * PAST_LEN
    wq = C_HEADS * C_HEAD_PAD
    kr = jnp.pad(cache_krope.reshape(n, C_ROPE), ((0, 0), (C_NOPE, C_HEAD_PAD - C_NOPE - C_ROPE)))
    return pl.pallas_call(
        _cctx_kernel,
        grid=(n // TM,),
        in_specs=[_tok_spec(C_KV_LORA), _tok_spec(LANES), _const_spec((C_KV_LORA, wq)),
                  _const_spec((1, C_HEAD_PAD)), _const_spec((1, C_HEAD_PAD))],
        out_specs=[_tok_spec(wq), _tok_spec(wq)],
        out_shape=[jax.ShapeDtypeStruct((n, wq), BF), jax.ShapeDtypeStruct((n, wq), BF)],
        compiler_params=_params("arbitrary"),
        name="mla_context_keys",
    )(cache_ckv.reshape(n, C_KV_LORA), kr, wukv, kg, k_shift)


def _mla_attend(ok_ref, q_ref, kv_refs, o_ref, pair=0):
    tq = q_ref.shape[0]
    cols = slice(pair * LANES, (pair + 1) * LANES)
    head_cols = [slice((2 * pair + e) * C_HEAD_PAD, (2 * pair + e + 1) * C_HEAD_PAD) for e in range(2)]

    def normalised(sum_acc, rows):
        lo = lax.broadcasted_iota(jnp.int32, (rows, LANES), 1) < C_VDIM
        r0 = pltpu.roll(sum_acc[0], C_VDIM, 1)
        r1 = pltpu.roll(sum_acc[1], C_VDIM, 1)
        return jnp.where(lo, r0 / sum_acc[0], sum_acc[1] / r1).astype(BF)

    chunks = []
    for k_ref, v_ref in kv_refs:
        nk = k_ref.shape[0]
        if nk < CK and chunks:
            chunks[-1].append((k_ref, v_ref, slice(0, nk)))
        else:
            ck = min(CK, nk)
            chunks += [[(k_ref, v_ref, slice(c * ck, (c + 1) * ck))] for c in range(nk // ck)]
    def fixed_shift():
        sum_acc = []
        for sl in head_cols:
            qh = q_ref[:, sl]
            acc = None
            for parts in chunks:
                kk = [k_ref[rows, sl] for k_ref, _, rows in parts]
                vv = [v_ref[rows, sl] for _, v_ref, rows in parts]
                p = jnp.exp2(_dot_t(qh, kk[0] if len(kk) == 1 else jnp.concatenate(kk, axis=0))).astype(BF)
                pv = _dot(p, vv[0] if len(vv) == 1 else jnp.concatenate(vv, axis=0))
                acc = pv if acc is None else acc + pv
            sum_acc.append(acc)
        o_ref[:, cols] = normalised(sum_acc, tq)

    def row_max_shift():
        blk = min(tq, LANES)

        def body(i, carry):
            rows = pl.ds(pl.multiple_of(i * blk, blk), blk)
            sum_acc = []
            for sl in head_cols:
                qh = q_ref[rows, sl]
                scores = [_dot_t(qh, k_ref[:, sl]) for k_ref, _ in kv_refs]
                m = functools.reduce(jnp.maximum, [jnp.max(sc, axis=-1, keepdims=True) for sc in scores])
                sum_acc.append(functools.reduce(jnp.add, [_dot(jnp.exp2(sc - m).astype(BF), v_ref[:, sl])
                                                          for sc, (_, v_ref) in zip(scores, kv_refs)]))
            o_ref[rows, cols] = normalised(sum_acc, blk)
            return carry

        lax.fori_loop(0, tq // blk, body, 0)

    lax.cond(ok_ref[0] != 0, fixed_shift, row_max_shift)


def _cattn_lat_kernel(ok_ref, q_ref, k_ref, v_ref, ck_ref, cv_ref, o_ref):
    _mla_attend(ok_ref, q_ref, ((k_ref, v_ref), (ck_ref, cv_ref)), o_ref)


def _cattn_ctx_kernel(ok_ref, q_ref, k_ref, v_ref, oin_ref, o_ref):
    del oin_ref
    for pair in range(C_HEADS // 2):
        _mla_attend(ok_ref, q_ref, ((k_ref, v_ref),), o_ref, pair)


def _cattn(fixed_shift_ok, q, k, v, ck, cv):
    pair_w = 2 * C_HEAD_PAD
    n_pairs = C_HEADS // 2
    nqt = DEC_SEQ // CQ
    smem = pl.BlockSpec(memory_space=pltpu.SMEM)
    o = pl.pallas_call(
        _cattn_lat_kernel,
        grid=(DEC_BATCH, n_pairs, nqt),
        in_specs=[smem, pl.BlockSpec((CQ, pair_w), lambda b, p, t: (b * nqt + t, p)),
                  pl.BlockSpec((DEC_SEQ, pair_w), lambda b, p, t: (b, p)),
                  pl.BlockSpec((DEC_SEQ, pair_w), lambda b, p, t: (b, p)),
                  pl.BlockSpec((PAST_LEN, pair_w), lambda b, p, t: (b, p)),
                  pl.BlockSpec((PAST_LEN, pair_w), lambda b, p, t: (b, p))],
        out_specs=pl.BlockSpec((CQ, LANES), lambda b, p, t: (b * nqt + t, p)),
        out_shape=jax.ShapeDtypeStruct((N_TOK, C_HEADS * C_VDIM), BF),
        compiler_params=_params("arbitrary", "arbitrary", "arbitrary"),
        name="mla_attn_latent",
    )(fixed_shift_ok, q, k, v, ck, cv)
    off = N_SAMPLE // SEQ
    return pl.pallas_call(
        _cattn_ctx_kernel,
        grid=(BATCH,),
        in_specs=[smem, pl.BlockSpec((SEQ, n_pairs * pair_w), lambda b: (off + b, 0)),
                  pl.BlockSpec((SEQ, n_pairs * pair_w), lambda b: (off + b, 0)),
                  pl.BlockSpec((SEQ, n_pairs * pair_w), lambda b: (off + b, 0)),
                  pl.BlockSpec(memory_space=pl.ANY)],
        out_specs=pl.BlockSpec((SEQ, n_pairs * LANES), lambda b: (off + b, 0)),
        out_shape=jax.ShapeDtypeStruct((N_TOK, C_HEADS * C_VDIM), BF),
        input_output_aliases={4: 0},
        compiler_params=_params("arbitrary"),
        name="mla_attn_context",
    )(fixed_shift_ok, q, k, v, o)


def kernel(x_prompt, x_sample, c, cache_win_k, cache_win_v, cache_mla_ckv, cache_mla_krope, c_ctx,
           ada_w, ada_b, ffn_w_in, ffn_w_out,
           gmlp_w_in, gmlp_v_gain, gmlp_w_s, gmlp_b_s, gmlp_w_out,
           win_w_qkv, win_q_gain, win_k_gain, win_sink, win_w_o,
           mla_w_down, mla_cq_gain, mla_ckv_gain, mla_w_uq, mla_w_ukv, mla_q_gain, mla_k_gain, mla_w_o):
    x = (x_sample.reshape(N_SAMPLE, D_MODEL), x_prompt.reshape(N_PROMPT, D_MODEL))
    cond = jnp.concatenate([c, c_ctx[None, :], jnp.zeros((N_COND - DEC_BATCH - 1, D_MODEL), F32)], axis=0)
    mods = _modulation(cond, ada_w, ada_b)
    w_in_b, shift_terms = _ffn_prep(mods, ffn_w_in)
    w_out_b = ffn_w_out.astype(BF)

    lane = np.arange(LANES)
    b_cos, b_sin = _rope_tables(B_HEAD_DIM, lane % B_HEAD_DIM)
    c_lane = np.where((lane >= C_NOPE) & (lane < C_NOPE + C_ROPE), lane - C_NOPE, -1)
    c_cos, c_sin = _rope_tables(C_ROPE, c_lane)

    nk = B_KV_HEADS * B_HEAD_DIM
    win_k, win_v, mla_ckv, mla_krope = [], [], [], []
    ia = ib = ic = 0
    for l in range(DEPTH):
        mod = mods[l]
        pre_proj = None
        x = _ffn(x, mod, shift_terms, w_in_b, w_out_b, l, 0, split_in=(l == 0))
        kind = l % N_MIXERS
        if kind == 0:
            x = _gmlp(x, mod, gmlp_w_in[ia], gmlp_v_gain[ia], gmlp_w_s[ia], gmlp_b_s[ia], gmlp_w_out[ia])
            ia += 1
        elif kind == 1:
            q, k, v = _bproj(x, mod, win_w_qkv[ib], win_q_gain[ib], win_k_gain[ib], b_cos, b_sin)
            x = _battn(x, mod, q, k, v,
                       cache_win_k[:, ib].reshape(DEC_BATCH, PAST_LEN, nk),
                       cache_win_v[:, ib].reshape(DEC_BATCH, PAST_LEN, nk),
                       win_sink[ib], win_w_o[ib].astype(BF))
            win_k.append(k[N_SAMPLE:].reshape(BATCH, SEQ, B_KV_HEADS, B_HEAD_DIM))
            win_v.append(v[N_SAMPLE:].reshape(BATCH, SEQ, B_KV_HEADS, B_HEAD_DIM))
            ib += 1
        else:
            wd, wuq, qg, qgs, kg, kgs, q_one, k_shift, fixed_shift_ok = _mla_weights(
                mla_w_down[ic], mla_w_uq[ic], mla_q_gain[ic], mla_k_gain[ic])
            wukv = mla_w_ukv[ic].astype(BF)
            q_scale = (C_NOPE + C_ROPE) ** -0.5 * LOG2E
            q, k, v, ckv, kr = _cproj(x, mod, wd, mla_cq_gain[ic], mla_ckv_gain[ic], wuq, wukv, q_one, k_shift,
                                      c_cos * (qg * q_scale), c_sin * (qgs * q_scale), c_cos * kg, c_sin * kgs)
            ck, cv = _cctx(cache_mla_ckv[:, ic], cache_mla_krope[:, ic], wukv, kg, k_shift)
            pre_proj = (_cattn(fixed_shift_ok, q, k, v, ck, cv), mla_w_o[ic].astype(BF))
            mla_ckv.append(ckv[N_SAMPLE:].reshape(BATCH, SEQ, C_KV_LORA))
            mla_krope.append(kr[N_SAMPLE:, C_NOPE:C_NOPE + C_ROPE].reshape(BATCH, SEQ, C_ROPE))
            ic += 1
        x = _ffn(x, mod, shift_terms, w_in_b, w_out_b, l, 1, split_out=(l == DEPTH - 1), pre_proj=pre_proj)
    y_sample, y_prompt = x
    return (y_prompt.reshape(BATCH, SEQ, D_MODEL), y_sample.reshape(DEC_BATCH, DEC_SEQ, D_MODEL),
            jnp.stack(win_k, axis=1), jnp.stack(win_v, axis=1),
            jnp.stack(mla_ckv, axis=1), jnp.stack(mla_krope, axis=1))
```

```python
import functools
import math

import jax
import jax.numpy as jnp
import numpy as np
from jax import lax
from jax.experimental import pallas as pl
from jax.experimental.pallas import tpu as pltpu

D_MODEL = 1024
BATCH = 16
SEQ = 256
DEPTH = 4
DEC_BATCH = 8
DEC_SEQ = 4096
PAST_LEN = 256
GRID_W = 64
N_MIXERS = 3
N_MOD = 9
D_FF = 2816
A_WIDTH = D_MODEL
A_GROUPS = 8
A_CHUNK = 128
B_HEADS = 16
B_KV_HEADS = 4
B_HEAD_DIM = 64
B_WINDOW = 128
C_HEADS = 16
C_NOPE = 64
C_ROPE = 32
C_VDIM = 64
C_Q_LORA = 512
C_KV_LORA = 256
ROPE_BASE = 10000.0
EPS = 1e-6
NEG_INF = -1e30

LANES = 128
N_SAMPLE = DEC_BATCH * DEC_SEQ
N_PROMPT = BATCH * SEQ
N_TOK = N_SAMPLE + N_PROMPT
N_COND = 16
TM = 512
FM = 1024
FF_CHUNKS = ((0, 1536), (1536, 1280))
N_TILES = N_TOK // TM
N_SAMPLE_TILES = N_SAMPLE // TM
TILES_PER_SEQ = DEC_SEQ // TM
MOD_TN = 1536
SHIFT_TN = 1408
BQ = 256
CQ = 1024
CK = 256
C_HEAD_PAD = 128
C_DOWN_PAD = C_Q_LORA + C_KV_LORA + 2 * LANES
VMEM_LIMIT_BYTES = 56 * 1024 * 1024

LOG2E = math.log2(math.e)

BF = jnp.bfloat16
F32 = jnp.float32


def _params(*sem):
    return pltpu.CompilerParams(dimension_semantics=sem, vmem_limit_bytes=VMEM_LIMIT_BYTES)


def _dot(a, b):
    return jnp.dot(a, b, preferred_element_type=F32)


def _dot_t(a, b):
    return lax.dot_general(a, b, (((1,), (1,)), ((), ())), preferred_element_type=F32)


def _rms(x):
    return x * lax.rsqrt(jnp.mean(x * x, axis=-1, keepdims=True) + EPS)


def _ada(x, mod_ref, k):
    shift = mod_ref[3 * k:3 * k + 1, :]
    scale = mod_ref[3 * k + 1:3 * k + 2, :]
    return _rms(x) * (1.0 + scale) + shift


def _const_spec(shape):
    nd = len(shape)
    return pl.BlockSpec(shape, lambda *_: (0,) * nd, pipeline_mode=pl.Buffered(1))


def _tok_spec(width, tm=TM):
    return pl.BlockSpec((tm, width), lambda i: (i, 0))


def _mod_spec(tm):
    return pl.BlockSpec((None, N_MOD, D_MODEL), lambda i: (jnp.minimum(i * tm // DEC_SEQ, DEC_BATCH), 0, 0))


_MOD_SPEC = _mod_spec(TM)


def _rope_tile(i):
    return jnp.where(i < N_SAMPLE_TILES, i % TILES_PER_SEQ, TILES_PER_SEQ)


_ROPE_SPEC = pl.BlockSpec((TM, LANES), lambda i: (_rope_tile(i), 0))


def _mod_kernel(c_ref, w_ref, b_ref, o_ref):
    a = jax.nn.silu(c_ref[...]).astype(BF)
    o_ref[...] = _dot(a, w_ref[...].astype(BF)) + b_ref[...]


def _modulation(cond, ada_w, ada_b):
    n_out = N_MOD * D_MODEL
    out = pl.pallas_call(
        _mod_kernel,
        grid=(DEPTH, n_out // MOD_TN),
        in_specs=[
            pl.BlockSpec((N_COND, D_MODEL), lambda l, j: (0, 0)),
            pl.BlockSpec((None, D_MODEL, MOD_TN), lambda l, j: (l, 0, j)),
            pl.BlockSpec((None, 1, MOD_TN), lambda l, j: (l, 0, j)),
        ],
        out_specs=pl.BlockSpec((None, N_COND, MOD_TN), lambda l, j: (l, 0, j)),
        out_shape=jax.ShapeDtypeStruct((DEPTH, N_COND, n_out), F32),
        compiler_params=_params("arbitrary", "arbitrary"),
        name="modulation",
    )(cond, ada_w, ada_b.reshape(DEPTH, 1, n_out))
    return out.reshape(DEPTH, N_COND, N_MOD, D_MODEL)


def _ffn_prep_kernel(s_ref, w_ref, wb_ref, o_ref):
    wb = w_ref[...].astype(BF)
    wb_ref[...] = wb
    o_ref[...] = _dot(s_ref[...].astype(BF), wb)


def _ffn_prep(mods, w_in):
    shifts = jnp.stack([mods[:, :, 0], mods[:, :, 6]], axis=1)
    w_spec = pl.BlockSpec((None, None, D_MODEL, SHIFT_TN), lambda l, h, j: (l, h, 0, j))
    w_b, out = pl.pallas_call(
        _ffn_prep_kernel,
        grid=(DEPTH, 2, 2 * D_FF // SHIFT_TN),
        in_specs=[pl.BlockSpec((None, None, N_COND, D_MODEL), lambda l, h, j: (l, h, 0, 0)), w_spec],
        out_specs=[w_spec, pl.BlockSpec((None, None, N_COND, SHIFT_TN), lambda l, h, j: (l, h, 0, j))],
        out_shape=[jax.ShapeDtypeStruct(w_in.shape, BF),
                   jax.ShapeDtypeStruct((DEPTH, 2, N_COND, 2 * D_FF), F32)],
        compiler_params=_params("arbitrary", "arbitrary", "arbitrary"),
        name="ffn_prep",
    )(shifts, w_in)
    return w_b, out.reshape(DEPTH, 2, N_COND, 1, 2 * D_FF)


def _ffn_kernel(*refs, k, split_in, split_out, pre_proj):
    n_x = 2 if split_in else 1
    n_in = n_x + (2 if pre_proj else 0)
    x_refs, (mod_ref, sh_ref, win_ref, wout_ref), o_refs = refs[:n_x], refs[n_in:n_in + 4], refs[n_in + 4:]
    is_sample = pl.program_id(0) < N_SAMPLE // FM
    x = jnp.where(is_sample, x_refs[0][...], x_refs[1][...]) if split_in else x_refs[0][...]
    if pre_proj:
        attn_ref, wo_ref = refs[n_x:n_in]
        x = x + mod_ref[5:6, :] * _dot(attn_ref[...], wo_ref[...])
    xa = (x * (1.0 + mod_ref[3 * k + 1:3 * k + 2, :])).astype(BF)
    rinv = lax.rsqrt(jnp.mean(x * x, axis=-1, keepdims=True) + EPS)
    y = None
    for c0, cw in FF_CHUNKS:
        g = _dot(xa, win_ref[:, c0:c0 + cw]) * rinv + sh_ref[:, c0:c0 + cw]
        u = _dot(xa, win_ref[:, D_FF + c0:D_FF + c0 + cw]) * rinv + sh_ref[:, D_FF + c0:D_FF + c0 + cw]
        yc = _dot((jax.nn.silu(g) * u).astype(BF), wout_ref[c0:c0 + cw, :])
        y = yc if y is None else y + yc
    gate = mod_ref[3 * k + 2:3 * k + 3, :]
    out = x + (0.5 * gate) * y
    if split_out:
        o_refs[1][...] = out

        @pl.when(is_sample)
        def _():
            o_refs[0][...] = out
    else:
        o_refs[0][...] = out


_SAMPLE_SPEC = pl.BlockSpec((FM, D_MODEL), lambda i: (jnp.minimum(i, N_SAMPLE // FM - 1), 0))
_PROMPT_SPEC = pl.BlockSpec((FM, D_MODEL), lambda i: (jnp.maximum(i - N_SAMPLE // FM, 0), 0))


def _ffn(xs, mod, shift_terms, w_in, w_out, layer, half, split_in=False, split_out=False, pre_proj=None):
    def w_spec(rows, cols):
        return pl.BlockSpec((None, None, rows, cols), lambda i: (layer, half, 0, 0), pipeline_mode=pl.Buffered(1))

    sh_spec = pl.BlockSpec((None, None, None, 1, 2 * D_FF),
                           lambda i: (layer, half, jnp.minimum(i * FM // DEC_SEQ, DEC_BATCH), 0, 0))

    x_specs = [_SAMPLE_SPEC, _PROMPT_SPEC] if split_in else [_tok_spec(D_MODEL, FM)]
    operands = list(xs) if split_in else [xs]
    if pre_proj is not None:
        x_specs += [_tok_spec(D_MODEL, FM), _const_spec((D_MODEL, D_MODEL))]
        operands += list(pre_proj)
    if split_out:
        out_specs = [_SAMPLE_SPEC, _PROMPT_SPEC]
        out_shape = [jax.ShapeDtypeStruct((N_SAMPLE, D_MODEL), F32), jax.ShapeDtypeStruct((N_PROMPT, D_MODEL), F32)]
    else:
        out_specs = _tok_spec(D_MODEL, FM)
        out_shape = jax.ShapeDtypeStruct((N_TOK, D_MODEL), F32)
    return pl.pallas_call(
        functools.partial(_ffn_kernel, k=2 * half, split_in=split_in, split_out=split_out,
                          pre_proj=pre_proj is not None),
        grid=(N_TOK // FM,),
        in_specs=x_specs + [_mod_spec(FM), sh_spec, w_spec(D_MODEL, 2 * D_FF), w_spec(D_FF, D_MODEL)],
        out_specs=out_specs,
        out_shape=out_shape,
        compiler_params=_params("arbitrary"),
        name="ffn",
    )(*operands, mod, shift_terms, w_in, w_out)


def _gmlp_kernel(x_ref, mod_ref, win_ref, vg_ref, ws_ref, bs_ref, wout_ref, o_ref):
    x = x_ref[...]
    hb = _ada(x, mod_ref, 1).astype(BF)
    pre = _dot(hb, win_ref[...])
    uv = 0.5 * pre * (1.0 + lax.erf(pre * math.sqrt(0.5)))
    u = uv[:, :A_WIDTH]
    v = (_rms(uv[:, A_WIDTH:]) * vg_ref[...]).astype(BF)
    bias = bs_ref[...]
    rows = []
    for c in range(FM // A_CHUNK):
        cols = [_dot(ws_ref[g], v[c * A_CHUNK:(c + 1) * A_CHUNK, g * LANES:(g + 1) * LANES])
                for g in range(A_GROUPS)]
        rows.append(jnp.concatenate(cols, axis=1) + bias)
    sv = jnp.concatenate(rows, axis=0)
    y = _dot((u * sv).astype(BF), wout_ref[...])
    o_ref[...] = x + mod_ref[5:6, :] * y


def _gmlp(x, mod, w_in, v_gain, w_s, b_s, w_out):
    bias = jnp.repeat(b_s.T, A_WIDTH // A_GROUPS, axis=1)
    return pl.pallas_call(
        _gmlp_kernel,
        grid=(N_TOK // FM,),
        in_specs=[_tok_spec(D_MODEL, FM), _mod_spec(FM),
                  _const_spec((D_MODEL, 2 * A_WIDTH)), _const_spec((1, A_WIDTH)),
                  _const_spec((A_GROUPS, A_CHUNK, A_CHUNK)), _const_spec((A_CHUNK, A_WIDTH)),
                  _const_spec((A_WIDTH, D_MODEL))],
        out_specs=_tok_spec(D_MODEL, FM),
        out_shape=jax.ShapeDtypeStruct((N_TOK, D_MODEL), F32),
        compiler_params=_params("arbitrary"),
        name="gmlp",
    )(x, mod, w_in.astype(BF), v_gain.reshape(1, A_WIDTH), w_s.astype(BF), bias, w_out.astype(BF))


def _swap_pairs(y, step):
    lane = lax.broadcasted_iota(jnp.int32, y.shape, 1)
    return jnp.where((lane & step) != 0, pltpu.roll(y, step, 1), pltpu.roll(y, LANES - step, 1))


def _rope_tables(rot_dim, lane_of_dim):
    quarter = rot_dim // 4
    inv = np.float32(ROPE_BASE) ** (-np.arange(quarter, dtype=np.float32) / np.float32(quarter))
    t = np.arange(DEC_SEQ)
    row = (t // GRID_W).astype(np.float32)
    col = (t % GRID_W).astype(np.float32)
    ang = np.stack([row[:, None] * inv, col[:, None] * inv], axis=1)
    cos, sin = np.cos(ang), np.sin(ang)
    d = np.asarray(lane_of_dim)
    dd = np.maximum(d, 0)
    axis, member, freq = dd // (2 * quarter), (dd % (2 * quarter)) // quarter, dd % quarter
    rot = (d >= 0)[None, :]
    c_tab = np.where(rot, cos[:, axis, freq], 1.0)
    s_tab = np.where(rot, np.where(member == 0, -1.0, 1.0)[None, :] * sin[:, axis, freq], 0.0)
    ident_c = np.ones((TM, LANES), np.float32)
    ident_s = np.zeros((TM, LANES), np.float32)
    return (np.concatenate([c_tab, ident_c], axis=0).astype(np.float32),
            np.concatenate([s_tab, ident_s], axis=0).astype(np.float32))


def _bproj_kernel(x_ref, mod_ref, w_ref, ones_ref, qc_ref, qs_ref, kc_ref, ks_ref, q_ref, k_ref, v_ref):
    hb = _ada(x_ref[...], mod_ref, 1).astype(BF)
    qkv = _dot(hb, w_ref[...])
    head_ones = ones_ref[...]

    def norm_rope(t4, tab, tab_swap):
        sq = t4 * t4
        hi = sq.astype(BF)
        lo = (sq - hi.astype(F32)).astype(BF)
        r = lax.rsqrt((_dot(hi, head_ones) + _dot(lo, head_ones)) / B_HEAD_DIM + EPS)
        halves = []
        for j in range(2):
            sl = slice(j * LANES, (j + 1) * LANES)
            t = t4[:, sl]
            halves.append(r[:, sl] * (t * tab + _swap_pairs(t, B_HEAD_DIM // 4) * tab_swap))
        return halves

    nq = B_HEADS * B_HEAD_DIM
    nk = B_KV_HEADS * B_HEAD_DIM
    q_tab, q_tab_swap = qc_ref[...], qs_ref[...]
    for j in range(nq // (2 * LANES)):
        halves = norm_rope(qkv[:, 2 * j * LANES:(2 * j + 2) * LANES], q_tab, q_tab_swap)
        q_ref[:, 2 * j * LANES:(2 * j + 1) * LANES] = halves[0].astype(BF)
        q_ref[:, (2 * j + 1) * LANES:(2 * j + 2) * LANES] = halves[1].astype(BF)
    halves = norm_rope(qkv[:, nq:nq + nk], kc_ref[...], ks_ref[...])
    k_ref[:, :LANES] = halves[0]
    k_ref[:, LANES:] = halves[1]
    v_ref[...] = qkv[:, nq + nk:]


def _bproj(x, mod, w_qkv, q_gain, k_gain, cos, sin):
    nq = B_HEADS * B_HEAD_DIM
    nk = B_KV_HEADS * B_HEAD_DIM
    q_scale = B_HEAD_DIM ** -0.5 * LOG2E
    lane = np.arange(LANES)
    qg = jnp.tile(q_gain, LANES // B_HEAD_DIM)
    kg = jnp.tile(k_gain, LANES // B_HEAD_DIM)
    partner = lane ^ (B_HEAD_DIM // 4)
    head = np.arange(nk) // B_HEAD_DIM
    head_ones = jnp.asarray(head[:, None] == head[None, :], BF)
    return pl.pallas_call(
        _bproj_kernel,
        grid=(N_TILES,),
        in_specs=[_tok_spec(D_MODEL), _MOD_SPEC, _const_spec((D_MODEL, nq + 2 * nk)), _const_spec((nk, nk)),
                  _ROPE_SPEC, _ROPE_SPEC, _ROPE_SPEC, _ROPE_SPEC],
        out_specs=[_tok_spec(nq), _tok_spec(nk), _tok_spec(nk)],
        out_shape=[jax.ShapeDtypeStruct((N_TOK, nq), BF),
                   jax.ShapeDtypeStruct((N_TOK, nk), F32),
                   jax.ShapeDtypeStruct((N_TOK, nk), F32)],
        compiler_params=_params("arbitrary"),
        name="gqa_proj",
    )(x, mod, w_qkv.astype(BF), head_ones,
      cos * (qg * q_scale), sin * (qg[partner] * q_scale), cos * kg, sin * kg[partner])


def _gqa_attend(q, kcat, vcat, bias, sink_ref, shift):
    tq = q.shape[0]
    nk = kcat.shape[0]
    lo = lax.broadcasted_iota(jnp.int32, (nk, LANES), 1) < B_HEAD_DIM
    lo_q = lax.broadcasted_iota(jnp.int32, (2 * tq, LANES), 1) < B_HEAD_DIM
    first = lax.broadcasted_iota(jnp.int32, (2 * tq, 1), 0) < tq
    if bias is not None:
        bias = jnp.concatenate([bias, bias], axis=0)
    outs = []
    for g in range(B_KV_HEADS):
        sl = slice((g // 2) * LANES, (g // 2 + 1) * LANES)
        own = lo if g % 2 == 0 else jnp.logical_not(lo)
        k_own = jnp.where(own, kcat[:, sl], 0.0)
        k_swp = pltpu.roll(k_own, B_HEAD_DIM, 1)
        v_own = jnp.where(own, vcat[:, sl], 1.0)
        v_swp = pltpu.roll(v_own, B_HEAD_DIM, 1)
        k_half = (k_own, k_swp) if g % 2 == 0 else (k_swp, k_own)
        v_half = (v_own, v_swp) if g % 2 == 0 else (v_swp, v_own)
        qg = jnp.concatenate([q[:, (2 * g) * LANES:(2 * g + 1) * LANES],
                              q[:, (2 * g + 1) * LANES:(2 * g + 2) * LANES]], axis=0)
        s_all = _dot_t(qg, jnp.concatenate(k_half, axis=0).astype(BF))
        o_half = []
        for e in range(2):
            s = s_all[:, e * nk:(e + 1) * nk]
            if bias is not None:
                nb = bias.shape[1]
                s = jnp.concatenate([s[:, :nb] + bias, s[:, nb:]], axis=1)
            sk = jnp.where(first, sink_ref[4 * g + e], sink_ref[4 * g + 2 + e]) * LOG2E
            m = jnp.maximum(sk, jnp.max(s, axis=-1, keepdims=True) if shift is None else shift)
            ov = _dot(jnp.exp2(s - m).astype(BF), v_half[e].astype(BF))
            o_half.append(ov / (pltpu.roll(ov, B_HEAD_DIM, 1) + jnp.exp2(sk - m)))
        o_g = jnp.where(lo_q, o_half[0], o_half[1])
        outs += [o_g[:tq], o_g[tq:]]
    return jnp.concatenate(outs, axis=1)


def _with_softmax_shift(ok_ref, shift_ref, body):
    lax.cond(ok_ref[0] != 0, lambda: body(shift_ref[0]), lambda: body(None))


def _battn_lat_kernel(ok_ref, shift_ref, x_ref, mod_ref, q_ref, kp_ref, kc_ref, kn_ref, vp_ref, vc_ref, vn_ref,
                      ck_ref, cv_ref, sink_ref, wo_ref, o_ref):
    def body(shift):
        j = pl.program_id(1)
        kcat = jnp.concatenate([kp_ref[...], kc_ref[...], kn_ref[...], ck_ref[...]], axis=0)
        vcat = jnp.concatenate([vp_ref[...], vc_ref[...], vn_ref[...], cv_ref[...]], axis=0)
        n_lat = BQ + 2 * B_WINDOW
        qi = lax.broadcasted_iota(jnp.int32, (BQ, n_lat), 0)
        pk = lax.broadcasted_iota(jnp.int32, (BQ, n_lat), 1)
        kpos = j * BQ + pk - B_WINDOW
        valid = (jnp.abs(pk - B_WINDOW - qi) <= B_WINDOW) & (kpos >= 0) & (kpos < DEC_SEQ)
        bias = jnp.where(valid, 0.0, NEG_INF)
        o = _gqa_attend(q_ref[...], kcat, vcat, bias, sink_ref, shift)
        y = _dot(o.astype(BF), wo_ref[...])
        o_ref[...] = x_ref[...] + mod_ref[5:6, :] * y

    _with_softmax_shift(ok_ref, shift_ref, body)


def _battn_ctx_kernel(ok_ref, shift_ref, x_ref, mod_ref, q_ref, k_ref, v_ref, sink_ref, wo_ref, o_ref):
    def body(shift):
        o = _gqa_attend(q_ref[...], k_ref[...], v_ref[...], None, sink_ref, shift)
        y = _dot(o.astype(BF), wo_ref[...])
        o_ref[...] = x_ref[...] + mod_ref[5:6, :] * y

    _with_softmax_shift(ok_ref, shift_ref, body)


def _battn(x, mod, q, k, v, cache_k, cache_v, sink, w_o, q_gain, k_gain):
    nq = B_HEADS * B_HEAD_DIM
    nk = B_KV_HEADS * B_HEAD_DIM
    cache_norm = jnp.sqrt(jnp.max(jnp.sum(jnp.square(cache_k.reshape(-1, B_HEAD_DIM)), axis=-1)))
    k_norm = jnp.maximum(math.sqrt(B_HEAD_DIM) * jnp.max(jnp.abs(k_gain)), cache_norm)
    bound = 1.02 * LOG2E * jnp.max(jnp.abs(q_gain)) * k_norm
    shift = bound.reshape(1)
    fixed_shift_ok = (bound <= C_SHIFT_MAX).astype(jnp.int32).reshape(1)
    nb = DEC_SEQ // BQ
    nw = DEC_SEQ // B_WINDOW
    per = BQ // B_WINDOW
    smem = pl.BlockSpec(memory_space=pltpu.SMEM)
    cur_spec = pl.BlockSpec((BQ, nk), lambda b, j: (b * nb + j, 0))
    prev_spec = pl.BlockSpec((B_WINDOW, nk), lambda b, j: (b * nw + jnp.maximum(per * j - 1, 0), 0))
    next_spec = pl.BlockSpec((B_WINDOW, nk), lambda b, j: (b * nw + jnp.minimum(per * j + per, nw - 1), 0))

    x = pl.pallas_call(
        _battn_lat_kernel,
        grid=(DEC_BATCH, nb),
        in_specs=[smem, smem, pl.BlockSpec((BQ, D_MODEL), lambda b, j: (b * nb + j, 0)),
                  pl.BlockSpec((None, N_MOD, D_MODEL), lambda b, j: (b, 0, 0)),
                  pl.BlockSpec((BQ, nq), lambda b, j: (b * nb + j, 0)),
                  prev_spec, cur_spec, next_spec, prev_spec, cur_spec, next_spec,
                  pl.BlockSpec((None, PAST_LEN, nk), lambda b, j: (b, 0, 0)),
                  pl.BlockSpec((None, PAST_LEN, nk), lambda b, j: (b, 0, 0)),
                  smem, _const_spec((nq, D_MODEL))],
        out_specs=pl.BlockSpec((BQ, D_MODEL), lambda b, j: (b * nb + j, 0)),
        out_shape=jax.ShapeDtypeStruct((N_TOK, D_MODEL), F32),
        input_output_aliases={2: 0},
        compiler_params=_params("arbitrary", "arbitrary"),
        name="gqa_attn_latent",
    )(fixed_shift_ok, shift, x, mod, q, k, k, k, v, v, v, cache_k, cache_v, sink, w_o)
    off = N_SAMPLE // SEQ
    return pl.pallas_call(
        _battn_ctx_kernel,
        grid=(BATCH,),
        in_specs=[smem, smem, pl.BlockSpec((SEQ, D_MODEL), lambda b: (off + b, 0)),
                  pl.BlockSpec((None, N_MOD, D_MODEL), lambda b: (DEC_BATCH, 0, 0)),
                  pl.BlockSpec((SEQ, nq), lambda b: (off + b, 0)),
                  pl.BlockSpec((SEQ, nk), lambda b: (off + b, 0)),
                  pl.BlockSpec((SEQ, nk), lambda b: (off + b, 0)),
                  smem, _const_spec((nq, D_MODEL))],
        out_specs=pl.BlockSpec((SEQ, D_MODEL), lambda b: (off + b, 0)),
        out_shape=jax.ShapeDtypeStruct((N_TOK, D_MODEL), F32),
        input_output_aliases={2: 0},
        compiler_params=_params("arbitrary"),
        name="gqa_attn_context",
    )(fixed_shift_ok, shift, x, mod, q, k, v, sink, w_o)


C_SWAP_W = C_HEADS * C_ROPE
C_SHIFT_LANE = C_NOPE + C_ROPE
C_SHIFT_MAX = 50.0


def _mla_head_norm_rope(t, t_swap, tab, tab_swap):
    r = lax.rsqrt(jnp.sum(t * t, axis=-1, keepdims=True) / (C_NOPE + C_ROPE) + EPS)
    if t_swap is None:
        return t * r * tab
    return r * (t * tab + t_swap * tab_swap)


def _mla_keys_values(c_kv_b, k_rope, k_rope_swap, wukv_ref, tab, tab_swap, k_shift, k_ref, v_ref):
    kv = _dot(c_kv_b, wukv_ref[...])
    lower = (lax.broadcasted_iota(jnp.int32, kv.shape, 1) & C_NOPE) == 0
    v_ref[...] = jnp.where(lower, 1.0, kv).astype(BF)
    k_nope = jnp.where(lower, kv, 0.0)
    for h in range(C_HEADS):
        sl = slice(h * C_HEAD_PAD, (h + 1) * C_HEAD_PAD)
        kh = _mla_head_norm_rope(k_nope[:, sl] + k_rope, k_rope_swap, tab, tab_swap)
        k_ref[:, sl] = (kh + k_shift).astype(BF)


def _cproj_kernel(x_ref, mod_ref, wd_ref, cqg_ref, ckvg_ref, wuq_ref, wukv_ref, qone_ref, kshift_ref,
                  qc_ref, qs_ref, kc_ref, ks_ref, q_ref, k_ref, v_ref, ckv_ref, kr_ref):
    hb = _ada(x_ref[...], mod_ref, 1).astype(BF)
    d = _dot(hb, wd_ref[...])
    c_q = _rms(d[:, :C_Q_LORA]) * cqg_ref[...]
    c_kv = _rms(d[:, C_Q_LORA:C_Q_LORA + C_KV_LORA]) * ckvg_ref[...]
    k_rope = d[:, C_Q_LORA + C_KV_LORA:C_Q_LORA + C_KV_LORA + LANES]
    k_rope_swap = d[:, C_Q_LORA + C_KV_LORA + LANES:]
    ckv_ref[...] = c_kv
    kr_ref[...] = k_rope
    q2 = _dot(c_q.astype(BF), wuq_ref[...])
    wq = C_HEADS * C_HEAD_PAD
    q_tab, q_tab_swap = qc_ref[...], qs_ref[...]
    per_tile = LANES // C_ROPE
    for h in range(C_HEADS):
        sl = slice(h * C_HEAD_PAD, (h + 1) * C_HEAD_PAD)
        t_swap = q2[:, wq + (h // per_tile) * LANES:wq + (h // per_tile + 1) * LANES]
        shift = (C_NOPE - C_ROPE * (h % per_tile)) % LANES
        if shift:
            t_swap = pltpu.roll(t_swap, shift, 1)
        q_ref[:, sl] = (_mla_head_norm_rope(q2[:, sl], t_swap, q_tab, q_tab_swap) + qone_ref[...]).astype(BF)
    _mla_keys_values(c_kv.astype(BF), k_rope, k_rope_swap, wukv_ref, kc_ref[...], ks_ref[...], kshift_ref[...],
                     k_ref, v_ref)


def _cctx_kernel(ckv_ref, kr_ref, wukv_ref, kg_ref, kshift_ref, k_ref, v_ref):
    _mla_keys_values(ckv_ref[...].astype(BF), kr_ref[...], None, wukv_ref, kg_ref[...], None, kshift_ref[...],
                     k_ref, v_ref)


def _mla_weights(w_down, w_uq, q_gain, k_gain):
    hd = C_NOPE + C_ROPE
    pad_lanes = C_HEAD_PAD - hd
    lane = np.arange(C_HEAD_PAD)
    is_rope = (lane >= C_NOPE) & (lane < hd)
    partner = np.where(is_rope, lane ^ (C_ROPE // 4), lane)

    def swapped(t):
        return jnp.where(is_rope, jnp.take(t, partner, axis=-1), 0.0)

    kr_cols = jnp.pad(w_down[:, C_Q_LORA + C_KV_LORA:], ((0, 0), (C_NOPE, pad_lanes)))
    wd = jnp.concatenate([w_down[:, :C_Q_LORA + C_KV_LORA], kr_cols, swapped(kr_cols)], axis=1)
    wuq = jnp.pad(w_uq.reshape(C_Q_LORA, C_HEADS, hd), ((0, 0), (0, 0), (0, pad_lanes)))
    wuq_swap = swapped(wuq)[:, :, C_NOPE:hd].reshape(C_Q_LORA, C_SWAP_W)
    wuq = jnp.concatenate([wuq.reshape(C_Q_LORA, C_HEADS * C_HEAD_PAD), wuq_swap], axis=1)
    qg = jnp.pad(q_gain, (0, pad_lanes))
    kg = jnp.pad(k_gain, (0, pad_lanes))
    row = lambda t: t.reshape(1, C_HEAD_PAD)
    bound = 1.02 * math.sqrt(hd) * LOG2E * jnp.max(jnp.abs(q_gain)) * jnp.max(jnp.abs(k_gain))
    shift_lane = lane == C_SHIFT_LANE
    q_one = row(jnp.asarray(shift_lane, F32))
    k_shift = row(jnp.where(shift_lane, -bound, 0.0))
    fixed_shift_ok = (bound <= C_SHIFT_MAX).astype(jnp.int32).reshape(1)
    return (wd.astype(BF), wuq.astype(BF), row(qg), row(swapped(qg)), row(kg), row(swapped(kg)),
            q_one, k_shift, fixed_shift_ok)


def _cproj(x, mod, wd, cq_gain, ckv_gain, wuq, wukv, q_one, k_shift, q_tab, q_tab_swap, k_tab, k_tab_swap):
    wq = C_HEADS * C_HEAD_PAD
    return pl.pallas_call(
        _cproj_kernel,
        grid=(N_TILES,),
        in_specs=[_tok_spec(D_MODEL), _MOD_SPEC, _const_spec((D_MODEL, C_DOWN_PAD)),
                  _const_spec((1, C_Q_LORA)), _const_spec((1, C_KV_LORA)),
                  _const_spec((C_Q_LORA, wq + C_SWAP_W)), _const_spec((C_KV_LORA, wq)),
                  _const_spec((1, C_HEAD_PAD)), _const_spec((1, C_HEAD_PAD)),
                  _ROPE_SPEC, _ROPE_SPEC, _ROPE_SPEC, _ROPE_SPEC],
        out_specs=[_tok_spec(wq), _tok_spec(wq), _tok_spec(wq), _tok_spec(C_KV_LORA), _tok_spec(LANES)],
        out_shape=[jax.ShapeDtypeStruct((N_TOK, wq), BF), jax.ShapeDtypeStruct((N_TOK, wq), BF),
                   jax.ShapeDtypeStruct((N_TOK, wq), BF), jax.ShapeDtypeStruct((N_TOK, C_KV_LORA), F32),
                   jax.ShapeDtypeStruct((N_TOK, LANES), F32)],
        compiler_params=_params("arbitrary"),
        name="mla_proj",
    )(x, mod, wd, cq_gain.reshape(1, C_Q_LORA), ckv_gain.reshape(1, C_KV_LORA), wuq, wukv, q_one, k_shift,
      q_tab, q_tab_swap, k_tab, k_tab_swap)


def _cctx(cache_ckv, cache_krope, wukv, kg, k_shift):
    n = DEC_BATCH * PAST_LEN
    wq = C_HEADS * C_HEAD_PAD
    kr = jnp.pad(cache_krope.reshape(n, C_ROPE), ((0, 0), (C_NOPE, C_HEAD_PAD - C_NOPE - C_ROPE)))
    return pl.pallas_call(
        _cctx_kernel,
        grid=(n // TM,),
        in_specs=[_tok_spec(C_KV_LORA), _tok_spec(LANES), _const_spec((C_KV_LORA, wq)),
                  _const_spec((1, C_HEAD_PAD)), _const_spec((1, C_HEAD_PAD))],
        out_specs=[_tok_spec(wq), _tok_spec(wq)],
        out_shape=[jax.ShapeDtypeStruct((n, wq), BF), jax.ShapeDtypeStruct((n, wq), BF)],
        compiler_params=_params("arbitrary"),
        name="mla_context_keys",
    )(cache_ckv.reshape(n, C_KV_LORA), kr, wukv, kg, k_shift)


def _mla_attend(ok_ref, q_ref, kv_refs, o_ref, pair=0):
    tq = q_ref.shape[0]
    cols = slice(pair * LANES, (pair + 1) * LANES)
    head_cols = [slice((2 * pair + e) * C_HEAD_PAD, (2 * pair + e + 1) * C_HEAD_PAD) for e in range(2)]

    def normalised(sum_acc, rows):
        lo = lax.broadcasted_iota(jnp.int32, (rows, LANES), 1) < C_VDIM
        r0 = pltpu.roll(sum_acc[0], C_VDIM, 1)
        r1 = pltpu.roll(sum_acc[1], C_VDIM, 1)
        return jnp.where(lo, r0 / sum_acc[0], sum_acc[1] / r1).astype(BF)

    chunks = []
    for k_ref, v_ref in kv_refs:
        nk = k_ref.shape[0]
        if nk < CK and chunks:
            chunks[-1].append((k_ref, v_ref, slice(0, nk)))
        else:
            ck = min(CK, nk)
            chunks += [[(k_ref, v_ref, slice(c * ck, (c + 1) * ck))] for c in range(nk // ck)]
    def fixed_shift():
        sum_acc = []
        for sl in head_cols:
            qh = q_ref[:, sl]
            acc = None
            for parts in chunks:
                kk = [k_ref[rows, sl] for k_ref, _, rows in parts]
                vv = [v_ref[rows, sl] for _, v_ref, rows in parts]
                p = jnp.exp2(_dot_t(qh, kk[0] if len(kk) == 1 else jnp.concatenate(kk, axis=0))).astype(BF)
                pv = _dot(p, vv[0] if len(vv) == 1 else jnp.concatenate(vv, axis=0))
                acc = pv if acc is None else acc + pv
            sum_acc.append(acc)
        o_ref[:, cols] = normalised(sum_acc, tq)

    def row_max_shift():
        blk = min(tq, LANES)

        def body(i, carry):
            rows = pl.ds(pl.multiple_of(i * blk, blk), blk)
            sum_acc = []
            for sl in head_cols:
                qh = q_ref[rows, sl]
                scores = [_dot_t(qh, k_ref[:, sl]) for k_ref, _ in kv_refs]
                m = functools.reduce(jnp.maximum, [jnp.max(sc, axis=-1, keepdims=True) for sc in scores])
                sum_acc.append(functools.reduce(jnp.add, [_dot(jnp.exp2(sc - m).astype(BF), v_ref[:, sl])
                                                          for sc, (_, v_ref) in zip(scores, kv_refs)]))
            o_ref[rows, cols] = normalised(sum_acc, blk)
            return carry

        lax.fori_loop(0, tq // blk, body, 0)

    lax.cond(ok_ref[0] != 0, fixed_shift, row_max_shift)


def _cattn_lat_kernel(ok_ref, q_ref, k_ref, v_ref, ck_ref, cv_ref, o_ref):
    _mla_attend(ok_ref, q_ref, ((k_ref, v_ref), (ck_ref, cv_ref)), o_ref)


def _cattn_ctx_kernel(ok_ref, q_ref, k_ref, v_ref, oin_ref, o_ref):
    del oin_ref
    for pair in range(C_HEADS // 2):
        _mla_attend(ok_ref, q_ref, ((k_ref, v_ref),), o_ref, pair)


def _cattn(fixed_shift_ok, q, k, v, ck, cv):
    pair_w = 2 * C_HEAD_PAD
    n_pairs = C_HEADS // 2
    nqt = DEC_SEQ // CQ
    smem = pl.BlockSpec(memory_space=pltpu.SMEM)
    o = pl.pallas_call(
        _cattn_lat_kernel,
        grid=(DEC_BATCH, n_pairs, nqt),
        in_specs=[smem, pl.BlockSpec((CQ, pair_w), lambda b, p, t: (b * nqt + t, p)),
                  pl.BlockSpec((DEC_SEQ, pair_w), lambda b, p, t: (b, p)),
                  pl.BlockSpec((DEC_SEQ, pair_w), lambda b, p, t: (b, p)),
                  pl.BlockSpec((PAST_LEN, pair_w), lambda b, p, t: (b, p)),
                  pl.BlockSpec((PAST_LEN, pair_w), lambda b, p, t: (b, p))],
        out_specs=pl.BlockSpec((CQ, LANES), lambda b, p, t: (b * nqt + t, p)),
        out_shape=jax.ShapeDtypeStruct((N_TOK, C_HEADS * C_VDIM), BF),
        compiler_params=_params("arbitrary", "arbitrary", "arbitrary"),
        name="mla_attn_latent",
    )(fixed_shift_ok, q, k, v, ck, cv)
    off = N_SAMPLE // SEQ
    return pl.pallas_call(
        _cattn_ctx_kernel,
        grid=(BATCH,),
        in_specs=[smem, pl.BlockSpec((SEQ, n_pairs * pair_w), lambda b: (off + b, 0)),
                  pl.BlockSpec((SEQ, n_pairs * pair_w), lambda b: (off + b, 0)),
                  pl.BlockSpec((SEQ, n_pairs * pair_w), lambda b: (off + b, 0)),
                  pl.BlockSpec(memory_space=pl.ANY)],
        out_specs=pl.BlockSpec((SEQ, n_pairs * LANES), lambda b: (off + b, 0)),
        out_shape=jax.ShapeDtypeStruct((N_TOK, C_HEADS * C_VDIM), BF),
        input_output_aliases={4: 0},
        compiler_params=_params("arbitrary"),
        name="mla_attn_context",
    )(fixed_shift_ok, q, k, v, o)


def kernel(x_prompt, x_sample, c, cache_win_k, cache_win_v, cache_mla_ckv, cache_mla_krope, c_ctx,
           ada_w, ada_b, ffn_w_in, ffn_w_out,
           gmlp_w_in, gmlp_v_gain, gmlp_w_s, gmlp_b_s, gmlp_w_out,
           win_w_qkv, win_q_gain, win_k_gain, win_sink, win_w_o,
           mla_w_down, mla_cq_gain, mla_ckv_gain, mla_w_uq, mla_w_ukv, mla_q_gain, mla_k_gain, mla_w_o):
    x = (x_sample.reshape(N_SAMPLE, D_MODEL), x_prompt.reshape(N_PROMPT, D_MODEL))
    cond = jnp.concatenate([c, c_ctx[None, :], jnp.zeros((N_COND - DEC_BATCH - 1, D_MODEL), F32)], axis=0)
    mods = _modulation(cond, ada_w, ada_b)
    w_in_b, shift_terms = _ffn_prep(mods, ffn_w_in)
    w_out_b = ffn_w_out.astype(BF)

    lane = np.arange(LANES)
    b_cos, b_sin = _rope_tables(B_HEAD_DIM, lane % B_HEAD_DIM)
    c_lane = np.where((lane >= C_NOPE) & (lane < C_NOPE + C_ROPE), lane - C_NOPE, -1)
    c_cos, c_sin = _rope_tables(C_ROPE, c_lane)

    nk = B_KV_HEADS * B_HEAD_DIM
    win_k, win_v, mla_ckv, mla_krope = [], [], [], []
    ia = ib = ic = 0
    for l in range(DEPTH):
        mod = mods[l]
        pre_proj = None
        x = _ffn(x, mod, shift_terms, w_in_b, w_out_b, l, 0, split_in=(l == 0))
        kind = l % N_MIXERS
        if kind == 0:
            x = _gmlp(x, mod, gmlp_w_in[ia], gmlp_v_gain[ia], gmlp_w_s[ia], gmlp_b_s[ia], gmlp_w_out[ia])
            ia += 1
        elif kind == 1:
            q, k, v = _bproj(x, mod, win_w_qkv[ib], win_q_gain[ib], win_k_gain[ib], b_cos, b_sin)
            x = _battn(x, mod, q, k, v,
                       cache_win_k[:, ib].reshape(DEC_BATCH, PAST_LEN, nk),
                       cache_win_v[:, ib].reshape(DEC_BATCH, PAST_LEN, nk),
                       win_sink[ib], win_w_o[ib].astype(BF), win_q_gain[ib], win_k_gain[ib])
            win_k.append(k[N_SAMPLE:].reshape(BATCH, SEQ, B_KV_HEADS, B_HEAD_DIM))
            win_v.append(v[N_SAMPLE:].reshape(BATCH, SEQ, B_KV_HEADS, B_HEAD_DIM))
            ib += 1
        else:
            wd, wuq, qg, qgs, kg, kgs, q_one, k_shift, fixed_shift_ok = _mla_weights(
                mla_w_down[ic], mla_w_uq[ic], mla_q_gain[ic], mla_k_gain[ic])
            wukv = mla_w_ukv[ic].astype(BF)
            q_scale = (C_NOPE + C_ROPE) ** -0.5 * LOG2E
            q, k, v, ckv, kr = _cproj(x, mod, wd, mla_cq_gain[ic], mla_ckv_gain[ic], wuq, wukv, q_one, k_shift,
                                      c_cos * (qg * q_scale), c_sin * (qgs * q_scale), c_cos * kg, c_sin * kgs)
            ck, cv = _cctx(cache_mla_ckv[:, ic], cache_mla_krope[:, ic], wukv, kg, k_shift)
            pre_proj = (_cattn(fixed_shift_ok, q, k, v, ck, cv), mla_w_o[ic].astype(BF))
            mla_ckv.append(ckv[N_SAMPLE:].reshape(BATCH, SEQ, C_KV_LORA))
            mla_krope.append(kr[N_SAMPLE:, C_NOPE:C_NOPE + C_ROPE].reshape(BATCH, SEQ, C_ROPE))
            ic += 1
        x = _ffn(x, mod, shift_terms, w_in_b, w_out_b, l, 1, split_out=(l == DEPTH - 1), pre_proj=pre_proj)
    y_sample, y_prompt = x
    return (y_prompt.reshape(BATCH, SEQ, D_MODEL), y_sample.reshape(DEC_BATCH, DEC_SEQ, D_MODEL),
            jnp.stack(win_k, axis=1), jnp.stack(win_v, axis=1),
            jnp.stack(mla_ckv, axis=1), jnp.stack(mla_krope, axis=1))
```

```python
import functools
import math

import jax
import jax.numpy as jnp
import numpy as np
from jax import lax
from jax.experimental import pallas as pl
from jax.experimental.pallas import tpu as pltpu

D_MODEL = 1024
BATCH = 16
SEQ = 256
DEPTH = 4
DEC_BATCH = 8
DEC_SEQ = 4096
PAST_LEN = 256
GRID_W = 64
N_MIXERS = 3
N_MOD = 9
D_FF = 2816
A_WIDTH = D_MODEL
A_GROUPS = 8
A_CHUNK = 128
B_HEADS = 16
B_KV_HEADS = 4
B_HEAD_DIM = 64
B_WINDOW = 128
C_HEADS = 16
C_NOPE = 64
C_ROPE = 32
C_VDIM = 64
C_Q_LORA = 512
C_KV_LORA = 256
ROPE_BASE = 10000.0
EPS = 1e-6
NEG_INF = -1e30

LANES = 128
N_SAMPLE = DEC_BATCH * DEC_SEQ
N_PROMPT = BATCH * SEQ
N_TOK = N_SAMPLE + N_PROMPT
N_COND = 16
TM = 512
FM = 1024
FF_CHUNKS = ((0, 1536), (1536, 1280))
N_TILES = N_TOK // TM
N_SAMPLE_TILES = N_SAMPLE // TM
TILES_PER_SEQ = DEC_SEQ // TM
MOD_TN = 1536
SHIFT_TN = 1408
BQ = 256
CQ = 1024
CK = 256
C_HEAD_PAD = 128
C_DOWN_PAD = C_Q_LORA + C_KV_LORA + 2 * LANES
VMEM_LIMIT_BYTES = 56 * 1024 * 1024

LOG2E = math.log2(math.e)

BF = jnp.bfloat16
F32 = jnp.float32


def _params(*sem):
    return pltpu.CompilerParams(dimension_semantics=sem, vmem_limit_bytes=VMEM_LIMIT_BYTES)


def _dot(a, b):
    return jnp.dot(a, b, preferred_element_type=F32)


def _dot_t(a, b):
    return lax.dot_general(a, b, (((1,), (1,)), ((), ())), preferred_element_type=F32)


def _rms(x):
    return x * lax.rsqrt(jnp.mean(x * x, axis=-1, keepdims=True) + EPS)


def _ada(x, mod_ref, k):
    shift = mod_ref[3 * k:3 * k + 1, :]
    scale = mod_ref[3 * k + 1:3 * k + 2, :]
    return _rms(x) * (1.0 + scale) + shift


def _const_spec(shape):
    nd = len(shape)
    return pl.BlockSpec(shape, lambda *_: (0,) * nd, pipeline_mode=pl.Buffered(1))


def _tok_spec(width, tm=TM):
    return pl.BlockSpec((tm, width), lambda i: (i, 0))


def _mod_spec(tm):
    return pl.BlockSpec((None, N_MOD, D_MODEL), lambda i: (jnp.minimum(i * tm // DEC_SEQ, DEC_BATCH), 0, 0))


_MOD_SPEC = _mod_spec(TM)


def _rope_tile(i):
    return jnp.where(i < N_SAMPLE_TILES, i % TILES_PER_SEQ, TILES_PER_SEQ)


_ROPE_SPEC = pl.BlockSpec((TM, LANES), lambda i: (_rope_tile(i), 0))


def _mod_kernel(c_ref, w_ref, b_ref, o_ref):
    a = jax.nn.silu(c_ref[...]).astype(BF)
    o_ref[...] = _dot(a, w_ref[...].astype(BF)) + b_ref[...]


def _modulation(cond, ada_w, ada_b):
    n_out = N_MOD * D_MODEL
    out = pl.pallas_call(
        _mod_kernel,
        grid=(DEPTH, n_out // MOD_TN),
        in_specs=[
            pl.BlockSpec((N_COND, D_MODEL), lambda l, j: (0, 0)),
            pl.BlockSpec((None, D_MODEL, MOD_TN), lambda l, j: (l, 0, j)),
            pl.BlockSpec((None, 1, MOD_TN), lambda l, j: (l, 0, j)),
        ],
        out_specs=pl.BlockSpec((None, N_COND, MOD_TN), lambda l, j: (l, 0, j)),
        out_shape=jax.ShapeDtypeStruct((DEPTH, N_COND, n_out), F32),
        compiler_params=_params("arbitrary", "arbitrary"),
        name="modulation",
    )(cond, ada_w, ada_b.reshape(DEPTH, 1, n_out))
    return out.reshape(DEPTH, N_COND, N_MOD, D_MODEL)


def _ffn_prep_kernel(s_ref, w_ref, wb_ref, o_ref):
    wb = w_ref[...].astype(BF)
    wb_ref[...] = wb
    o_ref[...] = _dot(s_ref[...].astype(BF), wb)


def _ffn_prep(mods, w_in):
    shifts = jnp.stack([mods[:, :, 0], mods[:, :, 6]], axis=1)
    w_spec = pl.BlockSpec((None, None, D_MODEL, SHIFT_TN), lambda l, h, j: (l, h, 0, j))
    w_b, out = pl.pallas_call(
        _ffn_prep_kernel,
        grid=(DEPTH, 2, 2 * D_FF // SHIFT_TN),
        in_specs=[pl.BlockSpec((None, None, N_COND, D_MODEL), lambda l, h, j: (l, h, 0, 0)), w_spec],
        out_specs=[w_spec, pl.BlockSpec((None, None, N_COND, SHIFT_TN), lambda l, h, j: (l, h, 0, j))],
        out_shape=[jax.ShapeDtypeStruct(w_in.shape, BF),
                   jax.ShapeDtypeStruct((DEPTH, 2, N_COND, 2 * D_FF), F32)],
        compiler_params=_params("arbitrary", "arbitrary", "arbitrary"),
        name="ffn_prep",
    )(shifts, w_in)
    return w_b, out.reshape(DEPTH, 2, N_COND, 1, 2 * D_FF)


def _ffn_kernel(*refs, k, split_in, split_out, pre_proj):
    n_x = 2 if split_in else 1
    n_in = n_x + (2 if pre_proj else 0)
    x_refs, (mod_ref, sh_ref, win_ref, wout_ref), o_refs = refs[:n_x], refs[n_in:n_in + 4], refs[n_in + 4:]
    is_sample = pl.program_id(0) < N_SAMPLE // FM
    x = jnp.where(is_sample, x_refs[0][...], x_refs[1][...]) if split_in else x_refs[0][...]
    if pre_proj:
        attn_ref, wo_ref = refs[n_x:n_in]
        x = x + mod_ref[5:6, :] * _dot(attn_ref[...], wo_ref[...])
    xa = (x * (1.0 + mod_ref[3 * k + 1:3 * k + 2, :])).astype(BF)
    rinv = lax.rsqrt(jnp.mean(x * x, axis=-1, keepdims=True) + EPS)
    y = None
    for c0, cw in FF_CHUNKS:
        g = _dot(xa, win_ref[:, c0:c0 + cw]) * rinv + sh_ref[:, c0:c0 + cw]
        u = _dot(xa, win_ref[:, D_FF + c0:D_FF + c0 + cw]) * rinv + sh_ref[:, D_FF + c0:D_FF + c0 + cw]
        yc = _dot((jax.nn.silu(g) * u).astype(BF), wout_ref[c0:c0 + cw, :])
        y = yc if y is None else y + yc
    gate = mod_ref[3 * k + 2:3 * k + 3, :]
    out = x + (0.5 * gate) * y
    if split_out:
        o_refs[1][...] = out

        @pl.when(is_sample)
        def _():
            o_refs[0][...] = out
    else:
        o_refs[0][...] = out


_SAMPLE_SPEC = pl.BlockSpec((FM, D_MODEL), lambda i: (jnp.minimum(i, N_SAMPLE // FM - 1), 0))
_PROMPT_SPEC = pl.BlockSpec((FM, D_MODEL), lambda i: (jnp.maximum(i - N_SAMPLE // FM, 0), 0))


def _ffn(xs, mod, shift_terms, w_in, w_out, layer, half, split_in=False, split_out=False, pre_proj=None):
    def w_spec(rows, cols):
        return pl.BlockSpec((None, None, rows, cols), lambda i: (layer, half, 0, 0), pipeline_mode=pl.Buffered(1))

    sh_spec = pl.BlockSpec((None, None, None, 1, 2 * D_FF),
                           lambda i: (layer, half, jnp.minimum(i * FM // DEC_SEQ, DEC_BATCH), 0, 0))

    x_specs = [_SAMPLE_SPEC, _PROMPT_SPEC] if split_in else [_tok_spec(D_MODEL, FM)]
    operands = list(xs) if split_in else [xs]
    if pre_proj is not None:
        x_specs += [_tok_spec(D_MODEL, FM), _const_spec((D_MODEL, D_MODEL))]
        operands += list(pre_proj)
    if split_out:
        out_specs = [_SAMPLE_SPEC, _PROMPT_SPEC]
        out_shape = [jax.ShapeDtypeStruct((N_SAMPLE, D_MODEL), F32), jax.ShapeDtypeStruct((N_PROMPT, D_MODEL), F32)]
    else:
        out_specs = _tok_spec(D_MODEL, FM)
        out_shape = jax.ShapeDtypeStruct((N_TOK, D_MODEL), F32)
    return pl.pallas_call(
        functools.partial(_ffn_kernel, k=2 * half, split_in=split_in, split_out=split_out,
                          pre_proj=pre_proj is not None),
        grid=(N_TOK // FM,),
        in_specs=x_specs + [_mod_spec(FM), sh_spec, w_spec(D_MODEL, 2 * D_FF), w_spec(D_FF, D_MODEL)],
        out_specs=out_specs,
        out_shape=out_shape,
        compiler_params=_params("arbitrary"),
        name="ffn",
    )(*operands, mod, shift_terms, w_in, w_out)


def _gmlp_kernel(x_ref, mod_ref, win_ref, vg_ref, ws_ref, bs_ref, wout_ref, o_ref):
    x = x_ref[...]
    hb = _ada(x, mod_ref, 1).astype(BF)
    pre = _dot(hb, win_ref[...])
    uv = 0.5 * pre * (1.0 + lax.erf(pre * math.sqrt(0.5)))
    u = uv[:, :A_WIDTH]
    v = (_rms(uv[:, A_WIDTH:]) * vg_ref[...]).astype(BF)
    bias = bs_ref[...]
    rows = []
    for c in range(FM // A_CHUNK):
        cols = [_dot(ws_ref[g], v[c * A_CHUNK:(c + 1) * A_CHUNK, g * LANES:(g + 1) * LANES])
                for g in range(A_GROUPS)]
        rows.append(jnp.concatenate(cols, axis=1) + bias)
    sv = jnp.concatenate(rows, axis=0)
    y = _dot((u * sv).astype(BF), wout_ref[...])
    o_ref[...] = x + mod_ref[5:6, :] * y


def _gmlp(x, mod, w_in, v_gain, w_s, b_s, w_out):
    bias = jnp.repeat(b_s.T, A_WIDTH // A_GROUPS, axis=1)
    return pl.pallas_call(
        _gmlp_kernel,
        grid=(N_TOK // FM,),
        in_specs=[_tok_spec(D_MODEL, FM), _mod_spec(FM),
                  _const_spec((D_MODEL, 2 * A_WIDTH)), _const_spec((1, A_WIDTH)),
                  _const_spec((A_GROUPS, A_CHUNK, A_CHUNK)), _const_spec((A_CHUNK, A_WIDTH)),
                  _const_spec((A_WIDTH, D_MODEL))],
        out_specs=_tok_spec(D_MODEL, FM),
        out_shape=jax.ShapeDtypeStruct((N_TOK, D_MODEL), F32),
        compiler_params=_params("arbitrary"),
        name="gmlp",
    )(x, mod, w_in.astype(BF), v_gain.reshape(1, A_WIDTH), w_s.astype(BF), bias, w_out.astype(BF))


def _swap_pairs(y, step):
    lane = lax.broadcasted_iota(jnp.int32, y.shape, 1)
    return jnp.where((lane & step) != 0, pltpu.roll(y, step, 1), pltpu.roll(y, LANES - step, 1))


def _rope_tables(rot_dim, lane_of_dim):
    quarter = rot_dim // 4
    inv = np.float32(ROPE_BASE) ** (-np.arange(quarter, dtype=np.float32) / np.float32(quarter))
    t = np.arange(DEC_SEQ)
    row = (t // GRID_W).astype(np.float32)
    col = (t % GRID_W).astype(np.float32)
    ang = np.stack([row[:, None] * inv, col[:, None] * inv], axis=1)
    cos, sin = np.cos(ang), np.sin(ang)
    d = np.asarray(lane_of_dim)
    dd = np.maximum(d, 0)
    axis, member, freq = dd // (2 * quarter), (dd % (2 * quarter)) // quarter, dd % quarter
    rot = (d >= 0)[None, :]
    c_tab = np.where(rot, cos[:, axis, freq], 1.0)
    s_tab = np.where(rot, np.where(member == 0, -1.0, 1.0)[None, :] * sin[:, axis, freq], 0.0)
    ident_c = np.ones((TM, LANES), np.float32)
    ident_s = np.zeros((TM, LANES), np.float32)
    return (np.concatenate([c_tab, ident_c], axis=0).astype(np.float32),
            np.concatenate([s_tab, ident_s], axis=0).astype(np.float32))


def _bproj_kernel(x_ref, mod_ref, w_ref, ones_ref, qc_ref, qs_ref, kc_ref, ks_ref, q_ref, k_ref, v_ref):
    hb = _ada(x_ref[...], mod_ref, 1).astype(BF)
    qkv = _dot(hb, w_ref[...])
    head_ones = ones_ref[...]

    def norm_rope(t4, tab, tab_swap):
        sq = t4 * t4
        hi = sq.astype(BF)
        lo = (sq - hi.astype(F32)).astype(BF)
        r = lax.rsqrt((_dot(hi, head_ones) + _dot(lo, head_ones)) / B_HEAD_DIM + EPS)
        halves = []
        for j in range(2):
            sl = slice(j * LANES, (j + 1) * LANES)
            t = t4[:, sl]
            halves.append(r[:, sl] * (t * tab + _swap_pairs(t, B_HEAD_DIM // 4) * tab_swap))
        return halves

    nq = B_HEADS * B_HEAD_DIM
    nk = B_KV_HEADS * B_HEAD_DIM
    q_tab, q_tab_swap = qc_ref[...], qs_ref[...]
    for j in range(nq // (2 * LANES)):
        halves = norm_rope(qkv[:, 2 * j * LANES:(2 * j + 2) * LANES], q_tab, q_tab_swap)
        q_ref[:, 2 * j * LANES:(2 * j + 1) * LANES] = halves[0].astype(BF)
        q_ref[:, (2 * j + 1) * LANES:(2 * j + 2) * LANES] = halves[1].astype(BF)
    halves = norm_rope(qkv[:, nq:nq + nk], kc_ref[...], ks_ref[...])
    k_ref[:, :LANES] = halves[0]
    k_ref[:, LANES:] = halves[1]
    v_ref[...] = qkv[:, nq + nk:]


def _bproj(x, mod, w_qkv, q_gain, k_gain, cos, sin):
    nq = B_HEADS * B_HEAD_DIM
    nk = B_KV_HEADS * B_HEAD_DIM
    q_scale = B_HEAD_DIM ** -0.5 * LOG2E
    lane = np.arange(LANES)
    qg = jnp.tile(q_gain, LANES // B_HEAD_DIM)
    kg = jnp.tile(k_gain, LANES // B_HEAD_DIM)
    partner = lane ^ (B_HEAD_DIM // 4)
    head = np.arange(nk) // B_HEAD_DIM
    head_ones = jnp.asarray(head[:, None] == head[None, :], BF)
    return pl.pallas_call(
        _bproj_kernel,
        grid=(N_TILES,),
        in_specs=[_tok_spec(D_MODEL), _MOD_SPEC, _const_spec((D_MODEL, nq + 2 * nk)), _const_spec((nk, nk)),
                  _ROPE_SPEC, _ROPE_SPEC, _ROPE_SPEC, _ROPE_SPEC],
        out_specs=[_tok_spec(nq), _tok_spec(nk), _tok_spec(nk)],
        out_shape=[jax.ShapeDtypeStruct((N_TOK, nq), BF),
                   jax.ShapeDtypeStruct((N_TOK, nk), F32),
                   jax.ShapeDtypeStruct((N_TOK, nk), F32)],
        compiler_params=_params("arbitrary"),
        name="gqa_proj",
    )(x, mod, w_qkv.astype(BF), head_ones,
      cos * (qg * q_scale), sin * (qg[partner] * q_scale), cos * kg, sin * kg[partner])


def _gqa_attend(q, kcat, vcat, bias, sink_ref, shift):
    tq = q.shape[0]
    nk = kcat.shape[0]
    lo = lax.broadcasted_iota(jnp.int32, (nk, LANES), 1) < B_HEAD_DIM
    lo_q = lax.broadcasted_iota(jnp.int32, (2 * tq, LANES), 1) < B_HEAD_DIM
    first = lax.broadcasted_iota(jnp.int32, (2 * tq, 1), 0) < tq
    if bias is not None:
        bias = jnp.concatenate([bias, bias], axis=0)
    outs = []
    for g in range(B_KV_HEADS):
        sl = slice((g // 2) * LANES, (g // 2 + 1) * LANES)
        own = lo if g % 2 == 0 else jnp.logical_not(lo)
        k_own = jnp.where(own, kcat[:, sl], 0.0)
        k_swp = pltpu.roll(k_own, B_HEAD_DIM, 1)
        v_own = jnp.where(own, vcat[:, sl], 1.0)
        v_swp = pltpu.roll(v_own, B_HEAD_DIM, 1)
        k_half = (k_own, k_swp) if g % 2 == 0 else (k_swp, k_own)
        v_half = (v_own, v_swp) if g % 2 == 0 else (v_swp, v_own)
        qg = jnp.concatenate([q[:, (2 * g) * LANES:(2 * g + 1) * LANES],
                              q[:, (2 * g + 1) * LANES:(2 * g + 2) * LANES]], axis=0)
        s_all = _dot_t(qg, jnp.concatenate(k_half, axis=0).astype(BF))
        o_half = []
        for e in range(2):
            s = s_all[:, e * nk:(e + 1) * nk]
            if bias is not None:
                nb = bias.shape[1]
                s = jnp.concatenate([s[:, :nb] + bias, s[:, nb:]], axis=1)
            sk = jnp.where(first, sink_ref[4 * g + e], sink_ref[4 * g + 2 + e]) * LOG2E
            m = jnp.maximum(sk, jnp.max(s, axis=-1, keepdims=True) if shift is None else shift)
            ov = _dot(jnp.exp2(s - m).astype(BF), v_half[e].astype(BF))
            o_half.append(ov / (pltpu.roll(ov, B_HEAD_DIM, 1) + jnp.exp2(sk - m)))
        o_g = jnp.where(lo_q, o_half[0], o_half[1])
        outs += [o_g[:tq], o_g[tq:]]
    return jnp.concatenate(outs, axis=1)


def _with_softmax_shift(ok_ref, shift_ref, body):
    lax.cond(ok_ref[0] != 0, lambda: body(shift_ref[0]), lambda: body(None))


def _battn_lat_kernel(ok_ref, shift_ref, x_ref, mod_ref, q_ref, kp_ref, kc_ref, kn_ref, vp_ref, vc_ref, vn_ref,
                      ck_ref, cv_ref, sink_ref, wo_ref, o_ref):
    def body(shift):
        j = pl.program_id(1)
        kcat = jnp.concatenate([kp_ref[...], kc_ref[...], kn_ref[...], ck_ref[...]], axis=0)
        vcat = jnp.concatenate([vp_ref[...], vc_ref[...], vn_ref[...], cv_ref[...]], axis=0)
        n_lat = BQ + 2 * B_WINDOW
        qi = lax.broadcasted_iota(jnp.int32, (BQ, n_lat), 0)
        pk = lax.broadcasted_iota(jnp.int32, (BQ, n_lat), 1)
        kpos = j * BQ + pk - B_WINDOW
        valid = (jnp.abs(pk - B_WINDOW - qi) <= B_WINDOW) & (kpos >= 0) & (kpos < DEC_SEQ)
        bias = jnp.where(valid, 0.0, NEG_INF)
        o = _gqa_attend(q_ref[...], kcat, vcat, bias, sink_ref, shift)
        y = _dot(o.astype(BF), wo_ref[...])
        o_ref[...] = x_ref[...] + mod_ref[5:6, :] * y

    _with_softmax_shift(ok_ref, shift_ref, body)


def _battn_ctx_kernel(ok_ref, shift_ref, x_ref, mod_ref, q_ref, k_ref, v_ref, sink_ref, wo_ref, o_ref):
    def body(shift):
        o = _gqa_attend(q_ref[...], k_ref[...], v_ref[...], None, sink_ref, shift)
        y = _dot(o.astype(BF), wo_ref[...])
        o_ref[...] = x_ref[...] + mod_ref[5:6, :] * y

    _with_softmax_shift(ok_ref, shift_ref, body)


def _battn(x, mod, q, k, v, cache_k, cache_v, sink, w_o, q_gain, k_gain):
    nq = B_HEADS * B_HEAD_DIM
    nk = B_KV_HEADS * B_HEAD_DIM
    cache_norm = jnp.sqrt(jnp.max(jnp.sum(jnp.square(cache_k.reshape(-1, B_HEAD_DIM)), axis=-1)))
    k_norm = jnp.maximum(math.sqrt(B_HEAD_DIM) * jnp.max(jnp.abs(k_gain)), cache_norm)
    bound = 1.02 * LOG2E * jnp.max(jnp.abs(q_gain)) * k_norm
    shift = bound.reshape(1)
    fixed_shift_ok = (bound <= C_SHIFT_MAX).astype(jnp.int32).reshape(1)
    nb = DEC_SEQ // BQ
    nw = DEC_SEQ // B_WINDOW
    per = BQ // B_WINDOW
    smem = pl.BlockSpec(memory_space=pltpu.SMEM)
    cur_spec = pl.BlockSpec((BQ, nk), lambda b, j: (b * nb + j, 0))
    prev_spec = pl.BlockSpec((B_WINDOW, nk), lambda b, j: (b * nw + jnp.maximum(per * j - 1, 0), 0))
    next_spec = pl.BlockSpec((B_WINDOW, nk), lambda b, j: (b * nw + jnp.minimum(per * j + per, nw - 1), 0))

    x = pl.pallas_call(
        _battn_lat_kernel,
        grid=(DEC_BATCH, nb),
        in_specs=[smem, smem, pl.BlockSpec((BQ, D_MODEL), lambda b, j: (b * nb + j, 0)),
                  pl.BlockSpec((None, N_MOD, D_MODEL), lambda b, j: (b, 0, 0)),
                  pl.BlockSpec((BQ, nq), lambda b, j: (b * nb + j, 0)),
                  prev_spec, cur_spec, next_spec, prev_spec, cur_spec, next_spec,
                  pl.BlockSpec((None, PAST_LEN, nk), lambda b, j: (b, 0, 0)),
                  pl.BlockSpec((None, PAST_LEN, nk), lambda b, j: (b, 0, 0)),
                  smem, _const_spec((nq, D_MODEL))],
        out_specs=pl.BlockSpec((BQ, D_MODEL), lambda b, j: (b * nb + j, 0)),
        out_shape=jax.ShapeDtypeStruct((N_TOK, D_MODEL), F32),
        input_output_aliases={2: 0},
        compiler_params=_params("arbitrary", "arbitrary"),
        name="gqa_attn_latent",
    )(fixed_shift_ok, shift, x, mod, q, k, k, k, v, v, v, cache_k, cache_v, sink, w_o)
    off = N_SAMPLE // SEQ
    return pl.pallas_call(
        _battn_ctx_kernel,
        grid=(BATCH,),
        in_specs=[smem, smem, pl.BlockSpec((SEQ, D_MODEL), lambda b: (off + b, 0)),
                  pl.BlockSpec((None, N_MOD, D_MODEL), lambda b: (DEC_BATCH, 0, 0)),
                  pl.BlockSpec((SEQ, nq), lambda b: (off + b, 0)),
                  pl.BlockSpec((SEQ, nk), lambda b: (off + b, 0)),
                  pl.BlockSpec((SEQ, nk), lambda b: (off + b, 0)),
                  smem, _const_spec((nq, D_MODEL))],
        out_specs=pl.BlockSpec((SEQ, D_MODEL), lambda b: (off + b, 0)),
        out_shape=jax.ShapeDtypeStruct((N_TOK, D_MODEL), F32),
        input_output_aliases={2: 0},
        compiler_params=_params("arbitrary"),
        name="gqa_attn_context",
    )(fixed_shift_ok, shift, x, mod, q, k, v, sink, w_o)


C_SWAP_W = C_HEADS * C_ROPE
C_SHIFT_LANE = C_NOPE + C_ROPE
C_SHIFT_MAX = 50.0


def _mla_head_norm_rope(t, t_swap, tab, tab_swap):
    r = lax.rsqrt(jnp.sum(t * t, axis=-1, keepdims=True) / (C_NOPE + C_ROPE) + EPS)
    if t_swap is None:
        return t * r * tab
    return r * (t * tab + t_swap * tab_swap)


def _mla_keys_values(c_kv_b, k_rope, k_rope_swap, wukv_ref, tab, tab_swap, k_shift, k_ref, v_ref):
    kv = _dot(c_kv_b, wukv_ref[...])
    lower = (lax.broadcasted_iota(jnp.int32, kv.shape, 1) & C_NOPE) == 0
    v_ref[...] = jnp.where(lower, 1.0, kv).astype(BF)
    k_nope = jnp.where(lower, kv, 0.0)
    for h in range(C_HEADS):
        sl = slice(h * C_HEAD_PAD, (h + 1) * C_HEAD_PAD)
        kh = _mla_head_norm_rope(k_nope[:, sl] + k_rope, k_rope_swap, tab, tab_swap)
        k_ref[:, sl] = (kh + k_shift).astype(BF)


def _cproj_kernel(x_ref, mod_ref, wd_ref, cqg_ref, ckvg_ref, wuq_ref, wukv_ref, qone_ref, kshift_ref,
                  qc_ref, qs_ref, kc_ref, ks_ref, q_ref, k_ref, v_ref, ckv_ref, kr_ref):
    hb = _ada(x_ref[...], mod_ref, 1).astype(BF)
    d = _dot(hb, wd_ref[...])
    c_q = _rms(d[:, :C_Q_LORA]) * cqg_ref[...]
    c_kv = _rms(d[:, C_Q_LORA:C_Q_LORA + C_KV_LORA]) * ckvg_ref[...]
    k_rope = d[:, C_Q_LORA + C_KV_LORA:C_Q_LORA + C_KV_LORA + LANES]
    k_rope_swap = d[:, C_Q_LORA + C_KV_LORA + LANES:]
    ckv_ref[...] = c_kv
    kr_ref[...] = k_rope
    q2 = _dot(c_q.astype(BF), wuq_ref[...])
    wq = C_HEADS * C_HEAD_PAD
    q_tab, q_tab_swap = qc_ref[...], qs_ref[...]
    per_tile = LANES // C_ROPE
    for h in range(C_HEADS):
        sl = slice(h * C_HEAD_PAD, (h + 1) * C_HEAD_PAD)
        t_swap = q2[:, wq + (h // per_tile) * LANES:wq + (h // per_tile + 1) * LANES]
        shift = (C_NOPE - C_ROPE * (h % per_tile)) % LANES
        if shift:
            t_swap = pltpu.roll(t_swap, shift, 1)
        q_ref[:, sl] = (_mla_head_norm_rope(q2[:, sl], t_swap, q_tab, q_tab_swap) + qone_ref[...]).astype(BF)
    _mla_keys_values(c_kv.astype(BF), k_rope, k_rope_swap, wukv_ref, kc_ref[...], ks_ref[...], kshift_ref[...],
                     k_ref, v_ref)


def _cctx_kernel(ckv_ref, kr_ref, wukv_ref, kg_ref, kshift_ref, k_ref, v_ref):
    _mla_keys_values(ckv_ref[...].astype(BF), kr_ref[...], None, wukv_ref, kg_ref[...], None, kshift_ref[...],
                     k_ref, v_ref)


def _mla_weights(w_down, w_uq, q_gain, k_gain):
    hd = C_NOPE + C_ROPE
    pad_lanes = C_HEAD_PAD - hd
    lane = np.arange(C_HEAD_PAD)
    is_rope = (lane >= C_NOPE) & (lane < hd)
    partner = np.where(is_rope, lane ^ (C_ROPE // 4), lane)

    def swapped(t):
        return jnp.where(is_rope, jnp.take(t, partner, axis=-1), 0.0)

    kr_cols = jnp.pad(w_down[:, C_Q_LORA + C_KV_LORA:], ((0, 0), (C_NOPE, pad_lanes)))
    wd = jnp.concatenate([w_down[:, :C_Q_LORA + C_KV_LORA], kr_cols, swapped(kr_cols)], axis=1)
    wuq = jnp.pad(w_uq.reshape(C_Q_LORA, C_HEADS, hd), ((0, 0), (0, 0), (0, pad_lanes)))
    wuq_swap = swapped(wuq)[:, :, C_NOPE:hd].reshape(C_Q_LORA, C_SWAP_W)
    wuq = jnp.concatenate([wuq.reshape(C_Q_LORA, C_HEADS * C_HEAD_PAD), wuq_swap], axis=1)
    qg = jnp.pad(q_gain, (0, pad_lanes))
    kg = jnp.pad(k_gain, (0, pad_lanes))
    row = lambda t: t.reshape(1, C_HEAD_PAD)
    bound = 1.02 * math.sqrt(hd) * LOG2E * jnp.max(jnp.abs(q_gain)) * jnp.max(jnp.abs(k_gain))
    shift_lane = lane == C_SHIFT_LANE
    q_one = row(jnp.asarray(shift_lane, F32))
    k_shift = row(jnp.where(shift_lane, -bound, 0.0))
    fixed_shift_ok = (bound <= C_SHIFT_MAX).astype(jnp.int32).reshape(1)
    return (wd.astype(BF), wuq.astype(BF), row(qg), row(swapped(qg)), row(kg), row(swapped(kg)),
            q_one, k_shift, fixed_shift_ok)


def _cproj(x, mod, wd, cq_gain, ckv_gain, wuq, wukv, q_one, k_shift, q_tab, q_tab_swap, k_tab, k_tab_swap):
    wq = C_HEADS * C_HEAD_PAD
    return pl.pallas_call(
        _cproj_kernel,
        grid=(N_TILES,),
        in_specs=[_tok_spec(D_MODEL), _MOD_SPEC, _const_spec((D_MODEL, C_DOWN_PAD)),
                  _const_spec((1, C_Q_LORA)), _const_spec((1, C_KV_LORA)),
                  _const_spec((C_Q_LORA, wq + C_SWAP_W)), _const_spec((C_KV_LORA, wq)),
                  _const_spec((1, C_HEAD_PAD)), _const_spec((1, C_HEAD_PAD)),
                  _ROPE_SPEC, _ROPE_SPEC, _ROPE_SPEC, _ROPE_SPEC],
        out_specs=[_tok_spec(wq), _tok_spec(wq), _tok_spec(wq), _tok_spec(C_KV_LORA), _tok_spec(LANES)],
        out_shape=[jax.ShapeDtypeStruct((N_TOK, wq), BF), jax.ShapeDtypeStruct((N_TOK, wq), BF),
                   jax.ShapeDtypeStruct((N_TOK, wq), BF), jax.ShapeDtypeStruct((N_TOK, C_KV_LORA), F32),
                   jax.ShapeDtypeStruct((N_TOK, LANES), F32)],
        compiler_params=_params("arbitrary"),
        name="mla_proj",
    )(x, mod, wd, cq_gain.reshape(1, C_Q_LORA), ckv_gain.reshape(1, C_KV_LORA), wuq, wukv, q_one, k_shift,
      q_tab, q_tab_swap, k_tab, k_tab_swap)


def _cctx(cache_ckv, cache_krope, wukv, kg, k_shift):
    n = DEC_BATCH * PAST_LEN
    wq = C_HEADS * C_HEAD_PAD
    kr = jnp.pad(cache_krope.reshape(n, C_ROPE), ((0, 0), (C_NOPE, C_HEAD_PAD - C_NOPE - C_ROPE)))
    return pl.pallas_call(
        _cctx_kernel,
        grid=(n // TM,),
        in_specs=[_tok_spec(C_KV_LORA), _tok_spec(LANES), _const_spec((C_KV_LORA, wq)),
                  _const_spec((1, C_HEAD_PAD)), _const_spec((1, C_HEAD_PAD))],
        out_specs=[_tok_spec(wq), _tok_spec(wq)],
        out_shape=[jax.ShapeDtypeStruct((n, wq), BF), jax.ShapeDtypeStruct((n, wq), BF)],
        compiler_params=_params("arbitrary"),
        name="mla_context_keys",
    )(cache_ckv.reshape(n, C_KV_LORA), kr, wukv, kg, k_shift)


def _mla_attend(ok_ref, q_ref, kv_refs, o_ref, pair=0):
    tq = q_ref.shape[0]
    cols = slice(pair * LANES, (pair + 1) * LANES)
    head_cols = [slice((2 * pair + e) * C_HEAD_PAD, (2 * pair + e + 1) * C_HEAD_PAD) for e in range(2)]

    def normalised(sum_acc, rows):
        lo = lax.broadcasted_iota(jnp.int32, (rows, LANES), 1) < C_VDIM
        r0 = pltpu.roll(sum_acc[0], C_VDIM, 1)
        r1 = pltpu.roll(sum_acc[1], C_VDIM, 1)
        return jnp.where(lo, r0 / sum_acc[0], sum_acc[1] / r1).astype(BF)

    chunks = []
    for k_ref, v_ref in kv_refs:
        nk = k_ref.shape[0]
        if nk < CK and chunks:
            chunks[-1].append((k_ref, v_ref, slice(0, nk)))
        else:
            ck = min(CK, nk)
            chunks += [[(k_ref, v_ref, slice(c * ck, (c + 1) * ck))] for c in range(nk // ck)]
    def fixed_shift():
        sum_acc = [None, None]
        for parts in chunks:
            for e, sl in enumerate(head_cols):
                kk = [k_ref[rows, sl] for k_ref, _, rows in parts]
                vv = [v_ref[rows, sl] for _, v_ref, rows in parts]
                p = jnp.exp2(_dot_t(q_ref[:, sl], kk[0] if len(kk) == 1 else jnp.concatenate(kk, axis=0)))
                pv = _dot(p.astype(BF), vv[0] if len(vv) == 1 else jnp.concatenate(vv, axis=0))
                sum_acc[e] = pv if sum_acc[e] is None else sum_acc[e] + pv
        o_ref[:, cols] = normalised(sum_acc, tq)

    def row_max_shift():
        blk = min(tq, LANES)

        def body(i, carry):
            rows = pl.ds(pl.multiple_of(i * blk, blk), blk)
            sum_acc = []
            for sl in head_cols:
                qh = q_ref[rows, sl]
                scores = [_dot_t(qh, k_ref[:, sl]) for k_ref, _ in kv_refs]
                m = functools.reduce(jnp.maximum, [jnp.max(sc, axis=-1, keepdims=True) for sc in scores])
                sum_acc.append(functools.reduce(jnp.add, [_dot(jnp.exp2(sc - m).astype(BF), v_ref[:, sl])
                                                          for sc, (_, v_ref) in zip(scores, kv_refs)]))
            o_ref[rows, cols] = normalised(sum_acc, blk)
            return carry

        lax.fori_loop(0, tq // blk, body, 0)

    lax.cond(ok_ref[0] != 0, fixed_shift, row_max_shift)


def _cattn_lat_kernel(ok_ref, q_ref, k_ref, v_ref, ck_ref, cv_ref, o_ref):
    _mla_attend(ok_ref, q_ref, ((k_ref, v_ref), (ck_ref, cv_ref)), o_ref)


def _cattn_ctx_kernel(ok_ref, q_ref, k_ref, v_ref, oin_ref, o_ref):
    del oin_ref
    for pair in range(C_HEADS // 2):
        _mla_attend(ok_ref, q_ref, ((k_ref, v_ref),), o_ref, pair)


def _cattn(fixed_shift_ok, q, k, v, ck, cv):
    pair_w = 2 * C_HEAD_PAD
    n_pairs = C_HEADS // 2
    nqt = DEC_SEQ // CQ
    smem = pl.BlockSpec(memory_space=pltpu.SMEM)
    o = pl.pallas_call(
        _cattn_lat_kernel,
        grid=(DEC_BATCH, n_pairs, nqt),
        in_specs=[smem, pl.BlockSpec((CQ, pair_w), lambda b, p, t: (b * nqt + t, p)),
                  pl.BlockSpec((DEC_SEQ, pair_w), lambda b, p, t: (b, p)),
                  pl.BlockSpec((DEC_SEQ, pair_w), lambda b, p, t: (b, p)),
                  pl.BlockSpec((PAST_LEN, pair_w), lambda b, p, t: (b, p)),
                  pl.BlockSpec((PAST_LEN, pair_w), lambda b, p, t: (b, p))],
        out_specs=pl.BlockSpec((CQ, LANES), lambda b, p, t: (b * nqt + t, p)),
        out_shape=jax.ShapeDtypeStruct((N_TOK, C_HEADS * C_VDIM), BF),
        compiler_params=_params("arbitrary", "arbitrary", "arbitrary"),
        name="mla_attn_latent",
    )(fixed_shift_ok, q, k, v, ck, cv)
    off = N_SAMPLE // SEQ
    return pl.pallas_call(
        _cattn_ctx_kernel,
        grid=(BATCH,),
        in_specs=[smem, pl.BlockSpec((SEQ, n_pairs * pair_w), lambda b: (off + b, 0)),
                  pl.BlockSpec((SEQ, n_pairs * pair_w), lambda b: (off + b, 0)),
                  pl.BlockSpec((SEQ, n_pairs * pair_w), lambda b: (off + b, 0)),
                  pl.BlockSpec(memory_space=pl.ANY)],
        out_specs=pl.BlockSpec((SEQ, n_pairs * LANES), lambda b: (off + b, 0)),
        out_shape=jax.ShapeDtypeStruct((N_TOK, C_HEADS * C_VDIM), BF),
        input_output_aliases={4: 0},
        compiler_params=_params("arbitrary"),
        name="mla_attn_context",
    )(fixed_shift_ok, q, k, v, o)


def kernel(x_prompt, x_sample, c, cache_win_k, cache_win_v, cache_mla_ckv, cache_mla_krope, c_ctx,
           ada_w, ada_b, ffn_w_in, ffn_w_out,
           gmlp_w_in, gmlp_v_gain, gmlp_w_s, gmlp_b_s, gmlp_w_out,
           win_w_qkv, win_q_gain, win_k_gain, win_sink, win_w_o,
           mla_w_down, mla_cq_gain, mla_ckv_gain, mla_w_uq, mla_w_ukv, mla_q_gain, mla_k_gain, mla_w_o):
    x = (x_sample.reshape(N_SAMPLE, D_MODEL), x_prompt.reshape(N_PROMPT, D_MODEL))
    cond = jnp.concatenate([c, c_ctx[None, :], jnp.zeros((N_COND - DEC_BATCH - 1, D_MODEL), F32)], axis=0)
    mods = _modulation(cond, ada_w, ada_b)
    w_in_b, shift_terms = _ffn_prep(mods, ffn_w_in)
    w_out_b = ffn_w_out.astype(BF)

    lane = np.arange(LANES)
    b_cos, b_sin = _rope_tables(B_HEAD_DIM, lane % B_HEAD_DIM)
    c_lane = np.where((lane >= C_NOPE) & (lane < C_NOPE + C_ROPE), lane - C_NOPE, -1)
    c_cos, c_sin = _rope_tables(C_ROPE, c_lane)

    nk = B_KV_HEADS * B_HEAD_DIM
    win_k, win_v, mla_ckv, mla_krope = [], [], [], []
    ia = ib = ic = 0
    for l in range(DEPTH):
        mod = mods[l]
        pre_proj = None
        x = _ffn(x, mod, shift_terms, w_in_b, w_out_b, l, 0, split_in=(l == 0))
        kind = l % N_MIXERS
        if kind == 0:
            x = _gmlp(x, mod, gmlp_w_in[ia], gmlp_v_gain[ia], gmlp_w_s[ia], gmlp_b_s[ia], gmlp_w_out[ia])
            ia += 1
        elif kind == 1:
            q, k, v = _bproj(x, mod, win_w_qkv[ib], win_q_gain[ib], win_k_gain[ib], b_cos, b_sin)
            x = _battn(x, mod, q, k, v,
                       cache_win_k[:, ib].reshape(DEC_BATCH, PAST_LEN, nk),
                       cache_win_v[:, ib].reshape(DEC_BATCH, PAST_LEN, nk),
                       win_sink[ib], win_w_o[ib].astype(BF), win_q_gain[ib], win_k_gain[ib])
            win_k.append(k[N_SAMPLE:].reshape(BATCH, SEQ, B_KV_HEADS, B_HEAD_DIM))
            win_v.append(v[N_SAMPLE:].reshape(BATCH, SEQ, B_KV_HEADS, B_HEAD_DIM))
            ib += 1
        else:
            wd, wuq, qg, qgs, kg, kgs, q_one, k_shift, fixed_shift_ok = _mla_weights(
                mla_w_down[ic], mla_w_uq[ic], mla_q_gain[ic], mla_k_gain[ic])
            wukv = mla_w_ukv[ic].astype(BF)
            q_scale = (C_NOPE + C_ROPE) ** -0.5 * LOG2E
            q, k, v, ckv, kr = _cproj(x, mod, wd, mla_cq_gain[ic], mla_ckv_gain[ic], wuq, wukv, q_one, k_shift,
                                      c_cos * (qg * q_scale), c_sin * (qgs * q_scale), c_cos * kg, c_sin * kgs)
            ck, cv = _cctx(cache_mla_ckv[:, ic], cache_mla_krope[:, ic], wukv, kg, k_shift)
            pre_proj = (_cattn(fixed_shift_ok, q, k, v, ck, cv), mla_w_o[ic].astype(BF))
            mla_ckv.append(ckv[N_SAMPLE:].reshape(BATCH, SEQ, C_KV_LORA))
            mla_krope.append(kr[N_SAMPLE:, C_NOPE:C_NOPE + C_ROPE].reshape(BATCH, SEQ, C_ROPE))
            ic += 1
        x = _ffn(x, mod, shift_terms, w_in_b, w_out_b, l, 1, split_out=(l == DEPTH - 1), pre_proj=pre_proj)
    y_sample, y_prompt = x
    return (y_prompt.reshape(BATCH, SEQ, D_MODEL), y_sample.reshape(DEC_BATCH, DEC_SEQ, D_MODEL),
            jnp.stack(win_k, axis=1), jnp.stack(win_v, axis=1),
            jnp.stack(mla_ckv, axis=1), jnp.stack(mla_krope, axis=1))
```

```python
import functools
import math

import jax
import jax.numpy as jnp
import numpy as np
from jax import lax
from jax.experimental import pallas as pl
from jax.experimental.pallas import tpu as pltpu

D_MODEL = 1024
BATCH = 16
SEQ = 256
DEPTH = 4
DEC_BATCH = 8
DEC_SEQ = 4096
PAST_LEN = 256
GRID_W = 64
N_MIXERS = 3
N_MOD = 9
D_FF = 2816
A_WIDTH = D_MODEL
A_GROUPS = 8
A_CHUNK = 128
B_HEADS = 16
B_KV_HEADS = 4
B_HEAD_DIM = 64
B_WINDOW = 128
C_HEADS = 16
C_NOPE = 64
C_ROPE = 32
C_VDIM = 64
C_Q_LORA = 512
C_KV_LORA = 256
ROPE_BASE = 10000.0
EPS = 1e-6
NEG_INF = -1e30

LANES = 128
N_SAMPLE = DEC_BATCH * DEC_SEQ
N_PROMPT = BATCH * SEQ
N_TOK = N_SAMPLE + N_PROMPT
N_COND = 16
TM = 512
FM = 1024
FF_CHUNKS = ((0, 1536), (1536, 1280))
N_TILES = N_TOK // TM
N_SAMPLE_TILES = N_SAMPLE // TM
TILES_PER_SEQ = DEC_SEQ // TM
MOD_TN = 1536
SHIFT_TN = 1408
BQ = 256
CQ = 2048
CK = 256
C_HEAD_PAD = 128
C_DOWN_PAD = C_Q_LORA + C_KV_LORA + 2 * LANES
VMEM_LIMIT_BYTES = 56 * 1024 * 1024

LOG2E = math.log2(math.e)

BF = jnp.bfloat16
F32 = jnp.float32


def _params(*sem):
    return pltpu.CompilerParams(dimension_semantics=sem, vmem_limit_bytes=VMEM_LIMIT_BYTES)


def _dot(a, b):
    return jnp.dot(a, b, preferred_element_type=F32)


def _dot_t(a, b):
    return lax.dot_general(a, b, (((1,), (1,)), ((), ())), preferred_element_type=F32)


def _rms(x):
    return x * lax.rsqrt(jnp.mean(x * x, axis=-1, keepdims=True) + EPS)


def _ada(x, mod_ref, k):
    shift = mod_ref[3 * k:3 * k + 1, :]
    scale = mod_ref[3 * k + 1:3 * k + 2, :]
    return _rms(x) * (1.0 + scale) + shift


def _const_spec(shape):
    nd = len(shape)
    return pl.BlockSpec(shape, lambda *_: (0,) * nd, pipeline_mode=pl.Buffered(1))


def _tok_spec(width, tm=TM):
    return pl.BlockSpec((tm, width), lambda i: (i, 0))


def _mod_spec(tm):
    return pl.BlockSpec((None, N_MOD, D_MODEL), lambda i: (jnp.minimum(i * tm // DEC_SEQ, DEC_BATCH), 0, 0))


_MOD_SPEC = _mod_spec(TM)


def _rope_tile(i):
    return jnp.where(i < N_SAMPLE_TILES, i % TILES_PER_SEQ, TILES_PER_SEQ)


_ROPE_SPEC = pl.BlockSpec((TM, LANES), lambda i: (_rope_tile(i), 0))


def _mod_kernel(c_ref, w_ref, b_ref, o_ref):
    a = jax.nn.silu(c_ref[...]).astype(BF)
    o_ref[...] = _dot(a, w_ref[...].astype(BF)) + b_ref[...]


def _modulation(cond, ada_w, ada_b):
    n_out = N_MOD * D_MODEL
    out = pl.pallas_call(
        _mod_kernel,
        grid=(DEPTH, n_out // MOD_TN),
        in_specs=[
            pl.BlockSpec((N_COND, D_MODEL), lambda l, j: (0, 0)),
            pl.BlockSpec((None, D_MODEL, MOD_TN), lambda l, j: (l, 0, j)),
            pl.BlockSpec((None, 1, MOD_TN), lambda l, j: (l, 0, j)),
        ],
        out_specs=pl.BlockSpec((None, N_COND, MOD_TN), lambda l, j: (l, 0, j)),
        out_shape=jax.ShapeDtypeStruct((DEPTH, N_COND, n_out), F32),
        compiler_params=_params("arbitrary", "arbitrary"),
        name="modulation",
    )(cond, ada_w, ada_b.reshape(DEPTH, 1, n_out))
    return out.reshape(DEPTH, N_COND, N_MOD, D_MODEL)


def _ffn_prep_kernel(s_ref, w_ref, wb_ref, o_ref):
    wb = w_ref[...].astype(BF)
    wb_ref[...] = wb
    o_ref[...] = _dot(s_ref[...].astype(BF), wb)


def _ffn_prep(mods, w_in):
    shifts = jnp.stack([mods[:, :, 0], mods[:, :, 6]], axis=1)
    w_spec = pl.BlockSpec((None, None, D_MODEL, SHIFT_TN), lambda l, h, j: (l, h, 0, j))
    w_b, out = pl.pallas_call(
        _ffn_prep_kernel,
        grid=(DEPTH, 2, 2 * D_FF // SHIFT_TN),
        in_specs=[pl.BlockSpec((None, None, N_COND, D_MODEL), lambda l, h, j: (l, h, 0, 0)), w_spec],
        out_specs=[w_spec, pl.BlockSpec((None, None, N_COND, SHIFT_TN), lambda l, h, j: (l, h, 0, j))],
        out_shape=[jax.ShapeDtypeStruct(w_in.shape, BF),
                   jax.ShapeDtypeStruct((DEPTH, 2, N_COND, 2 * D_FF), F32)],
        compiler_params=_params("arbitrary", "arbitrary", "arbitrary"),
        name="ffn_prep",
    )(shifts, w_in)
    return w_b, out.reshape(DEPTH, 2, N_COND, 1, 2 * D_FF)


def _ffn_kernel(*refs, k, split_in, split_out, pre_proj):
    n_x = 2 if split_in else 1
    n_in = n_x + (2 if pre_proj else 0)
    x_refs, (mod_ref, sh_ref, win_ref, wout_ref), o_refs = refs[:n_x], refs[n_in:n_in + 4], refs[n_in + 4:]
    is_sample = pl.program_id(0) < N_SAMPLE // FM
    x = jnp.where(is_sample, x_refs[0][...], x_refs[1][...]) if split_in else x_refs[0][...]
    if pre_proj:
        attn_ref, wo_ref = refs[n_x:n_in]
        x = x + mod_ref[5:6, :] * _dot(attn_ref[...], wo_ref[...])
    xa = (x * (1.0 + mod_ref[3 * k + 1:3 * k + 2, :])).astype(BF)
    rinv = lax.rsqrt(jnp.mean(x * x, axis=-1, keepdims=True) + EPS)
    y = None
    for c0, cw in FF_CHUNKS:
        g = _dot(xa, win_ref[:, c0:c0 + cw]) * rinv + sh_ref[:, c0:c0 + cw]
        u = _dot(xa, win_ref[:, D_FF + c0:D_FF + c0 + cw]) * rinv + sh_ref[:, D_FF + c0:D_FF + c0 + cw]
        yc = _dot((jax.nn.silu(g) * u).astype(BF), wout_ref[c0:c0 + cw, :])
        y = yc if y is None else y + yc
    gate = mod_ref[3 * k + 2:3 * k + 3, :]
    out = x + (0.5 * gate) * y
    if split_out:
        o_refs[1][...] = out

        @pl.when(is_sample)
        def _():
            o_refs[0][...] = out
    else:
        o_refs[0][...] = out


_SAMPLE_SPEC = pl.BlockSpec((FM, D_MODEL), lambda i: (jnp.minimum(i, N_SAMPLE // FM - 1), 0))
_PROMPT_SPEC = pl.BlockSpec((FM, D_MODEL), lambda i: (jnp.maximum(i - N_SAMPLE // FM, 0), 0))


def _ffn(xs, mod, shift_terms, w_in, w_out, layer, half, split_in=False, split_out=False, pre_proj=None):
    def w_spec(rows, cols):
        return pl.BlockSpec((None, None, rows, cols), lambda i: (layer, half, 0, 0), pipeline_mode=pl.Buffered(1))

    sh_spec = pl.BlockSpec((None, None, None, 1, 2 * D_FF),
                           lambda i: (layer, half, jnp.minimum(i * FM // DEC_SEQ, DEC_BATCH), 0, 0))

    x_specs = [_SAMPLE_SPEC, _PROMPT_SPEC] if split_in else [_tok_spec(D_MODEL, FM)]
    operands = list(xs) if split_in else [xs]
    if pre_proj is not None:
        x_specs += [_tok_spec(D_MODEL, FM), _const_spec((D_MODEL, D_MODEL))]
        operands += list(pre_proj)
    if split_out:
        out_specs = [_SAMPLE_SPEC, _PROMPT_SPEC]
        out_shape = [jax.ShapeDtypeStruct((N_SAMPLE, D_MODEL), F32), jax.ShapeDtypeStruct((N_PROMPT, D_MODEL), F32)]
    else:
        out_specs = _tok_spec(D_MODEL, FM)
        out_shape = jax.ShapeDtypeStruct((N_TOK, D_MODEL), F32)
    return pl.pallas_call(
        functools.partial(_ffn_kernel, k=2 * half, split_in=split_in, split_out=split_out,
                          pre_proj=pre_proj is not None),
        grid=(N_TOK // FM,),
        in_specs=x_specs + [_mod_spec(FM), sh_spec, w_spec(D_MODEL, 2 * D_FF), w_spec(D_FF, D_MODEL)],
        out_specs=out_specs,
        out_shape=out_shape,
        compiler_params=_params("arbitrary"),
        name="ffn",
    )(*operands, mod, shift_terms, w_in, w_out)


def _gmlp_kernel(x_ref, mod_ref, win_ref, vg_ref, ws_ref, bs_ref, wout_ref, o_ref):
    x = x_ref[...]
    hb = _ada(x, mod_ref, 1).astype(BF)

    def gelu_proj(c0):
        pre = _dot(hb, win_ref[:, c0:c0 + A_WIDTH])
        return 0.5 * pre * (1.0 + lax.erf(pre * math.sqrt(0.5)))

    v = (_rms(gelu_proj(A_WIDTH)) * vg_ref[...]).astype(BF)
    u = gelu_proj(0)
    bias = bs_ref[...]
    rows = []
    for c in range(FM // A_CHUNK):
        cols = [_dot(ws_ref[g], v[c * A_CHUNK:(c + 1) * A_CHUNK, g * LANES:(g + 1) * LANES])
                for g in range(A_GROUPS)]
        rows.append(jnp.concatenate(cols, axis=1) + bias)
    sv = jnp.concatenate(rows, axis=0)
    y = _dot((u * sv).astype(BF), wout_ref[...])
    o_ref[...] = x + mod_ref[5:6, :] * y


def _gmlp(x, mod, w_in, v_gain, w_s, b_s, w_out):
    bias = jnp.repeat(b_s.T, A_WIDTH // A_GROUPS, axis=1)
    return pl.pallas_call(
        _gmlp_kernel,
        grid=(N_TOK // FM,),
        in_specs=[_tok_spec(D_MODEL, FM), _mod_spec(FM),
                  _const_spec((D_MODEL, 2 * A_WIDTH)), _const_spec((1, A_WIDTH)),
                  _const_spec((A_GROUPS, A_CHUNK, A_CHUNK)), _const_spec((A_CHUNK, A_WIDTH)),
                  _const_spec((A_WIDTH, D_MODEL))],
        out_specs=_tok_spec(D_MODEL, FM),
        out_shape=jax.ShapeDtypeStruct((N_TOK, D_MODEL), F32),
        compiler_params=_params("arbitrary"),
        name="gmlp",
    )(x, mod, w_in.astype(BF), v_gain.reshape(1, A_WIDTH), w_s.astype(BF), bias, w_out.astype(BF))


def _swap_pairs(y, step):
    lane = lax.broadcasted_iota(jnp.int32, y.shape, 1)
    return jnp.where((lane & step) != 0, pltpu.roll(y, step, 1), pltpu.roll(y, LANES - step, 1))


def _rope_tables(rot_dim, lane_of_dim):
    quarter = rot_dim // 4
    inv = np.float32(ROPE_BASE) ** (-np.arange(quarter, dtype=np.float32) / np.float32(quarter))
    t = np.arange(DEC_SEQ)
    row = (t // GRID_W).astype(np.float32)
    col = (t % GRID_W).astype(np.float32)
    ang = np.stack([row[:, None] * inv, col[:, None] * inv], axis=1)
    cos, sin = np.cos(ang), np.sin(ang)
    d = np.asarray(lane_of_dim)
    dd = np.maximum(d, 0)
    axis, member, freq = dd // (2 * quarter), (dd % (2 * quarter)) // quarter, dd % quarter
    rot = (d >= 0)[None, :]
    c_tab = np.where(rot, cos[:, axis, freq], 1.0)
    s_tab = np.where(rot, np.where(member == 0, -1.0, 1.0)[None, :] * sin[:, axis, freq], 0.0)
    ident_c = np.ones((TM, LANES), np.float32)
    ident_s = np.zeros((TM, LANES), np.float32)
    return (np.concatenate([c_tab, ident_c], axis=0).astype(np.float32),
            np.concatenate([s_tab, ident_s], axis=0).astype(np.float32))


def _bproj_kernel(x_ref, mod_ref, w_ref, ones_ref, qc_ref, qs_ref, kc_ref, ks_ref, q_ref, k_ref, v_ref):
    hb = _ada(x_ref[...], mod_ref, 1).astype(BF)
    qkv = _dot(hb, w_ref[...])
    head_ones = ones_ref[...]

    def norm_rope(t4, tab, tab_swap):
        sq = t4 * t4
        hi = sq.astype(BF)
        lo = (sq - hi.astype(F32)).astype(BF)
        r = lax.rsqrt((_dot(hi, head_ones) + _dot(lo, head_ones)) / B_HEAD_DIM + EPS)
        halves = []
        for j in range(2):
            sl = slice(j * LANES, (j + 1) * LANES)
            t = t4[:, sl]
            halves.append(r[:, sl] * (t * tab + _swap_pairs(t, B_HEAD_DIM // 4) * tab_swap))
        return halves

    nq = B_HEADS * B_HEAD_DIM
    nk = B_KV_HEADS * B_HEAD_DIM
    q_tab, q_tab_swap = qc_ref[...], qs_ref[...]
    for j in range(nq // (2 * LANES)):
        halves = norm_rope(qkv[:, 2 * j * LANES:(2 * j + 2) * LANES], q_tab, q_tab_swap)
        q_ref[:, 2 * j * LANES:(2 * j + 1) * LANES] = halves[0].astype(BF)
        q_ref[:, (2 * j + 1) * LANES:(2 * j + 2) * LANES] = halves[1].astype(BF)
    halves = norm_rope(qkv[:, nq:nq + nk], kc_ref[...], ks_ref[...])
    k_ref[:, :LANES] = halves[0]
    k_ref[:, LANES:] = halves[1]
    v_ref[...] = qkv[:, nq + nk:]


def _bproj(x, mod, w_qkv, q_gain, k_gain, cos, sin):
    nq = B_HEADS * B_HEAD_DIM
    nk = B_KV_HEADS * B_HEAD_DIM
    q_scale = B_HEAD_DIM ** -0.5 * LOG2E
    lane = np.arange(LANES)
    qg = jnp.tile(q_gain, LANES // B_HEAD_DIM)
    kg = jnp.tile(k_gain, LANES // B_HEAD_DIM)
    partner = lane ^ (B_HEAD_DIM // 4)
    head = np.arange(nk) // B_HEAD_DIM
    head_ones = jnp.asarray(head[:, None] == head[None, :], BF)
    return pl.pallas_call(
        _bproj_kernel,
        grid=(N_TILES,),
        in_specs=[_tok_spec(D_MODEL), _MOD_SPEC, _const_spec((D_MODEL, nq + 2 * nk)), _const_spec((nk, nk)),
                  _ROPE_SPEC, _ROPE_SPEC, _ROPE_SPEC, _ROPE_SPEC],
        out_specs=[_tok_spec(nq), _tok_spec(nk), _tok_spec(nk)],
        out_shape=[jax.ShapeDtypeStruct((N_TOK, nq), BF),
                   jax.ShapeDtypeStruct((N_TOK, nk), F32),
                   jax.ShapeDtypeStruct((N_TOK, nk), F32)],
        compiler_params=_params("arbitrary"),
        name="gqa_proj",
    )(x, mod, w_qkv.astype(BF), head_ones,
      cos * (qg * q_scale), sin * (qg[partner] * q_scale), cos * kg, sin * kg[partner])


def _gqa_attend(q, kcat, vcat, bias, sink_ref, shift):
    tq = q.shape[0]
    nk = kcat.shape[0]
    lo = lax.broadcasted_iota(jnp.int32, (nk, LANES), 1) < B_HEAD_DIM
    lo_q = lax.broadcasted_iota(jnp.int32, (2 * tq, LANES), 1) < B_HEAD_DIM
    first = lax.broadcasted_iota(jnp.int32, (2 * tq, 1), 0) < tq
    if bias is not None:
        bias = jnp.concatenate([bias, bias], axis=0)
    outs = []
    for g in range(B_KV_HEADS):
        sl = slice((g // 2) * LANES, (g // 2 + 1) * LANES)
        own = lo if g % 2 == 0 else jnp.logical_not(lo)
        k_own = jnp.where(own, kcat[:, sl], 0.0)
        k_swp = pltpu.roll(k_own, B_HEAD_DIM, 1)
        v_own = jnp.where(own, vcat[:, sl], 1.0)
        v_swp = pltpu.roll(v_own, B_HEAD_DIM, 1)
        k_half = (k_own, k_swp) if g % 2 == 0 else (k_swp, k_own)
        v_half = (v_own, v_swp) if g % 2 == 0 else (v_swp, v_own)
        qg = jnp.concatenate([q[:, (2 * g) * LANES:(2 * g + 1) * LANES],
                              q[:, (2 * g + 1) * LANES:(2 * g + 2) * LANES]], axis=0)
        s_all = _dot_t(qg, jnp.concatenate(k_half, axis=0).astype(BF))
        o_half = []
        for e in range(2):
            s = s_all[:, e * nk:(e + 1) * nk]
            if bias is not None:
                nb = bias.shape[1]
                s = jnp.concatenate([s[:, :nb] + bias, s[:, nb:]], axis=1)
            sk = jnp.where(first, sink_ref[4 * g + e], sink_ref[4 * g + 2 + e]) * LOG2E
            m = jnp.maximum(sk, jnp.max(s, axis=-1, keepdims=True) if shift is None else shift)
            ov = _dot(jnp.exp2(s - m).astype(BF), v_half[e].astype(BF))
            o_half.append(ov / (pltpu.roll(ov, B_HEAD_DIM, 1) + jnp.exp2(sk - m)))
        o_g = jnp.where(lo_q, o_half[0], o_half[1])
        outs += [o_g[:tq], o_g[tq:]]
    return jnp.concatenate(outs, axis=1)


def _with_softmax_shift(ok_ref, shift_ref, body):
    lax.cond(ok_ref[0] != 0, lambda: body(shift_ref[0]), lambda: body(None))


def _battn_lat_kernel(ok_ref, shift_ref, x_ref, mod_ref, q_ref, kp_ref, kc_ref, kn_ref, vp_ref, vc_ref, vn_ref,
                      ck_ref, cv_ref, sink_ref, wo_ref, o_ref):
    def body(shift):
        j = pl.program_id(1)
        kcat = jnp.concatenate([kp_ref[...], kc_ref[...], kn_ref[...], ck_ref[...]], axis=0)
        vcat = jnp.concatenate([vp_ref[...], vc_ref[...], vn_ref[...], cv_ref[...]], axis=0)
        n_lat = BQ + 2 * B_WINDOW
        qi = lax.broadcasted_iota(jnp.int32, (BQ, n_lat), 0)
        pk = lax.broadcasted_iota(jnp.int32, (BQ, n_lat), 1)
        kpos = j * BQ + pk - B_WINDOW
        valid = (jnp.abs(pk - B_WINDOW - qi) <= B_WINDOW) & (kpos >= 0) & (kpos < DEC_SEQ)
        bias = jnp.where(valid, 0.0, NEG_INF)
        o = _gqa_attend(q_ref[...], kcat, vcat, bias, sink_ref, shift)
        y = _dot(o.astype(BF), wo_ref[...])
        o_ref[...] = x_ref[...] + mod_ref[5:6, :] * y

    _with_softmax_shift(ok_ref, shift_ref, body)


def _battn_ctx_kernel(ok_ref, shift_ref, x_ref, mod_ref, q_ref, k_ref, v_ref, sink_ref, wo_ref, o_ref):
    def body(shift):
        o = _gqa_attend(q_ref[...], k_ref[...], v_ref[...], None, sink_ref, shift)
        y = _dot(o.astype(BF), wo_ref[...])
        o_ref[...] = x_ref[...] + mod_ref[5:6, :] * y

    _with_softmax_shift(ok_ref, shift_ref, body)


def _battn(x, mod, q, k, v, cache_k, cache_v, sink, w_o, q_gain, k_gain):
    nq = B_HEADS * B_HEAD_DIM
    nk = B_KV_HEADS * B_HEAD_DIM
    cache_norm = jnp.sqrt(jnp.max(jnp.sum(jnp.square(cache_k.reshape(-1, B_HEAD_DIM)), axis=-1)))
    k_norm = jnp.maximum(math.sqrt(B_HEAD_DIM) * jnp.max(jnp.abs(k_gain)), cache_norm)
    bound = 1.02 * LOG2E * jnp.max(jnp.abs(q_gain)) * k_norm
    shift = bound.reshape(1)
    fixed_shift_ok = (bound <= C_SHIFT_MAX).astype(jnp.int32).reshape(1)
    nb = DEC_SEQ // BQ
    nw = DEC_SEQ // B_WINDOW
    per = BQ // B_WINDOW
    smem = pl.BlockSpec(memory_space=pltpu.SMEM)
    cur_spec = pl.BlockSpec((BQ, nk), lambda b, j: (b * nb + j, 0))
    prev_spec = pl.BlockSpec((B_WINDOW, nk), lambda b, j: (b * nw + jnp.maximum(per * j - 1, 0), 0))
    next_spec = pl.BlockSpec((B_WINDOW, nk), lambda b, j: (b * nw + jnp.minimum(per * j + per, nw - 1), 0))

    x = pl.pallas_call(
        _battn_lat_kernel,
        grid=(DEC_BATCH, nb),
        in_specs=[smem, smem, pl.BlockSpec((BQ, D_MODEL), lambda b, j: (b * nb + j, 0)),
                  pl.BlockSpec((None, N_MOD, D_MODEL), lambda b, j: (b, 0, 0)),
                  pl.BlockSpec((BQ, nq), lambda b, j: (b * nb + j, 0)),
                  prev_spec, cur_spec, next_spec, prev_spec, cur_spec, next_spec,
                  pl.BlockSpec((None, PAST_LEN, nk), lambda b, j: (b, 0, 0)),
                  pl.BlockSpec((None, PAST_LEN, nk), lambda b, j: (b, 0, 0)),
                  smem, _const_spec((nq, D_MODEL))],
        out_specs=pl.BlockSpec((BQ, D_MODEL), lambda b, j: (b * nb + j, 0)),
        out_shape=jax.ShapeDtypeStruct((N_TOK, D_MODEL), F32),
        input_output_aliases={2: 0},
        compiler_params=_params("arbitrary", "arbitrary"),
        name="gqa_attn_latent",
    )(fixed_shift_ok, shift, x, mod, q, k, k, k, v, v, v, cache_k, cache_v, sink, w_o)
    off = N_SAMPLE // SEQ
    return pl.pallas_call(
        _battn_ctx_kernel,
        grid=(BATCH,),
        in_specs=[smem, smem, pl.BlockSpec((SEQ, D_MODEL), lambda b: (off + b, 0)),
                  pl.BlockSpec((None, N_MOD, D_MODEL), lambda b: (DEC_BATCH, 0, 0)),
                  pl.BlockSpec((SEQ, nq), lambda b: (off + b, 0)),
                  pl.BlockSpec((SEQ, nk), lambda b: (off + b, 0)),
                  pl.BlockSpec((SEQ, nk), lambda b: (off + b, 0)),
                  smem, _const_spec((nq, D_MODEL))],
        out_specs=pl.BlockSpec((SEQ, D_MODEL), lambda b: (off + b, 0)),
        out_shape=jax.ShapeDtypeStruct((N_TOK, D_MODEL), F32),
        input_output_aliases={2: 0},
        compiler_params=_params("arbitrary"),
        name="gqa_attn_context",
    )(fixed_shift_ok, shift, x, mod, q, k, v, sink, w_o)


C_SWAP_W = C_HEADS * C_ROPE
C_SHIFT_LANE = C_NOPE + C_ROPE
C_SHIFT_MAX = 50.0


def _mla_head_norm_rope(t, t_swap, tab, tab_swap):
    r = lax.rsqrt(jnp.sum(t * t, axis=-1, keepdims=True) / (C_NOPE + C_ROPE) + EPS)
    if t_swap is None:
        return t * r * tab
    return r * (t * tab + t_swap * tab_swap)


def _mla_keys_values(c_kv_b, k_rope, k_rope_swap, wukv_ref, tab, tab_swap, k_shift, k_ref, v_ref):
    kv = _dot(c_kv_b, wukv_ref[...])
    lower = (lax.broadcasted_iota(jnp.int32, kv.shape, 1) & C_NOPE) == 0
    v_ref[...] = jnp.where(lower, 1.0, kv).astype(BF)
    k_nope = jnp.where(lower, kv, 0.0)
    for h in range(C_HEADS):
        sl = slice(h * C_HEAD_PAD, (h + 1) * C_HEAD_PAD)
        kh = _mla_head_norm_rope(k_nope[:, sl] + k_rope, k_rope_swap, tab, tab_swap)
        k_ref[:, sl] = (kh + k_shift).astype(BF)


def _cproj_kernel(x_ref, mod_ref, wd_ref, cqg_ref, ckvg_ref, wuq_ref, wukv_ref, qone_ref, kshift_ref,
                  qc_ref, qs_ref, kc_ref, ks_ref, q_ref, k_ref, v_ref, ckv_ref, kr_ref):
    hb = _ada(x_ref[...], mod_ref, 1).astype(BF)
    d = _dot(hb, wd_ref[...])
    c_q = _rms(d[:, :C_Q_LORA]) * cqg_ref[...]
    c_kv = _rms(d[:, C_Q_LORA:C_Q_LORA + C_KV_LORA]) * ckvg_ref[...]
    k_rope = d[:, C_Q_LORA + C_KV_LORA:C_Q_LORA + C_KV_LORA + LANES]
    k_rope_swap = d[:, C_Q_LORA + C_KV_LORA + LANES:]
    ckv_ref[...] = c_kv
    kr_ref[...] = k_rope
    q2 = _dot(c_q.astype(BF), wuq_ref[...])
    wq = C_HEADS * C_HEAD_PAD
    q_tab, q_tab_swap = qc_ref[...], qs_ref[...]
    per_tile = LANES // C_ROPE
    for h in range(C_HEADS):
        sl = slice(h * C_HEAD_PAD, (h + 1) * C_HEAD_PAD)
        t_swap = q2[:, wq + (h // per_tile) * LANES:wq + (h // per_tile + 1) * LANES]
        shift = (C_NOPE - C_ROPE * (h % per_tile)) % LANES
        if shift:
            t_swap = pltpu.roll(t_swap, shift, 1)
        q_ref[:, sl] = (_mla_head_norm_rope(q2[:, sl], t_swap, q_tab, q_tab_swap) + qone_ref[...]).astype(BF)
    _mla_keys_values(c_kv.astype(BF), k_rope, k_rope_swap, wukv_ref, kc_ref[...], ks_ref[...], kshift_ref[...],
                     k_ref, v_ref)


def _cctx_kernel(ckv_ref, kr_ref, wukv_ref, kg_ref, kshift_ref, k_ref, v_ref):
    _mla_keys_values(ckv_ref[...].astype(BF), kr_ref[...], None, wukv_ref, kg_ref[...], None, kshift_ref[...],
                     k_ref, v_ref)


def _mla_weights(w_down, w_uq, q_gain, k_gain):
    hd = C_NOPE + C_ROPE
    pad_lanes = C_HEAD_PAD - hd
    lane = np.arange(C_HEAD_PAD)
    is_rope = (lane >= C_NOPE) & (lane < hd)
    partner = np.where(is_rope, lane ^ (C_ROPE // 4), lane)

    def swapped(t):
        return jnp.where(is_rope, jnp.take(t, partner, axis=-1), 0.0)

    kr_cols = jnp.pad(w_down[:, C_Q_LORA + C_KV_LORA:], ((0, 0), (C_NOPE, pad_lanes)))
    wd = jnp.concatenate([w_down[:, :C_Q_LORA + C_KV_LORA], kr_cols, swapped(kr_cols)], axis=1)
    wuq = jnp.pad(w_uq.reshape(C_Q_LORA, C_HEADS, hd), ((0, 0), (0, 0), (0, pad_lanes)))
    wuq_swap = swapped(wuq)[:, :, C_NOPE:hd].reshape(C_Q_LORA, C_SWAP_W)
    wuq = jnp.concatenate([wuq.reshape(C_Q_LORA, C_HEADS * C_HEAD_PAD), wuq_swap], axis=1)
    qg = jnp.pad(q_gain, (0, pad_lanes))
    kg = jnp.pad(k_gain, (0, pad_lanes))
    row = lambda t: t.reshape(1, C_HEAD_PAD)
    bound = 1.02 * math.sqrt(hd) * LOG2E * jnp.max(jnp.abs(q_gain)) * jnp.max(jnp.abs(k_gain))
    shift_lane = lane == C_SHIFT_LANE
    q_one = row(jnp.asarray(shift_lane, F32))
    k_shift = row(jnp.where(shift_lane, -bound, 0.0))
    fixed_shift_ok = (bound <= C_SHIFT_MAX).astype(jnp.int32).reshape(1)
    return (wd.astype(BF), wuq.astype(BF), row(qg), row(swapped(qg)), row(kg), row(swapped(kg)),
            q_one, k_shift, fixed_shift_ok)


def _cproj(x, mod, wd, cq_gain, ckv_gain, wuq, wukv, q_one, k_shift, q_tab, q_tab_swap, k_tab, k_tab_swap):
    wq = C_HEADS * C_HEAD_PAD
    return pl.pallas_call(
        _cproj_kernel,
        grid=(N_TILES,),
        in_specs=[_tok_spec(D_MODEL), _MOD_SPEC, _const_spec((D_MODEL, C_DOWN_PAD)),
                  _const_spec((1, C_Q_LORA)), _const_spec((1, C_KV_LORA)),
                  _const_spec((C_Q_LORA, wq + C_SWAP_W)), _const_spec((C_KV_LORA, wq)),
                  _const_spec((1, C_HEAD_PAD)), _const_spec((1, C_HEAD_PAD)),
                  _ROPE_SPEC, _ROPE_SPEC, _ROPE_SPEC, _ROPE_SPEC],
        out_specs=[_tok_spec(wq), _tok_spec(wq), _tok_spec(wq), _tok_spec(C_KV_LORA), _tok_spec(LANES)],
        out_shape=[jax.ShapeDtypeStruct((N_TOK, wq), BF), jax.ShapeDtypeStruct((N_TOK, wq), BF),
                   jax.ShapeDtypeStruct((N_TOK, wq), BF), jax.ShapeDtypeStruct((N_TOK, C_KV_LORA), F32),
                   jax.ShapeDtypeStruct((N_TOK, LANES), F32)],
        compiler_params=_params("arbitrary"),
        name="mla_proj",
    )(x, mod, wd, cq_gain.reshape(1, C_Q_LORA), ckv_gain.reshape(1, C_KV_LORA), wuq, wukv, q_one, k_shift,
      q_tab, q_tab_swap, k_tab, k_tab_swap)


def _cctx(cache_ckv, cache_krope, wukv, kg, k_shift):
    n = DEC_BATCH * PAST_LEN
    wq = C_HEADS * C_HEAD_PAD
    kr = jnp.pad(cache_krope.reshape(n, C_ROPE), ((0, 0), (C_NOPE, C_HEAD_PAD - C_NOPE - C_ROPE)))
    return pl.pallas_call(
        _cctx_kernel,
        grid=(n // TM,),
        in_specs=[_tok_spec(C_KV_LORA), _tok_spec(LANES), _const_spec((C_KV_LORA, wq)),
                  _const_spec((1, C_HEAD_PAD)), _const_spec((1, C_HEAD_PAD))],
        out_specs=[_tok_spec(wq), _tok_spec(wq)],
        out_shape=[jax.ShapeDtypeStruct((n, wq), BF), jax.ShapeDtypeStruct((n, wq), BF)],
        compiler_params=_params("arbitrary"),
        name="mla_context_keys",
    )(cache_ckv.reshape(n, C_KV_LORA), kr, wukv, kg, k_shift)


def _mla_attend(ok_ref, q_ref, kv_refs, o_ref, pair=0):
    tq = q_ref.shape[0]
    cols = slice(pair * LANES, (pair + 1) * LANES)
    head_cols = [slice((2 * pair + e) * C_HEAD_PAD, (2 * pair + e + 1) * C_HEAD_PAD) for e in range(2)]

    def normalised(sum_acc, rows):
        lo = lax.broadcasted_iota(jnp.int32, (rows, LANES), 1) < C_VDIM
        r0 = pltpu.roll(sum_acc[0], C_VDIM, 1)
        r1 = pltpu.roll(sum_acc[1], C_VDIM, 1)
        return jnp.where(lo, r0 / sum_acc[0], sum_acc[1] / r1).astype(BF)

    chunks = []
    for k_ref, v_ref in kv_refs:
        nk = k_ref.shape[0]
        if nk < CK and chunks:
            chunks[-1].append((k_ref, v_ref, slice(0, nk)))
        else:
            ck = min(CK, nk)
            chunks += [[(k_ref, v_ref, slice(c * ck, (c + 1) * ck))] for c in range(nk // ck)]
    def fixed_shift():
        sum_acc = [None, None]
        for parts in chunks:
            for e, sl in enumerate(head_cols):
                kk = [k_ref[rows, sl] for k_ref, _, rows in parts]
                vv = [v_ref[rows, sl] for _, v_ref, rows in parts]
                p = jnp.exp2(_dot_t(q_ref[:, sl], kk[0] if len(kk) == 1 else jnp.concatenate(kk, axis=0)))
                pv = _dot(p.astype(BF), vv[0] if len(vv) == 1 else jnp.concatenate(vv, axis=0))
                sum_acc[e] = pv if sum_acc[e] is None else sum_acc[e] + pv
        o_ref[:, cols] = normalised(sum_acc, tq)

    def row_max_shift():
        blk = min(tq, LANES)

        def body(i, carry):
            rows = pl.ds(pl.multiple_of(i * blk, blk), blk)
            sum_acc = []
            for sl in head_cols:
                qh = q_ref[rows, sl]
                scores = [_dot_t(qh, k_ref[:, sl]) for k_ref, _ in kv_refs]
                m = functools.reduce(jnp.maximum, [jnp.max(sc, axis=-1, keepdims=True) for sc in scores])
                sum_acc.append(functools.reduce(jnp.add, [_dot(jnp.exp2(sc - m).astype(BF), v_ref[:, sl])
                                                          for sc, (_, v_ref) in zip(scores, kv_refs)]))
            o_ref[rows, cols] = normalised(sum_acc, blk)
            return carry

        lax.fori_loop(0, tq // blk, body, 0)

    lax.cond(ok_ref[0] != 0, fixed_shift, row_max_shift)


def _cattn_lat_kernel(ok_ref, q_ref, k_ref, v_ref, ck_ref, cv_ref, o_ref):
    _mla_attend(ok_ref, q_ref, ((k_ref, v_ref), (ck_ref, cv_ref)), o_ref)


def _cattn_ctx_kernel(ok_ref, q_ref, k_ref, v_ref, oin_ref, o_ref):
    del oin_ref
    for pair in range(C_HEADS // 2):
        _mla_attend(ok_ref, q_ref, ((k_ref, v_ref),), o_ref, pair)


def _cattn(fixed_shift_ok, q, k, v, ck, cv):
    pair_w = 2 * C_HEAD_PAD
    n_pairs = C_HEADS // 2
    nqt = DEC_SEQ // CQ
    smem = pl.BlockSpec(memory_space=pltpu.SMEM)
    o = pl.pallas_call(
        _cattn_lat_kernel,
        grid=(DEC_BATCH, n_pairs, nqt),
        in_specs=[smem, pl.BlockSpec((CQ, pair_w), lambda b, p, t: (b * nqt + t, p)),
                  pl.BlockSpec((DEC_SEQ, pair_w), lambda b, p, t: (b, p)),
                  pl.BlockSpec((DEC_SEQ, pair_w), lambda b, p, t: (b, p)),
                  pl.BlockSpec((PAST_LEN, pair_w), lambda b, p, t: (b, p)),
                  pl.BlockSpec((PAST_LEN, pair_w), lambda b, p, t: (b, p))],
        out_specs=pl.BlockSpec((CQ, LANES), lambda b, p, t: (b * nqt + t, p)),
        out_shape=jax.ShapeDtypeStruct((N_TOK, C_HEADS * C_VDIM), BF),
        compiler_params=_params("arbitrary", "arbitrary", "arbitrary"),
        name="mla_attn_latent",
    )(fixed_shift_ok, q, k, v, ck, cv)
    off = N_SAMPLE // SEQ
    return pl.pallas_call(
        _cattn_ctx_kernel,
        grid=(BATCH,),
        in_specs=[smem, pl.BlockSpec((SEQ, n_pairs * pair_w), lambda b: (off + b, 0)),
                  pl.BlockSpec((SEQ, n_pairs * pair_w), lambda b: (off + b, 0)),
                  pl.BlockSpec((SEQ, n_pairs * pair_w), lambda b: (off + b, 0)),
                  pl.BlockSpec(memory_space=pl.ANY)],
        out_specs=pl.BlockSpec((SEQ, n_pairs * LANES), lambda b: (off + b, 0)),
        out_shape=jax.ShapeDtypeStruct((N_TOK, C_HEADS * C_VDIM), BF),
        input_output_aliases={4: 0},
        compiler_params=_params("arbitrary"),
        name="mla_attn_context",
    )(fixed_shift_ok, q, k, v, o)


def kernel(x_prompt, x_sample, c, cache_win_k, cache_win_v, cache_mla_ckv, cache_mla_krope, c_ctx,
           ada_w, ada_b, ffn_w_in, ffn_w_out,
           gmlp_w_in, gmlp_v_gain, gmlp_w_s, gmlp_b_s, gmlp_w_out,
           win_w_qkv, win_q_gain, win_k_gain, win_sink, win_w_o,
           mla_w_down, mla_cq_gain, mla_ckv_gain, mla_w_uq, mla_w_ukv, mla_q_gain, mla_k_gain, mla_w_o):
    x = (x_sample.reshape(N_SAMPLE, D_MODEL), x_prompt.reshape(N_PROMPT, D_MODEL))
    cond = jnp.concatenate([c, c_ctx[None, :], jnp.zeros((N_COND - DEC_BATCH - 1, D_MODEL), F32)], axis=0)
    mods = _modulation(cond, ada_w, ada_b)
    w_in_b, shift_terms = _ffn_prep(mods, ffn_w_in)
    w_out_b = ffn_w_out.astype(BF)

    lane = np.arange(LANES)
    b_cos, b_sin = _rope_tables(B_HEAD_DIM, lane % B_HEAD_DIM)
    c_lane = np.where((lane >= C_NOPE) & (lane < C_NOPE + C_ROPE), lane - C_NOPE, -1)
    c_cos, c_sin = _rope_tables(C_ROPE, c_lane)

    nk = B_KV_HEADS * B_HEAD_DIM
    win_k, win_v, mla_ckv, mla_krope = [], [], [], []
    ia = ib = ic = 0
    for l in range(DEPTH):
        mod = mods[l]
        pre_proj = None
        x = _ffn(x, mod, shift_terms, w_in_b, w_out_b, l, 0, split_in=(l == 0))
        kind = l % N_MIXERS
        if kind == 0:
            x = _gmlp(x, mod, gmlp_w_in[ia], gmlp_v_gain[ia], gmlp_w_s[ia], gmlp_b_s[ia], gmlp_w_out[ia])
            ia += 1
        elif kind == 1:
            q, k, v = _bproj(x, mod, win_w_qkv[ib], win_q_gain[ib], win_k_gain[ib], b_cos, b_sin)
            x = _battn(x, mod, q, k, v,
                       cache_win_k[:, ib].reshape(DEC_BATCH, PAST_LEN, nk),
                       cache_win_v[:, ib].reshape(DEC_BATCH, PAST_LEN, nk),
                       win_sink[ib], win_w_o[ib].astype(BF), win_q_gain[ib], win_k_gain[ib])
            win_k.append(k[N_SAMPLE:].reshape(BATCH, SEQ, B_KV_HEADS, B_HEAD_DIM))
            win_v.append(v[N_SAMPLE:].reshape(BATCH, SEQ, B_KV_HEADS, B_HEAD_DIM))
            ib += 1
        else:
            wd, wuq, qg, qgs, kg, kgs, q_one, k_shift, fixed_shift_ok = _mla_weights(
                mla_w_down[ic], mla_w_uq[ic], mla_q_gain[ic], mla_k_gain[ic])
            wukv = mla_w_ukv[ic].astype(BF)
            q_scale = (C_NOPE + C_ROPE) ** -0.5 * LOG2E
            q, k, v, ckv, kr = _cproj(x, mod, wd, mla_cq_gain[ic], mla_ckv_gain[ic], wuq, wukv, q_one, k_shift,
                                      c_cos * (qg * q_scale), c_sin * (qgs * q_scale), c_cos * kg, c_sin * kgs)
            ck, cv = _cctx(cache_mla_ckv[:, ic], cache_mla_krope[:, ic], wukv, kg, k_shift)
            pre_proj = (_cattn(fixed_shift_ok, q, k, v, ck, cv), mla_w_o[ic].astype(BF))
            mla_ckv.append(ckv[N_SAMPLE:].reshape(BATCH, SEQ, C_KV_LORA))
            mla_krope.append(kr[N_SAMPLE:, C_NOPE:C_NOPE + C_ROPE].reshape(BATCH, SEQ, C_ROPE))
            ic += 1
        x = _ffn(x, mod, shift_terms, w_in_b, w_out_b, l, 1, split_out=(l == DEPTH - 1), pre_proj=pre_proj)
    y_sample, y_prompt = x
    return (y_prompt.reshape(BATCH, SEQ, D_MODEL), y_sample.reshape(DEC_BATCH, DEC_SEQ, D_MODEL),
            jnp.stack(win_k, axis=1), jnp.stack(win_v, axis=1),
            jnp.stack(mla_ckv, axis=1), jnp.stack(mla_krope, axis=1))
```

```python
import functools
import math

import jax
import jax.numpy as jnp
import numpy as np
from jax import lax
from jax.experimental import pallas as pl
from jax.experimental.pallas import tpu as pltpu

D_MODEL = 1024
BATCH = 16
SEQ = 256
DEPTH = 4
DEC_BATCH = 8
DEC_SEQ = 4096
PAST_LEN = 256
GRID_W = 64
N_MIXERS = 3
N_MOD = 9
D_FF = 2816
A_WIDTH = D_MODEL
A_GROUPS = 8
A_CHUNK = 128
B_HEADS = 16
B_KV_HEADS = 4
B_HEAD_DIM = 64
B_WINDOW = 128
C_HEADS = 16
C_NOPE = 64
C_ROPE = 32
C_VDIM = 64
C_Q_LORA = 512
C_KV_LORA = 256
ROPE_BASE = 10000.0
EPS = 1e-6
NEG_INF = -1e30

LANES = 128
N_SAMPLE = DEC_BATCH * DEC_SEQ
N_PROMPT = BATCH * SEQ
N_TOK = N_SAMPLE + N_PROMPT
N_COND = 16
TM = 512
FM = 1024
FF_CHUNKS = ((0, 1536), (1536, 1280))
N_TILES = N_TOK // TM
N_SAMPLE_TILES = N_SAMPLE // TM
TILES_PER_SEQ = DEC_SEQ // TM
MOD_TN = 1536
SHIFT_TN = 1408
BQ = 256
CQ = 2048
CK = 256
C_HEAD_PAD = 128
C_DOWN_PAD = C_Q_LORA + C_KV_LORA + 2 * LANES
VMEM_LIMIT_BYTES = 60000 * 1024

LOG2E = math.log2(math.e)

BF = jnp.bfloat16
F32 = jnp.float32


def _params(*sem):
    return pltpu.CompilerParams(dimension_semantics=sem, vmem_limit_bytes=VMEM_LIMIT_BYTES)


def _dot(a, b):
    return jnp.dot(a, b, preferred_element_type=F32)


def _dot_t(a, b):
    return lax.dot_general(a, b, (((1,), (1,)), ((), ())), preferred_element_type=F32)


def _rms(x):
    return x * lax.rsqrt(jnp.mean(x * x, axis=-1, keepdims=True) + EPS)


def _ada(x, mod_ref, k):
    shift = mod_ref[3 * k:3 * k + 1, :]
    scale = mod_ref[3 * k + 1:3 * k + 2, :]
    return _rms(x) * (1.0 + scale) + shift


def _const_spec(shape):
    nd = len(shape)
    return pl.BlockSpec(shape, lambda *_: (0,) * nd, pipeline_mode=pl.Buffered(1))


def _tok_spec(width, tm=TM):
    return pl.BlockSpec((tm, width), lambda i: (i, 0))


def _mod_spec(tm):
    return pl.BlockSpec((None, N_MOD, D_MODEL), lambda i: (jnp.minimum(i * tm // DEC_SEQ, DEC_BATCH), 0, 0))


_MOD_SPEC = _mod_spec(TM)


def _rope_tile(i):
    return jnp.where(i < N_SAMPLE_TILES, i % TILES_PER_SEQ, TILES_PER_SEQ)


_ROPE_SPEC = pl.BlockSpec((TM, LANES), lambda i: (_rope_tile(i), 0))


def _mod_kernel(c_ref, w_ref, b_ref, o_ref):
    a = jax.nn.silu(c_ref[...]).astype(BF)
    o_ref[...] = _dot(a, w_ref[...].astype(BF)) + b_ref[...]


def _modulation(cond, ada_w, ada_b):
    n_out = N_MOD * D_MODEL
    out = pl.pallas_call(
        _mod_kernel,
        grid=(DEPTH, n_out // MOD_TN),
        in_specs=[
            pl.BlockSpec((N_COND, D_MODEL), lambda l, j: (0, 0)),
            pl.BlockSpec((None, D_MODEL, MOD_TN), lambda l, j: (l, 0, j)),
            pl.BlockSpec((None, 1, MOD_TN), lambda l, j: (l, 0, j)),
        ],
        out_specs=pl.BlockSpec((None, N_COND, MOD_TN), lambda l, j: (l, 0, j)),
        out_shape=jax.ShapeDtypeStruct((DEPTH, N_COND, n_out), F32),
        compiler_params=_params("arbitrary", "arbitrary"),
        name="modulation",
    )(cond, ada_w, ada_b.reshape(DEPTH, 1, n_out))
    return out.reshape(DEPTH, N_COND, N_MOD, D_MODEL)


def _ffn_prep_kernel(s_ref, w_ref, wb_ref, o_ref):
    wb = w_ref[...].astype(BF)
    wb_ref[...] = wb
    o_ref[...] = _dot(s_ref[...].astype(BF), wb)


def _ffn_prep(mods, w_in):
    shifts = jnp.stack([mods[:, :, 0], mods[:, :, 6]], axis=1)
    w_spec = pl.BlockSpec((None, None, D_MODEL, SHIFT_TN), lambda l, h, j: (l, h, 0, j))
    w_b, out = pl.pallas_call(
        _ffn_prep_kernel,
        grid=(DEPTH, 2, 2 * D_FF // SHIFT_TN),
        in_specs=[pl.BlockSpec((None, None, N_COND, D_MODEL), lambda l, h, j: (l, h, 0, 0)), w_spec],
        out_specs=[w_spec, pl.BlockSpec((None, None, N_COND, SHIFT_TN), lambda l, h, j: (l, h, 0, j))],
        out_shape=[jax.ShapeDtypeStruct(w_in.shape, BF),
                   jax.ShapeDtypeStruct((DEPTH, 2, N_COND, 2 * D_FF), F32)],
        compiler_params=_params("arbitrary", "arbitrary", "arbitrary"),
        name="ffn_prep",
    )(shifts, w_in)
    return w_b, out.reshape(DEPTH, 2, N_COND, 1, 2 * D_FF)


def _ffn_kernel(*refs, k, split_in, split_out, pre_proj):
    n_x = 2 if split_in else 1
    n_in = n_x + (3 if pre_proj else 0)
    x_refs, (mod_ref, sh_ref, win_ref, wout_ref), o_refs = refs[:n_x], refs[n_in:n_in + 4], refs[n_in + 4:]
    is_sample = pl.program_id(0) < N_SAMPLE // FM
    x = jnp.where(is_sample, x_refs[0][...], x_refs[1][...]) if split_in else x_refs[0][...]
    if pre_proj:
        attn_s_ref, attn_p_ref, wo_ref = refs[n_x:n_in]
        attn = jnp.where(is_sample, attn_s_ref[...], attn_p_ref[...])
        x = x + mod_ref[5:6, :] * _dot(attn, wo_ref[...])
    xa = (x * (1.0 + mod_ref[3 * k + 1:3 * k + 2, :])).astype(BF)
    rinv = lax.rsqrt(jnp.mean(x * x, axis=-1, keepdims=True) + EPS)
    y = None
    for c0, cw in FF_CHUNKS:
        g = _dot(xa, win_ref[:, c0:c0 + cw]) * rinv + sh_ref[:, c0:c0 + cw]
        u = _dot(xa, win_ref[:, D_FF + c0:D_FF + c0 + cw]) * rinv + sh_ref[:, D_FF + c0:D_FF + c0 + cw]
        yc = _dot((jax.nn.silu(g) * u).astype(BF), wout_ref[c0:c0 + cw, :])
        y = yc if y is None else y + yc
    gate = mod_ref[3 * k + 2:3 * k + 3, :]
    out = x + (0.5 * gate) * y
    if split_out:
        o_refs[1][...] = out

        @pl.when(is_sample)
        def _():
            o_refs[0][...] = out
    else:
        o_refs[0][...] = out


_SAMPLE_SPEC = pl.BlockSpec((FM, D_MODEL), lambda i: (jnp.minimum(i, N_SAMPLE // FM - 1), 0))
_PROMPT_SPEC = pl.BlockSpec((FM, D_MODEL), lambda i: (jnp.maximum(i - N_SAMPLE // FM, 0), 0))


def _ffn(xs, mod, shift_terms, w_in, w_out, layer, half, split_in=False, split_out=False, pre_proj=None):
    def w_spec(rows, cols):
        return pl.BlockSpec((None, None, rows, cols), lambda i: (layer, half, 0, 0), pipeline_mode=pl.Buffered(1))

    sh_spec = pl.BlockSpec((None, None, None, 1, 2 * D_FF),
                           lambda i: (layer, half, jnp.minimum(i * FM // DEC_SEQ, DEC_BATCH), 0, 0))

    x_specs = [_SAMPLE_SPEC, _PROMPT_SPEC] if split_in else [_tok_spec(D_MODEL, FM)]
    operands = list(xs) if split_in else [xs]
    if pre_proj is not None:
        x_specs += [_SAMPLE_SPEC, _PROMPT_SPEC, _const_spec((D_MODEL, D_MODEL))]
        operands += list(pre_proj)
    if split_out:
        out_specs = [_SAMPLE_SPEC, _PROMPT_SPEC]
        out_shape = [jax.ShapeDtypeStruct((N_SAMPLE, D_MODEL), F32), jax.ShapeDtypeStruct((N_PROMPT, D_MODEL), F32)]
    else:
        out_specs = _tok_spec(D_MODEL, FM)
        out_shape = jax.ShapeDtypeStruct((N_TOK, D_MODEL), F32)
    return pl.pallas_call(
        functools.partial(_ffn_kernel, k=2 * half, split_in=split_in, split_out=split_out,
                          pre_proj=pre_proj is not None),
        grid=(N_TOK // FM,),
        in_specs=x_specs + [_mod_spec(FM), sh_spec, w_spec(D_MODEL, 2 * D_FF), w_spec(D_FF, D_MODEL)],
        out_specs=out_specs,
        out_shape=out_shape,
        compiler_params=_params("arbitrary"),
        name="ffn",
    )(*operands, mod, shift_terms, w_in, w_out)


def _gmlp_kernel(x_ref, mod_ref, win_ref, vg_ref, ws_ref, bs_ref, wout_ref, o_ref):
    x = x_ref[...]
    hb = _ada(x, mod_ref, 1).astype(BF)

    def gelu_proj(c0):
        pre = _dot(hb, win_ref[:, c0:c0 + A_WIDTH])
        return 0.5 * pre * (1.0 + lax.erf(pre * math.sqrt(0.5)))

    v = (_rms(gelu_proj(A_WIDTH)) * vg_ref[...]).astype(BF)
    u = gelu_proj(0)
    bias = bs_ref[...]
    rows = []
    for c in range(FM // A_CHUNK):
        cols = [_dot(ws_ref[g], v[c * A_CHUNK:(c + 1) * A_CHUNK, g * LANES:(g + 1) * LANES])
                for g in range(A_GROUPS)]
        rows.append(jnp.concatenate(cols, axis=1) + bias)
    sv = jnp.concatenate(rows, axis=0)
    y = _dot((u * sv).astype(BF), wout_ref[...])
    o_ref[...] = x + mod_ref[5:6, :] * y


def _gmlp(x, mod, w_in, v_gain, w_s, b_s, w_out):
    bias = jnp.repeat(b_s.T, A_WIDTH // A_GROUPS, axis=1)
    return pl.pallas_call(
        _gmlp_kernel,
        grid=(N_TOK // FM,),
        in_specs=[_tok_spec(D_MODEL, FM), _mod_spec(FM),
                  _const_spec((D_MODEL, 2 * A_WIDTH)), _const_spec((1, A_WIDTH)),
                  _const_spec((A_GROUPS, A_CHUNK, A_CHUNK)), _const_spec((A_CHUNK, A_WIDTH)),
                  _const_spec((A_WIDTH, D_MODEL))],
        out_specs=_tok_spec(D_MODEL, FM),
        out_shape=jax.ShapeDtypeStruct((N_TOK, D_MODEL), F32),
        compiler_params=_params("arbitrary"),
        name="gmlp",
    )(x, mod, w_in.astype(BF), v_gain.reshape(1, A_WIDTH), w_s.astype(BF), bias, w_out.astype(BF))


def _swap_pairs(y, step):
    lane = lax.broadcasted_iota(jnp.int32, y.shape, 1)
    return jnp.where((lane & step) != 0, pltpu.roll(y, step, 1), pltpu.roll(y, LANES - step, 1))


def _rope_tables(rot_dim, lane_of_dim):
    quarter = rot_dim // 4
    inv = np.float32(ROPE_BASE) ** (-np.arange(quarter, dtype=np.float32) / np.float32(quarter))
    t = np.arange(DEC_SEQ)
    row = (t // GRID_W).astype(np.float32)
    col = (t % GRID_W).astype(np.float32)
    ang = np.stack([row[:, None] * inv, col[:, None] * inv], axis=1)
    cos, sin = np.cos(ang), np.sin(ang)
    d = np.asarray(lane_of_dim)
    dd = np.maximum(d, 0)
    axis, member, freq = dd // (2 * quarter), (dd % (2 * quarter)) // quarter, dd % quarter
    rot = (d >= 0)[None, :]
    c_tab = np.where(rot, cos[:, axis, freq], 1.0)
    s_tab = np.where(rot, np.where(member == 0, -1.0, 1.0)[None, :] * sin[:, axis, freq], 0.0)
    ident_c = np.ones((TM, LANES), np.float32)
    ident_s = np.zeros((TM, LANES), np.float32)
    return (np.concatenate([c_tab, ident_c], axis=0).astype(np.float32),
            np.concatenate([s_tab, ident_s], axis=0).astype(np.float32))


def _bproj_kernel(x_ref, mod_ref, w_ref, ones_ref, qc_ref, qs_ref, kc_ref, ks_ref, q_ref, k_ref, v_ref):
    hb = _ada(x_ref[...], mod_ref, 1).astype(BF)
    qkv = _dot(hb, w_ref[...])
    head_ones = ones_ref[...]

    def norm_rope(t4, tab, tab_swap):
        sq = t4 * t4
        hi = sq.astype(BF)
        lo = (sq - hi.astype(F32)).astype(BF)
        r = lax.rsqrt((_dot(hi, head_ones) + _dot(lo, head_ones)) / B_HEAD_DIM + EPS)
        halves = []
        for j in range(2):
            sl = slice(j * LANES, (j + 1) * LANES)
            t = t4[:, sl]
            halves.append(r[:, sl] * (t * tab + _swap_pairs(t, B_HEAD_DIM // 4) * tab_swap))
        return halves

    nq = B_HEADS * B_HEAD_DIM
    nk = B_KV_HEADS * B_HEAD_DIM
    q_tab, q_tab_swap = qc_ref[...], qs_ref[...]
    for j in range(nq // (2 * LANES)):
        halves = norm_rope(qkv[:, 2 * j * LANES:(2 * j + 2) * LANES], q_tab, q_tab_swap)
        q_ref[:, 2 * j * LANES:(2 * j + 1) * LANES] = halves[0].astype(BF)
        q_ref[:, (2 * j + 1) * LANES:(2 * j + 2) * LANES] = halves[1].astype(BF)
    halves = norm_rope(qkv[:, nq:nq + nk], kc_ref[...], ks_ref[...])
    k_ref[:, :LANES] = halves[0]
    k_ref[:, LANES:] = halves[1]
    v_ref[...] = qkv[:, nq + nk:]


def _bproj(x, mod, w_qkv, q_gain, k_gain, cos, sin):
    nq = B_HEADS * B_HEAD_DIM
    nk = B_KV_HEADS * B_HEAD_DIM
    q_scale = B_HEAD_DIM ** -0.5 * LOG2E
    lane = np.arange(LANES)
    qg = jnp.tile(q_gain, LANES // B_HEAD_DIM)
    kg = jnp.tile(k_gain, LANES // B_HEAD_DIM)
    partner = lane ^ (B_HEAD_DIM // 4)
    head = np.arange(nk) // B_HEAD_DIM
    head_ones = jnp.asarray(head[:, None] == head[None, :], BF)
    return pl.pallas_call(
        _bproj_kernel,
        grid=(N_TILES,),
        in_specs=[_tok_spec(D_MODEL), _MOD_SPEC, _const_spec((D_MODEL, nq + 2 * nk)), _const_spec((nk, nk)),
                  _ROPE_SPEC, _ROPE_SPEC, _ROPE_SPEC, _ROPE_SPEC],
        out_specs=[_tok_spec(nq), _tok_spec(nk), _tok_spec(nk)],
        out_shape=[jax.ShapeDtypeStruct((N_TOK, nq), BF),
                   jax.ShapeDtypeStruct((N_TOK, nk), F32),
                   jax.ShapeDtypeStruct((N_TOK, nk), F32)],
        compiler_params=_params("arbitrary"),
        name="gqa_proj",
    )(x, mod, w_qkv.astype(BF), head_ones,
      cos * (qg * q_scale), sin * (qg[partner] * q_scale), cos * kg, sin * kg[partner])


def _gqa_attend(q, kcat, vcat, bias, sink_ref, shift):
    tq = q.shape[0]
    nk = kcat.shape[0]
    lo = lax.broadcasted_iota(jnp.int32, (nk, LANES), 1) < B_HEAD_DIM
    lo_q = lax.broadcasted_iota(jnp.int32, (2 * tq, LANES), 1) < B_HEAD_DIM
    first = lax.broadcasted_iota(jnp.int32, (2 * tq, 1), 0) < tq
    if bias is not None:
        bias = jnp.concatenate([bias, bias], axis=0)
    outs = []
    for g in range(B_KV_HEADS):
        sl = slice((g // 2) * LANES, (g // 2 + 1) * LANES)
        own = lo if g % 2 == 0 else jnp.logical_not(lo)
        k_own = jnp.where(own, kcat[:, sl], 0.0)
        k_swp = pltpu.roll(k_own, B_HEAD_DIM, 1)
        v_own = jnp.where(own, vcat[:, sl], 1.0)
        v_swp = pltpu.roll(v_own, B_HEAD_DIM, 1)
        k_half = (k_own, k_swp) if g % 2 == 0 else (k_swp, k_own)
        v_half = (v_own, v_swp) if g % 2 == 0 else (v_swp, v_own)
        qg = jnp.concatenate([q[:, (2 * g) * LANES:(2 * g + 1) * LANES],
                              q[:, (2 * g + 1) * LANES:(2 * g + 2) * LANES]], axis=0)
        s_all = _dot_t(qg, jnp.concatenate(k_half, axis=0).astype(BF))
        o_half = []
        for e in range(2):
            s = s_all[:, e * nk:(e + 1) * nk]
            if bias is not None:
                nb = bias.shape[1]
                s = jnp.concatenate([s[:, :nb] + bias, s[:, nb:]], axis=1)
            sk = jnp.where(first, sink_ref[4 * g + e], sink_ref[4 * g + 2 + e]) * LOG2E
            m = jnp.maximum(sk, jnp.max(s, axis=-1, keepdims=True) if shift is None else shift)
            ov = _dot(jnp.exp2(s - m).astype(BF), v_half[e].astype(BF))
            o_half.append(ov / (pltpu.roll(ov, B_HEAD_DIM, 1) + jnp.exp2(sk - m)))
        o_g = jnp.where(lo_q, o_half[0], o_half[1])
        outs += [o_g[:tq], o_g[tq:]]
    return jnp.concatenate(outs, axis=1)


def _with_softmax_shift(ok_ref, shift_ref, body):
    lax.cond(ok_ref[0] != 0, lambda: body(shift_ref[0]), lambda: body(None))


def _battn_lat_kernel(ok_ref, shift_ref, x_ref, mod_ref, q_ref, kp_ref, kc_ref, kn_ref, vp_ref, vc_ref, vn_ref,
                      ck_ref, cv_ref, sink_ref, wo_ref, o_ref):
    def body(shift):
        j = pl.program_id(1)
        kcat = jnp.concatenate([kp_ref[...], kc_ref[...], kn_ref[...], ck_ref[...]], axis=0)
        vcat = jnp.concatenate([vp_ref[...], vc_ref[...], vn_ref[...], cv_ref[...]], axis=0)
        n_lat = BQ + 2 * B_WINDOW
        qi = lax.broadcasted_iota(jnp.int32, (BQ, n_lat), 0)
        pk = lax.broadcasted_iota(jnp.int32, (BQ, n_lat), 1)
        kpos = j * BQ + pk - B_WINDOW
        valid = (jnp.abs(pk - B_WINDOW - qi) <= B_WINDOW) & (kpos >= 0) & (kpos < DEC_SEQ)
        bias = jnp.where(valid, 0.0, NEG_INF)
        o = _gqa_attend(q_ref[...], kcat, vcat, bias, sink_ref, shift)
        y = _dot(o.astype(BF), wo_ref[...])
        o_ref[...] = x_ref[...] + mod_ref[5:6, :] * y

    _with_softmax_shift(ok_ref, shift_ref, body)


def _battn_ctx_kernel(ok_ref, shift_ref, x_ref, mod_ref, q_ref, k_ref, v_ref, sink_ref, wo_ref, o_ref):
    def body(shift):
        o = _gqa_attend(q_ref[...], k_ref[...], v_ref[...], None, sink_ref, shift)
        y = _dot(o.astype(BF), wo_ref[...])
        o_ref[...] = x_ref[...] + mod_ref[5:6, :] * y

    _with_softmax_shift(ok_ref, shift_ref, body)


def _battn(x, mod, q, k, v, cache_k, cache_v, sink, w_o, q_gain, k_gain):
    nq = B_HEADS * B_HEAD_DIM
    nk = B_KV_HEADS * B_HEAD_DIM
    cache_norm = jnp.sqrt(jnp.max(jnp.sum(jnp.square(cache_k.reshape(-1, B_HEAD_DIM)), axis=-1)))
    k_norm = jnp.maximum(math.sqrt(B_HEAD_DIM) * jnp.max(jnp.abs(k_gain)), cache_norm)
    bound = 1.02 * LOG2E * jnp.max(jnp.abs(q_gain)) * k_norm
    shift = bound.reshape(1)
    fixed_shift_ok = (bound <= C_SHIFT_MAX).astype(jnp.int32).reshape(1)
    nb = DEC_SEQ // BQ
    nw = DEC_SEQ // B_WINDOW
    per = BQ // B_WINDOW
    smem = pl.BlockSpec(memory_space=pltpu.SMEM)
    cur_spec = pl.BlockSpec((BQ, nk), lambda b, j: (b * nb + j, 0))
    prev_spec = pl.BlockSpec((B_WINDOW, nk), lambda b, j: (b * nw + jnp.maximum(per * j - 1, 0), 0))
    next_spec = pl.BlockSpec((B_WINDOW, nk), lambda b, j: (b * nw + jnp.minimum(per * j + per, nw - 1), 0))

    x = pl.pallas_call(
        _battn_lat_kernel,
        grid=(DEC_BATCH, nb),
        in_specs=[smem, smem, pl.BlockSpec((BQ, D_MODEL), lambda b, j: (b * nb + j, 0)),
                  pl.BlockSpec((None, N_MOD, D_MODEL), lambda b, j: (b, 0, 0)),
                  pl.BlockSpec((BQ, nq), lambda b, j: (b * nb + j, 0)),
                  prev_spec, cur_spec, next_spec, prev_spec, cur_spec, next_spec,
                  pl.BlockSpec((None, PAST_LEN, nk), lambda b, j: (b, 0, 0)),
                  pl.BlockSpec((None, PAST_LEN, nk), lambda b, j: (b, 0, 0)),
                  smem, _const_spec((nq, D_MODEL))],
        out_specs=pl.BlockSpec((BQ, D_MODEL), lambda b, j: (b * nb + j, 0)),
        out_shape=jax.ShapeDtypeStruct((N_TOK, D_MODEL), F32),
        input_output_aliases={2: 0},
        compiler_params=_params("arbitrary", "arbitrary"),
        name="gqa_attn_latent",
    )(fixed_shift_ok, shift, x, mod, q, k, k, k, v, v, v, cache_k, cache_v, sink, w_o)
    off = N_SAMPLE // SEQ
    return pl.pallas_call(
        _battn_ctx_kernel,
        grid=(BATCH,),
        in_specs=[smem, smem, pl.BlockSpec((SEQ, D_MODEL), lambda b: (off + b, 0)),
                  pl.BlockSpec((None, N_MOD, D_MODEL), lambda b: (DEC_BATCH, 0, 0)),
                  pl.BlockSpec((SEQ, nq), lambda b: (off + b, 0)),
                  pl.BlockSpec((SEQ, nk), lambda b: (off + b, 0)),
                  pl.BlockSpec((SEQ, nk), lambda b: (off + b, 0)),
                  smem, _const_spec((nq, D_MODEL))],
        out_specs=pl.BlockSpec((SEQ, D_MODEL), lambda b: (off + b, 0)),
        out_shape=jax.ShapeDtypeStruct((N_TOK, D_MODEL), F32),
        input_output_aliases={2: 0},
        compiler_params=_params("arbitrary"),
        name="gqa_attn_context",
    )(fixed_shift_ok, shift, x, mod, q, k, v, sink, w_o)


C_SWAP_W = C_HEADS * C_ROPE
C_SHIFT_LANE = C_NOPE + C_ROPE
C_SHIFT_MAX = 50.0


def _mla_head_norm_rope(t, t_swap, tab, tab_swap):
    r = lax.rsqrt(jnp.sum(t * t, axis=-1, keepdims=True) / (C_NOPE + C_ROPE) + EPS)
    if t_swap is None:
        return t * r * tab
    return r * (t * tab + t_swap * tab_swap)


def _mla_keys_values(c_kv_b, k_rope, k_rope_swap, wukv_ref, tab, tab_swap, k_shift, k_ref, v_ref):
    kv = _dot(c_kv_b, wukv_ref[...])
    lower = (lax.broadcasted_iota(jnp.int32, kv.shape, 1) & C_NOPE) == 0
    v_ref[...] = jnp.where(lower, 1.0, kv).astype(BF)
    k_nope = jnp.where(lower, kv, 0.0)
    for h in range(C_HEADS):
        sl = slice(h * C_HEAD_PAD, (h + 1) * C_HEAD_PAD)
        kh = _mla_head_norm_rope(k_nope[:, sl] + k_rope, k_rope_swap, tab, tab_swap)
        k_ref[:, sl] = (kh + k_shift).astype(BF)


def _cproj_kernel(x_ref, mod_ref, wd_ref, cqg_ref, ckvg_ref, wuq_ref, wukv_ref, qone_ref, kshift_ref,
                  qc_ref, qs_ref, kc_ref, ks_ref, q_ref, k_ref, v_ref, ckv_ref, kr_ref):
    hb = _ada(x_ref[...], mod_ref, 1).astype(BF)
    d = _dot(hb, wd_ref[...])
    c_q = _rms(d[:, :C_Q_LORA]) * cqg_ref[...]
    c_kv = _rms(d[:, C_Q_LORA:C_Q_LORA + C_KV_LORA]) * ckvg_ref[...]
    k_rope = d[:, C_Q_LORA + C_KV_LORA:C_Q_LORA + C_KV_LORA + LANES]
    k_rope_swap = d[:, C_Q_LORA + C_KV_LORA + LANES:]
    ckv_ref[...] = c_kv
    kr_ref[...] = k_rope
    q2 = _dot(c_q.astype(BF), wuq_ref[...])
    wq = C_HEADS * C_HEAD_PAD
    q_tab, q_tab_swap = qc_ref[...], qs_ref[...]
    per_tile = LANES // C_ROPE
    for h in range(C_HEADS):
        sl = slice(h * C_HEAD_PAD, (h + 1) * C_HEAD_PAD)
        t_swap = q2[:, wq + (h // per_tile) * LANES:wq + (h // per_tile + 1) * LANES]
        shift = (C_NOPE - C_ROPE * (h % per_tile)) % LANES
        if shift:
            t_swap = pltpu.roll(t_swap, shift, 1)
        q_ref[:, sl] = (_mla_head_norm_rope(q2[:, sl], t_swap, q_tab, q_tab_swap) + qone_ref[...]).astype(BF)
    _mla_keys_values(c_kv.astype(BF), k_rope, k_rope_swap, wukv_ref, kc_ref[...], ks_ref[...], kshift_ref[...],
                     k_ref, v_ref)


def _cctx_kernel(ckv_ref, kr_ref, wukv_ref, kg_ref, kshift_ref, k_ref, v_ref):
    _mla_keys_values(ckv_ref[...].astype(BF), kr_ref[...], None, wukv_ref, kg_ref[...], None, kshift_ref[...],
                     k_ref, v_ref)


def _mla_weights(w_down, w_uq, q_gain, k_gain):
    hd = C_NOPE + C_ROPE
    pad_lanes = C_HEAD_PAD - hd
    lane = np.arange(C_HEAD_PAD)
    is_rope = (lane >= C_NOPE) & (lane < hd)
    partner = np.where(is_rope, lane ^ (C_ROPE // 4), lane)

    def swapped(t):
        return jnp.where(is_rope, jnp.take(t, partner, axis=-1), 0.0)

    kr_cols = jnp.pad(w_down[:, C_Q_LORA + C_KV_LORA:], ((0, 0), (C_NOPE, pad_lanes)))
    wd = jnp.concatenate([w_down[:, :C_Q_LORA + C_KV_LORA], kr_cols, swapped(kr_cols)], axis=1)
    wuq = jnp.pad(w_uq.reshape(C_Q_LORA, C_HEADS, hd), ((0, 0), (0, 0), (0, pad_lanes)))
    wuq_swap = swapped(wuq)[:, :, C_NOPE:hd].reshape(C_Q_LORA, C_SWAP_W)
    wuq = jnp.concatenate([wuq.reshape(C_Q_LORA, C_HEADS * C_HEAD_PAD), wuq_swap], axis=1)
    qg = jnp.pad(q_gain, (0, pad_lanes))
    kg = jnp.pad(k_gain, (0, pad_lanes))
    row = lambda t: t.reshape(1, C_HEAD_PAD)
    bound = 1.02 * math.sqrt(hd) * LOG2E * jnp.max(jnp.abs(q_gain)) * jnp.max(jnp.abs(k_gain))
    shift_lane = lane == C_SHIFT_LANE
    q_one = row(jnp.asarray(shift_lane, F32))
    k_shift = row(jnp.where(shift_lane, -bound, 0.0))
    fixed_shift_ok = (bound <= C_SHIFT_MAX).astype(jnp.int32).reshape(1)
    return (wd.astype(BF), wuq.astype(BF), row(qg), row(swapped(qg)), row(kg), row(swapped(kg)),
            q_one, k_shift, fixed_shift_ok)


def _cproj(x, mod, wd, cq_gain, ckv_gain, wuq, wukv, q_one, k_shift, q_tab, q_tab_swap, k_tab, k_tab_swap):
    wq = C_HEADS * C_HEAD_PAD
    return pl.pallas_call(
        _cproj_kernel,
        grid=(N_TILES,),
        in_specs=[_tok_spec(D_MODEL), _MOD_SPEC, _const_spec((D_MODEL, C_DOWN_PAD)),
                  _const_spec((1, C_Q_LORA)), _const_spec((1, C_KV_LORA)),
                  _const_spec((C_Q_LORA, wq + C_SWAP_W)), _const_spec((C_KV_LORA, wq)),
                  _const_spec((1, C_HEAD_PAD)), _const_spec((1, C_HEAD_PAD)),
                  _ROPE_SPEC, _ROPE_SPEC, _ROPE_SPEC, _ROPE_SPEC],
        out_specs=[_tok_spec(wq), _tok_spec(wq), _tok_spec(wq), _tok_spec(C_KV_LORA), _tok_spec(LANES)],
        out_shape=[jax.ShapeDtypeStruct((N_TOK, wq), BF), jax.ShapeDtypeStruct((N_TOK, wq), BF),
                   jax.ShapeDtypeStruct((N_TOK, wq), BF), jax.ShapeDtypeStruct((N_TOK, C_KV_LORA), F32),
                   jax.ShapeDtypeStruct((N_TOK, LANES), F32)],
        compiler_params=_params("arbitrary"),
        name="mla_proj",
    )(x, mod, wd, cq_gain.reshape(1, C_Q_LORA), ckv_gain.reshape(1, C_KV_LORA), wuq, wukv, q_one, k_shift,
      q_tab, q_tab_swap, k_tab, k_tab_swap)


def _cctx(cache_ckv, cache_krope, wukv, kg, k_shift):
    n = DEC_BATCH * PAST_LEN
    wq = C_HEADS * C_HEAD_PAD
    kr = jnp.pad(cache_krope.reshape(n, C_ROPE), ((0, 0), (C_NOPE, C_HEAD_PAD - C_NOPE - C_ROPE)))
    return pl.pallas_call(
        _cctx_kernel,
        grid=(n // TM,),
        in_specs=[_tok_spec(C_KV_LORA), _tok_spec(LANES), _const_spec((C_KV_LORA, wq)),
                  _const_spec((1, C_HEAD_PAD)), _const_spec((1, C_HEAD_PAD))],
        out_specs=[_tok_spec(wq), _tok_spec(wq)],
        out_shape=[jax.ShapeDtypeStruct((n, wq), BF), jax.ShapeDtypeStruct((n, wq), BF)],
        compiler_params=_params("arbitrary"),
        name="mla_context_keys",
    )(cache_ckv.reshape(n, C_KV_LORA), kr, wukv, kg, k_shift)


def _mla_attend(ok_ref, q_ref, kv_refs, o_ref, pair=0):
    tq = q_ref.shape[0]
    cols = slice(pair * LANES, (pair + 1) * LANES)
    head_cols = [slice((2 * pair + e) * C_HEAD_PAD, (2 * pair + e + 1) * C_HEAD_PAD) for e in range(2)]

    def normalised(sum_acc, rows):
        lo = lax.broadcasted_iota(jnp.int32, (rows, LANES), 1) < C_VDIM
        r0 = pltpu.roll(sum_acc[0], C_VDIM, 1)
        r1 = pltpu.roll(sum_acc[1], C_VDIM, 1)
        return jnp.where(lo, r0 / sum_acc[0], sum_acc[1] / r1).astype(BF)

    chunks = []
    for k_ref, v_ref in kv_refs:
        nk = k_ref.shape[0]
        if nk < CK and chunks:
            chunks[-1].append((k_ref, v_ref, slice(0, nk)))
        else:
            ck = min(CK, nk)
            chunks += [[(k_ref, v_ref, slice(c * ck, (c + 1) * ck))] for c in range(nk // ck)]
    def fixed_shift():
        sum_acc = [None, None]
        for parts in chunks:
            for e, sl in enumerate(head_cols):
                kk = [k_ref[rows, sl] for k_ref, _, rows in parts]
                vv = [v_ref[rows, sl] for _, v_ref, rows in parts]
                p = jnp.exp2(_dot_t(q_ref[:, sl], kk[0] if len(kk) == 1 else jnp.concatenate(kk, axis=0)))
                pv = _dot(p.astype(BF), vv[0] if len(vv) == 1 else jnp.concatenate(vv, axis=0))
                sum_acc[e] = pv if sum_acc[e] is None else sum_acc[e] + pv
        o_ref[:, cols] = normalised(sum_acc, tq)

    def row_max_shift():
        blk = min(tq, LANES)

        def body(i, carry):
            rows = pl.ds(pl.multiple_of(i * blk, blk), blk)
            sum_acc = []
            for sl in head_cols:
                qh = q_ref[rows, sl]
                scores = [_dot_t(qh, k_ref[:, sl]) for k_ref, _ in kv_refs]
                m = functools.reduce(jnp.maximum, [jnp.max(sc, axis=-1, keepdims=True) for sc in scores])
                sum_acc.append(functools.reduce(jnp.add, [_dot(jnp.exp2(sc - m).astype(BF), v_ref[:, sl])
                                                          for sc, (_, v_ref) in zip(scores, kv_refs)]))
            o_ref[rows, cols] = normalised(sum_acc, blk)
            return carry

        lax.fori_loop(0, tq // blk, body, 0)

    lax.cond(ok_ref[0] != 0, fixed_shift, row_max_shift)


def _cattn_lat_kernel(ok_ref, q_ref, k_ref, v_ref, ck_ref, cv_ref, o_ref):
    _mla_attend(ok_ref, q_ref, ((k_ref, v_ref), (ck_ref, cv_ref)), o_ref)


def _cattn_ctx_kernel(ok_ref, q_ref, k_ref, v_ref, o_ref):
    for pair in range(C_HEADS // 2):
        _mla_attend(ok_ref, q_ref, ((k_ref, v_ref),), o_ref, pair)


def _cattn(fixed_shift_ok, q, k, v, ck, cv):
    pair_w = 2 * C_HEAD_PAD
    n_pairs = C_HEADS // 2
    nqt = DEC_SEQ // CQ
    smem = pl.BlockSpec(memory_space=pltpu.SMEM)
    o_sample = pl.pallas_call(
        _cattn_lat_kernel,
        grid=(DEC_BATCH, n_pairs, nqt),
        in_specs=[smem, pl.BlockSpec((CQ, pair_w), lambda b, p, t: (b * nqt + t, p)),
                  pl.BlockSpec((DEC_SEQ, pair_w), lambda b, p, t: (b, p)),
                  pl.BlockSpec((DEC_SEQ, pair_w), lambda b, p, t: (b, p)),
                  pl.BlockSpec((PAST_LEN, pair_w), lambda b, p, t: (b, p)),
                  pl.BlockSpec((PAST_LEN, pair_w), lambda b, p, t: (b, p))],
        out_specs=pl.BlockSpec((CQ, LANES), lambda b, p, t: (b * nqt + t, p)),
        out_shape=jax.ShapeDtypeStruct((N_SAMPLE, C_HEADS * C_VDIM), BF),
        compiler_params=_params("arbitrary", "arbitrary", "arbitrary"),
        name="mla_attn_latent",
    )(fixed_shift_ok, q, k, v, ck, cv)
    off = N_SAMPLE // SEQ
    o_prompt = pl.pallas_call(
        _cattn_ctx_kernel,
        grid=(BATCH,),
        in_specs=[smem, pl.BlockSpec((SEQ, n_pairs * pair_w), lambda b: (off + b, 0)),
                  pl.BlockSpec((SEQ, n_pairs * pair_w), lambda b: (off + b, 0)),
                  pl.BlockSpec((SEQ, n_pairs * pair_w), lambda b: (off + b, 0))],
        out_specs=pl.BlockSpec((SEQ, n_pairs * LANES), lambda b: (b, 0)),
        out_shape=jax.ShapeDtypeStruct((N_PROMPT, C_HEADS * C_VDIM), BF),
        compiler_params=_params("arbitrary"),
        name="mla_attn_context",
    )(fixed_shift_ok, q, k, v)
    return o_sample, o_prompt


def kernel(x_prompt, x_sample, c, cache_win_k, cache_win_v, cache_mla_ckv, cache_mla_krope, c_ctx,
           ada_w, ada_b, ffn_w_in, ffn_w_out,
           gmlp_w_in, gmlp_v_gain, gmlp_w_s, gmlp_b_s, gmlp_w_out,
           win_w_qkv, win_q_gain, win_k_gain, win_sink, win_w_o,
           mla_w_down, mla_cq_gain, mla_ckv_gain, mla_w_uq, mla_w_ukv, mla_q_gain, mla_k_gain, mla_w_o):
    x = (x_sample.reshape(N_SAMPLE, D_MODEL), x_prompt.reshape(N_PROMPT, D_MODEL))
    cond = jnp.concatenate([c, c_ctx[None, :], jnp.zeros((N_COND - DEC_BATCH - 1, D_MODEL), F32)], axis=0)
    mods = _modulation(cond, ada_w, ada_b)
    w_in_b, shift_terms = _ffn_prep(mods, ffn_w_in)
    w_out_b = ffn_w_out.astype(BF)

    lane = np.arange(LANES)
    b_cos, b_sin = _rope_tables(B_HEAD_DIM, lane % B_HEAD_DIM)
    c_lane = np.where((lane >= C_NOPE) & (lane < C_NOPE + C_ROPE), lane - C_NOPE, -1)
    c_cos, c_sin = _rope_tables(C_ROPE, c_lane)

    nk = B_KV_HEADS * B_HEAD_DIM
    win_k, win_v, mla_ckv, mla_krope = [], [], [], []
    ia = ib = ic = 0
    for l in range(DEPTH):
        mod = mods[l]
        pre_proj = None
        x = _ffn(x, mod, shift_terms, w_in_b, w_out_b, l, 0, split_in=(l == 0))
        kind = l % N_MIXERS
        if kind == 0:
            x = _gmlp(x, mod, gmlp_w_in[ia], gmlp_v_gain[ia], gmlp_w_s[ia], gmlp_b_s[ia], gmlp_w_out[ia])
            ia += 1
        elif kind == 1:
            q, k, v = _bproj(x, mod, win_w_qkv[ib], win_q_gain[ib], win_k_gain[ib], b_cos, b_sin)
            x = _battn(x, mod, q, k, v,
                       cache_win_k[:, ib].reshape(DEC_BATCH, PAST_LEN, nk),
                       cache_win_v[:, ib].reshape(DEC_BATCH, PAST_LEN, nk),
                       win_sink[ib], win_w_o[ib].astype(BF), win_q_gain[ib], win_k_gain[ib])
            win_k.append(k[N_SAMPLE:].reshape(BATCH, SEQ, B_KV_HEADS, B_HEAD_DIM))
            win_v.append(v[N_SAMPLE:].reshape(BATCH, SEQ, B_KV_HEADS, B_HEAD_DIM))
            ib += 1
        else:
            wd, wuq, qg, qgs, kg, kgs, q_one, k_shift, fixed_shift_ok = _mla_weights(
                mla_w_down[ic], mla_w_uq[ic], mla_q_gain[ic], mla_k_gain[ic])
            wukv = mla_w_ukv[ic].astype(BF)
            q_scale = (C_NOPE + C_ROPE) ** -0.5 * LOG2E
            q, k, v, ckv, kr = _cproj(x, mod, wd, mla_cq_gain[ic], mla_ckv_gain[ic], wuq, wukv, q_one, k_shift,
                                      c_cos * (qg * q_scale), c_sin * (qgs * q_scale), c_cos * kg, c_sin * kgs)
            ck, cv = _cctx(cache_mla_ckv[:, ic], cache_mla_krope[:, ic], wukv, kg, k_shift)
            pre_proj = (*_cattn(fixed_shift_ok, q, k, v, ck, cv), mla_w_o[ic].astype(BF))
            mla_ckv.append(ckv[N_SAMPLE:].reshape(BATCH, SEQ, C_KV_LORA))
            mla_krope.append(kr[N_SAMPLE:, C_NOPE:C_NOPE + C_ROPE].reshape(BATCH, SEQ, C_ROPE))
            ic += 1
        x = _ffn(x, mod, shift_terms, w_in_b, w_out_b, l, 1, split_out=(l == DEPTH - 1), pre_proj=pre_proj)
    y_sample, y_prompt = x
    return (y_prompt.reshape(BATCH, SEQ, D_MODEL), y_sample.reshape(DEC_BATCH, DEC_SEQ, D_MODEL),
            jnp.stack(win_k, axis=1), jnp.stack(win_v, axis=1),
            jnp.stack(mla_ckv, axis=1), jnp.stack(mla_krope, axis=1))
```

```python
import functools
import math

import jax
import jax.numpy as jnp
import numpy as np
from jax import lax
from jax.experimental import pallas as pl
from jax.experimental.pallas import tpu as pltpu

D_MODEL = 1024
BATCH = 16
SEQ = 256
DEPTH = 4
DEC_BATCH = 8
DEC_SEQ = 4096
PAST_LEN = 256
GRID_W = 64
N_MIXERS = 3
N_MOD = 9
D_FF = 2816
A_WIDTH = D_MODEL
A_GROUPS = 8
A_CHUNK = 128
B_HEADS = 16
B_KV_HEADS = 4
B_HEAD_DIM = 64
B_WINDOW = 128
C_HEADS = 16
C_NOPE = 64
C_ROPE = 32
C_VDIM = 64
C_Q_LORA = 512
C_KV_LORA = 256
ROPE_BASE = 10000.0
EPS = 1e-6
NEG_INF = -1e30

LANES = 128
N_SAMPLE = DEC_BATCH * DEC_SEQ
N_PROMPT = BATCH * SEQ
N_TOK = N_SAMPLE + N_PROMPT
N_COND = 16
TM = 512
FM = 1024
FF_CHUNKS = ((0, 1536), (1536, 1280))
N_TILES = N_TOK // TM
N_SAMPLE_TILES = N_SAMPLE // TM
TILES_PER_SEQ = DEC_SEQ // TM
MOD_TN = 1536
SHIFT_TN = 1408
BQ = 256
CQ = 2048
CK = 256
C_HEAD_PAD = 128
C_DOWN_PAD = C_Q_LORA + C_KV_LORA + 2 * LANES
VMEM_LIMIT_BYTES = 60000 * 1024

LOG2E = math.log2(math.e)

BF = jnp.bfloat16
F32 = jnp.float32


def _params(*sem):
    return pltpu.CompilerParams(dimension_semantics=sem, vmem_limit_bytes=VMEM_LIMIT_BYTES)


def _dot(a, b):
    return jnp.dot(a, b, preferred_element_type=F32)


def _dot_t(a, b):
    return lax.dot_general(a, b, (((1,), (1,)), ((), ())), preferred_element_type=F32)


def _rms(x):
    return x * lax.rsqrt(jnp.mean(x * x, axis=-1, keepdims=True) + EPS)


def _ada(x, mod_ref, k):
    shift = mod_ref[3 * k:3 * k + 1, :]
    scale = mod_ref[3 * k + 1:3 * k + 2, :]
    return _rms(x) * (1.0 + scale) + shift


def _const_spec(shape):
    nd = len(shape)
    return pl.BlockSpec(shape, lambda *_: (0,) * nd, pipeline_mode=pl.Buffered(1))


def _tok_spec(width, tm=TM):
    return pl.BlockSpec((tm, width), lambda i: (i, 0))


def _mod_spec(tm):
    return pl.BlockSpec((None, N_MOD, D_MODEL), lambda i: (jnp.minimum(i * tm // DEC_SEQ, DEC_BATCH), 0, 0))


_MOD_SPEC = _mod_spec(TM)


def _rope_tile(i):
    return jnp.where(i < N_SAMPLE_TILES, i % TILES_PER_SEQ, TILES_PER_SEQ)


_ROPE_SPEC = pl.BlockSpec((TM, LANES), lambda i: (_rope_tile(i), 0))


def _mod_kernel(c_ref, w_ref, b_ref, o_ref):
    a = jax.nn.silu(c_ref[...]).astype(BF)
    o_ref[...] = _dot(a, w_ref[...].astype(BF)) + b_ref[...]


def _modulation(cond, ada_w, ada_b):
    n_out = N_MOD * D_MODEL
    out = pl.pallas_call(
        _mod_kernel,
        grid=(DEPTH, n_out // MOD_TN),
        in_specs=[
            pl.BlockSpec((N_COND, D_MODEL), lambda l, j: (0, 0)),
            pl.BlockSpec((None, D_MODEL, MOD_TN), lambda l, j: (l, 0, j)),
            pl.BlockSpec((None, 1, MOD_TN), lambda l, j: (l, 0, j)),
        ],
        out_specs=pl.BlockSpec((None, N_COND, MOD_TN), lambda l, j: (l, 0, j)),
        out_shape=jax.ShapeDtypeStruct((DEPTH, N_COND, n_out), F32),
        compiler_params=_params("arbitrary", "arbitrary"),
        name="modulation",
    )(cond, ada_w, ada_b.reshape(DEPTH, 1, n_out))
    return out.reshape(DEPTH, N_COND, N_MOD, D_MODEL)


def _ffn_prep_kernel(s_ref, w_ref, wb_ref, o_ref):
    wb = w_ref[...].astype(BF)
    wb_ref[...] = wb
    o_ref[...] = _dot(s_ref[...].astype(BF), wb)


def _ffn_prep(mods, w_in):
    shifts = jnp.stack([mods[:, :, 0], mods[:, :, 6]], axis=1)
    w_spec = pl.BlockSpec((None, None, D_MODEL, SHIFT_TN), lambda l, h, j: (l, h, 0, j))
    w_b, out = pl.pallas_call(
        _ffn_prep_kernel,
        grid=(DEPTH, 2, 2 * D_FF // SHIFT_TN),
        in_specs=[pl.BlockSpec((None, None, N_COND, D_MODEL), lambda l, h, j: (l, h, 0, 0)), w_spec],
        out_specs=[w_spec, pl.BlockSpec((None, None, N_COND, SHIFT_TN), lambda l, h, j: (l, h, 0, j))],
        out_shape=[jax.ShapeDtypeStruct(w_in.shape, BF),
                   jax.ShapeDtypeStruct((DEPTH, 2, N_COND, 2 * D_FF), F32)],
        compiler_params=_params("arbitrary", "arbitrary", "arbitrary"),
        name="ffn_prep",
    )(shifts, w_in)
    return w_b, out.reshape(DEPTH, 2, N_COND, 1, 2 * D_FF)


def _ffn_kernel(*refs, k, split_in, split_out, pre_proj):
    n_x = 2 if split_in else 1
    n_in = n_x + (3 if pre_proj else 0)
    x_refs, (mod_ref, sh_ref, win_ref, wout_ref), o_refs = refs[:n_x], refs[n_in:n_in + 4], refs[n_in + 4:]
    is_sample = pl.program_id(0) < N_SAMPLE // FM
    x = jnp.where(is_sample, x_refs[0][...], x_refs[1][...]) if split_in else x_refs[0][...]
    if pre_proj:
        attn_s_ref, attn_p_ref, wo_ref = refs[n_x:n_in]
        attn = jnp.where(is_sample, attn_s_ref[...], attn_p_ref[...])
        x = x + mod_ref[5:6, :] * _dot(attn, wo_ref[...])
    xa = (x * (1.0 + mod_ref[3 * k + 1:3 * k + 2, :])).astype(BF)
    rinv = lax.rsqrt(jnp.mean(x * x, axis=-1, keepdims=True) + EPS)
    y = None
    for c0, cw in FF_CHUNKS:
        g = _dot(xa, win_ref[:, c0:c0 + cw]) * rinv + sh_ref[:, c0:c0 + cw]
        u = _dot(xa, win_ref[:, D_FF + c0:D_FF + c0 + cw]) * rinv + sh_ref[:, D_FF + c0:D_FF + c0 + cw]
        yc = _dot((jax.nn.silu(g) * u).astype(BF), wout_ref[c0:c0 + cw, :])
        y = yc if y is None else y + yc
    gate = mod_ref[3 * k + 2:3 * k + 3, :]
    out = x + (0.5 * gate) * y
    if split_out:
        o_refs[1][...] = out

        @pl.when(is_sample)
        def _():
            o_refs[0][...] = out
    else:
        o_refs[0][...] = out


_SAMPLE_SPEC = pl.BlockSpec((FM, D_MODEL), lambda i: (jnp.minimum(i, N_SAMPLE // FM - 1), 0))
_PROMPT_SPEC = pl.BlockSpec((FM, D_MODEL), lambda i: (jnp.maximum(i - N_SAMPLE // FM, 0), 0))


def _ffn(xs, mod, shift_terms, w_in, w_out, layer, half, split_in=False, split_out=False, pre_proj=None):
    def w_spec(rows, cols):
        return pl.BlockSpec((None, None, rows, cols), lambda i: (layer, half, 0, 0), pipeline_mode=pl.Buffered(1))

    sh_spec = pl.BlockSpec((None, None, None, 1, 2 * D_FF),
                           lambda i: (layer, half, jnp.minimum(i * FM // DEC_SEQ, DEC_BATCH), 0, 0))

    x_specs = [_SAMPLE_SPEC, _PROMPT_SPEC] if split_in else [_tok_spec(D_MODEL, FM)]
    operands = list(xs) if split_in else [xs]
    if pre_proj is not None:
        x_specs += [_SAMPLE_SPEC, _PROMPT_SPEC, _const_spec((D_MODEL, D_MODEL))]
        operands += list(pre_proj)
    if split_out:
        out_specs = [_SAMPLE_SPEC, _PROMPT_SPEC]
        out_shape = [jax.ShapeDtypeStruct((N_SAMPLE, D_MODEL), F32), jax.ShapeDtypeStruct((N_PROMPT, D_MODEL), F32)]
    else:
        out_specs = _tok_spec(D_MODEL, FM)
        out_shape = jax.ShapeDtypeStruct((N_TOK, D_MODEL), F32)
    return pl.pallas_call(
        functools.partial(_ffn_kernel, k=2 * half, split_in=split_in, split_out=split_out,
                          pre_proj=pre_proj is not None),
        grid=(N_TOK // FM,),
        in_specs=x_specs + [_mod_spec(FM), sh_spec, w_spec(D_MODEL, 2 * D_FF), w_spec(D_FF, D_MODEL)],
        out_specs=out_specs,
        out_shape=out_shape,
        compiler_params=_params("arbitrary"),
        name="ffn",
    )(*operands, mod, shift_terms, w_in, w_out)


def _gmlp_kernel(x_ref, mod_ref, win_ref, vg_ref, ws_ref, bs_ref, wout_ref, o_ref):
    x = x_ref[...]
    hb = _ada(x, mod_ref, 1).astype(BF)

    def gelu_proj(c0):
        pre = _dot(hb, win_ref[:, c0:c0 + A_WIDTH])
        return 0.5 * pre * (1.0 + lax.erf(pre * math.sqrt(0.5)))

    v = (_rms(gelu_proj(A_WIDTH)) * vg_ref[...]).astype(BF)
    u = gelu_proj(0)
    bias = bs_ref[...]
    rows = []
    for c in range(FM // A_CHUNK):
        cols = [_dot(ws_ref[g], v[c * A_CHUNK:(c + 1) * A_CHUNK, g * LANES:(g + 1) * LANES])
                for g in range(A_GROUPS)]
        rows.append(jnp.concatenate(cols, axis=1) + bias)
    sv = jnp.concatenate(rows, axis=0)
    y = _dot((u * sv).astype(BF), wout_ref[...])
    o_ref[...] = x + mod_ref[5:6, :] * y


def _gmlp(x, mod, w_in, v_gain, w_s, b_s, w_out):
    bias = jnp.repeat(b_s.T, A_WIDTH // A_GROUPS, axis=1)
    return pl.pallas_call(
        _gmlp_kernel,
        grid=(N_TOK // FM,),
        in_specs=[_tok_spec(D_MODEL, FM), _mod_spec(FM),
                  _const_spec((D_MODEL, 2 * A_WIDTH)), _const_spec((1, A_WIDTH)),
                  _const_spec((A_GROUPS, A_CHUNK, A_CHUNK)), _const_spec((A_CHUNK, A_WIDTH)),
                  _const_spec((A_WIDTH, D_MODEL))],
        out_specs=_tok_spec(D_MODEL, FM),
        out_shape=jax.ShapeDtypeStruct((N_TOK, D_MODEL), F32),
        compiler_params=_params("arbitrary"),
        name="gmlp",
    )(x, mod, w_in.astype(BF), v_gain.reshape(1, A_WIDTH), w_s.astype(BF), bias, w_out.astype(BF))


def _swap_pairs(y, step):
    lane = lax.broadcasted_iota(jnp.int32, y.shape, 1)
    return jnp.where((lane & step) != 0, pltpu.roll(y, step, 1), pltpu.roll(y, LANES - step, 1))


def _rope_tables(rot_dim, lane_of_dim):
    quarter = rot_dim // 4
    inv = np.float32(ROPE_BASE) ** (-np.arange(quarter, dtype=np.float32) / np.float32(quarter))
    t = np.arange(DEC_SEQ)
    row = (t // GRID_W).astype(np.float32)
    col = (t % GRID_W).astype(np.float32)
    ang = np.stack([row[:, None] * inv, col[:, None] * inv], axis=1)
    cos, sin = np.cos(ang), np.sin(ang)
    d = np.asarray(lane_of_dim)
    dd = np.maximum(d, 0)
    axis, member, freq = dd // (2 * quarter), (dd % (2 * quarter)) // quarter, dd % quarter
    rot = (d >= 0)[None, :]
    c_tab = np.where(rot, cos[:, axis, freq], 1.0)
    s_tab = np.where(rot, np.where(member == 0, -1.0, 1.0)[None, :] * sin[:, axis, freq], 0.0)
    ident_c = np.ones((TM, LANES), np.float32)
    ident_s = np.zeros((TM, LANES), np.float32)
    return (np.concatenate([c_tab, ident_c], axis=0).astype(np.float32),
            np.concatenate([s_tab, ident_s], axis=0).astype(np.float32))


def _bproj_kernel(x_ref, mod_ref, w_ref, ones_ref, qc_ref, qs_ref, kc_ref, ks_ref, q_ref, k_ref, v_ref):
    hb = _ada(x_ref[...], mod_ref, 1).astype(BF)
    qkv = _dot(hb, w_ref[...])
    head_ones = ones_ref[...]

    def norm_rope(t4, tab, tab_swap):
        sq = t4 * t4
        hi = sq.astype(BF)
        lo = (sq - hi.astype(F32)).astype(BF)
        r = lax.rsqrt((_dot(hi, head_ones) + _dot(lo, head_ones)) / B_HEAD_DIM + EPS)
        halves = []
        for j in range(2):
            sl = slice(j * LANES, (j + 1) * LANES)
            t = t4[:, sl]
            halves.append(r[:, sl] * (t * tab + _swap_pairs(t, B_HEAD_DIM // 4) * tab_swap))
        return halves

    nq = B_HEADS * B_HEAD_DIM
    nk = B_KV_HEADS * B_HEAD_DIM
    q_tab, q_tab_swap = qc_ref[...], qs_ref[...]
    for j in range(nq // (2 * LANES)):
        halves = norm_rope(qkv[:, 2 * j * LANES:(2 * j + 2) * LANES], q_tab, q_tab_swap)
        q_ref[:, 2 * j * LANES:(2 * j + 1) * LANES] = halves[0].astype(BF)
        q_ref[:, (2 * j + 1) * LANES:(2 * j + 2) * LANES] = halves[1].astype(BF)
    halves = norm_rope(qkv[:, nq:nq + nk], kc_ref[...], ks_ref[...])
    k_ref[:, :LANES] = halves[0]
    k_ref[:, LANES:] = halves[1]
    v_ref[...] = qkv[:, nq + nk:]


def _bproj(x, mod, w_qkv, q_gain, k_gain, cos, sin):
    nq = B_HEADS * B_HEAD_DIM
    nk = B_KV_HEADS * B_HEAD_DIM
    q_scale = B_HEAD_DIM ** -0.5 * LOG2E
    lane = np.arange(LANES)
    qg = jnp.tile(q_gain, LANES // B_HEAD_DIM)
    kg = jnp.tile(k_gain, LANES // B_HEAD_DIM)
    partner = lane ^ (B_HEAD_DIM // 4)
    head = np.arange(nk) // B_HEAD_DIM
    head_ones = jnp.asarray(head[:, None] == head[None, :], BF)
    return pl.pallas_call(
        _bproj_kernel,
        grid=(N_TILES,),
        in_specs=[_tok_spec(D_MODEL), _MOD_SPEC, _const_spec((D_MODEL, nq + 2 * nk)), _const_spec((nk, nk)),
                  _ROPE_SPEC, _ROPE_SPEC, _ROPE_SPEC, _ROPE_SPEC],
        out_specs=[_tok_spec(nq), _tok_spec(nk), _tok_spec(nk)],
        out_shape=[jax.ShapeDtypeStruct((N_TOK, nq), BF),
                   jax.ShapeDtypeStruct((N_TOK, nk), F32),
                   jax.ShapeDtypeStruct((N_TOK, nk), F32)],
        compiler_params=_params("arbitrary"),
        name="gqa_proj",
    )(x, mod, w_qkv.astype(BF), head_ones,
      cos * (qg * q_scale), sin * (qg[partner] * q_scale), cos * kg, sin * kg[partner])


def _gqa_attend(q, kcat, vcat, bias, sink_ref, shift):
    tq = q.shape[0]
    nk = kcat.shape[0]
    lo = lax.broadcasted_iota(jnp.int32, (nk, LANES), 1) < B_HEAD_DIM
    lo_q = lax.broadcasted_iota(jnp.int32, (2 * tq, LANES), 1) < B_HEAD_DIM
    first = lax.broadcasted_iota(jnp.int32, (2 * tq, 1), 0) < tq
    if bias is not None:
        bias = jnp.concatenate([bias, bias], axis=0)
    outs = []
    for g in range(B_KV_HEADS):
        sl = slice((g // 2) * LANES, (g // 2 + 1) * LANES)
        own = lo if g % 2 == 0 else jnp.logical_not(lo)
        k_own = jnp.where(own, kcat[:, sl], 0.0)
        k_swp = pltpu.roll(k_own, B_HEAD_DIM, 1)
        v_own = jnp.where(own, vcat[:, sl], 1.0)
        v_swp = pltpu.roll(v_own, B_HEAD_DIM, 1)
        k_half = (k_own, k_swp) if g % 2 == 0 else (k_swp, k_own)
        v_half = (v_own, v_swp) if g % 2 == 0 else (v_swp, v_own)
        qg = jnp.concatenate([q[:, (2 * g) * LANES:(2 * g + 1) * LANES],
                              q[:, (2 * g + 1) * LANES:(2 * g + 2) * LANES]], axis=0)
        s_all = _dot_t(qg, jnp.concatenate(k_half, axis=0).astype(BF))
        o_half = []
        for e in range(2):
            s = s_all[:, e * nk:(e + 1) * nk]
            if bias is not None:
                nb = bias.shape[1]
                s = jnp.concatenate([s[:, :nb] + bias, s[:, nb:]], axis=1)
            sk = jnp.where(first, sink_ref[4 * g + e], sink_ref[4 * g + 2 + e]) * LOG2E
            m = jnp.maximum(sk, jnp.max(s, axis=-1, keepdims=True) if shift is None else shift)
            ov = _dot(jnp.exp2(s - m).astype(BF), v_half[e].astype(BF))
            o_half.append(ov / (pltpu.roll(ov, B_HEAD_DIM, 1) + jnp.exp2(sk - m)))
        o_g = jnp.where(lo_q, o_half[0], o_half[1])
        outs += [o_g[:tq], o_g[tq:]]
    return jnp.concatenate(outs, axis=1)


def _with_softmax_shift(ok_ref, shift_ref, body):
    lax.cond(ok_ref[0] != 0, lambda: body(shift_ref[0]), lambda: body(None))


def _battn_lat_kernel(ok_ref, shift_ref, x_ref, mod_ref, q_ref, kp_ref, kc_ref, kn_ref, vp_ref, vc_ref, vn_ref,
                      ck_ref, cv_ref, sink_ref, wo_ref, o_ref):
    def body(shift):
        j = pl.program_id(1)
        kcat = jnp.concatenate([kp_ref[...], kc_ref[...], kn_ref[...], ck_ref[...]], axis=0)
        vcat = jnp.concatenate([vp_ref[...], vc_ref[...], vn_ref[...], cv_ref[...]], axis=0)
        n_lat = BQ + 2 * B_WINDOW
        qi = lax.broadcasted_iota(jnp.int32, (BQ, n_lat), 0)
        pk = lax.broadcasted_iota(jnp.int32, (BQ, n_lat), 1)
        kpos = j * BQ + pk - B_WINDOW
        valid = (jnp.abs(pk - B_WINDOW - qi) <= B_WINDOW) & (kpos >= 0) & (kpos < DEC_SEQ)
        bias = jnp.where(valid, 0.0, NEG_INF)
        o = _gqa_attend(q_ref[...], kcat, vcat, bias, sink_ref, shift)
        y = _dot(o.astype(BF), wo_ref[...])
        o_ref[...] = x_ref[...] + mod_ref[5:6, :] * y

    _with_softmax_shift(ok_ref, shift_ref, body)


def _battn_ctx_kernel(ok_ref, shift_ref, x_ref, mod_ref, q_ref, k_ref, v_ref, sink_ref, wo_ref, o_ref):
    def body(shift):
        o = _gqa_attend(q_ref[...], k_ref[...], v_ref[...], None, sink_ref, shift)
        y = _dot(o.astype(BF), wo_ref[...])
        o_ref[...] = x_ref[...] + mod_ref[5:6, :] * y

    _with_softmax_shift(ok_ref, shift_ref, body)


def _battn(x, mod, q, k, v, cache_k, cache_v, sink, w_o, q_gain, k_gain):
    nq = B_HEADS * B_HEAD_DIM
    nk = B_KV_HEADS * B_HEAD_DIM
    cache_norm = jnp.sqrt(jnp.max(jnp.sum(jnp.square(cache_k.reshape(-1, B_HEAD_DIM)), axis=-1)))
    k_norm = jnp.maximum(math.sqrt(B_HEAD_DIM) * jnp.max(jnp.abs(k_gain)), cache_norm)
    bound = 1.02 * LOG2E * jnp.max(jnp.abs(q_gain)) * k_norm
    shift = bound.reshape(1)
    fixed_shift_ok = (bound <= C_SHIFT_MAX).astype(jnp.int32).reshape(1)
    nb = DEC_SEQ // BQ
    nw = DEC_SEQ // B_WINDOW
    per = BQ // B_WINDOW
    smem = pl.BlockSpec(memory_space=pltpu.SMEM)
    cur_spec = pl.BlockSpec((BQ, nk), lambda b, j: (b * nb + j, 0))
    prev_spec = pl.BlockSpec((B_WINDOW, nk), lambda b, j: (b * nw + jnp.maximum(per * j - 1, 0), 0))
    next_spec = pl.BlockSpec((B_WINDOW, nk), lambda b, j: (b * nw + jnp.minimum(per * j + per, nw - 1), 0))

    x = pl.pallas_call(
        _battn_lat_kernel,
        grid=(DEC_BATCH, nb),
        in_specs=[smem, smem, pl.BlockSpec((BQ, D_MODEL), lambda b, j: (b * nb + j, 0)),
                  pl.BlockSpec((None, N_MOD, D_MODEL), lambda b, j: (b, 0, 0)),
                  pl.BlockSpec((BQ, nq), lambda b, j: (b * nb + j, 0)),
                  prev_spec, cur_spec, next_spec, prev_spec, cur_spec, next_spec,
                  pl.BlockSpec((None, PAST_LEN, nk), lambda b, j: (b, 0, 0)),
                  pl.BlockSpec((None, PAST_LEN, nk), lambda b, j: (b, 0, 0)),
                  smem, _const_spec((nq, D_MODEL))],
        out_specs=pl.BlockSpec((BQ, D_MODEL), lambda b, j: (b * nb + j, 0)),
        out_shape=jax.ShapeDtypeStruct((N_TOK, D_MODEL), F32),
        input_output_aliases={2: 0},
        compiler_params=_params("arbitrary", "arbitrary"),
        name="gqa_attn_latent",
    )(fixed_shift_ok, shift, x, mod, q, k, k, k, v, v, v, cache_k, cache_v, sink, w_o)
    off = N_SAMPLE // SEQ
    return pl.pallas_call(
        _battn_ctx_kernel,
        grid=(BATCH,),
        in_specs=[smem, smem, pl.BlockSpec((SEQ, D_MODEL), lambda b: (off + b, 0)),
                  pl.BlockSpec((None, N_MOD, D_MODEL), lambda b: (DEC_BATCH, 0, 0)),
                  pl.BlockSpec((SEQ, nq), lambda b: (off + b, 0)),
                  pl.BlockSpec((SEQ, nk), lambda b: (off + b, 0)),
                  pl.BlockSpec((SEQ, nk), lambda b: (off + b, 0)),
                  smem, _const_spec((nq, D_MODEL))],
        out_specs=pl.BlockSpec((SEQ, D_MODEL), lambda b: (off + b, 0)),
        out_shape=jax.ShapeDtypeStruct((N_TOK, D_MODEL), F32),
        input_output_aliases={2: 0},
        compiler_params=_params("arbitrary"),
        name="gqa_attn_context",
    )(fixed_shift_ok, shift, x, mod, q, k, v, sink, w_o)


C_SWAP_W = C_HEADS * C_ROPE
C_SHIFT_LANE = C_NOPE + C_ROPE
C_SHIFT_MAX = 50.0


def _mla_head_norm_rope(t, t_swap, tab, tab_swap):
    r = lax.rsqrt(jnp.sum(t * t, axis=-1, keepdims=True) / (C_NOPE + C_ROPE) + EPS)
    if t_swap is None:
        return t * r * tab
    return r * (t * tab + t_swap * tab_swap)


def _mla_keys_values(c_kv_b, k_rope, k_rope_swap, wukv_ref, tab, tab_swap, k_shift, k_ref, v_ref):
    kv = _dot(c_kv_b, wukv_ref[...])
    lower = (lax.broadcasted_iota(jnp.int32, kv.shape, 1) & C_NOPE) == 0
    v_ref[...] = jnp.where(lower, 1.0, kv).astype(BF)
    k_nope = jnp.where(lower, kv, 0.0)
    for h in range(C_HEADS):
        sl = slice(h * C_HEAD_PAD, (h + 1) * C_HEAD_PAD)
        kh = _mla_head_norm_rope(k_nope[:, sl] + k_rope, k_rope_swap, tab, tab_swap)
        k_ref[:, sl] = (kh + k_shift).astype(BF)


def _cproj_kernel(x_ref, mod_ref, wd_ref, cqg_ref, ckvg_ref, wuq_ref, wukv_ref, qone_ref, kshift_ref,
                  qc_ref, qs_ref, kc_ref, ks_ref, q_ref, k_ref, v_ref, ckv_ref, kr_ref):
    hb = _ada(x_ref[...], mod_ref, 1).astype(BF)
    d = _dot(hb, wd_ref[...])
    c_q = _rms(d[:, :C_Q_LORA]) * cqg_ref[...]
    c_kv = _rms(d[:, C_Q_LORA:C_Q_LORA + C_KV_LORA]) * ckvg_ref[...]
    k_rope = d[:, C_Q_LORA + C_KV_LORA:C_Q_LORA + C_KV_LORA + LANES]
    k_rope_swap = d[:, C_Q_LORA + C_KV_LORA + LANES:]
    ckv_ref[...] = c_kv
    kr_ref[...] = k_rope
    q2 = _dot(c_q.astype(BF), wuq_ref[...])
    wq = C_HEADS * C_HEAD_PAD
    q_tab, q_tab_swap = qc_ref[...], qs_ref[...]
    per_tile = LANES // C_ROPE
    for h in range(C_HEADS):
        sl = slice(h * C_HEAD_PAD, (h + 1) * C_HEAD_PAD)
        t_swap = q2[:, wq + (h // per_tile) * LANES:wq + (h // per_tile + 1) * LANES]
        shift = (C_NOPE - C_ROPE * (h % per_tile)) % LANES
        if shift:
            t_swap = pltpu.roll(t_swap, shift, 1)
        q_ref[:, sl] = (_mla_head_norm_rope(q2[:, sl], t_swap, q_tab, q_tab_swap) + qone_ref[...]).astype(BF)
    _mla_keys_values(c_kv.astype(BF), k_rope, k_rope_swap, wukv_ref, kc_ref[...], ks_ref[...], kshift_ref[...],
                     k_ref, v_ref)


def _cctx_kernel(ckv_ref, kr_ref, wukv_ref, kg_ref, kshift_ref, k_ref, v_ref):
    _mla_keys_values(ckv_ref[...].astype(BF), kr_ref[...], None, wukv_ref, kg_ref[...], None, kshift_ref[...],
                     k_ref, v_ref)


def _mla_weights(w_down, w_uq, q_gain, k_gain):
    hd = C_NOPE + C_ROPE
    pad_lanes = C_HEAD_PAD - hd
    lane = np.arange(C_HEAD_PAD)
    is_rope = (lane >= C_NOPE) & (lane < hd)
    partner = np.where(is_rope, lane ^ (C_ROPE // 4), lane)

    def swapped(t):
        return jnp.where(is_rope, jnp.take(t, partner, axis=-1), 0.0)

    kr_cols = jnp.pad(w_down[:, C_Q_LORA + C_KV_LORA:], ((0, 0), (C_NOPE, pad_lanes)))
    wd = jnp.concatenate([w_down[:, :C_Q_LORA + C_KV_LORA], kr_cols, swapped(kr_cols)], axis=1)
    wuq = jnp.pad(w_uq.reshape(C_Q_LORA, C_HEADS, hd), ((0, 0), (0, 0), (0, pad_lanes)))
    wuq_swap = swapped(wuq)[:, :, C_NOPE:hd].reshape(C_Q_LORA, C_SWAP_W)
    wuq = jnp.concatenate([wuq.reshape(C_Q_LORA, C_HEADS * C_HEAD_PAD), wuq_swap], axis=1)
    qg = jnp.pad(q_gain, (0, pad_lanes))
    kg = jnp.pad(k_gain, (0, pad_lanes))
    row = lambda t: t.reshape(1, C_HEAD_PAD)
    bound = 1.02 * math.sqrt(hd) * LOG2E * jnp.max(jnp.abs(q_gain)) * jnp.max(jnp.abs(k_gain))
    shift_lane = lane == C_SHIFT_LANE
    q_one = row(jnp.asarray(shift_lane, F32))
    k_shift = row(jnp.where(shift_lane, -bound, 0.0))
    fixed_shift_ok = (bound <= C_SHIFT_MAX).astype(jnp.int32).reshape(1)
    return (wd.astype(BF), wuq.astype(BF), row(qg), row(swapped(qg)), row(kg), row(swapped(kg)),
            q_one, k_shift, fixed_shift_ok)


def _cproj(x, mod, wd, cq_gain, ckv_gain, wuq, wukv, q_one, k_shift, q_tab, q_tab_swap, k_tab, k_tab_swap):
    wq = C_HEADS * C_HEAD_PAD
    return pl.pallas_call(
        _cproj_kernel,
        grid=(N_TILES,),
        in_specs=[_tok_spec(D_MODEL), _MOD_SPEC, _const_spec((D_MODEL, C_DOWN_PAD)),
                  _const_spec((1, C_Q_LORA)), _const_spec((1, C_KV_LORA)),
                  _const_spec((C_Q_LORA, wq + C_SWAP_W)), _const_spec((C_KV_LORA, wq)),
                  _const_spec((1, C_HEAD_PAD)), _const_spec((1, C_HEAD_PAD)),
                  _ROPE_SPEC, _ROPE_SPEC, _ROPE_SPEC, _ROPE_SPEC],
        out_specs=[_tok_spec(wq), _tok_spec(wq), _tok_spec(wq), _tok_spec(C_KV_LORA), _tok_spec(LANES)],
        out_shape=[jax.ShapeDtypeStruct((N_TOK, wq), BF), jax.ShapeDtypeStruct((N_TOK, wq), BF),
                   jax.ShapeDtypeStruct((N_TOK, wq), BF), jax.ShapeDtypeStruct((N_TOK, C_KV_LORA), F32),
                   jax.ShapeDtypeStruct((N_TOK, LANES), F32)],
        compiler_params=_params("arbitrary"),
        name="mla_proj",
    )(x, mod, wd, cq_gain.reshape(1, C_Q_LORA), ckv_gain.reshape(1, C_KV_LORA), wuq, wukv, q_one, k_shift,
      q_tab, q_tab_swap, k_tab, k_tab_swap)


def _cctx(cache_ckv, cache_krope, wukv, kg, k_shift):
    n = DEC_BATCH * PAST_LEN
    wq = C_HEADS * C_HEAD_PAD
    kr = jnp.pad(cache_krope.reshape(n, C_ROPE), ((0, 0), (C_NOPE, C_HEAD_PAD - C_NOPE - C_ROPE)))
    return pl.pallas_call(
        _cctx_kernel,
        grid=(n // TM,),
        in_specs=[_tok_spec(C_KV_LORA), _tok_spec(LANES), _const_spec((C_KV_LORA, wq)),
                  _const_spec((1, C_HEAD_PAD)), _const_spec((1, C_HEAD_PAD))],
        out_specs=[_tok_spec(wq), _tok_spec(wq)],
        out_shape=[jax.ShapeDtypeStruct((n, wq), BF), jax.ShapeDtypeStruct((n, wq), BF)],
        compiler_params=_params("arbitrary"),
        name="mla_context_keys",
    )(cache_ckv.reshape(n, C_KV_LORA), kr, wukv, kg, k_shift)


def _mla_attend(ok_ref, q_ref, kv_refs, o_ref, n_pairs=1):
    tq = q_ref.shape[0]

    def pair_cols(pair):
        return (slice(pair * LANES, (pair + 1) * LANES),
                [slice((2 * pair + e) * C_HEAD_PAD, (2 * pair + e + 1) * C_HEAD_PAD) for e in range(2)])

    def normalised(sum_acc, rows):
        lo = lax.broadcasted_iota(jnp.int32, (rows, LANES), 1) < C_VDIM
        r0 = pltpu.roll(sum_acc[0], C_VDIM, 1)
        r1 = pltpu.roll(sum_acc[1], C_VDIM, 1)
        return jnp.where(lo, r0 / sum_acc[0], sum_acc[1] / r1).astype(BF)

    chunks = []
    for k_ref, v_ref in kv_refs:
        nk = k_ref.shape[0]
        if nk < CK and chunks:
            chunks[-1].append((k_ref, v_ref, slice(0, nk)))
        else:
            ck = min(CK, nk)
            chunks += [[(k_ref, v_ref, slice(c * ck, (c + 1) * ck))] for c in range(nk // ck)]
    def fixed_shift():
        for pair in range(n_pairs):
            cols, head_cols = pair_cols(pair)
            sum_acc = [None, None]
            for parts in chunks:
                for e, sl in enumerate(head_cols):
                    kk = [k_ref[rows, sl] for k_ref, _, rows in parts]
                    vv = [v_ref[rows, sl] for _, v_ref, rows in parts]
                    p = jnp.exp2(_dot_t(q_ref[:, sl], kk[0] if len(kk) == 1 else jnp.concatenate(kk, axis=0)))
                    pv = _dot(p.astype(BF), vv[0] if len(vv) == 1 else jnp.concatenate(vv, axis=0))
                    sum_acc[e] = pv if sum_acc[e] is None else sum_acc[e] + pv
            o_ref[:, cols] = normalised(sum_acc, tq)

    def row_max_shift():
        blk = min(tq, LANES)

        def body(i, carry):
            rows = pl.ds(pl.multiple_of(i * blk, blk), blk)
            for pair in range(n_pairs):
                cols, head_cols = pair_cols(pair)
                sum_acc = []
                for sl in head_cols:
                    qh = q_ref[rows, sl]
                    scores = [_dot_t(qh, k_ref[:, sl]) for k_ref, _ in kv_refs]
                    m = functools.reduce(jnp.maximum, [jnp.max(sc, axis=-1, keepdims=True) for sc in scores])
                    sum_acc.append(functools.reduce(jnp.add, [_dot(jnp.exp2(sc - m).astype(BF), v_ref[:, sl])
                                                              for sc, (_, v_ref) in zip(scores, kv_refs)]))
                o_ref[rows, cols] = normalised(sum_acc, blk)
            return carry

        lax.fori_loop(0, tq // blk, body, 0)

    lax.cond(ok_ref[0] != 0, fixed_shift, row_max_shift)


def _cattn_lat_kernel(ok_ref, q_ref, k_ref, v_ref, ck_ref, cv_ref, o_ref):
    _mla_attend(ok_ref, q_ref, ((k_ref, v_ref), (ck_ref, cv_ref)), o_ref)


def _cattn_ctx_kernel(ok_ref, q_ref, k_ref, v_ref, o_ref):
    _mla_attend(ok_ref, q_ref, ((k_ref, v_ref),), o_ref, C_HEADS // 2)


def _cattn(fixed_shift_ok, q, k, v, ck, cv):
    pair_w = 2 * C_HEAD_PAD
    n_pairs = C_HEADS // 2
    nqt = DEC_SEQ // CQ
    smem = pl.BlockSpec(memory_space=pltpu.SMEM)
    o_sample = pl.pallas_call(
        _cattn_lat_kernel,
        grid=(DEC_BATCH, n_pairs, nqt),
        in_specs=[smem, pl.BlockSpec((CQ, pair_w), lambda b, p, t: (b * nqt + t, p)),
                  pl.BlockSpec((DEC_SEQ, pair_w), lambda b, p, t: (b, p)),
                  pl.BlockSpec((DEC_SEQ, pair_w), lambda b, p, t: (b, p)),
                  pl.BlockSpec((PAST_LEN, pair_w), lambda b, p, t: (b, p)),
                  pl.BlockSpec((PAST_LEN, pair_w), lambda b, p, t: (b, p))],
        out_specs=pl.BlockSpec((CQ, LANES), lambda b, p, t: (b * nqt + t, p)),
        out_shape=jax.ShapeDtypeStruct((N_SAMPLE, C_HEADS * C_VDIM), BF),
        compiler_params=_params("arbitrary", "arbitrary", "arbitrary"),
        name="mla_attn_latent",
    )(fixed_shift_ok, q, k, v, ck, cv)
    off = N_SAMPLE // SEQ
    o_prompt = pl.pallas_call(
        _cattn_ctx_kernel,
        grid=(BATCH,),
        in_specs=[smem, pl.BlockSpec((SEQ, n_pairs * pair_w), lambda b: (off + b, 0)),
                  pl.BlockSpec((SEQ, n_pairs * pair_w), lambda b: (off + b, 0)),
                  pl.BlockSpec((SEQ, n_pairs * pair_w), lambda b: (off + b, 0))],
        out_specs=pl.BlockSpec((SEQ, n_pairs * LANES), lambda b: (b, 0)),
        out_shape=jax.ShapeDtypeStruct((N_PROMPT, C_HEADS * C_VDIM), BF),
        compiler_params=_params("arbitrary"),
        name="mla_attn_context",
    )(fixed_shift_ok, q, k, v)
    return o_sample, o_prompt


def kernel(x_prompt, x_sample, c, cache_win_k, cache_win_v, cache_mla_ckv, cache_mla_krope, c_ctx,
           ada_w, ada_b, ffn_w_in, ffn_w_out,
           gmlp_w_in, gmlp_v_gain, gmlp_w_s, gmlp_b_s, gmlp_w_out,
           win_w_qkv, win_q_gain, win_k_gain, win_sink, win_w_o,
           mla_w_down, mla_cq_gain, mla_ckv_gain, mla_w_uq, mla_w_ukv, mla_q_gain, mla_k_gain, mla_w_o):
    x = (x_sample.reshape(N_SAMPLE, D_MODEL), x_prompt.reshape(N_PROMPT, D_MODEL))
    cond = jnp.concatenate([c, c_ctx[None, :], jnp.zeros((N_COND - DEC_BATCH - 1, D_MODEL), F32)], axis=0)
    mods = _modulation(cond, ada_w, ada_b)
    w_in_b, shift_terms = _ffn_prep(mods, ffn_w_in)
    w_out_b = ffn_w_out.astype(BF)

    lane = np.arange(LANES)
    b_cos, b_sin = _rope_tables(B_HEAD_DIM, lane % B_HEAD_DIM)
    c_lane = np.where((lane >= C_NOPE) & (lane < C_NOPE + C_ROPE), lane - C_NOPE, -1)
    c_cos, c_sin = _rope_tables(C_ROPE, c_lane)

    nk = B_KV_HEADS * B_HEAD_DIM
    win_k, win_v, mla_ckv, mla_krope = [], [], [], []
    ia = ib = ic = 0
    for l in range(DEPTH):
        mod = mods[l]
        pre_proj = None
        x = _ffn(x, mod, shift_terms, w_in_b, w_out_b, l, 0, split_in=(l == 0))
        kind = l % N_MIXERS
        if kind == 0:
            x = _gmlp(x, mod, gmlp_w_in[ia], gmlp_v_gain[ia], gmlp_w_s[ia], gmlp_b_s[ia], gmlp_w_out[ia])
            ia += 1
        elif kind == 1:
            q, k, v = _bproj(x, mod, win_w_qkv[ib], win_q_gain[ib], win_k_gain[ib], b_cos, b_sin)
            x = _battn(x, mod, q, k, v,
                       cache_win_k[:, ib].reshape(DEC_BATCH, PAST_LEN, nk),
                       cache_win_v[:, ib].reshape(DEC_BATCH, PAST_LEN, nk),
                       win_sink[ib], win_w_o[ib].astype(BF), win_q_gain[ib], win_k_gain[ib])
            win_k.append(k[N_SAMPLE:].reshape(BATCH, SEQ, B_KV_HEADS, B_HEAD_DIM))
            win_v.append(v[N_SAMPLE:].reshape(BATCH, SEQ, B_KV_HEADS, B_HEAD_DIM))
            ib += 1
        else:
            wd, wuq, qg, qgs, kg, kgs, q_one, k_shift, fixed_shift_ok = _mla_weights(
                mla_w_down[ic], mla_w_uq[ic], mla_q_gain[ic], mla_k_gain[ic])
            wukv = mla_w_ukv[ic].astype(BF)
            q_scale = (C_NOPE + C_ROPE) ** -0.5 * LOG2E
            q, k, v, ckv, kr = _cproj(x, mod, wd, mla_cq_gain[ic], mla_ckv_gain[ic], wuq, wukv, q_one, k_shift,
                                      c_cos * (qg * q_scale), c_sin * (qgs * q_scale), c_cos * kg, c_sin * kgs)
            ck, cv = _cctx(cache_mla_ckv[:, ic], cache_mla_krope[:, ic], wukv, kg, k_shift)
            pre_proj = (*_cattn(fixed_shift_ok, q, k, v, ck, cv), mla_w_o[ic].astype(BF))
            mla_ckv.append(ckv[N_SAMPLE:].reshape(BATCH, SEQ, C_KV_LORA))
            mla_krope.append(kr[N_SAMPLE:, C_NOPE:C_NOPE + C_ROPE].reshape(BATCH, SEQ, C_ROPE))
            ic += 1
        x = _ffn(x, mod, shift_terms, w_in_b, w_out_b, l, 1, split_out=(l == DEPTH - 1), pre_proj=pre_proj)
    y_sample, y_prompt = x
    return (y_prompt.reshape(BATCH, SEQ, D_MODEL), y_sample.reshape(DEC_BATCH, DEC_SEQ, D_MODEL),
            jnp.stack(win_k, axis=1), jnp.stack(win_v, axis=1),
            jnp.stack(mla_ckv, axis=1), jnp.stack(mla_krope, axis=1))
```

```python
import functools
import math

import jax
import jax.numpy as jnp
import numpy as np
from jax import lax
from jax.experimental import pallas as pl
from jax.experimental.pallas import tpu as pltpu

D_MODEL = 1024
BATCH = 16
SEQ = 256
DEPTH = 4
DEC_BATCH = 8
DEC_SEQ = 4096
PAST_LEN = 256
GRID_W = 64
N_MIXERS = 3
N_MOD = 9
D_FF = 2816
A_WIDTH = D_MODEL
A_GROUPS = 8
A_CHUNK = 128
B_HEADS = 16
B_KV_HEADS = 4
B_HEAD_DIM = 64
B_WINDOW = 128
C_HEADS = 16
C_NOPE = 64
C_ROPE = 32
C_VDIM = 64
C_Q_LORA = 512
C_KV_LORA = 256
ROPE_BASE = 10000.0
EPS = 1e-6
NEG_INF = -1e30

LANES = 128
N_SAMPLE = DEC_BATCH * DEC_SEQ
N_PROMPT = BATCH * SEQ
N_TOK = N_SAMPLE + N_PROMPT
N_COND = 16
TM = 512
FM = 1024
FF_CHUNKS = ((0, 1536), (1536, 1280))
N_TILES = N_TOK // TM
N_SAMPLE_TILES = N_SAMPLE // TM
TILES_PER_SEQ = DEC_SEQ // TM
MOD_TN = 1536
SHIFT_TN = 1408
BQ = 256
CQ = 2048
CK = 256
C_HEAD_PAD = 128
C_DOWN_PAD = C_Q_LORA + C_KV_LORA + 2 * LANES
VMEM_LIMIT_BYTES = 60000 * 1024

LOG2E = math.log2(math.e)
SOFTMAX_SHIFT_MAX = 50.0

BF = jnp.bfloat16
F32 = jnp.float32


def _params(*sem):
    return pltpu.CompilerParams(dimension_semantics=sem, vmem_limit_bytes=VMEM_LIMIT_BYTES)


def _dot(a, b):
    return jnp.dot(a, b, preferred_element_type=F32)


def _dot_t(a, b):
    return lax.dot_general(a, b, (((1,), (1,)), ((), ())), preferred_element_type=F32)


def _rms(x):
    return x * lax.rsqrt(jnp.mean(x * x, axis=-1, keepdims=True) + EPS)


def _ada(x, mod_ref, k):
    shift = mod_ref[3 * k:3 * k + 1, :]
    scale = mod_ref[3 * k + 1:3 * k + 2, :]
    return _rms(x) * (1.0 + scale) + shift


def _const_spec(shape):
    nd = len(shape)
    return pl.BlockSpec(shape, lambda *_: (0,) * nd, pipeline_mode=pl.Buffered(1))


def _tok_spec(width, tm=TM):
    return pl.BlockSpec((tm, width), lambda i: (i, 0))


def _mod_spec(tm):
    return pl.BlockSpec((None, N_MOD, D_MODEL), lambda i: (jnp.minimum(i * tm // DEC_SEQ, DEC_BATCH), 0, 0))


_MOD_SPEC = _mod_spec(TM)


def _rope_tile(i):
    return jnp.where(i < N_SAMPLE_TILES, i % TILES_PER_SEQ, TILES_PER_SEQ)


_ROPE_SPEC = pl.BlockSpec((TM, LANES), lambda i: (_rope_tile(i), 0))


def _mod_kernel(c_ref, w_ref, b_ref, o_ref):
    a = jax.nn.silu(c_ref[...]).astype(BF)
    o_ref[...] = _dot(a, w_ref[...].astype(BF)) + b_ref[...]


def _modulation(cond, ada_w, ada_b):
    n_out = N_MOD * D_MODEL
    out = pl.pallas_call(
        _mod_kernel,
        grid=(DEPTH, n_out // MOD_TN),
        in_specs=[
            pl.BlockSpec((N_COND, D_MODEL), lambda l, j: (0, 0)),
            pl.BlockSpec((None, D_MODEL, MOD_TN), lambda l, j: (l, 0, j)),
            pl.BlockSpec((None, 1, MOD_TN), lambda l, j: (l, 0, j)),
        ],
        out_specs=pl.BlockSpec((None, N_COND, MOD_TN), lambda l, j: (l, 0, j)),
        out_shape=jax.ShapeDtypeStruct((DEPTH, N_COND, n_out), F32),
        compiler_params=_params("arbitrary", "arbitrary"),
        name="modulation",
    )(cond, ada_w, ada_b.reshape(DEPTH, 1, n_out))
    return out.reshape(DEPTH, N_COND, N_MOD, D_MODEL)


def _ffn_prep_kernel(s_ref, w_ref, wb_ref, o_ref):
    wb = w_ref[...].astype(BF)
    wb_ref[...] = wb
    o_ref[...] = _dot(s_ref[...].astype(BF), wb)


def _ffn_prep(mods, w_in):
    shifts = jnp.stack([mods[:, :, 0], mods[:, :, 6]], axis=1)
    w_spec = pl.BlockSpec((None, None, D_MODEL, SHIFT_TN), lambda l, h, j: (l, h, 0, j))
    w_b, out = pl.pallas_call(
        _ffn_prep_kernel,
        grid=(DEPTH, 2, 2 * D_FF // SHIFT_TN),
        in_specs=[pl.BlockSpec((None, None, N_COND, D_MODEL), lambda l, h, j: (l, h, 0, 0)), w_spec],
        out_specs=[w_spec, pl.BlockSpec((None, None, N_COND, SHIFT_TN), lambda l, h, j: (l, h, 0, j))],
        out_shape=[jax.ShapeDtypeStruct(w_in.shape, BF),
                   jax.ShapeDtypeStruct((DEPTH, 2, N_COND, 2 * D_FF), F32)],
        compiler_params=_params("arbitrary", "arbitrary", "arbitrary"),
        name="ffn_prep",
    )(shifts, w_in)
    return w_b, out.reshape(DEPTH, 2, N_COND, 1, 2 * D_FF)


def _ffn_kernel(*refs, k, split_in, split_out, pre_proj):
    n_x = 2 if split_in else 1
    n_in = n_x + (3 if pre_proj else 0)
    x_refs, (mod_ref, sh_ref, win_ref, wout_ref), o_refs = refs[:n_x], refs[n_in:n_in + 4], refs[n_in + 4:]
    is_sample = pl.program_id(0) < N_SAMPLE // FM
    x = jnp.where(is_sample, x_refs[0][...], x_refs[1][...]) if split_in else x_refs[0][...]
    if pre_proj:
        attn_s_ref, attn_p_ref, wo_ref = refs[n_x:n_in]
        attn = jnp.where(is_sample, attn_s_ref[...], attn_p_ref[...])
        x = x + mod_ref[5:6, :] * _dot(attn, wo_ref[...])
    xa = (x * (1.0 + mod_ref[3 * k + 1:3 * k + 2, :])).astype(BF)
    rinv = lax.rsqrt(jnp.mean(x * x, axis=-1, keepdims=True) + EPS)
    y = None
    for c0, cw in FF_CHUNKS:
        g = _dot(xa, win_ref[:, c0:c0 + cw]) * rinv + sh_ref[:, c0:c0 + cw]
        u = _dot(xa, win_ref[:, D_FF + c0:D_FF + c0 + cw]) * rinv + sh_ref[:, D_FF + c0:D_FF + c0 + cw]
        yc = _dot((jax.nn.silu(g) * u).astype(BF), wout_ref[c0:c0 + cw, :])
        y = yc if y is None else y + yc
    gate = mod_ref[3 * k + 2:3 * k + 3, :]
    out = x + (0.5 * gate) * y
    if split_out:
        o_refs[1][...] = out

        @pl.when(is_sample)
        def _():
            o_refs[0][...] = out
    else:
        o_refs[0][...] = out


_SAMPLE_SPEC = pl.BlockSpec((FM, D_MODEL), lambda i: (jnp.minimum(i, N_SAMPLE // FM - 1), 0))
_PROMPT_SPEC = pl.BlockSpec((FM, D_MODEL), lambda i: (jnp.maximum(i - N_SAMPLE // FM, 0), 0))


def _ffn(xs, mod, shift_terms, w_in, w_out, layer, half, split_in=False, split_out=False, pre_proj=None):
    def w_spec(rows, cols):
        return pl.BlockSpec((None, None, rows, cols), lambda i: (layer, half, 0, 0), pipeline_mode=pl.Buffered(1))

    sh_spec = pl.BlockSpec((None, None, None, 1, 2 * D_FF),
                           lambda i: (layer, half, jnp.minimum(i * FM // DEC_SEQ, DEC_BATCH), 0, 0))

    x_specs = [_SAMPLE_SPEC, _PROMPT_SPEC] if split_in else [_tok_spec(D_MODEL, FM)]
    operands = list(xs) if split_in else [xs]
    if pre_proj is not None:
        x_specs += [_SAMPLE_SPEC, _PROMPT_SPEC, _const_spec((D_MODEL, D_MODEL))]
        operands += list(pre_proj)
    if split_out:
        out_specs = [_SAMPLE_SPEC, _PROMPT_SPEC]
        out_shape = [jax.ShapeDtypeStruct((N_SAMPLE, D_MODEL), F32), jax.ShapeDtypeStruct((N_PROMPT, D_MODEL), F32)]
    else:
        out_specs = _tok_spec(D_MODEL, FM)
        out_shape = jax.ShapeDtypeStruct((N_TOK, D_MODEL), F32)
    return pl.pallas_call(
        functools.partial(_ffn_kernel, k=2 * half, split_in=split_in, split_out=split_out,
                          pre_proj=pre_proj is not None),
        grid=(N_TOK // FM,),
        in_specs=x_specs + [_mod_spec(FM), sh_spec, w_spec(D_MODEL, 2 * D_FF), w_spec(D_FF, D_MODEL)],
        out_specs=out_specs,
        out_shape=out_shape,
        compiler_params=_params("arbitrary"),
        name="ffn",
    )(*operands, mod, shift_terms, w_in, w_out)


def _gmlp_kernel(x_ref, mod_ref, win_ref, vg_ref, ws_ref, bs_ref, wout_ref, o_ref):
    x = x_ref[...]
    hb = _ada(x, mod_ref, 1).astype(BF)

    def gelu_proj(c0):
        pre = _dot(hb, win_ref[:, c0:c0 + A_WIDTH])
        return 0.5 * pre * (1.0 + lax.erf(pre * math.sqrt(0.5)))

    v = (_rms(gelu_proj(A_WIDTH)) * vg_ref[...]).astype(BF)
    u = gelu_proj(0)
    bias = bs_ref[...]
    rows = []
    for c in range(FM // A_CHUNK):
        cols = [_dot(ws_ref[g], v[c * A_CHUNK:(c + 1) * A_CHUNK, g * LANES:(g + 1) * LANES])
                for g in range(A_GROUPS)]
        rows.append(jnp.concatenate(cols, axis=1) + bias)
    sv = jnp.concatenate(rows, axis=0)
    y = _dot((u * sv).astype(BF), wout_ref[...])
    o_ref[...] = x + mod_ref[5:6, :] * y


def _gmlp(x, mod, w_in, v_gain, w_s, b_s, w_out):
    bias = jnp.repeat(b_s.T, A_WIDTH // A_GROUPS, axis=1)
    return pl.pallas_call(
        _gmlp_kernel,
        grid=(N_TOK // FM,),
        in_specs=[_tok_spec(D_MODEL, FM), _mod_spec(FM),
                  _const_spec((D_MODEL, 2 * A_WIDTH)), _const_spec((1, A_WIDTH)),
                  _const_spec((A_GROUPS, A_CHUNK, A_CHUNK)), _const_spec((A_CHUNK, A_WIDTH)),
                  _const_spec((A_WIDTH, D_MODEL))],
        out_specs=_tok_spec(D_MODEL, FM),
        out_shape=jax.ShapeDtypeStruct((N_TOK, D_MODEL), F32),
        compiler_params=_params("arbitrary"),
        name="gmlp",
    )(x, mod, w_in.astype(BF), v_gain.reshape(1, A_WIDTH), w_s.astype(BF), bias, w_out.astype(BF))


def _swap_pairs(y, step):
    lane = lax.broadcasted_iota(jnp.int32, y.shape, 1)
    return jnp.where((lane & step) != 0, pltpu.roll(y, step, 1), pltpu.roll(y, LANES - step, 1))


def _rope_tables(rot_dim, lane_of_dim):
    quarter = rot_dim // 4
    inv = np.float32(ROPE_BASE) ** (-np.arange(quarter, dtype=np.float32) / np.float32(quarter))
    t = np.arange(DEC_SEQ)
    row = (t // GRID_W).astype(np.float32)
    col = (t % GRID_W).astype(np.float32)
    ang = np.stack([row[:, None] * inv, col[:, None] * inv], axis=1)
    cos, sin = np.cos(ang), np.sin(ang)
    d = np.asarray(lane_of_dim)
    dd = np.maximum(d, 0)
    axis, member, freq = dd // (2 * quarter), (dd % (2 * quarter)) // quarter, dd % quarter
    rot = (d >= 0)[None, :]
    c_tab = np.where(rot, cos[:, axis, freq], 1.0)
    s_tab = np.where(rot, np.where(member == 0, -1.0, 1.0)[None, :] * sin[:, axis, freq], 0.0)
    ident_c = np.ones((TM, LANES), np.float32)
    ident_s = np.zeros((TM, LANES), np.float32)
    return (np.concatenate([c_tab, ident_c], axis=0).astype(np.float32),
            np.concatenate([s_tab, ident_s], axis=0).astype(np.float32))


def _bproj_kernel(x_ref, mod_ref, w_ref, ones_ref, qc_ref, qs_ref, kc_ref, ks_ref, q_ref, k_ref, v_ref):
    hb = _ada(x_ref[...], mod_ref, 1).astype(BF)
    qkv = _dot(hb, w_ref[...])
    head_ones = ones_ref[...]

    def norm_rope(t4, tab, tab_swap):
        sq = t4 * t4
        hi = sq.astype(BF)
        lo = (sq - hi.astype(F32)).astype(BF)
        r = lax.rsqrt((_dot(hi, head_ones) + _dot(lo, head_ones)) / B_HEAD_DIM + EPS)
        halves = []
        for j in range(2):
            sl = slice(j * LANES, (j + 1) * LANES)
            t = t4[:, sl]
            halves.append(r[:, sl] * (t * tab + _swap_pairs(t, B_HEAD_DIM // 4) * tab_swap))
        return halves

    nq = B_HEADS * B_HEAD_DIM
    nk = B_KV_HEADS * B_HEAD_DIM
    q_tab, q_tab_swap = qc_ref[...], qs_ref[...]
    for j in range(nq // (2 * LANES)):
        halves = norm_rope(qkv[:, 2 * j * LANES:(2 * j + 2) * LANES], q_tab, q_tab_swap)
        q_ref[:, 2 * j * LANES:(2 * j + 1) * LANES] = halves[0].astype(BF)
        q_ref[:, (2 * j + 1) * LANES:(2 * j + 2) * LANES] = halves[1].astype(BF)
    halves = norm_rope(qkv[:, nq:nq + nk], kc_ref[...], ks_ref[...])
    k_ref[:, :LANES] = halves[0]
    k_ref[:, LANES:] = halves[1]
    v_ref[...] = qkv[:, nq + nk:]


def _bproj(x, mod, w_qkv, q_gain, k_gain, cos, sin):
    nq = B_HEADS * B_HEAD_DIM
    nk = B_KV_HEADS * B_HEAD_DIM
    q_scale = B_HEAD_DIM ** -0.5 * LOG2E
    lane = np.arange(LANES)
    qg = jnp.tile(q_gain, LANES // B_HEAD_DIM)
    kg = jnp.tile(k_gain, LANES // B_HEAD_DIM)
    partner = lane ^ (B_HEAD_DIM // 4)
    head = np.arange(nk) // B_HEAD_DIM
    head_ones = jnp.asarray(head[:, None] == head[None, :], BF)
    return pl.pallas_call(
        _bproj_kernel,
        grid=(N_TILES,),
        in_specs=[_tok_spec(D_MODEL), _MOD_SPEC, _const_spec((D_MODEL, nq + 2 * nk)), _const_spec((nk, nk)),
                  _ROPE_SPEC, _ROPE_SPEC, _ROPE_SPEC, _ROPE_SPEC],
        out_specs=[_tok_spec(nq), _tok_spec(nk), _tok_spec(nk)],
        out_shape=[jax.ShapeDtypeStruct((N_TOK, nq), BF),
                   jax.ShapeDtypeStruct((N_TOK, nk), F32),
                   jax.ShapeDtypeStruct((N_TOK, nk), F32)],
        compiler_params=_params("arbitrary"),
        name="gqa_proj",
    )(x, mod, w_qkv.astype(BF), head_ones,
      cos * (qg * q_scale), sin * (qg[partner] * q_scale), cos * kg, sin * kg[partner])


def _gqa_attend(q, kcat, vcat, bias, sink_ref, shift):
    tq = q.shape[0]
    nk = kcat.shape[0]
    lo = lax.broadcasted_iota(jnp.int32, (nk, LANES), 1) < B_HEAD_DIM
    lo_q = lax.broadcasted_iota(jnp.int32, (2 * tq, LANES), 1) < B_HEAD_DIM
    first = lax.broadcasted_iota(jnp.int32, (2 * tq, 1), 0) < tq
    if bias is not None:
        bias = jnp.concatenate([bias, bias], axis=0)
    outs = []
    for g in range(B_KV_HEADS):
        sl = slice((g // 2) * LANES, (g // 2 + 1) * LANES)
        own = lo if g % 2 == 0 else jnp.logical_not(lo)
        k_own = jnp.where(own, kcat[:, sl], 0.0)
        k_swp = pltpu.roll(k_own, B_HEAD_DIM, 1)
        v_own = jnp.where(own, vcat[:, sl], 1.0)
        v_swp = pltpu.roll(v_own, B_HEAD_DIM, 1)
        k_half = (k_own, k_swp) if g % 2 == 0 else (k_swp, k_own)
        v_half = (v_own, v_swp) if g % 2 == 0 else (v_swp, v_own)
        qg = jnp.concatenate([q[:, (2 * g) * LANES:(2 * g + 1) * LANES],
                              q[:, (2 * g + 1) * LANES:(2 * g + 2) * LANES]], axis=0)
        s_all = _dot_t(qg, jnp.concatenate(k_half, axis=0).astype(BF))
        o_half = []
        for e in range(2):
            s = s_all[:, e * nk:(e + 1) * nk]
            if bias is not None:
                nb = bias.shape[1]
                s = jnp.concatenate([s[:, :nb] + bias, s[:, nb:]], axis=1)
            sk = jnp.where(first, sink_ref[4 * g + e], sink_ref[4 * g + 2 + e]) * LOG2E
            m = jnp.maximum(sk, jnp.max(s, axis=-1, keepdims=True) if shift is None else shift)
            ov = _dot(jnp.exp2(s - m).astype(BF), v_half[e].astype(BF))
            o_half.append(ov / (pltpu.roll(ov, B_HEAD_DIM, 1) + jnp.exp2(sk - m)))
        o_g = jnp.where(lo_q, o_half[0], o_half[1])
        outs += [o_g[:tq], o_g[tq:]]
    return jnp.concatenate(outs, axis=1)


def _with_softmax_shift(ok_ref, shift_ref, body):
    lax.cond(ok_ref[0] != 0, lambda: body(shift_ref[0]), lambda: body(None))


def _battn_lat_kernel(ok_ref, shift_ref, x_ref, mod_ref, q_ref, kp_ref, kc_ref, kn_ref, vp_ref, vc_ref, vn_ref,
                      ck_ref, cv_ref, sink_ref, wo_ref, o_ref):
    def body(shift):
        j = pl.program_id(1)
        kcat = jnp.concatenate([kp_ref[...], kc_ref[...], kn_ref[...], ck_ref[...]], axis=0)
        vcat = jnp.concatenate([vp_ref[...], vc_ref[...], vn_ref[...], cv_ref[...]], axis=0)
        n_lat = BQ + 2 * B_WINDOW
        qi = lax.broadcasted_iota(jnp.int32, (BQ, n_lat), 0)
        pk = lax.broadcasted_iota(jnp.int32, (BQ, n_lat), 1)
        kpos = j * BQ + pk - B_WINDOW
        valid = (jnp.abs(pk - B_WINDOW - qi) <= B_WINDOW) & (kpos >= 0) & (kpos < DEC_SEQ)
        bias = jnp.where(valid, 0.0, NEG_INF)
        o = _gqa_attend(q_ref[...], kcat, vcat, bias, sink_ref, shift)
        y = _dot(o.astype(BF), wo_ref[...])
        o_ref[...] = x_ref[...] + mod_ref[5:6, :] * y

    _with_softmax_shift(ok_ref, shift_ref, body)


def _battn_ctx_kernel(ok_ref, shift_ref, x_ref, mod_ref, q_ref, k_ref, v_ref, sink_ref, wo_ref, o_ref):
    def body(shift):
        o = _gqa_attend(q_ref[...], k_ref[...], v_ref[...], None, sink_ref, shift)
        y = _dot(o.astype(BF), wo_ref[...])
        o_ref[...] = x_ref[...] + mod_ref[5:6, :] * y

    _with_softmax_shift(ok_ref, shift_ref, body)


def _battn(x, mod, q, k, v, cache_k, cache_v, sink, w_o, q_gain, k_gain):
    nq = B_HEADS * B_HEAD_DIM
    nk = B_KV_HEADS * B_HEAD_DIM
    cache_norm = jnp.sqrt(jnp.max(jnp.sum(jnp.square(cache_k.reshape(-1, B_HEAD_DIM)), axis=-1)))
    k_norm = jnp.maximum(math.sqrt(B_HEAD_DIM) * jnp.max(jnp.abs(k_gain)), cache_norm)
    bound = 1.02 * LOG2E * jnp.max(jnp.abs(q_gain)) * k_norm
    shift = bound.reshape(1)
    fixed_shift_ok = (bound <= SOFTMAX_SHIFT_MAX).astype(jnp.int32).reshape(1)
    nb = DEC_SEQ // BQ
    nw = DEC_SEQ // B_WINDOW
    per = BQ // B_WINDOW
    smem = pl.BlockSpec(memory_space=pltpu.SMEM)
    cur_spec = pl.BlockSpec((BQ, nk), lambda b, j: (b * nb + j, 0))
    prev_spec = pl.BlockSpec((B_WINDOW, nk), lambda b, j: (b * nw + jnp.maximum(per * j - 1, 0), 0))
    next_spec = pl.BlockSpec((B_WINDOW, nk), lambda b, j: (b * nw + jnp.minimum(per * j + per, nw - 1), 0))

    x = pl.pallas_call(
        _battn_lat_kernel,
        grid=(DEC_BATCH, nb),
        in_specs=[smem, smem, pl.BlockSpec((BQ, D_MODEL), lambda b, j: (b * nb + j, 0)),
                  pl.BlockSpec((None, N_MOD, D_MODEL), lambda b, j: (b, 0, 0)),
                  pl.BlockSpec((BQ, nq), lambda b, j: (b * nb + j, 0)),
                  prev_spec, cur_spec, next_spec, prev_spec, cur_spec, next_spec,
                  pl.BlockSpec((None, PAST_LEN, nk), lambda b, j: (b, 0, 0)),
                  pl.BlockSpec((None, PAST_LEN, nk), lambda b, j: (b, 0, 0)),
                  smem, _const_spec((nq, D_MODEL))],
        out_specs=pl.BlockSpec((BQ, D_MODEL), lambda b, j: (b * nb + j, 0)),
        out_shape=jax.ShapeDtypeStruct((N_TOK, D_MODEL), F32),
        input_output_aliases={2: 0},
        compiler_params=_params("arbitrary", "arbitrary"),
        name="gqa_attn_latent",
    )(fixed_shift_ok, shift, x, mod, q, k, k, k, v, v, v, cache_k, cache_v, sink, w_o)
    off = N_SAMPLE // SEQ
    return pl.pallas_call(
        _battn_ctx_kernel,
        grid=(BATCH,),
        in_specs=[smem, smem, pl.BlockSpec((SEQ, D_MODEL), lambda b: (off + b, 0)),
                  pl.BlockSpec((None, N_MOD, D_MODEL), lambda b: (DEC_BATCH, 0, 0)),
                  pl.BlockSpec((SEQ, nq), lambda b: (off + b, 0)),
                  pl.BlockSpec((SEQ, nk), lambda b: (off + b, 0)),
                  pl.BlockSpec((SEQ, nk), lambda b: (off + b, 0)),
                  smem, _const_spec((nq, D_MODEL))],
        out_specs=pl.BlockSpec((SEQ, D_MODEL), lambda b: (off + b, 0)),
        out_shape=jax.ShapeDtypeStruct((N_TOK, D_MODEL), F32),
        input_output_aliases={2: 0},
        compiler_params=_params("arbitrary"),
        name="gqa_attn_context",
    )(fixed_shift_ok, shift, x, mod, q, k, v, sink, w_o)


C_SWAP_W = C_HEADS * C_ROPE
C_SHIFT_LANE = C_NOPE + C_ROPE


def _mla_head_norm_rope(t, t_swap, tab, tab_swap):
    r = lax.rsqrt(jnp.sum(t * t, axis=-1, keepdims=True) / (C_NOPE + C_ROPE) + EPS)
    if t_swap is None:
        return t * r * tab
    return r * (t * tab + t_swap * tab_swap)


def _mla_keys_values(c_kv_b, k_rope, k_rope_swap, wukv_ref, tab, tab_swap, k_shift, k_ref, v_ref):
    kv = _dot(c_kv_b, wukv_ref[...])
    lower = (lax.broadcasted_iota(jnp.int32, kv.shape, 1) & C_NOPE) == 0
    v_ref[...] = jnp.where(lower, 1.0, kv).astype(BF)
    k_nope = jnp.where(lower, kv, 0.0)
    for h in range(C_HEADS):
        sl = slice(h * C_HEAD_PAD, (h + 1) * C_HEAD_PAD)
        kh = _mla_head_norm_rope(k_nope[:, sl] + k_rope, k_rope_swap, tab, tab_swap)
        k_ref[:, sl] = (kh + k_shift).astype(BF)


def _cproj_kernel(x_ref, mod_ref, wd_ref, cqg_ref, ckvg_ref, wuq_ref, wukv_ref, qone_ref, kshift_ref,
                  qc_ref, qs_ref, kc_ref, ks_ref, q_ref, k_ref, v_ref, ckv_ref, kr_ref):
    hb = _ada(x_ref[...], mod_ref, 1).astype(BF)
    d = _dot(hb, wd_ref[...])
    c_q = _rms(d[:, :C_Q_LORA]) * cqg_ref[...]
    c_kv = _rms(d[:, C_Q_LORA:C_Q_LORA + C_KV_LORA]) * ckvg_ref[...]
    k_rope = d[:, C_Q_LORA + C_KV_LORA:C_Q_LORA + C_KV_LORA + LANES]
    k_rope_swap = d[:, C_Q_LORA + C_KV_LORA + LANES:]
    ckv_ref[...] = c_kv
    kr_ref[...] = k_rope
    q2 = _dot(c_q.astype(BF), wuq_ref[...])
    wq = C_HEADS * C_HEAD_PAD
    q_tab, q_tab_swap = qc_ref[...], qs_ref[...]
    per_tile = LANES // C_ROPE
    for h in range(C_HEADS):
        sl = slice(h * C_HEAD_PAD, (h + 1) * C_HEAD_PAD)
        t_swap = q2[:, wq + (h // per_tile) * LANES:wq + (h // per_tile + 1) * LANES]
        shift = (C_NOPE - C_ROPE * (h % per_tile)) % LANES
        if shift:
            t_swap = pltpu.roll(t_swap, shift, 1)
        q_ref[:, sl] = (_mla_head_norm_rope(q2[:, sl], t_swap, q_tab, q_tab_swap) + qone_ref[...]).astype(BF)
    _mla_keys_values(c_kv.astype(BF), k_rope, k_rope_swap, wukv_ref, kc_ref[...], ks_ref[...], kshift_ref[...],
                     k_ref, v_ref)


def _cctx_kernel(ckv_ref, kr_ref, wukv_ref, kg_ref, kshift_ref, k_ref, v_ref):
    _mla_keys_values(ckv_ref[...].astype(BF), kr_ref[...], None, wukv_ref, kg_ref[...], None, kshift_ref[...],
                     k_ref, v_ref)


def _mla_weights(w_down, w_uq, q_gain, k_gain):
    hd = C_NOPE + C_ROPE
    pad_lanes = C_HEAD_PAD - hd
    lane = np.arange(C_HEAD_PAD)
    is_rope = (lane >= C_NOPE) & (lane < hd)
    partner = np.where(is_rope, lane ^ (C_ROPE // 4), lane)

    def swapped(t):
        return jnp.where(is_rope, jnp.take(t, partner, axis=-1), 0.0)

    kr_cols = jnp.pad(w_down[:, C_Q_LORA + C_KV_LORA:], ((0, 0), (C_NOPE, pad_lanes)))
    wd = jnp.concatenate([w_down[:, :C_Q_LORA + C_KV_LORA], kr_cols, swapped(kr_cols)], axis=1)
    wuq = jnp.pad(w_uq.reshape(C_Q_LORA, C_HEADS, hd), ((0, 0), (0, 0), (0, pad_lanes)))
    wuq_swap = swapped(wuq)[:, :, C_NOPE:hd].reshape(C_Q_LORA, C_SWAP_W)
    wuq = jnp.concatenate([wuq.reshape(C_Q_LORA, C_HEADS * C_HEAD_PAD), wuq_swap], axis=1)
    qg = jnp.pad(q_gain, (0, pad_lanes))
    kg = jnp.pad(k_gain, (0, pad_lanes))
    row = lambda t: t.reshape(1, C_HEAD_PAD)
    bound = 1.02 * math.sqrt(hd) * LOG2E * jnp.max(jnp.abs(q_gain)) * jnp.max(jnp.abs(k_gain))
    shift_lane = lane == C_SHIFT_LANE
    q_one = row(jnp.asarray(shift_lane, F32))
    k_shift = row(jnp.where(shift_lane, -bound, 0.0))
    fixed_shift_ok = (bound <= SOFTMAX_SHIFT_MAX).astype(jnp.int32).reshape(1)
    return (wd.astype(BF), wuq.astype(BF), row(qg), row(swapped(qg)), row(kg), row(swapped(kg)),
            q_one, k_shift, fixed_shift_ok)


def _cproj(x, mod, wd, cq_gain, ckv_gain, wuq, wukv, q_one, k_shift, q_tab, q_tab_swap, k_tab, k_tab_swap):
    wq = C_HEADS * C_HEAD_PAD
    return pl.pallas_call(
        _cproj_kernel,
        grid=(N_TILES,),
        in_specs=[_tok_spec(D_MODEL), _MOD_SPEC, _const_spec((D_MODEL, C_DOWN_PAD)),
                  _const_spec((1, C_Q_LORA)), _const_spec((1, C_KV_LORA)),
                  _const_spec((C_Q_LORA, wq + C_SWAP_W)), _const_spec((C_KV_LORA, wq)),
                  _const_spec((1, C_HEAD_PAD)), _const_spec((1, C_HEAD_PAD)),
                  _ROPE_SPEC, _ROPE_SPEC, _ROPE_SPEC, _ROPE_SPEC],
        out_specs=[_tok_spec(wq), _tok_spec(wq), _tok_spec(wq), _tok_spec(C_KV_LORA), _tok_spec(LANES)],
        out_shape=[jax.ShapeDtypeStruct((N_TOK, wq), BF), jax.ShapeDtypeStruct((N_TOK, wq), BF),
                   jax.ShapeDtypeStruct((N_TOK, wq), BF), jax.ShapeDtypeStruct((N_TOK, C_KV_LORA), F32),
                   jax.ShapeDtypeStruct((N_TOK, LANES), F32)],
        compiler_params=_params("arbitrary"),
        name="mla_proj",
    )(x, mod, wd, cq_gain.reshape(1, C_Q_LORA), ckv_gain.reshape(1, C_KV_LORA), wuq, wukv, q_one, k_shift,
      q_tab, q_tab_swap, k_tab, k_tab_swap)


def _cctx(cache_ckv, cache_krope, wukv, kg, k_shift):
    n = DEC_BATCH * PAST_LEN
    wq = C_HEADS * C_HEAD_PAD
    kr = jnp.pad(cache_krope.reshape(n, C_ROPE), ((0, 0), (C_NOPE, C_HEAD_PAD - C_NOPE - C_ROPE)))
    return pl.pallas_call(
        _cctx_kernel,
        grid=(n // TM,),
        in_specs=[_tok_spec(C_KV_LORA), _tok_spec(LANES), _const_spec((C_KV_LORA, wq)),
                  _const_spec((1, C_HEAD_PAD)), _const_spec((1, C_HEAD_PAD))],
        out_specs=[_tok_spec(wq), _tok_spec(wq)],
        out_shape=[jax.ShapeDtypeStruct((n, wq), BF), jax.ShapeDtypeStruct((n, wq), BF)],
        compiler_params=_params("arbitrary"),
        name="mla_context_keys",
    )(cache_ckv.reshape(n, C_KV_LORA), kr, wukv, kg, k_shift)


def _mla_attend(ok_ref, q_ref, kv_refs, o_ref, n_pairs=1):
    tq = q_ref.shape[0]

    def pair_cols(pair):
        return (slice(pair * LANES, (pair + 1) * LANES),
                [slice((2 * pair + e) * C_HEAD_PAD, (2 * pair + e + 1) * C_HEAD_PAD) for e in range(2)])

    def normalised(sum_acc, rows):
        lo = lax.broadcasted_iota(jnp.int32, (rows, LANES), 1) < C_VDIM
        r0 = pltpu.roll(sum_acc[0], C_VDIM, 1)
        r1 = pltpu.roll(sum_acc[1], C_VDIM, 1)
        return jnp.where(lo, r0 / sum_acc[0], sum_acc[1] / r1).astype(BF)

    chunks = []
    for k_ref, v_ref in kv_refs:
        nk = k_ref.shape[0]
        if nk < CK and chunks:
            chunks[-1].append((k_ref, v_ref, slice(0, nk)))
        else:
            ck = min(CK, nk)
            chunks += [[(k_ref, v_ref, slice(c * ck, (c + 1) * ck))] for c in range(nk // ck)]
    def fixed_shift():
        for pair in range(n_pairs):
            cols, head_cols = pair_cols(pair)
            sum_acc = [None, None]
            for parts in chunks:
                for e, sl in enumerate(head_cols):
                    kk = [k_ref[rows, sl] for k_ref, _, rows in parts]
                    vv = [v_ref[rows, sl] for _, v_ref, rows in parts]
                    p = jnp.exp2(_dot_t(q_ref[:, sl], kk[0] if len(kk) == 1 else jnp.concatenate(kk, axis=0)))
                    pv = _dot(p.astype(BF), vv[0] if len(vv) == 1 else jnp.concatenate(vv, axis=0))
                    sum_acc[e] = pv if sum_acc[e] is None else sum_acc[e] + pv
            o_ref[:, cols] = normalised(sum_acc, tq)

    def row_max_shift():
        blk = min(tq, LANES)

        def body(i, carry):
            rows = pl.ds(pl.multiple_of(i * blk, blk), blk)
            for pair in range(n_pairs):
                cols, head_cols = pair_cols(pair)
                sum_acc = []
                for sl in head_cols:
                    qh = q_ref[rows, sl]
                    scores = [_dot_t(qh, k_ref[:, sl]) for k_ref, _ in kv_refs]
                    m = functools.reduce(jnp.maximum, [jnp.max(sc, axis=-1, keepdims=True) for sc in scores])
                    sum_acc.append(functools.reduce(jnp.add, [_dot(jnp.exp2(sc - m).astype(BF), v_ref[:, sl])
                                                              for sc, (_, v_ref) in zip(scores, kv_refs)]))
                o_ref[rows, cols] = normalised(sum_acc, blk)
            return carry

        lax.fori_loop(0, tq // blk, body, 0)

    lax.cond(ok_ref[0] != 0, fixed_shift, row_max_shift)


def _cattn_lat_kernel(ok_ref, q_ref, k_ref, v_ref, ck_ref, cv_ref, o_ref):
    _mla_attend(ok_ref, q_ref, ((k_ref, v_ref), (ck_ref, cv_ref)), o_ref)


def _cattn_ctx_kernel(ok_ref, q_ref, k_ref, v_ref, o_ref):
    _mla_attend(ok_ref, q_ref, ((k_ref, v_ref),), o_ref, C_HEADS // 2)


def _cattn(fixed_shift_ok, q, k, v, ck, cv):
    pair_w = 2 * C_HEAD_PAD
    n_pairs = C_HEADS // 2
    nqt = DEC_SEQ // CQ
    smem = pl.BlockSpec(memory_space=pltpu.SMEM)
    o_sample = pl.pallas_call(
        _cattn_lat_kernel,
        grid=(DEC_BATCH, n_pairs, nqt),
        in_specs=[smem, pl.BlockSpec((CQ, pair_w), lambda b, p, t: (b * nqt + t, p)),
                  pl.BlockSpec((DEC_SEQ, pair_w), lambda b, p, t: (b, p)),
                  pl.BlockSpec((DEC_SEQ, pair_w), lambda b, p, t: (b, p)),
                  pl.BlockSpec((PAST_LEN, pair_w), lambda b, p, t: (b, p)),
                  pl.BlockSpec((PAST_LEN, pair_w), lambda b, p, t: (b, p))],
        out_specs=pl.BlockSpec((CQ, LANES), lambda b, p, t: (b * nqt + t, p)),
        out_shape=jax.ShapeDtypeStruct((N_SAMPLE, C_HEADS * C_VDIM), BF),
        compiler_params=_params("arbitrary", "arbitrary", "arbitrary"),
        name="mla_attn_latent",
    )(fixed_shift_ok, q, k, v, ck, cv)
    off = N_SAMPLE // SEQ
    o_prompt = pl.pallas_call(
        _cattn_ctx_kernel,
        grid=(BATCH,),
        in_specs=[smem, pl.BlockSpec((SEQ, n_pairs * pair_w), lambda b: (off + b, 0)),
                  pl.BlockSpec((SEQ, n_pairs * pair_w), lambda b: (off + b, 0)),
                  pl.BlockSpec((SEQ, n_pairs * pair_w), lambda b: (off + b, 0))],
        out_specs=pl.BlockSpec((SEQ, n_pairs * LANES), lambda b: (b, 0)),
        out_shape=jax.ShapeDtypeStruct((N_PROMPT, C_HEADS * C_VDIM), BF),
        compiler_params=_params("arbitrary"),
        name="mla_attn_context",
    )(fixed_shift_ok, q, k, v)
    return o_sample, o_prompt


def kernel(x_prompt, x_sample, c, cache_win_k, cache_win_v, cache_mla_ckv, cache_mla_krope, c_ctx,
           ada_w, ada_b, ffn_w_in, ffn_w_out,
           gmlp_w_in, gmlp_v_gain, gmlp_w_s, gmlp_b_s, gmlp_w_out,
           win_w_qkv, win_q_gain, win_k_gain, win_sink, win_w_o,
           mla_w_down, mla_cq_gain, mla_ckv_gain, mla_w_uq, mla_w_ukv, mla_q_gain, mla_k_gain, mla_w_o):
    x = (x_sample.reshape(N_SAMPLE, D_MODEL), x_prompt.reshape(N_PROMPT, D_MODEL))
    cond = jnp.concatenate([c, c_ctx[None, :], jnp.zeros((N_COND - DEC_BATCH - 1, D_MODEL), F32)], axis=0)
    mods = _modulation(cond, ada_w, ada_b)
    w_in_b, shift_terms = _ffn_prep(mods, ffn_w_in)
    w_out_b = ffn_w_out.astype(BF)

    lane = np.arange(LANES)
    b_cos, b_sin = _rope_tables(B_HEAD_DIM, lane % B_HEAD_DIM)
    c_lane = np.where((lane >= C_NOPE) & (lane < C_NOPE + C_ROPE), lane - C_NOPE, -1)
    c_cos, c_sin = _rope_tables(C_ROPE, c_lane)

    nk = B_KV_HEADS * B_HEAD_DIM
    win_k, win_v, mla_ckv, mla_krope = [], [], [], []
    ia = ib = ic = 0
    for l in range(DEPTH):
        mod = mods[l]
        pre_proj = None
        x = _ffn(x, mod, shift_terms, w_in_b, w_out_b, l, 0, split_in=(l == 0))
        kind = l % N_MIXERS
        if kind == 0:
            x = _gmlp(x, mod, gmlp_w_in[ia], gmlp_v_gain[ia], gmlp_w_s[ia], gmlp_b_s[ia], gmlp_w_out[ia])
            ia += 1
        elif kind == 1:
            q, k, v = _bproj(x, mod, win_w_qkv[ib], win_q_gain[ib], win_k_gain[ib], b_cos, b_sin)
            x = _battn(x, mod, q, k, v,
                       cache_win_k[:, ib].reshape(DEC_BATCH, PAST_LEN, nk),
                       cache_win_v[:, ib].reshape(DEC_BATCH, PAST_LEN, nk),
                       win_sink[ib], win_w_o[ib].astype(BF), win_q_gain[ib], win_k_gain[ib])
            win_k.append(k[N_SAMPLE:].reshape(BATCH, SEQ, B_KV_HEADS, B_HEAD_DIM))
            win_v.append(v[N_SAMPLE:].reshape(BATCH, SEQ, B_KV_HEADS, B_HEAD_DIM))
            ib += 1
        else:
            wd, wuq, qg, qgs, kg, kgs, q_one, k_shift, fixed_shift_ok = _mla_weights(
                mla_w_down[ic], mla_w_uq[ic], mla_q_gain[ic], mla_k_gain[ic])
            wukv = mla_w_ukv[ic].astype(BF)
            q_scale = (C_NOPE + C_ROPE) ** -0.5 * LOG2E
            q, k, v, ckv, kr = _cproj(x, mod, wd, mla_cq_gain[ic], mla_ckv_gain[ic], wuq, wukv, q_one, k_shift,
                                      c_cos * (qg * q_scale), c_sin * (qgs * q_scale), c_cos * kg, c_sin * kgs)
            ck, cv = _cctx(cache_mla_ckv[:, ic], cache_mla_krope[:, ic], wukv, kg, k_shift)
            pre_proj = (*_cattn(fixed_shift_ok, q, k, v, ck, cv), mla_w_o[ic].astype(BF))
            mla_ckv.append(ckv[N_SAMPLE:].reshape(BATCH, SEQ, C_KV_LORA))
            mla_krope.append(kr[N_SAMPLE:, C_NOPE:C_NOPE + C_ROPE].reshape(BATCH, SEQ, C_ROPE))
            ic += 1
        x = _ffn(x, mod, shift_terms, w_in_b, w_out_b, l, 1, split_out=(l == DEPTH - 1), pre_proj=pre_proj)
    y_sample, y_prompt = x
    return (y_prompt.reshape(BATCH, SEQ, D_MODEL), y_sample.reshape(DEC_BATCH, DEC_SEQ, D_MODEL),
            jnp.stack(win_k, axis=1), jnp.stack(win_v, axis=1),
            jnp.stack(mla_ckv, axis=1), jnp.stack(mla_krope, axis=1))
```

```python
import functools
import math

import jax
import jax.numpy as jnp
import numpy as np
from jax import lax
from jax.experimental import pallas as pl
from jax.experimental.pallas import tpu as pltpu

D_MODEL = 1024
BATCH = 16
SEQ = 256
DEPTH = 4
DEC_BATCH = 8
DEC_SEQ = 4096
PAST_LEN = 256
GRID_W = 64
N_MIXERS = 3
N_MOD = 9
D_FF = 2816
A_WIDTH = D_MODEL
A_GROUPS = 8
A_CHUNK = 128
B_HEADS = 16
B_KV_HEADS = 4
B_HEAD_DIM = 64
B_WINDOW = 128
C_HEADS = 16
C_NOPE = 64
C_ROPE = 32
C_VDIM = 64
C_Q_LORA = 512
C_KV_LORA = 256
ROPE_BASE = 10000.0
EPS = 1e-6
NEG_INF = -1e30

LANES = 128
N_SAMPLE = DEC_BATCH * DEC_SEQ
N_PROMPT = BATCH * SEQ
N_TOK = N_SAMPLE + N_PROMPT
N_COND = 16
TM = 512
FM = 1024
FF_CHUNKS = ((0, 1536), (1536, 1280))
N_TILES = N_TOK // TM
N_SAMPLE_TILES = N_SAMPLE // TM
TILES_PER_SEQ = DEC_SEQ // TM
MOD_TN = 1536
SHIFT_TN = 1408
BQ = 256
CQ = 2048
CK = 256
C_HEAD_PAD = 128
NORM_ROWS = 64
C_DOWN_PAD = C_Q_LORA + C_KV_LORA + 2 * LANES
VMEM_LIMIT_BYTES = 60000 * 1024

LOG2E = math.log2(math.e)
SOFTMAX_SHIFT_MAX = 50.0

BF = jnp.bfloat16
F32 = jnp.float32


def _params(*sem):
    return pltpu.CompilerParams(dimension_semantics=sem, vmem_limit_bytes=VMEM_LIMIT_BYTES)


def _dot(a, b):
    return jnp.dot(a, b, preferred_element_type=F32)


def _dot_t(a, b):
    return lax.dot_general(a, b, (((1,), (1,)), ((), ())), preferred_element_type=F32)


def _rms(x):
    return x * lax.rsqrt(jnp.mean(x * x, axis=-1, keepdims=True) + EPS)


def _ada(x, mod_ref, k):
    shift = mod_ref[3 * k:3 * k + 1, :]
    scale = mod_ref[3 * k + 1:3 * k + 2, :]
    return _rms(x) * (1.0 + scale) + shift


def _const_spec(shape):
    nd = len(shape)
    return pl.BlockSpec(shape, lambda *_: (0,) * nd, pipeline_mode=pl.Buffered(1))


def _tok_spec(width, tm=TM):
    return pl.BlockSpec((tm, width), lambda i: (i, 0))


def _mod_spec(tm):
    return pl.BlockSpec((None, N_MOD, D_MODEL), lambda i: (jnp.minimum(i * tm // DEC_SEQ, DEC_BATCH), 0, 0))


_MOD_SPEC = _mod_spec(TM)


def _rope_tile(i):
    return jnp.where(i < N_SAMPLE_TILES, i % TILES_PER_SEQ, TILES_PER_SEQ)


_ROPE_SPEC = pl.BlockSpec((TM, LANES), lambda i: (_rope_tile(i), 0))


def _mod_kernel(c_ref, w_ref, b_ref, o_ref):
    a = jax.nn.silu(c_ref[...]).astype(BF)
    o_ref[...] = _dot(a, w_ref[...].astype(BF)) + b_ref[...]


def _modulation(cond, ada_w, ada_b):
    n_out = N_MOD * D_MODEL
    out = pl.pallas_call(
        _mod_kernel,
        grid=(DEPTH, n_out // MOD_TN),
        in_specs=[
            pl.BlockSpec((N_COND, D_MODEL), lambda l, j: (0, 0)),
            pl.BlockSpec((None, D_MODEL, MOD_TN), lambda l, j: (l, 0, j)),
            pl.BlockSpec((None, 1, MOD_TN), lambda l, j: (l, 0, j)),
        ],
        out_specs=pl.BlockSpec((None, N_COND, MOD_TN), lambda l, j: (l, 0, j)),
        out_shape=jax.ShapeDtypeStruct((DEPTH, N_COND, n_out), F32),
        compiler_params=_params("arbitrary", "arbitrary"),
        name="modulation",
    )(cond, ada_w, ada_b.reshape(DEPTH, 1, n_out))
    return out.reshape(DEPTH, N_COND, N_MOD, D_MODEL)


def _ffn_prep_kernel(s_ref, w_ref, wb_ref, o_ref):
    wb = w_ref[...].astype(BF)
    wb_ref[...] = wb
    o_ref[...] = _dot(s_ref[...].astype(BF), wb)


def _ffn_prep(mods, w_in):
    shifts = jnp.stack([mods[:, :, 0], mods[:, :, 6]], axis=1)
    w_spec = pl.BlockSpec((None, None, D_MODEL, SHIFT_TN), lambda l, h, j: (l, h, 0, j))
    w_b, out = pl.pallas_call(
        _ffn_prep_kernel,
        grid=(DEPTH, 2, 2 * D_FF // SHIFT_TN),
        in_specs=[pl.BlockSpec((None, None, N_COND, D_MODEL), lambda l, h, j: (l, h, 0, 0)), w_spec],
        out_specs=[w_spec, pl.BlockSpec((None, None, N_COND, SHIFT_TN), lambda l, h, j: (l, h, 0, j))],
        out_shape=[jax.ShapeDtypeStruct(w_in.shape, BF),
                   jax.ShapeDtypeStruct((DEPTH, 2, N_COND, 2 * D_FF), F32)],
        compiler_params=_params("arbitrary", "arbitrary", "arbitrary"),
        name="ffn_prep",
    )(shifts, w_in)
    return w_b, out.reshape(DEPTH, 2, N_COND, 1, 2 * D_FF)


def _ffn_kernel(*refs, k, split_in, split_out, pre_proj):
    n_x = 2 if split_in else 1
    n_in = n_x + (3 if pre_proj else 0)
    x_refs, (mod_ref, sh_ref, win_ref, wout_ref), o_refs = refs[:n_x], refs[n_in:n_in + 4], refs[n_in + 4:]
    is_sample = pl.program_id(0) < N_SAMPLE // FM
    x = jnp.where(is_sample, x_refs[0][...], x_refs[1][...]) if split_in else x_refs[0][...]
    if pre_proj:
        attn_s_ref, attn_p_ref, wo_ref = refs[n_x:n_in]
        attn = jnp.where(is_sample, attn_s_ref[...], attn_p_ref[...])
        x = x + mod_ref[5:6, :] * _dot(attn, wo_ref[...])
    xa = (x * (1.0 + mod_ref[3 * k + 1:3 * k + 2, :])).astype(BF)
    rinv = lax.rsqrt(jnp.mean(x * x, axis=-1, keepdims=True) + EPS)
    y = None
    for c0, cw in FF_CHUNKS:
        g = _dot(xa, win_ref[:, c0:c0 + cw]) * rinv + sh_ref[:, c0:c0 + cw]
        u = _dot(xa, win_ref[:, D_FF + c0:D_FF + c0 + cw]) * rinv + sh_ref[:, D_FF + c0:D_FF + c0 + cw]
        yc = _dot((jax.nn.silu(g) * u).astype(BF), wout_ref[c0:c0 + cw, :])
        y = yc if y is None else y + yc
    gate = mod_ref[3 * k + 2:3 * k + 3, :]
    out = x + (0.5 * gate) * y
    if split_out:
        o_refs[1][...] = out

        @pl.when(is_sample)
        def _():
            o_refs[0][...] = out
    else:
        o_refs[0][...] = out


_SAMPLE_SPEC = pl.BlockSpec((FM, D_MODEL), lambda i: (jnp.minimum(i, N_SAMPLE // FM - 1), 0))
_PROMPT_SPEC = pl.BlockSpec((FM, D_MODEL), lambda i: (jnp.maximum(i - N_SAMPLE // FM, 0), 0))


def _ffn(xs, mod, shift_terms, w_in, w_out, layer, half, split_in=False, split_out=False, pre_proj=None):
    def w_spec(rows, cols):
        return pl.BlockSpec((None, None, rows, cols), lambda i: (layer, half, 0, 0), pipeline_mode=pl.Buffered(1))

    sh_spec = pl.BlockSpec((None, None, None, 1, 2 * D_FF),
                           lambda i: (layer, half, jnp.minimum(i * FM // DEC_SEQ, DEC_BATCH), 0, 0))

    x_specs = [_SAMPLE_SPEC, _PROMPT_SPEC] if split_in else [_tok_spec(D_MODEL, FM)]
    operands = list(xs) if split_in else [xs]
    if pre_proj is not None:
        x_specs += [_SAMPLE_SPEC, _PROMPT_SPEC, _const_spec((D_MODEL, D_MODEL))]
        operands += list(pre_proj)
    if split_out:
        out_specs = [_SAMPLE_SPEC, _PROMPT_SPEC]
        out_shape = [jax.ShapeDtypeStruct((N_SAMPLE, D_MODEL), F32), jax.ShapeDtypeStruct((N_PROMPT, D_MODEL), F32)]
    else:
        out_specs = _tok_spec(D_MODEL, FM)
        out_shape = jax.ShapeDtypeStruct((N_TOK, D_MODEL), F32)
    return pl.pallas_call(
        functools.partial(_ffn_kernel, k=2 * half, split_in=split_in, split_out=split_out,
                          pre_proj=pre_proj is not None),
        grid=(N_TOK // FM,),
        in_specs=x_specs + [_mod_spec(FM), sh_spec, w_spec(D_MODEL, 2 * D_FF), w_spec(D_FF, D_MODEL)],
        out_specs=out_specs,
        out_shape=out_shape,
        compiler_params=_params("arbitrary"),
        name="ffn",
    )(*operands, mod, shift_terms, w_in, w_out)


def _gmlp_kernel(x_ref, mod_ref, win_ref, vg_ref, ws_ref, bs_ref, wout_ref, o_ref):
    x = x_ref[...]
    hb = _ada(x, mod_ref, 1).astype(BF)

    def gelu_proj(c0):
        pre = _dot(hb, win_ref[:, c0:c0 + A_WIDTH])
        return 0.5 * pre * (1.0 + lax.erf(pre * math.sqrt(0.5)))

    v = (_rms(gelu_proj(A_WIDTH)) * vg_ref[...]).astype(BF)
    u = gelu_proj(0)
    bias = bs_ref[...]
    rows = []
    for c in range(FM // A_CHUNK):
        cols = [_dot(ws_ref[g], v[c * A_CHUNK:(c + 1) * A_CHUNK, g * LANES:(g + 1) * LANES])
                for g in range(A_GROUPS)]
        rows.append(jnp.concatenate(cols, axis=1) + bias)
    sv = jnp.concatenate(rows, axis=0)
    y = _dot((u * sv).astype(BF), wout_ref[...])
    o_ref[...] = x + mod_ref[5:6, :] * y


def _gmlp(x, mod, w_in, v_gain, w_s, b_s, w_out):
    bias = jnp.repeat(b_s.T, A_WIDTH // A_GROUPS, axis=1)
    return pl.pallas_call(
        _gmlp_kernel,
        grid=(N_TOK // FM,),
        in_specs=[_tok_spec(D_MODEL, FM), _mod_spec(FM),
                  _const_spec((D_MODEL, 2 * A_WIDTH)), _const_spec((1, A_WIDTH)),
                  _const_spec((A_GROUPS, A_CHUNK, A_CHUNK)), _const_spec((A_CHUNK, A_WIDTH)),
                  _const_spec((A_WIDTH, D_MODEL))],
        out_specs=_tok_spec(D_MODEL, FM),
        out_shape=jax.ShapeDtypeStruct((N_TOK, D_MODEL), F32),
        compiler_params=_params("arbitrary"),
        name="gmlp",
    )(x, mod, w_in.astype(BF), v_gain.reshape(1, A_WIDTH), w_s.astype(BF), bias, w_out.astype(BF))


def _swap_pairs(y, step):
    lane = lax.broadcasted_iota(jnp.int32, y.shape, 1)
    return jnp.where((lane & step) != 0, pltpu.roll(y, step, 1), pltpu.roll(y, LANES - step, 1))


def _rope_tables(rot_dim, lane_of_dim):
    quarter = rot_dim // 4
    inv = np.float32(ROPE_BASE) ** (-np.arange(quarter, dtype=np.float32) / np.float32(quarter))
    t = np.arange(DEC_SEQ)
    row = (t // GRID_W).astype(np.float32)
    col = (t % GRID_W).astype(np.float32)
    ang = np.stack([row[:, None] * inv, col[:, None] * inv], axis=1)
    cos, sin = np.cos(ang), np.sin(ang)
    d = np.asarray(lane_of_dim)
    dd = np.maximum(d, 0)
    axis, member, freq = dd // (2 * quarter), (dd % (2 * quarter)) // quarter, dd % quarter
    rot = (d >= 0)[None, :]
    c_tab = np.where(rot, cos[:, axis, freq], 1.0)
    s_tab = np.where(rot, np.where(member == 0, -1.0, 1.0)[None, :] * sin[:, axis, freq], 0.0)
    ident_c = np.ones((TM, LANES), np.float32)
    ident_s = np.zeros((TM, LANES), np.float32)
    return (np.concatenate([c_tab, ident_c], axis=0).astype(np.float32),
            np.concatenate([s_tab, ident_s], axis=0).astype(np.float32))


def _bproj_kernel(x_ref, mod_ref, w_ref, ones_ref, qc_ref, qs_ref, kc_ref, ks_ref, q_ref, k_ref, v_ref):
    hb = _ada(x_ref[...], mod_ref, 1).astype(BF)
    qkv = _dot(hb, w_ref[...])
    head_ones = ones_ref[...]

    def norm_rope(t4, tab, tab_swap):
        sq = t4 * t4
        hi = sq.astype(BF)
        lo = (sq - hi.astype(F32)).astype(BF)
        r = lax.rsqrt((_dot(hi, head_ones) + _dot(lo, head_ones)) / B_HEAD_DIM + EPS)
        halves = []
        for j in range(2):
            sl = slice(j * LANES, (j + 1) * LANES)
            t = t4[:, sl]
            halves.append(r[:, sl] * (t * tab + _swap_pairs(t, B_HEAD_DIM // 4) * tab_swap))
        return halves

    nq = B_HEADS * B_HEAD_DIM
    nk = B_KV_HEADS * B_HEAD_DIM
    q_tab, q_tab_swap = qc_ref[...], qs_ref[...]
    for j in range(nq // (2 * LANES)):
        halves = norm_rope(qkv[:, 2 * j * LANES:(2 * j + 2) * LANES], q_tab, q_tab_swap)
        q_ref[:, 2 * j * LANES:(2 * j + 1) * LANES] = halves[0].astype(BF)
        q_ref[:, (2 * j + 1) * LANES:(2 * j + 2) * LANES] = halves[1].astype(BF)
    halves = norm_rope(qkv[:, nq:nq + nk], kc_ref[...], ks_ref[...])
    k_ref[:, :LANES] = halves[0]
    k_ref[:, LANES:] = halves[1]
    v_ref[...] = qkv[:, nq + nk:]


def _bproj(x, mod, w_qkv, q_gain, k_gain, cos, sin):
    nq = B_HEADS * B_HEAD_DIM
    nk = B_KV_HEADS * B_HEAD_DIM
    q_scale = B_HEAD_DIM ** -0.5 * LOG2E
    lane = np.arange(LANES)
    qg = jnp.tile(q_gain, LANES // B_HEAD_DIM)
    kg = jnp.tile(k_gain, LANES // B_HEAD_DIM)
    partner = lane ^ (B_HEAD_DIM // 4)
    head = np.arange(nk) // B_HEAD_DIM
    head_ones = jnp.asarray(head[:, None] == head[None, :], BF)
    return pl.pallas_call(
        _bproj_kernel,
        grid=(N_TILES,),
        in_specs=[_tok_spec(D_MODEL), _MOD_SPEC, _const_spec((D_MODEL, nq + 2 * nk)), _const_spec((nk, nk)),
                  _ROPE_SPEC, _ROPE_SPEC, _ROPE_SPEC, _ROPE_SPEC],
        out_specs=[_tok_spec(nq), _tok_spec(nk), _tok_spec(nk)],
        out_shape=[jax.ShapeDtypeStruct((N_TOK, nq), BF),
                   jax.ShapeDtypeStruct((N_TOK, nk), F32),
                   jax.ShapeDtypeStruct((N_TOK, nk), F32)],
        compiler_params=_params("arbitrary"),
        name="gqa_proj",
    )(x, mod, w_qkv.astype(BF), head_ones,
      cos * (qg * q_scale), sin * (qg[partner] * q_scale), cos * kg, sin * kg[partner])


def _gqa_attend(q, kcat, vcat, bias, sink_ref, shift):
    tq = q.shape[0]
    nk = kcat.shape[0]
    lo = lax.broadcasted_iota(jnp.int32, (nk, LANES), 1) < B_HEAD_DIM
    lo_q = lax.broadcasted_iota(jnp.int32, (2 * tq, LANES), 1) < B_HEAD_DIM
    first = lax.broadcasted_iota(jnp.int32, (2 * tq, 1), 0) < tq
    if bias is not None:
        bias = jnp.concatenate([bias, bias], axis=0)
    outs = []
    for g in range(B_KV_HEADS):
        sl = slice((g // 2) * LANES, (g // 2 + 1) * LANES)
        own = lo if g % 2 == 0 else jnp.logical_not(lo)
        k_own = jnp.where(own, kcat[:, sl], 0.0)
        k_swp = pltpu.roll(k_own, B_HEAD_DIM, 1)
        v_own = jnp.where(own, vcat[:, sl], 1.0)
        v_swp = pltpu.roll(v_own, B_HEAD_DIM, 1)
        k_half = (k_own, k_swp) if g % 2 == 0 else (k_swp, k_own)
        v_half = (v_own, v_swp) if g % 2 == 0 else (v_swp, v_own)
        qg = jnp.concatenate([q[:, (2 * g) * LANES:(2 * g + 1) * LANES],
                              q[:, (2 * g + 1) * LANES:(2 * g + 2) * LANES]], axis=0)
        s_all = _dot_t(qg, jnp.concatenate(k_half, axis=0).astype(BF))
        o_half = []
        for e in range(2):
            s = s_all[:, e * nk:(e + 1) * nk]
            if bias is not None:
                nb = bias.shape[1]
                s = jnp.concatenate([s[:, :nb] + bias, s[:, nb:]], axis=1)
            sk = jnp.where(first, sink_ref[4 * g + e], sink_ref[4 * g + 2 + e]) * LOG2E
            m = jnp.maximum(sk, jnp.max(s, axis=-1, keepdims=True) if shift is None else shift)
            ov = _dot(jnp.exp2(s - m).astype(BF), v_half[e].astype(BF))
            o_half.append(ov / (pltpu.roll(ov, B_HEAD_DIM, 1) + jnp.exp2(sk - m)))
        o_g = jnp.where(lo_q, o_half[0], o_half[1])
        outs += [o_g[:tq], o_g[tq:]]
    return jnp.concatenate(outs, axis=1)


def _with_softmax_shift(ok_ref, shift_ref, body):
    lax.cond(ok_ref[0] != 0, lambda: body(shift_ref[0]), lambda: body(None))


def _battn_lat_kernel(ok_ref, shift_ref, x_ref, mod_ref, q_ref, kp_ref, kc_ref, kn_ref, vp_ref, vc_ref, vn_ref,
                      ck_ref, cv_ref, sink_ref, wo_ref, o_ref):
    def body(shift):
        j = pl.program_id(1)
        kcat = jnp.concatenate([kp_ref[...], kc_ref[...], kn_ref[...], ck_ref[...]], axis=0)
        vcat = jnp.concatenate([vp_ref[...], vc_ref[...], vn_ref[...], cv_ref[...]], axis=0)
        n_lat = BQ + 2 * B_WINDOW
        qi = lax.broadcasted_iota(jnp.int32, (BQ, n_lat), 0)
        pk = lax.broadcasted_iota(jnp.int32, (BQ, n_lat), 1)
        kpos = j * BQ + pk - B_WINDOW
        valid = (jnp.abs(pk - B_WINDOW - qi) <= B_WINDOW) & (kpos >= 0) & (kpos < DEC_SEQ)
        bias = jnp.where(valid, 0.0, NEG_INF)
        o = _gqa_attend(q_ref[...], kcat, vcat, bias, sink_ref, shift)
        y = _dot(o.astype(BF), wo_ref[...])
        o_ref[...] = x_ref[...] + mod_ref[5:6, :] * y

    _with_softmax_shift(ok_ref, shift_ref, body)


def _battn_ctx_kernel(ok_ref, shift_ref, x_ref, mod_ref, q_ref, k_ref, v_ref, sink_ref, wo_ref, o_ref):
    def body(shift):
        o = _gqa_attend(q_ref[...], k_ref[...], v_ref[...], None, sink_ref, shift)
        y = _dot(o.astype(BF), wo_ref[...])
        o_ref[...] = x_ref[...] + mod_ref[5:6, :] * y

    _with_softmax_shift(ok_ref, shift_ref, body)


def _battn(x, mod, q, k, v, cache_k, cache_v, sink, w_o, q_gain, k_gain):
    nq = B_HEADS * B_HEAD_DIM
    nk = B_KV_HEADS * B_HEAD_DIM
    cache_norm = jnp.sqrt(jnp.max(jnp.sum(jnp.square(cache_k.reshape(-1, B_HEAD_DIM)), axis=-1)))
    k_norm = jnp.maximum(math.sqrt(B_HEAD_DIM) * jnp.max(jnp.abs(k_gain)), cache_norm)
    bound = 1.02 * LOG2E * jnp.max(jnp.abs(q_gain)) * k_norm
    shift = bound.reshape(1)
    fixed_shift_ok = (bound <= SOFTMAX_SHIFT_MAX).astype(jnp.int32).reshape(1)
    nb = DEC_SEQ // BQ
    nw = DEC_SEQ // B_WINDOW
    per = BQ // B_WINDOW
    smem = pl.BlockSpec(memory_space=pltpu.SMEM)
    cur_spec = pl.BlockSpec((BQ, nk), lambda b, j: (b * nb + j, 0))
    prev_spec = pl.BlockSpec((B_WINDOW, nk), lambda b, j: (b * nw + jnp.maximum(per * j - 1, 0), 0))
    next_spec = pl.BlockSpec((B_WINDOW, nk), lambda b, j: (b * nw + jnp.minimum(per * j + per, nw - 1), 0))

    x = pl.pallas_call(
        _battn_lat_kernel,
        grid=(DEC_BATCH, nb),
        in_specs=[smem, smem, pl.BlockSpec((BQ, D_MODEL), lambda b, j: (b * nb + j, 0)),
                  pl.BlockSpec((None, N_MOD, D_MODEL), lambda b, j: (b, 0, 0)),
                  pl.BlockSpec((BQ, nq), lambda b, j: (b * nb + j, 0)),
                  prev_spec, cur_spec, next_spec, prev_spec, cur_spec, next_spec,
                  pl.BlockSpec((None, PAST_LEN, nk), lambda b, j: (b, 0, 0)),
                  pl.BlockSpec((None, PAST_LEN, nk), lambda b, j: (b, 0, 0)),
                  smem, _const_spec((nq, D_MODEL))],
        out_specs=pl.BlockSpec((BQ, D_MODEL), lambda b, j: (b * nb + j, 0)),
        out_shape=jax.ShapeDtypeStruct((N_TOK, D_MODEL), F32),
        input_output_aliases={2: 0},
        compiler_params=_params("arbitrary", "arbitrary"),
        name="gqa_attn_latent",
    )(fixed_shift_ok, shift, x, mod, q, k, k, k, v, v, v, cache_k, cache_v, sink, w_o)
    off = N_SAMPLE // SEQ
    return pl.pallas_call(
        _battn_ctx_kernel,
        grid=(BATCH,),
        in_specs=[smem, smem, pl.BlockSpec((SEQ, D_MODEL), lambda b: (off + b, 0)),
                  pl.BlockSpec((None, N_MOD, D_MODEL), lambda b: (DEC_BATCH, 0, 0)),
                  pl.BlockSpec((SEQ, nq), lambda b: (off + b, 0)),
                  pl.BlockSpec((SEQ, nk), lambda b: (off + b, 0)),
                  pl.BlockSpec((SEQ, nk), lambda b: (off + b, 0)),
                  smem, _const_spec((nq, D_MODEL))],
        out_specs=pl.BlockSpec((SEQ, D_MODEL), lambda b: (off + b, 0)),
        out_shape=jax.ShapeDtypeStruct((N_TOK, D_MODEL), F32),
        input_output_aliases={2: 0},
        compiler_params=_params("arbitrary"),
        name="gqa_attn_context",
    )(fixed_shift_ok, shift, x, mod, q, k, v, sink, w_o)


C_SWAP_W = C_HEADS * C_ROPE
C_SHIFT_LANE = C_NOPE + C_ROPE


def _mla_head_norm_rope(t, t_swap, tab, tab_swap):
    r = lax.rsqrt(jnp.sum(t * t, axis=-1, keepdims=True) / (C_NOPE + C_ROPE) + EPS)
    if t_swap is None:
        return t * r * tab
    return r * (t * tab + t_swap * tab_swap)


def _mla_keys_values(c_kv_b, k_rope, k_rope_swap, wukv_ref, tab, tab_swap, k_shift, k_ref, v_ref):
    kv = _dot(c_kv_b, wukv_ref[...])
    lower = (lax.broadcasted_iota(jnp.int32, kv.shape, 1) & C_NOPE) == 0
    v_ref[...] = jnp.where(lower, 1.0, kv).astype(BF)
    k_nope = jnp.where(lower, kv, 0.0)
    for h in range(C_HEADS):
        sl = slice(h * C_HEAD_PAD, (h + 1) * C_HEAD_PAD)
        kh = _mla_head_norm_rope(k_nope[:, sl] + k_rope, k_rope_swap, tab, tab_swap)
        k_ref[:, sl] = (kh + k_shift).astype(BF)


def _cproj_kernel(x_ref, mod_ref, wd_ref, cqg_ref, ckvg_ref, wuq_ref, wukv_ref, qone_ref, kshift_ref,
                  qc_ref, qs_ref, kc_ref, ks_ref, q_ref, k_ref, v_ref, ckv_ref, kr_ref, q2_scr, kv_scr, krs_scr):
    hb = _ada(x_ref[...], mod_ref, 1).astype(BF)
    d = _dot(hb, wd_ref[...])
    c_q = _rms(d[:, :C_Q_LORA]) * cqg_ref[...]
    c_kv = _rms(d[:, C_Q_LORA:C_Q_LORA + C_KV_LORA]) * ckvg_ref[...]
    ckv_ref[...] = c_kv
    kr_ref[...] = d[:, C_Q_LORA + C_KV_LORA:C_Q_LORA + C_KV_LORA + LANES]
    krs_scr[...] = d[:, C_Q_LORA + C_KV_LORA + LANES:]
    q2_scr[...] = _dot(c_q.astype(BF), wuq_ref[...])
    kv = _dot(c_kv.astype(BF), wukv_ref[...])
    lower = (lax.broadcasted_iota(jnp.int32, kv.shape, 1) & C_NOPE) == 0
    v_ref[...] = jnp.where(lower, 1.0, kv).astype(BF)
    kv_scr[...] = jnp.where(lower, kv, 0.0)
    wq = C_HEADS * C_HEAD_PAD
    per_tile = LANES // C_ROPE

    def body(i, carry):
        rows = pl.ds(pl.multiple_of(i * NORM_ROWS, NORM_ROWS), NORM_ROWS)
        q_tab, q_tab_swap = qc_ref[rows, :], qs_ref[rows, :]
        k_tab, k_tab_swap = kc_ref[rows, :], ks_ref[rows, :]
        k_rope, k_rope_swap = kr_ref[rows, :], krs_scr[rows, :]
        for h in range(C_HEADS):
            sl = slice(h * C_HEAD_PAD, (h + 1) * C_HEAD_PAD)
            t_swap = q2_scr[rows, wq + (h // per_tile) * LANES:wq + (h // per_tile + 1) * LANES]
            shift = (C_NOPE - C_ROPE * (h % per_tile)) % LANES
            if shift:
                t_swap = pltpu.roll(t_swap, shift, 1)
            qh = _mla_head_norm_rope(q2_scr[rows, sl], t_swap, q_tab, q_tab_swap)
            q_ref[rows, sl] = (qh + qone_ref[...]).astype(BF)
            kh = _mla_head_norm_rope(kv_scr[rows, sl] + k_rope, k_rope_swap, k_tab, k_tab_swap)
            k_ref[rows, sl] = (kh + kshift_ref[...]).astype(BF)
        return carry

    lax.fori_loop(0, x_ref.shape[0] // NORM_ROWS, body, 0)


def _cctx_kernel(ckv_ref, kr_ref, wukv_ref, kg_ref, kshift_ref, k_ref, v_ref):
    _mla_keys_values(ckv_ref[...].astype(BF), kr_ref[...], None, wukv_ref, kg_ref[...], None, kshift_ref[...],
                     k_ref, v_ref)


def _mla_weights(w_down, w_uq, q_gain, k_gain):
    hd = C_NOPE + C_ROPE
    pad_lanes = C_HEAD_PAD - hd
    lane = np.arange(C_HEAD_PAD)
    is_rope = (lane >= C_NOPE) & (lane < hd)
    partner = np.where(is_rope, lane ^ (C_ROPE // 4), lane)

    def swapped(t):
        return jnp.where(is_rope, jnp.take(t, partner, axis=-1), 0.0)

    kr_cols = jnp.pad(w_down[:, C_Q_LORA + C_KV_LORA:], ((0, 0), (C_NOPE, pad_lanes)))
    wd = jnp.concatenate([w_down[:, :C_Q_LORA + C_KV_LORA], kr_cols, swapped(kr_cols)], axis=1)
    wuq = jnp.pad(w_uq.reshape(C_Q_LORA, C_HEADS, hd), ((0, 0), (0, 0), (0, pad_lanes)))
    wuq_swap = swapped(wuq)[:, :, C_NOPE:hd].reshape(C_Q_LORA, C_SWAP_W)
    wuq = jnp.concatenate([wuq.reshape(C_Q_LORA, C_HEADS * C_HEAD_PAD), wuq_swap], axis=1)
    qg = jnp.pad(q_gain, (0, pad_lanes))
    kg = jnp.pad(k_gain, (0, pad_lanes))
    row = lambda t: t.reshape(1, C_HEAD_PAD)
    bound = 1.02 * math.sqrt(hd) * LOG2E * jnp.max(jnp.abs(q_gain)) * jnp.max(jnp.abs(k_gain))
    shift_lane = lane == C_SHIFT_LANE
    q_one = row(jnp.asarray(shift_lane, F32))
    k_shift = row(jnp.where(shift_lane, -bound, 0.0))
    fixed_shift_ok = (bound <= SOFTMAX_SHIFT_MAX).astype(jnp.int32).reshape(1)
    return (wd.astype(BF), wuq.astype(BF), row(qg), row(swapped(qg)), row(kg), row(swapped(kg)),
            q_one, k_shift, fixed_shift_ok)


def _cproj(x, mod, wd, cq_gain, ckv_gain, wuq, wukv, q_one, k_shift, q_tab, q_tab_swap, k_tab, k_tab_swap):
    wq = C_HEADS * C_HEAD_PAD
    return pl.pallas_call(
        _cproj_kernel,
        grid=(N_TILES,),
        in_specs=[_tok_spec(D_MODEL), _MOD_SPEC, _const_spec((D_MODEL, C_DOWN_PAD)),
                  _const_spec((1, C_Q_LORA)), _const_spec((1, C_KV_LORA)),
                  _const_spec((C_Q_LORA, wq + C_SWAP_W)), _const_spec((C_KV_LORA, wq)),
                  _const_spec((1, C_HEAD_PAD)), _const_spec((1, C_HEAD_PAD)),
                  _ROPE_SPEC, _ROPE_SPEC, _ROPE_SPEC, _ROPE_SPEC],
        out_specs=[_tok_spec(wq), _tok_spec(wq), _tok_spec(wq), _tok_spec(C_KV_LORA), _tok_spec(LANES)],
        out_shape=[jax.ShapeDtypeStruct((N_TOK, wq), BF), jax.ShapeDtypeStruct((N_TOK, wq), BF),
                   jax.ShapeDtypeStruct((N_TOK, wq), BF), jax.ShapeDtypeStruct((N_TOK, C_KV_LORA), F32),
                   jax.ShapeDtypeStruct((N_TOK, LANES), F32)],
        scratch_shapes=[pltpu.VMEM((TM, wq + C_SWAP_W), F32), pltpu.VMEM((TM, wq), F32),
                        pltpu.VMEM((TM, LANES), F32)],
        compiler_params=_params("arbitrary"),
        name="mla_proj",
    )(x, mod, wd, cq_gain.reshape(1, C_Q_LORA), ckv_gain.reshape(1, C_KV_LORA), wuq, wukv, q_one, k_shift,
      q_tab, q_tab_swap, k_tab, k_tab_swap)


def _cctx(cache_ckv, cache_krope, wukv, kg, k_shift):
    n = DEC_BATCH * PAST_LEN
    wq = C_HEADS * C_HEAD_PAD
    kr = jnp.pad(cache_krope.reshape(n, C_ROPE), ((0, 0), (C_NOPE, C_HEAD_PAD - C_NOPE - C_ROPE)))
    return pl.pallas_call(
        _cctx_kernel,
        grid=(n // TM,),
        in_specs=[_tok_spec(C_KV_LORA), _tok_spec(LANES), _const_spec((C_KV_LORA, wq)),
                  _const_spec((1, C_HEAD_PAD)), _const_spec((1, C_HEAD_PAD))],
        out_specs=[_tok_spec(wq), _tok_spec(wq)],
        out_shape=[jax.ShapeDtypeStruct((n, wq), BF), jax.ShapeDtypeStruct((n, wq), BF)],
        compiler_params=_params("arbitrary"),
        name="mla_context_keys",
    )(cache_ckv.reshape(n, C_KV_LORA), kr, wukv, kg, k_shift)


def _mla_attend(ok_ref, q_ref, kv_refs, o_ref, n_pairs=1):
    tq = q_ref.shape[0]

    def pair_cols(pair):
        return (slice(pair * LANES, (pair + 1) * LANES),
                [slice((2 * pair + e) * C_HEAD_PAD, (2 * pair + e + 1) * C_HEAD_PAD) for e in range(2)])

    def normalised(sum_acc, rows):
        lo = lax.broadcasted_iota(jnp.int32, (rows, LANES), 1) < C_VDIM
        r0 = pltpu.roll(sum_acc[0], C_VDIM, 1)
        r1 = pltpu.roll(sum_acc[1], C_VDIM, 1)
        return jnp.where(lo, r0 / sum_acc[0], sum_acc[1] / r1).astype(BF)

    chunks = []
    for k_ref, v_ref in kv_refs:
        nk = k_ref.shape[0]
        if nk < CK and chunks:
            chunks[-1].append((k_ref, v_ref, slice(0, nk)))
        else:
            ck = min(CK, nk)
            chunks += [[(k_ref, v_ref, slice(c * ck, (c + 1) * ck))] for c in range(nk // ck)]
    def fixed_shift():
        for pair in range(n_pairs):
            cols, head_cols = pair_cols(pair)
            sum_acc = [None, None]
            for parts in chunks:
                for e, sl in enumerate(head_cols):
                    kk = [k_ref[rows, sl] for k_ref, _, rows in parts]
                    vv = [v_ref[rows, sl] for _, v_ref, rows in parts]
                    p = jnp.exp2(_dot_t(q_ref[:, sl], kk[0] if len(kk) == 1 else jnp.concatenate(kk, axis=0)))
                    pv = _dot(p.astype(BF), vv[0] if len(vv) == 1 else jnp.concatenate(vv, axis=0))
                    sum_acc[e] = pv if sum_acc[e] is None else sum_acc[e] + pv
            o_ref[:, cols] = normalised(sum_acc, tq)

    def row_max_shift():
        blk = min(tq, LANES)

        def body(i, carry):
            rows = pl.ds(pl.multiple_of(i * blk, blk), blk)
            for pair in range(n_pairs):
                cols, head_cols = pair_cols(pair)
                sum_acc = []
                for sl in head_cols:
                    qh = q_ref[rows, sl]
                    scores = [_dot_t(qh, k_ref[:, sl]) for k_ref, _ in kv_refs]
                    m = functools.reduce(jnp.maximum, [jnp.max(sc, axis=-1, keepdims=True) for sc in scores])
                    sum_acc.append(functools.reduce(jnp.add, [_dot(jnp.exp2(sc - m).astype(BF), v_ref[:, sl])
                                                              for sc, (_, v_ref) in zip(scores, kv_refs)]))
                o_ref[rows, cols] = normalised(sum_acc, blk)
            return carry

        lax.fori_loop(0, tq // blk, body, 0)

    lax.cond(ok_ref[0] != 0, fixed_shift, row_max_shift)


def _cattn_lat_kernel(ok_ref, q_ref, k_ref, v_ref, ck_ref, cv_ref, o_ref):
    _mla_attend(ok_ref, q_ref, ((k_ref, v_ref), (ck_ref, cv_ref)), o_ref)


def _cattn_ctx_kernel(ok_ref, q_ref, k_ref, v_ref, o_ref):
    _mla_attend(ok_ref, q_ref, ((k_ref, v_ref),), o_ref, C_HEADS // 2)


def _cattn(fixed_shift_ok, q, k, v, ck, cv):
    pair_w = 2 * C_HEAD_PAD
    n_pairs = C_HEADS // 2
    nqt = DEC_SEQ // CQ
    smem = pl.BlockSpec(memory_space=pltpu.SMEM)
    o_sample = pl.pallas_call(
        _cattn_lat_kernel,
        grid=(DEC_BATCH, n_pairs, nqt),
        in_specs=[smem, pl.BlockSpec((CQ, pair_w), lambda b, p, t: (b * nqt + t, p)),
                  pl.BlockSpec((DEC_SEQ, pair_w), lambda b, p, t: (b, p)),
                  pl.BlockSpec((DEC_SEQ, pair_w), lambda b, p, t: (b, p)),
                  pl.BlockSpec((PAST_LEN, pair_w), lambda b, p, t: (b, p)),
                  pl.BlockSpec((PAST_LEN, pair_w), lambda b, p, t: (b, p))],
        out_specs=pl.BlockSpec((CQ, LANES), lambda b, p, t: (b * nqt + t, p)),
        out_shape=jax.ShapeDtypeStruct((N_SAMPLE, C_HEADS * C_VDIM), BF),
        compiler_params=_params("arbitrary", "arbitrary", "arbitrary"),
        name="mla_attn_latent",
    )(fixed_shift_ok, q, k, v, ck, cv)
    off = N_SAMPLE // SEQ
    o_prompt = pl.pallas_call(
        _cattn_ctx_kernel,
        grid=(BATCH,),
        in_specs=[smem, pl.BlockSpec((SEQ, n_pairs * pair_w), lambda b: (off + b, 0)),
                  pl.BlockSpec((SEQ, n_pairs * pair_w), lambda b: (off + b, 0)),
                  pl.BlockSpec((SEQ, n_pairs * pair_w), lambda b: (off + b, 0))],
        out_specs=pl.BlockSpec((SEQ, n_pairs * LANES), lambda b: (b, 0)),
        out_shape=jax.ShapeDtypeStruct((N_PROMPT, C_HEADS * C_VDIM), BF),
        compiler_params=_params("arbitrary"),
        name="mla_attn_context",
    )(fixed_shift_ok, q, k, v)
    return o_sample, o_prompt


def kernel(x_prompt, x_sample, c, cache_win_k, cache_win_v, cache_mla_ckv, cache_mla_krope, c_ctx,
           ada_w, ada_b, ffn_w_in, ffn_w_out,
           gmlp_w_in, gmlp_v_gain, gmlp_w_s, gmlp_b_s, gmlp_w_out,
           win_w_qkv, win_q_gain, win_k_gain, win_sink, win_w_o,
           mla_w_down, mla_cq_gain, mla_ckv_gain, mla_w_uq, mla_w_ukv, mla_q_gain, mla_k_gain, mla_w_o):
    x = (x_sample.reshape(N_SAMPLE, D_MODEL), x_prompt.reshape(N_PROMPT, D_MODEL))
    cond = jnp.concatenate([c, c_ctx[None, :], jnp.zeros((N_COND - DEC_BATCH - 1, D_MODEL), F32)], axis=0)
    mods = _modulation(cond, ada_w, ada_b)
    w_in_b, shift_terms = _ffn_prep(mods, ffn_w_in)
    w_out_b = ffn_w_out.astype(BF)

    lane = np.arange(LANES)
    b_cos, b_sin = _rope_tables(B_HEAD_DIM, lane % B_HEAD_DIM)
    c_lane = np.where((lane >= C_NOPE) & (lane < C_NOPE + C_ROPE), lane - C_NOPE, -1)
    c_cos, c_sin = _rope_tables(C_ROPE, c_lane)

    nk = B_KV_HEADS * B_HEAD_DIM
    win_k, win_v, mla_ckv, mla_krope = [], [], [], []
    ia = ib = ic = 0
    for l in range(DEPTH):
        mod = mods[l]
        pre_proj = None
        x = _ffn(x, mod, shift_terms, w_in_b, w_out_b, l, 0, split_in=(l == 0))
        kind = l % N_MIXERS
        if kind == 0:
            x = _gmlp(x, mod, gmlp_w_in[ia], gmlp_v_gain[ia], gmlp_w_s[ia], gmlp_b_s[ia], gmlp_w_out[ia])
            ia += 1
        elif kind == 1:
            q, k, v = _bproj(x, mod, win_w_qkv[ib], win_q_gain[ib], win_k_gain[ib], b_cos, b_sin)
            x = _battn(x, mod, q, k, v,
                       cache_win_k[:, ib].reshape(DEC_BATCH, PAST_LEN, nk),
                       cache_win_v[:, ib].reshape(DEC_BATCH, PAST_LEN, nk),
                       win_sink[ib], win_w_o[ib].astype(BF), win_q_gain[ib], win_k_gain[ib])
            win_k.append(k[N_SAMPLE:].reshape(BATCH, SEQ, B_KV_HEADS, B_HEAD_DIM))
            win_v.append(v[N_SAMPLE:].reshape(BATCH, SEQ, B_KV_HEADS, B_HEAD_DIM))
            ib += 1
        else:
            wd, wuq, qg, qgs, kg, kgs, q_one, k_shift, fixed_shift_ok = _mla_weights(
                mla_w_down[ic], mla_w_uq[ic], mla_q_gain[ic], mla_k_gain[ic])
            wukv = mla_w_ukv[ic].astype(BF)
            q_scale = (C_NOPE + C_ROPE) ** -0.5 * LOG2E
            q, k, v, ckv, kr = _cproj(x, mod, wd, mla_cq_gain[ic], mla_ckv_gain[ic], wuq, wukv, q_one, k_shift,
                                      c_cos * (qg * q_scale), c_sin * (qgs * q_scale), c_cos * kg, c_sin * kgs)
            ck, cv = _cctx(cache_mla_ckv[:, ic], cache_mla_krope[:, ic], wukv, kg, k_shift)
            pre_proj = (*_cattn(fixed_shift_ok, q, k, v, ck, cv), mla_w_o[ic].astype(BF))
            mla_ckv.append(ckv[N_SAMPLE:].reshape(BATCH, SEQ, C_KV_LORA))
            mla_krope.append(kr[N_SAMPLE:, C_NOPE:C_NOPE + C_ROPE].reshape(BATCH, SEQ, C_ROPE))
            ic += 1
        x = _ffn(x, mod, shift_terms, w_in_b, w_out_b, l, 1, split_out=(l == DEPTH - 1), pre_proj=pre_proj)
    y_sample, y_prompt = x
    return (y_prompt.reshape(BATCH, SEQ, D_MODEL), y_sample.reshape(DEC_BATCH, DEC_SEQ, D_MODEL),
            jnp.stack(win_k, axis=1), jnp.stack(win_v, axis=1),
            jnp.stack(mla_ckv, axis=1), jnp.stack(mla_krope, axis=1))
```
